```python
import math
import jax
import jax.numpy as jnp
from jax import lax
import numpy as np

D_MODEL = 2048
BATCH = 2
SEQ = 4096
DEPTH = 1
DEC_BATCH = 8
DEC_SEQ = 4
PAST_LEN = 16384
PAGE_SIZE = 128

D_MIX = D_MODEL
D_ATTN = D_MIX // 2
D_SSM = D_MIX - D_ATTN
ATT_HEAD_DIM = 64
N_ATT_HEADS = D_ATTN // (2 * ATT_HEAD_DIM)
ATT_V_DIM = 2 * ATT_HEAD_DIM
ROT_DIM = ATT_HEAD_DIM // 4
ROPE_THETA = 500000.0
Q_BLOCK = 128
SSM_HEAD_DIM = 64
N_SSM_HEADS = D_SSM // SSM_HEAD_DIM
SSM_GROUPS = 2
SSM_STATE = 128
SSM_CHUNK = 128
CONV_WIDTH = 4
CONV_DIM = D_SSM + 2 * SSM_GROUPS * SSM_STATE
PROJ_SPLITS = (D_ATTN, 2 * D_ATTN, 3 * D_ATTN, 3 * D_ATTN + D_SSM, 3 * D_ATTN + D_SSM + CONV_DIM)
D_IN_PROJ = 3 * D_ATTN + D_SSM + CONV_DIM + N_SSM_HEADS
N_EXPERT_GROUPS = 4
EXPERTS_PER_GROUP = 8
N_EXPERTS = N_EXPERT_GROUPS * EXPERTS_PER_GROUP
TOP_K_IN_GROUP = 2
D_FF_EXPERT = 512
D_PLE = 256
RMS_EPS = 1e-6
POOL_FACTOR = 1.25
MOE_TOKEN_BLOCKS = (512, 128)

kernel_name = 'hymba_diffattn_ssd_hmoe_step'

F32 = jnp.float32


def rmsnorm(x, w):
    xf = x.astype(F32)
    y = xf * lax.rsqrt(jnp.mean(xf * xf, axis=-1, keepdims=True) + RMS_EPS)
    return (y * w.astype(F32)).astype(x.dtype)


def partial_rope(x, pos):
    half = ROT_DIM // 2
    inv_freq = jnp.power(ROPE_THETA, -jnp.arange(half, dtype=F32) * (2.0 / ROT_DIM))
    ang = pos.astype(F32)[:, None] * inv_freq[None, :]
    cos = jnp.cos(ang)[None, :, None, None, :]
    sin = jnp.sin(ang)[None, :, None, None, :]
    xr = x[..., :ROT_DIM].astype(F32)
    x1, x2 = xr[..., :half], xr[..., half:]
    rot = jnp.concatenate([x1 * cos - x2 * sin, x2 * cos + x1 * sin], axis=-1).astype(x.dtype)
    return jnp.concatenate([rot, x[..., ROT_DIM:]], axis=-1)


def diff_attention(q, k, v, q_pos, k_pos, lam):
    b, lq = q.shape[0], q.shape[1]
    qb = Q_BLOCK if lq % Q_BLOCK == 0 else lq
    nb = lq // qb
    kf = k.astype(F32)
    vf = v.astype(F32)
    qs = (q.astype(F32) * (ATT_HEAD_DIM ** -0.5)).reshape((b, nb, qb) + q.shape[2:]).swapaxes(0, 1)
    ps = q_pos.reshape(nb, qb)

    def one_block(args):
        qblk, pblk = args
        s = jnp.einsum('bqhmd,bkhmd->mbhqk', qblk, kf)
        mask = k_pos[None, :] <= pblk[:, None]
        s = jnp.where(mask, s, -jnp.inf)
        pr = jax.nn.softmax(s, axis=-1)
        a = pr[0] - lam * pr[1]
        return jnp.einsum('bhqk,bkhd->bqhd', a, vf)

    o = lax.map(one_block, (qs, ps))
    return o.swapaxes(0, 1).reshape((b, lq) + o.shape[3:])


def segsum(a):
    t = a.shape[-1]
    cs = jnp.cumsum(a, axis=-1)
    d = cs[..., :, None] - cs[..., None, :]
    mask = jnp.tril(jnp.ones((t, t), dtype=bool))
    return jnp.where(mask, d, -jnp.inf)


def ssd_scan(x, dt, A, Bm, Cm, h0):
    b, l = x.shape[0], x.shape[1]
    cl = SSM_CHUNK if l >= SSM_CHUNK else l
    pad = (-l) % cl
    if pad:
        x = jnp.pad(x, ((0, 0), (0, pad), (0, 0), (0, 0)))
        dt = jnp.pad(dt, ((0, 0), (0, pad), (0, 0)))
        Bm = jnp.pad(Bm, ((0, 0), (0, pad), (0, 0), (0, 0)))
        Cm = jnp.pad(Cm, ((0, 0), (0, pad), (0, 0), (0, 0)))
    nc = (l + pad) // cl
    g, kk = SSM_GROUPS, N_SSM_HEADS // SSM_GROUPS
    X = (x * dt[..., None]).reshape(b, nc, cl, g, kk, SSM_HEAD_DIM)
    Ad = (dt * A).reshape(b, nc, cl, g, kk).transpose(0, 3, 4, 1, 2)
    Bc = Bm.reshape(b, nc, cl, g, SSM_STATE)
    Cc = Cm.reshape(b, nc, cl, g, SSM_STATE)
    Acs = jnp.cumsum(Ad, axis=-1)
    Lmat = jnp.exp(segsum(Ad))
    CB = jnp.einsum('bclgn,bcsgn->bgcls', Cc, Bc)
    y_diag = jnp.einsum('bgkcls,bcsgkp->bclgkp', CB[:, :, None] * Lmat, X)
    decay_states = jnp.exp(Acs[..., -1:] - Acs).transpose(0, 3, 4, 1, 2)
    states = jnp.einsum('bclgn,bclgkp->bcgkpn', Bc, X * decay_states[..., None])
    states = jnp.concatenate([h0.reshape(b, 1, g, kk, SSM_HEAD_DIM, SSM_STATE), states], axis=1)
    chunk_decay = jnp.exp(segsum(jnp.pad(Acs[..., -1], ((0, 0), (0, 0), (0, 0), (1, 0)))))
    new_states = jnp.einsum('bgkzc,bcgkpn->bzgkpn', chunk_decay, states)
    states_in, final = new_states[:, :-1], new_states[:, -1]
    y_off = jnp.einsum('bclgn,bcgkpn->bclgkp', Cc, states_in) * jnp.exp(Acs).transpose(0, 3, 4, 1, 2)[..., None]
    y = (y_diag + y_off).reshape(b, nc * cl, N_SSM_HEADS, SSM_HEAD_DIM)[:, :l]
    return y, final.reshape(b, N_SSM_HEADS, SSM_HEAD_DIM, SSM_STATE)


def ssd_mixer(z, xbc, dt_raw, conv_prev, h0, conv_w, conv_b, dt_bias, A_log, D_skip, w_ssm_norm):
    b, l = xbc.shape[0], xbc.shape[1]
    full = jnp.concatenate([conv_prev.astype(xbc.dtype), xbc], axis=1)
    conv_new = full[:, full.shape[1] - (CONV_WIDTH - 1):]
    conv = lax.conv_general_dilated(full, conv_w[:, None, :], window_strides=(1,), padding='VALID',
                                    dimension_numbers=('NWC', 'WIO', 'NWC'), feature_group_count=CONV_DIM)
    xbc_c = jax.nn.silu(conv + conv_b)
    xs = xbc_c[..., :D_SSM].reshape(b, l, N_SSM_HEADS, SSM_HEAD_DIM).astype(F32)
    Bm = xbc_c[..., D_SSM:D_SSM + SSM_GROUPS * SSM_STATE].reshape(b, l, SSM_GROUPS, SSM_STATE).astype(F32)
    Cm = xbc_c[..., D_SSM + SSM_GROUPS * SSM_STATE:].reshape(b, l, SSM_GROUPS, SSM_STATE).astype(F32)
    dt = jax.nn.softplus(dt_raw.astype(F32) + dt_bias.astype(F32))
    A = -jnp.exp(A_log.astype(F32))
    y, h_final = ssd_scan(xs, dt, A, Bm, Cm, h0.astype(F32))
    y = (y + xs * D_skip.astype(F32)[:, None]).reshape(b, l, D_SSM)
    yg = (y * jax.nn.silu(z.astype(F32))).reshape(b, l, SSM_GROUPS, D_SSM // SSM_GROUPS)
    yg = yg * lax.rsqrt(jnp.mean(yg * yg, axis=-1, keepdims=True) + RMS_EPS)
    out = yg.reshape(b, l, D_SSM) * w_ssm_norm.astype(F32)
    return out.astype(z.dtype), h_final, conv_new


def token_block(n):
    for cand in MOE_TOKEN_BLOCKS:
        if n % cand == 0:
            return cand
    return n


def hier_moe(u, w_group_router, b_group_router, w_expert_router, b_expert_router, w_exp_gate, w_exp_up, w_exp_down):
    b, l, d = u.shape
    n_tok = b * l
    t = u.reshape(n_tok, d)
    g_logits = jnp.dot(t, w_group_router).astype(F32) + b_group_router.astype(F32)
    g_prob = jax.nn.softmax(g_logits, axis=-1)
    g_idx = jnp.argmax(g_logits, axis=-1)
    g_w = jnp.take_along_axis(g_prob, g_idx[:, None], axis=-1)
    e_logits = (jnp.dot(t, w_expert_router).astype(F32) + b_expert_router.astype(F32)).reshape(n_tok, N_EXPERT_GROUPS, EXPERTS_PER_GROUP)
    e_in = jnp.take_along_axis(e_logits, g_idx[:, None, None], axis=1)[:, 0]
    e_prob = jax.nn.softmax(e_in, axis=-1)
    top_p, top_i = lax.top_k(e_prob, TOP_K_IN_GROUP)
    top_p = top_p / jnp.sum(top_p, axis=-1, keepdims=True)
    within = jnp.einsum('tk,tke->te', top_p, jax.nn.one_hot(top_i, EXPERTS_PER_GROUP, dtype=F32))
    gates = (jax.nn.one_hot(g_idx, N_EXPERT_GROUPS, dtype=F32)[:, :, None] * (g_w * within)[:, None, :]).reshape(n_tok, N_EXPERTS)
    blk = token_block(n_tok)

    def expert_block(args):
        tb, gb = args
        hg = jnp.einsum('td,edf->tef', tb, w_exp_gate)
        hu = jnp.einsum('td,edf->tef', tb, w_exp_up)
        hh = jax.nn.silu(hg) * hu * gb[:, :, None].astype(tb.dtype)
        return jnp.einsum('tef,efd->td', hh, w_exp_down)

    out = lax.map(expert_block, (t.reshape(n_tok // blk, blk, d), gates.reshape(n_tok // blk, blk, N_EXPERTS)))
    return out.reshape(b, l, d)


def layer_forward(h, p, q_pos, k_past, v_past, conv_prev, ssm_prev, lam, lam_init,
                  w_norm_mix, w_in, w_subln, conv_w, conv_b, dt_bias, A_log, D_skip, w_ssm_norm, w_out,
                  w_norm_ffn, w_group_router, b_group_router, w_expert_router, b_expert_router,
                  w_exp_gate, w_exp_up, w_exp_down, w_norm_ple, w_ple_gate, w_ple_proj):
    b, l, _ = h.shape
    u = rmsnorm(h, w_norm_mix)
    proj = u @ w_in
    q, k, v, z, xbc, dt_raw = jnp.split(proj, PROJ_SPLITS, axis=-1)
    q = partial_rope(q.reshape(b, l, N_ATT_HEADS, 2, ATT_HEAD_DIM), q_pos)
    k = partial_rope(k.reshape(b, l, N_ATT_HEADS, 2, ATT_HEAD_DIM), q_pos)
    v = v.reshape(b, l, N_ATT_HEADS, ATT_V_DIM)
    if k_past is None:
        k_all, v_all, k_pos = k, v, q_pos
    else:
        past = k_past.shape[1]
        k_all = jnp.concatenate([k_past.reshape(b, past, N_ATT_HEADS, 2, ATT_HEAD_DIM).astype(k.dtype), k], axis=1)
        v_all = jnp.concatenate([v_past.astype(v.dtype), v], axis=1)
        k_pos = jnp.arange(past + l, dtype=jnp.int32)
    attn = diff_attention(q, k_all, v_all, q_pos, k_pos, lam)
    attn = (rmsnorm(attn, w_subln) * (1.0 - lam_init)).astype(h.dtype).reshape(b, l, D_ATTN)
    ssd, ssm_new, conv_new = ssd_mixer(z, xbc, dt_raw, conv_prev, ssm_prev, conv_w, conv_b, dt_bias, A_log, D_skip, w_ssm_norm)
    h = h + jnp.concatenate([attn, ssd], axis=-1) @ w_out
    h = h + hier_moe(rmsnorm(h, w_norm_ffn), w_group_router, b_group_router, w_expert_router, b_expert_router,
                     w_exp_gate, w_exp_up, w_exp_down)
    h = h + jax.nn.sigmoid(rmsnorm(h, w_norm_ple) @ w_ple_gate) * (p @ w_ple_proj)
    return h, k.reshape(b, l, N_ATT_HEADS, 2 * ATT_HEAD_DIM), v, conv_new, ssm_new


def _normal(key, shape, scale):
    return jax.random.normal(key, shape, F32) * scale


def setup_inputs(seed: int = 0) -> dict:
    key = jax.random.key(seed)
    ks = list(jax.random.split(key, 40))
    n_pages = PAST_LEN // PAGE_SIZE
    n_phys = int(math.ceil(POOL_FACTOR * DEC_BATCH * n_pages))
    page_table = jax.random.permutation(ks[0], n_phys)[:DEC_BATCH * n_pages].reshape(DEC_BATCH, n_pages).astype(jnp.int32)
    u_dt = jax.random.uniform(ks[1], (DEPTH, N_SSM_HEADS), F32)
    dt0 = jnp.exp(u_dt * (math.log(0.1) - math.log(0.001)) + math.log(0.001))
    dt_bias = dt0 + jnp.log(-jnp.expm1(-dt0))
    A_log = jnp.log(jax.random.uniform(ks[2], (DEPTH, N_SSM_HEADS), F32, minval=1.0, maxval=16.0))
    return {
        'x_prompt': _normal(ks[3], (BATCH, SEQ, D_MODEL), 1.0),
        'x_sample': _normal(ks[4], (DEC_BATCH, DEC_SEQ, D_MODEL), 1.0),
        'cache_k': _normal(ks[5], (DEPTH, n_phys, PAGE_SIZE, N_ATT_HEADS, 2 * ATT_HEAD_DIM), 1.0),
        'cache_v': _normal(ks[6], (DEPTH, n_phys, PAGE_SIZE, N_ATT_HEADS, ATT_V_DIM), 1.0),
        'state_conv': _normal(ks[7], (DEPTH, DEC_BATCH, CONV_WIDTH - 1, CONV_DIM), 1.0),
        'state_ssm': _normal(ks[8], (DEPTH, DEC_BATCH, N_SSM_HEADS, SSM_HEAD_DIM, SSM_STATE), 0.5),
        'page_table': page_table,
        'p_prompt': _normal(ks[9], (DEPTH, BATCH, SEQ, D_PLE), 1.0),
        'p_sample': _normal(ks[10], (DEPTH, DEC_BATCH, DEC_SEQ, D_PLE), 1.0),
        'w_norm_mix': 1.0 + _normal(ks[11], (DEPTH, D_MODEL), 0.01),
        'w_in': _normal(ks[12], (DEPTH, D_MODEL, D_IN_PROJ), D_MODEL ** -0.5),
        'lambda_q1': _normal(ks[13], (DEPTH, ATT_HEAD_DIM), 0.1),
        'lambda_k1': _normal(ks[14], (DEPTH, ATT_HEAD_DIM), 0.1),
        'lambda_q2': _normal(ks[15], (DEPTH, ATT_HEAD_DIM), 0.1),
        'lambda_k2': _normal(ks[16], (DEPTH, ATT_HEAD_DIM), 0.1),
        'w_subln': 1.0 + _normal(ks[17], (DEPTH, ATT_V_DIM), 0.01),
        'conv_w': _normal(ks[18], (DEPTH, CONV_WIDTH, CONV_DIM), CONV_WIDTH ** -0.5),
        'conv_b': _normal(ks[19], (DEPTH, CONV_DIM), 0.01),
        'dt_bias': dt_bias,
        'A_log': A_log,
        'D_skip': 1.0 + _normal(ks[20], (DEPTH, N_SSM_HEADS), 0.01),
        'w_ssm_norm': 1.0 + _normal(ks[21], (DEPTH, D_SSM), 0.01),
        'w_out': _normal(ks[22], (DEPTH, D_MIX, D_MODEL), D_MIX ** -0.5),
        'w_norm_ffn': 1.0 + _normal(ks[23], (DEPTH, D_MODEL), 0.01),
        'w_group_router': _normal(ks[24], (DEPTH, D_MODEL, N_EXPERT_GROUPS), D_MODEL ** -0.5),
        'b_group_router': _normal(ks[25], (DEPTH, N_EXPERT_GROUPS), 0.01),
        'w_expert_router': _normal(ks[26], (DEPTH, D_MODEL, N_EXPERTS), D_MODEL ** -0.5),
        'b_expert_router': _normal(ks[27], (DEPTH, N_EXPERTS), 0.01),
        'w_exp_gate': _normal(ks[28], (DEPTH, N_EXPERTS, D_MODEL, D_FF_EXPERT), D_MODEL ** -0.5),
        'w_exp_up': _normal(ks[29], (DEPTH, N_EXPERTS, D_MODEL, D_FF_EXPERT), D_MODEL ** -0.5),
        'w_exp_down': _normal(ks[30], (DEPTH, N_EXPERTS, D_FF_EXPERT, D_MODEL), D_FF_EXPERT ** -0.5),
        'w_norm_ple': 1.0 + _normal(ks[31], (DEPTH, D_MODEL), 0.01),
        'w_ple_gate': _normal(ks[32], (DEPTH, D_MODEL, D_MODEL), D_MODEL ** -0.5),
        'w_ple_proj': _normal(ks[33], (DEPTH, D_PLE, D_MODEL), D_PLE ** -0.5),
        'w_norm_final': 1.0 + _normal(ks[34], (D_MODEL,), 0.01),
    }


def reference(x_prompt, x_sample, cache_k, cache_v, state_conv, state_ssm, page_table, p_prompt, p_sample,
              w_norm_mix, w_in, lambda_q1, lambda_k1, lambda_q2, lambda_k2, w_subln, conv_w, conv_b,
              dt_bias, A_log, D_skip, w_ssm_norm, w_out, w_norm_ffn, w_group_router, b_group_router,
              w_expert_router, b_expert_router, w_exp_gate, w_exp_up, w_exp_down, w_norm_ple,
              w_ple_gate, w_ple_proj, w_norm_final):
    bp, lp = x_prompt.shape[0], x_prompt.shape[1]
    bs, ls = x_sample.shape[0], x_sample.shape[1]
    n_pages = page_table.shape[1]
    past = n_pages * cache_k.shape[2]
    pos_prompt = jnp.arange(lp, dtype=jnp.int32)
    pos_sample = past + jnp.arange(ls, dtype=jnp.int32)
    hp, hs = x_prompt, x_sample
    kp_list, vp_list, cp_list, sp_list = [], [], [], []
    ks_list, vs_list, cs_list, ss_list = [], [], [], []
    for i in range(DEPTH):
        lam_init = 0.8 - 0.6 * math.exp(-0.3 * i)
        lam = (jnp.exp(jnp.sum(lambda_q1[i].astype(F32) * lambda_k1[i].astype(F32)))
               - jnp.exp(jnp.sum(lambda_q2[i].astype(F32) * lambda_k2[i].astype(F32))) + lam_init)
        lw = (w_norm_mix[i], w_in[i], w_subln[i], conv_w[i], conv_b[i], dt_bias[i], A_log[i], D_skip[i],
              w_ssm_norm[i], w_out[i], w_norm_ffn[i], w_group_router[i], b_group_router[i],
              w_expert_router[i], b_expert_router[i], w_exp_gate[i], w_exp_up[i], w_exp_down[i],
              w_norm_ple[i], w_ple_gate[i], w_ple_proj[i])
        conv0 = jnp.zeros((bp, CONV_WIDTH - 1, CONV_DIM), x_prompt.dtype)
        ssm0 = jnp.zeros((bp, N_SSM_HEADS, SSM_HEAD_DIM, SSM_STATE), F32)
        hp, kp, vp, cp, sp = layer_forward(hp, p_prompt[i], pos_prompt, None, None, conv0, ssm0, lam, lam_init, *lw)
        k_past = cache_k[i][page_table].reshape(bs, past, N_ATT_HEADS, 2 * ATT_HEAD_DIM)
        v_past = cache_v[i][page_table].reshape(bs, past, N_ATT_HEADS, ATT_V_DIM)
        hs, kn, vn, cn, sn = layer_forward(hs, p_sample[i], pos_sample, k_past, v_past, state_conv[i], state_ssm[i],
                                           lam, lam_init, *lw)
        kp_list.append(kp); vp_list.append(vp); cp_list.append(cp); sp_list.append(sp)
        ks_list.append(kn); vs_list.append(vn); cs_list.append(cn); ss_list.append(sn)
    y_prompt = rmsnorm(hp, w_norm_final)
    y_sample = rmsnorm(hs, w_norm_final)
    k_prompt = jnp.stack(kp_list, axis=0)
    v_prompt = jnp.stack(vp_list, axis=0)
    conv_prompt = jnp.stack(cp_list, axis=0)
    ssm_prompt = jnp.stack(sp_list, axis=0)
    k_sample = jnp.stack(ks_list, axis=0)
    v_sample = jnp.stack(vs_list, axis=0)
    conv_sample = jnp.stack(cs_list, axis=0)
    ssm_sample = jnp.stack(ss_list, axis=0)
    return (y_prompt, y_sample, k_prompt, v_prompt, conv_prompt, ssm_prompt, k_sample, v_sample, conv_sample, ssm_sample)
```

```python
import functools
import math

import jax
import jax.numpy as jnp
from jax import lax
from jax.experimental import pallas as pl
from jax.experimental.pallas import tpu as pltpu

D_MODEL = 2048
D_ATTN = 1024
D_SSM = 1024
ATT_HEAD_DIM = 64
N_ATT_HEADS = 8
ATT_V_DIM = 128
ROT_DIM = 16
ROPE_THETA = 500000.0
SSM_HEAD_DIM = 64
N_SSM_HEADS = 16
SSM_GROUPS = 2
SSM_STATE = 128
SSM_CHUNK = 128
CONV_WIDTH = 4
CONV_DIM = D_SSM + 2 * SSM_GROUPS * SSM_STATE
D_IN_PROJ = 3 * D_ATTN + D_SSM + CONV_DIM + N_SSM_HEADS
N_EXPERT_GROUPS = 4
EXPERTS_PER_GROUP = 8
N_EXPERTS = 32
D_FF_EXPERT = 512
D_PLE = 256
RMS_EPS = 1e-6
LAM_INIT = 0.8 - 0.6 * math.exp(-0.3 * 0)

LANES = 128
SUBLANES = 8
VMEM_LIMIT_BYTES = 56 * 2 ** 20

SAMPLE_ROWS = 8
EXPERT_TILE = 256
PAGES_PER_STEP = 8

_F32 = jnp.float32
_BF16 = jnp.bfloat16
_HI = lax.Precision.HIGHEST
_NT = (((1,), (1,)), ((), ()))


def _cparams(semantics):
    return pltpu.CompilerParams(dimension_semantics=semantics, vmem_limit_bytes=VMEM_LIMIT_BYTES)


def _rms(x, w):
    return x * lax.rsqrt(jnp.mean(x * x, axis=-1, keepdims=True) + RMS_EPS) * w


def _silu(x):
    return x * (1.0 / (1.0 + jnp.exp(-x)))


def _softplus(x):
    return jnp.maximum(x, 0.0) + jnp.log(1.0 + jnp.exp(-jnp.abs(x)))


def _dot(a, b, precise=False):
    if precise:
        return jnp.dot(a.astype(_F32), b.astype(_F32), preferred_element_type=_F32, precision=_HI)
    return jnp.dot(a.astype(_BF16), b.astype(_BF16), preferred_element_type=_F32)


def _dot_nt(a, b, precise=False):
    if precise:
        return lax.dot_general(a.astype(_F32), b.astype(_F32), _NT, preferred_element_type=_F32, precision=_HI)
    return lax.dot_general(a.astype(_BF16), b.astype(_BF16), _NT, preferred_element_type=_F32)


def _rope_block(y, c, s1, s2):
    return y * c + pltpu.roll(y, LANES - ROT_DIM // 2, 1) * s1 + pltpu.roll(y, ROT_DIM // 2, 1) * s2


def _rope_tables(pos):
    half = ROT_DIM // 2
    inv_freq = jnp.power(ROPE_THETA, -jnp.arange(half, dtype=_F32) * (2.0 / ROT_DIM))
    ang = pos.astype(_F32)[:, None] * inv_freq[None, :]
    cos, sin = jnp.cos(ang), jnp.sin(ang)
    n = pos.shape[0]
    ones = jnp.ones((n, ATT_HEAD_DIM - ROT_DIM), _F32)
    zeros = jnp.zeros((n, ATT_HEAD_DIM - ROT_DIM), _F32)
    zh = jnp.zeros((n, half), _F32)
    c = jnp.concatenate([cos, cos, ones], axis=1)
    s1 = jnp.concatenate([-sin, zh, zeros], axis=1)
    s2 = jnp.concatenate([zh, sin, zeros], axis=1)
    return tuple(jnp.concatenate([t, t], axis=1) for t in (c, s1, s2))


def _inproj_qk_kernel(x_ref, wn_ref, w_ref, c_ref, s1_ref, s2_ref, q_ref, kf_ref, kb_ref):
    u = _rms(x_ref[...], wn_ref[...]).astype(_BF16)
    y = jnp.dot(u, w_ref[...], preferred_element_type=_F32)
    c, s1, s2 = c_ref[...], s1_ref[...], s2_ref[...]
    for j in range(2 * N_ATT_HEADS):
        r = _rope_block(y[:, j * LANES:(j + 1) * LANES], c, s1, s2)
        if j < N_ATT_HEADS:
            q_ref[:, j * LANES:(j + 1) * LANES] = (r * (ATT_HEAD_DIM ** -0.5)).astype(_BF16)
        else:
            jj = j - N_ATT_HEADS
            kf_ref[:, jj * LANES:(jj + 1) * LANES] = r
            kb_ref[:, jj * LANES:(jj + 1) * LANES] = r.astype(_BF16)


def _inproj_vz_kernel(x_ref, wn_ref, w_ref, vf_ref, vb_ref, z_ref):
    u = _rms(x_ref[...], wn_ref[...]).astype(_BF16)
    y = jnp.dot(u, w_ref[...], preferred_element_type=_F32)
    v = y[:, :D_ATTN]
    vf_ref[...] = v
    vb_ref[...] = v.astype(_BF16)
    z_ref[...] = y[:, D_ATTN:].astype(_BF16)


def _inproj_xbc_kernel(x_ref, wn_ref, w_ref, xbc_ref, dt_ref):
    u = _rms(x_ref[...], wn_ref[...]).astype(_BF16)
    y = jnp.dot(u, w_ref[...], preferred_element_type=_F32)
    xbc_ref[...] = y[:, :CONV_DIM]
    dt_ref[...] = y[:, CONV_DIM:]


def _prompt_inproj(x2d, w_norm, w_in, seq, tm):
    t = x2d.shape[0]
    nblk = t // tm
    per_seq = seq // tm
    wn = w_norm.reshape(1, D_MODEL)
    wb = w_in.astype(_BF16)
    row = lambda i: (i, 0)
    fixed = lambda i: (0, 0)
    x_spec = pl.BlockSpec((tm, D_MODEL), row)
    wn_spec = pl.BlockSpec((1, D_MODEL), fixed)

    c, s1, s2 = _rope_tables(jnp.arange(seq, dtype=jnp.int32))
    tab_spec = pl.BlockSpec((tm, LANES), lambda i: (i % per_seq, 0))
    q, kf, kb = pl.pallas_call(
        _inproj_qk_kernel,
        grid=(nblk,),
        in_specs=[x_spec, wn_spec, pl.BlockSpec((D_MODEL, 2 * D_ATTN), fixed), tab_spec, tab_spec, tab_spec],
        out_specs=[pl.BlockSpec((tm, D_ATTN), row)] * 3,
        out_shape=[jax.ShapeDtypeStruct((t, D_ATTN), _BF16), jax.ShapeDtypeStruct((t, D_ATTN), _F32),
                   jax.ShapeDtypeStruct((t, D_ATTN), _BF16)],
        compiler_params=_cparams(("arbitrary",)),
        name="inproj_qk",
    )(x2d, wn, wb[:, :2 * D_ATTN], c, s1, s2)

    vf, vb, z = pl.pallas_call(
        _inproj_vz_kernel,
        grid=(nblk,),
        in_specs=[x_spec, wn_spec, pl.BlockSpec((D_MODEL, D_ATTN + D_SSM), fixed)],
        out_specs=[pl.BlockSpec((tm, D_ATTN), row), pl.BlockSpec((tm, D_ATTN), row), pl.BlockSpec((tm, D_SSM), row)],
        out_shape=[jax.ShapeDtypeStruct((t, D_ATTN), _F32), jax.ShapeDtypeStruct((t, D_ATTN), _BF16),
                   jax.ShapeDtypeStruct((t, D_SSM), _BF16)],
        compiler_params=_cparams(("arbitrary",)),
        name="inproj_vz",
    )(x2d, wn, wb[:, 2 * D_ATTN:3 * D_ATTN + D_SSM])

    n_tail = CONV_DIM + LANES
    w_tail = jnp.pad(wb[:, 3 * D_ATTN + D_SSM:], ((0, 0), (0, n_tail - (CONV_DIM + N_SSM_HEADS))))
    xbc, dt = pl.pallas_call(
        _inproj_xbc_kernel,
        grid=(nblk,),
        in_specs=[x_spec, wn_spec, pl.BlockSpec((D_MODEL, n_tail), fixed)],
        out_specs=[pl.BlockSpec((tm, CONV_DIM), row), pl.BlockSpec((tm, LANES), row)],
        out_shape=[jax.ShapeDtypeStruct((t, CONV_DIM), _F32), jax.ShapeDtypeStruct((t, LANES), _F32)],
        compiler_params=_cparams(("arbitrary",)),
        name="inproj_xbc",
    )(x2d, wn, w_tail)
    return q, kf, kb, vf, vb, z, xbc, dt


def _lambda(lq1_ref, lk1_ref, lq2_ref, lk2_ref):
    a = jnp.sum(lq1_ref[...] * lk1_ref[...], axis=-1, keepdims=True)
    b = jnp.sum(lq2_ref[...] * lk2_ref[...], axis=-1, keepdims=True)
    return jnp.exp(a) - jnp.exp(b) + LAM_INIT


def _diff_finalize(o1, o2, lam, wsub):
    a = o1 - lam * o2
    return _rms(a, wsub) * (1.0 - LAM_INIT)


def _attn_kernel(q_ref, k_ref, v_ref, lq1_ref, lk1_ref, lq2_ref, lk2_ref, wsub_ref, o_ref,
                 qs_sc, m_sc, l_sc, acc_sc, *, tq):
    qi = pl.program_id(2)
    q = q_ref[...]
    lane = lax.broadcasted_iota(jnp.int32, (tq, LANES), 1)
    zero = jnp.zeros_like(q)
    qs_sc[0:tq, :] = jnp.where(lane < ATT_HEAD_DIM, q, zero)
    qs_sc[tq:2 * tq, :] = jnp.where(lane >= ATT_HEAD_DIM, q, zero)
    m_sc[...] = jnp.full(m_sc.shape, -jnp.inf, _F32)
    l_sc[...] = jnp.zeros(l_sc.shape, _F32)
    acc_sc[...] = jnp.zeros(acc_sc.shape, _F32)

    def step(ki, masked):
        start = pl.multiple_of(ki * tq, tq)
        k = k_ref[pl.ds(start, tq), :]
        v = v_ref[pl.ds(start, tq), :]
        s = lax.dot_general(qs_sc[...], k, _NT, preferred_element_type=_F32)
        if masked:
            row = lax.broadcasted_iota(jnp.int32, s.shape, 0)
            col = lax.broadcasted_iota(jnp.int32, s.shape, 1)
            rq = jnp.where(row >= tq, row - tq, row)
            s = jnp.where(col <= rq, s, -jnp.inf)
        m_prev = m_sc[...]
        m_new = jnp.maximum(m_prev, jnp.max(s, axis=1, keepdims=True))
        alpha = jnp.exp(m_prev - m_new)
        p = jnp.exp(s - m_new[:, :1])
        l_sc[...] = alpha * l_sc[...] + jnp.sum(p, axis=1, keepdims=True)
        acc_sc[...] = alpha * acc_sc[...] + jnp.dot(p.astype(_BF16), v, preferred_element_type=_F32)
        m_sc[...] = m_new

    def body(ki, carry):
        step(ki, False)
        return carry

    lax.fori_loop(0, qi, body, 0)
    step(qi, True)

    o = acc_sc[...] / l_sc[...]
    lam = _lambda(lq1_ref, lk1_ref, lq2_ref, lk2_ref)
    o_ref[...] = _diff_finalize(o[:tq], o[tq:], lam, wsub_ref[...]).astype(o_ref.dtype)


def _prompt_attention(q, kb, vb, lam_params, w_subln, batch, seq, tq):
    t = q.shape[0]
    nq = seq // tq
    lq1, lk1, lq2, lk2 = [p.reshape(1, ATT_HEAD_DIM) for p in lam_params]
    vec = pl.BlockSpec((1, ATT_HEAD_DIM), lambda b, h, i: (0, 0))
    kv_spec = pl.BlockSpec((seq, LANES), lambda b, h, i: (b, h))
    qo_spec = pl.BlockSpec((tq, LANES), lambda b, h, i: (b * nq + i, h))
    return pl.pallas_call(
        functools.partial(_attn_kernel, tq=tq),
        grid=(batch, N_ATT_HEADS, nq),
        in_specs=[qo_spec, kv_spec, kv_spec, vec, vec, vec, vec,
                  pl.BlockSpec((1, ATT_V_DIM), lambda b, h, i: (0, 0))],
        out_specs=qo_spec,
        out_shape=jax.ShapeDtypeStruct((t, D_ATTN), _BF16),
        scratch_shapes=[pltpu.VMEM((2 * tq, LANES), _BF16), pltpu.VMEM((2 * tq, LANES), _F32),
                        pltpu.VMEM((2 * tq, LANES), _F32), pltpu.VMEM((2 * tq, LANES), _F32)],
        compiler_params=_cparams(("arbitrary", "arbitrary", "arbitrary")),
        name="prompt_attn",
    )(q, kb, vb, lq1, lk1, lq2, lk2, w_subln.reshape(1, ATT_V_DIM))


def _ssd_kernel(xbc_ref, z_ref, dt_ref, dtt_ref, cprev_ref, h0_ref, convw_ref, convb_ref, dtb_row_ref, dtb_col_ref,
                alog_row_ref, alog_col_ref, dskip_ref, wnorm_ref, y_ref, hout_ref,
                xp_sc, st_sc, yg_sc, *, n_valid, precise):
    c = pl.program_id(1)
    L = SSM_CHUNK
    n_pairs = N_SSM_HEADS // 2
    hp = 2 * SSM_HEAD_DIM

    @pl.when(c == 0)
    def _():
        xp_sc[0:SUBLANES, :] = jnp.zeros((SUBLANES, CONV_DIM), _F32)
        xp_sc[SUBLANES - (CONV_WIDTH - 1):SUBLANES, :] = cprev_ref[...]
        st_sc[...] = h0_ref[...]

    xp_sc[SUBLANES:SUBLANES + L, :] = xbc_ref[...]
    base = SUBLANES - (CONV_WIDTH - 1)
    conv = convb_ref[...] + convw_ref[0:1, :] * xp_sc[base:base + L, :]
    for j in range(1, CONV_WIDTH):
        conv = conv + convw_ref[j:j + 1, :] * xp_sc[base + j:base + j + L, :]
    xp_sc[0:SUBLANES, :] = xp_sc[L:L + SUBLANES, :]
    xc = _silu(conv)
    xs = xc[:, :D_SSM]

    row_i = lax.broadcasted_iota(jnp.int32, (L, LANES), 0)
    col_i = lax.broadcasted_iota(jnp.int32, (L, LANES), 1)
    dt = _softplus(dt_ref[...] + dtb_row_ref[...])
    dtt = _softplus(dtt_ref[...] + dtb_col_ref[...])
    if n_valid < L:
        dt = jnp.where(row_i < n_valid, dt, 0.0)
        dtt = jnp.where(lax.broadcasted_iota(jnp.int32, dtt.shape, 1) < n_valid, dtt, 0.0)
    ad = dt * (-jnp.exp(alog_row_ref[...]))
    adt = dtt * (-jnp.exp(alog_col_ref[...]))
    tril = (col_i <= row_i).astype(_F32)
    triu = (row_i <= col_i).astype(_F32)
    acs = jnp.dot(tril, ad, preferred_element_type=_F32, precision=_HI)
    acst = jnp.dot(adt, triu, preferred_element_type=_F32, precision=_HI)
    causal = col_i <= row_i
    lane_lo = col_i < SSM_HEAD_DIM

    cb = []
    for g in range(SSM_GROUPS):
        bg = xc[:, D_SSM + g * SSM_STATE:D_SSM + (g + 1) * SSM_STATE]
        cg = xc[:, D_SSM + (SSM_GROUPS + g) * SSM_STATE:D_SSM + (SSM_GROUPS + g + 1) * SSM_STATE]
        cb.append((bg, cg, _dot_nt(cg, bg, precise)))

    for p in range(n_pairs):
        ha, hb = 2 * p, 2 * p + 1
        bg, cg, cbg = cb[ha // (N_SSM_HEADS // SSM_GROUPS)]
        col_a, col_b = acs[:, ha:ha + 1], acs[:, hb:hb + 1]
        lm_a = jnp.where(causal, jnp.exp(col_a - acst[ha:ha + 1, :]), 0.0)
        lm_b = jnp.where(causal, jnp.exp(col_b - acst[hb:hb + 1, :]), 0.0)
        dt_pair = jnp.where(lane_lo, dt[:, ha:ha + 1], dt[:, hb:hb + 1])
        x_pair = xs[:, p * hp:(p + 1) * hp] * dt_pair
        y_diag = jnp.where(lane_lo, _dot(cbg * lm_a, x_pair, precise), _dot(cbg * lm_b, x_pair, precise))
        st = st_sc[p]
        y_off = _dot_nt(cg, st, precise) * jnp.where(lane_lo, jnp.exp(col_a), jnp.exp(col_b))
        last_a, last_b = acs[L - 1:L, ha:ha + 1], acs[L - 1:L, hb:hb + 1]
        decay = jnp.where(lane_lo, jnp.exp(last_a - col_a), jnp.exp(last_b - col_b))
        upd = _dot((x_pair * decay).T, bg, precise)
        row_lo = row_i < SSM_HEAD_DIM
        st_sc[p] = jnp.where(row_lo, jnp.exp(last_a), jnp.exp(last_b)) * st + upd
        y_pair = y_diag + y_off + xs[:, p * hp:(p + 1) * hp] * dskip_ref[:, p * hp:(p + 1) * hp]
        zg = z_ref[:, p * hp:(p + 1) * hp].astype(_F32)
        yg_sc[:, p * hp:(p + 1) * hp] = y_pair * _silu(zg)

    gw = D_SSM // SSM_GROUPS
    for g in range(SSM_GROUPS):
        y_ref[:, g * gw:(g + 1) * gw] = _rms(yg_sc[:, g * gw:(g + 1) * gw],
                                             wnorm_ref[:, g * gw:(g + 1) * gw]).astype(y_ref.dtype)

    hout_ref[...] = st_sc[...]


def _ssd(xbc, z, dt, conv_prev, h0, conv_w, conv_b, dt_bias, a_log, d_skip, w_ssm_norm, batch, n_chunks,
         n_valid, precise, out_dtype):
    L = SSM_CHUNK
    rows = xbc.shape[0]
    n_pairs = N_SSM_HEADS // 2
    dtt = dt[:, :N_SSM_HEADS].T
    pad = LANES - N_SSM_HEADS
    dtb_row = jnp.pad(dt_bias, (0, pad)).reshape(1, LANES)
    alog_row = jnp.pad(a_log, (0, pad)).reshape(1, LANES)
    dskip = jnp.repeat(d_skip, SSM_HEAD_DIM).reshape(1, D_SSM)
    blk = lambda b, c: (b * n_chunks + c, 0)
    fixed = lambda b, c: (0, 0)
    y, h_out = pl.pallas_call(
        functools.partial(_ssd_kernel, n_valid=n_valid, precise=precise),
        grid=(batch, n_chunks),
        in_specs=[
            pl.BlockSpec((L, CONV_DIM), blk),
            pl.BlockSpec((L, D_SSM), blk),
            pl.BlockSpec((L, LANES), blk),
            pl.BlockSpec((N_SSM_HEADS, L), lambda b, c: (0, b * n_chunks + c)),
            pl.BlockSpec((None, CONV_WIDTH - 1, CONV_DIM), lambda b, c: (b, 0, 0)),
            pl.BlockSpec((None, n_pairs, 2 * SSM_HEAD_DIM, SSM_STATE), lambda b, c: (b, 0, 0, 0)),
            pl.BlockSpec((CONV_WIDTH, CONV_DIM), fixed),
            pl.BlockSpec((1, CONV_DIM), fixed),
            pl.BlockSpec((1, LANES), fixed),
            pl.BlockSpec((N_SSM_HEADS, 1), fixed),
            pl.BlockSpec((1, LANES), fixed),
            pl.BlockSpec((N_SSM_HEADS, 1), fixed),
            pl.BlockSpec((1, D_SSM), fixed),
            pl.BlockSpec((1, D_SSM), fixed),
        ],
        out_specs=[pl.BlockSpec((L, D_SSM), blk),
                   pl.BlockSpec((None, n_pairs, 2 * SSM_HEAD_DIM, SSM_STATE), lambda b, c: (b, 0, 0, 0))],
        out_shape=[jax.ShapeDtypeStruct((rows, D_SSM), out_dtype),
                   jax.ShapeDtypeStruct((batch, n_pairs, 2 * SSM_HEAD_DIM, SSM_STATE), _F32)],
        scratch_shapes=[pltpu.VMEM((SUBLANES + L, CONV_DIM), _F32),
                        pltpu.VMEM((n_pairs, 2 * SSM_HEAD_DIM, SSM_STATE), _F32),
                        pltpu.VMEM((L, D_SSM), _F32)],
        compiler_params=_cparams(("arbitrary", "arbitrary")),
        name="ssd_precise" if precise else "ssd",
    )(xbc, z, dt, dtt, conv_prev, h0.reshape(batch, n_pairs, 2 * SSM_HEAD_DIM, SSM_STATE),
      conv_w, conv_b.reshape(1, CONV_DIM), dtb_row, dt_bias.reshape(N_SSM_HEADS, 1),
      alog_row, a_log.reshape(N_SSM_HEADS, 1), dskip, w_ssm_norm.reshape(1, D_SSM))
    return y, h_out.reshape(batch, N_SSM_HEADS, SSM_HEAD_DIM, SSM_STATE)


def _outproj_router_kernel(x_ref, attn_ref, ssd_ref, wa_ref, wb_ref, wn_ref, wr_ref, br_ref,
                           h1_ref, t_ref, route_ref, *, precise):
    h1 = x_ref[...] + _dot(attn_ref[...], wa_ref[...], precise) + _dot(ssd_ref[...], wb_ref[...], precise)
    h1_ref[...] = h1
    t = _rms(h1, wn_ref[...])
    t_ref[...] = t
    logits = _dot(t, wr_ref[...], precise) + br_ref[...]
    lane = lax.broadcasted_iota(jnp.int32, logits.shape, 1).astype(_F32)
    neg = -jnp.inf
    big = float(LANES)
    is_g = lane < N_EXPERT_GROUPS
    gl = jnp.where(is_g, logits, neg)
    gmax = jnp.max(gl, axis=1, keepdims=True)
    g_idx = jnp.min(jnp.where(gl == gmax, lane, big), axis=1, keepdims=True)
    g_w = 1.0 / jnp.sum(jnp.where(is_g, jnp.exp(logits - gmax), 0.0), axis=1, keepdims=True)
    lo = N_EXPERT_GROUPS + EXPERTS_PER_GROUP * g_idx
    l1 = jnp.where(lane >= lo, jnp.where(lane < lo + EXPERTS_PER_GROUP, logits, neg), neg)
    m1 = jnp.max(l1, axis=1, keepdims=True)
    i1 = jnp.min(jnp.where(l1 == m1, lane, big), axis=1, keepdims=True)
    l2 = jnp.where(lane == i1, neg, l1)
    m2 = jnp.max(l2, axis=1, keepdims=True)
    i2 = jnp.min(jnp.where(l2 == m2, lane, big), axis=1, keepdims=True)
    r = jnp.exp(m2 - m1)
    p1 = 1.0 / (1.0 + r)
    w1 = g_w * p1
    w2 = g_w * (r * p1)
    e1 = i1 - N_EXPERT_GROUPS
    e2 = i2 - N_EXPERT_GROUPS
    route_ref[...] = jnp.where(lane == 0, e1, jnp.where(lane == 1, e2, jnp.where(lane == 2, w1,
                               jnp.where(lane == 3, w2, 0.0))))


def _outproj_router(x2d, attn, ssd, w_out, w_norm_ffn, w_router, b_router, tm, precise):
    t = x2d.shape[0]
    wdt = _F32 if precise else _BF16
    wa = w_out[:D_ATTN].astype(wdt)
    wb = w_out[D_ATTN:].astype(wdt)
    row = lambda i: (i, 0)
    fixed = lambda i: (0, 0)
    return pl.pallas_call(
        functools.partial(_outproj_router_kernel, precise=precise),
        grid=(t // tm,),
        in_specs=[pl.BlockSpec((tm, D_MODEL), row), pl.BlockSpec((tm, D_ATTN), row), pl.BlockSpec((tm, D_SSM), row),
                  pl.BlockSpec((D_ATTN, D_MODEL), fixed), pl.BlockSpec((D_SSM, D_MODEL), fixed),
                  pl.BlockSpec((1, D_MODEL), fixed), pl.BlockSpec((D_MODEL, LANES), fixed),
                  pl.BlockSpec((1, LANES), fixed)],
        out_specs=[pl.BlockSpec((tm, D_MODEL), row), pl.BlockSpec((tm, D_MODEL), row), pl.BlockSpec((tm, LANES), row)],
        out_shape=[jax.ShapeDtypeStruct((t, D_MODEL), _F32), jax.ShapeDtypeStruct((t, D_MODEL), _F32),
                   jax.ShapeDtypeStruct((t, LANES), _F32)],
        compiler_params=_cparams(("arbitrary",)),
        name="outproj_router_precise" if precise else "outproj_router",
    )(x2d, attn, ssd, wa, wb, w_norm_ffn.reshape(1, D_MODEL), w_router.astype(wdt), b_router)


def _expert_kernel(te_ref, nv_ref, idx_ref, t_hbm, wg_ref, wu_ref, wd_ref, y_hbm, xbuf, ybuf, sem_in, sem_out, *, tm):
    i = pl.program_id(0)
    nv = nv_ref[i]

    @pl.when(i == 0)
    def _():
        xbuf[...] = jnp.zeros(xbuf.shape, _F32)

    def row_in(r):
        return pltpu.make_async_copy(t_hbm.at[pl.ds(idx_ref[0, 0, r], 1), :], xbuf.at[pl.ds(r, 1), :], sem_in)

    def row_out(r):
        return pltpu.make_async_copy(ybuf.at[pl.ds(r, 1), :], y_hbm.at[pl.ds(idx_ref[0, 0, tm + r], 1), :], sem_out)

    def for_valid_rows(fn):
        def body(r, carry):
            fn(r)
            return carry
        lax.fori_loop(0, nv, body, 0)

    @pl.when(nv > 0)
    def _():
        for_valid_rows(lambda r: row_in(r).start())
        for_valid_rows(lambda r: row_in(r).wait())
        x = xbuf[...].astype(_BF16)
        hg = jnp.dot(x, wg_ref[...].astype(_BF16), preferred_element_type=_F32)
        hu = jnp.dot(x, wu_ref[...].astype(_BF16), preferred_element_type=_F32)
        hh = (_silu(hg) * hu).astype(_BF16)
        ybuf[...] = jnp.dot(hh, wd_ref[...].astype(_BF16), preferred_element_type=_F32)
        for_valid_rows(lambda r: row_out(r).start())
        for_valid_rows(lambda r: row_out(r).wait())


def _moe_plan(e_ids, tm, n_tiles):
    n = e_ids.shape[0]
    n_assign = 2 * n
    e_flat = e_ids.reshape(-1)
    order = jnp.argsort(e_flat, stable=True).astype(jnp.int32)
    counts = jnp.sum((e_flat[:, None] == jnp.arange(N_EXPERTS, dtype=jnp.int32)[None, :]).astype(jnp.int32), axis=0)
    tiles_per = (counts + tm - 1) // tm
    cum_tiles = jnp.cumsum(tiles_per)
    tile_start = cum_tiles - tiles_per
    seg_start = jnp.cumsum(counts) - counts
    n_used = cum_tiles[-1]
    tile_id = jnp.arange(n_tiles, dtype=jnp.int32)
    used = tile_id < n_used
    te = jnp.searchsorted(cum_tiles, jnp.minimum(tile_id, n_used - 1), side="right").astype(jnp.int32)
    j = tile_id - tile_start[te]
    nvalid = jnp.where(used, jnp.clip(counts[te] - j * tm, 0, tm), 0).astype(jnp.int32)
    pos = seg_start[te][:, None] + j[:, None] * tm + jnp.arange(tm, dtype=jnp.int32)[None, :]
    a = order[jnp.clip(pos, 0, n_assign - 1)]
    tok = a // 2
    dst = (a % 2) * n + tok
    idx = jnp.concatenate([tok, dst], axis=1).reshape(n_tiles, 1, 2 * tm).astype(jnp.int32)
    return te, nvalid, idx


def _expert_mlp(t_all, e_ids, w_gate, w_up, w_down):
    n = t_all.shape[0]
    tm = EXPERT_TILE
    n_tiles = pl.cdiv(2 * n, tm) + N_EXPERTS
    te, nvalid, idx = _moe_plan(e_ids, tm, n_tiles)
    wmap = lambda i, te_ref, nv_ref: (te_ref[i], 0, 0)
    grid_spec = pltpu.PrefetchScalarGridSpec(
        num_scalar_prefetch=2,
        grid=(n_tiles,),
        in_specs=[
            pl.BlockSpec((1, 1, 2 * tm), lambda i, te_ref, nv_ref: (i, 0, 0), memory_space=pltpu.SMEM),
            pl.BlockSpec(memory_space=pl.ANY),
            pl.BlockSpec((None, D_MODEL, D_FF_EXPERT), wmap),
            pl.BlockSpec((None, D_MODEL, D_FF_EXPERT), wmap),
            pl.BlockSpec((None, D_FF_EXPERT, D_MODEL), wmap),
        ],
        out_specs=pl.BlockSpec(memory_space=pl.ANY),
        scratch_shapes=[pltpu.VMEM((tm, D_MODEL), _F32), pltpu.VMEM((tm, D_MODEL), _F32),
                        pltpu.SemaphoreType.DMA(()), pltpu.SemaphoreType.DMA(())],
    )
    return pl.pallas_call(
        functools.partial(_expert_kernel, tm=tm),
        grid_spec=grid_spec,
        out_shape=jax.ShapeDtypeStruct((2 * n, D_MODEL), _F32),
        compiler_params=_cparams(("arbitrary",)),
        name="expert_mlp",
    )(te, nvalid, idx, t_all, w_gate, w_up, w_down)


def _combine_ple_kernel(h1_ref, y_ref, route_ref, p_ref, wn_ref, wg_ref, wp_ref, wf_ref, o_ref):
    route = route_ref[...]
    h2 = h1_ref[...] + route[:, 2:3] * y_ref[0] + route[:, 3:4] * y_ref[1]
    u = _rms(h2, wn_ref[...])
    gate = jnp.dot(u.astype(_BF16), wg_ref[...], preferred_element_type=_F32)
    gate = 1.0 / (1.0 + jnp.exp(-gate))
    pp = jnp.dot(p_ref[...].astype(_BF16), wp_ref[...], preferred_element_type=_F32)
    h3 = h2 + gate * pp
    o_ref[...] = _rms(h3, wf_ref[...])


def _combine_ple(h1, y_stk, route, p2d, w_norm_ple, w_ple_gate, w_ple_proj, w_norm_final, tm, row_off):
    t = h1.shape[0]
    off = row_off // tm
    row = lambda i: (i, 0)
    fixed = lambda i: (0, 0)
    return pl.pallas_call(
        _combine_ple_kernel,
        grid=(t // tm,),
        in_specs=[pl.BlockSpec((tm, D_MODEL), row),
                  pl.BlockSpec((2, tm, D_MODEL), lambda i: (0, i + off, 0)),
                  pl.BlockSpec((tm, LANES), row), pl.BlockSpec((tm, D_PLE), row),
                  pl.BlockSpec((1, D_MODEL), fixed), pl.BlockSpec((D_MODEL, D_MODEL), fixed),
                  pl.BlockSpec((D_PLE, D_MODEL), fixed), pl.BlockSpec((1, D_MODEL), fixed)],
        out_specs=pl.BlockSpec((tm, D_MODEL), row),
        out_shape=jax.ShapeDtypeStruct((t, D_MODEL), _F32),
        compiler_params=_cparams(("arbitrary",)),
        name="combine_ple",
    )(h1, y_stk, route, p2d, w_norm_ple.reshape(1, D_MODEL), w_ple_gate.astype(_BF16), w_ple_proj.astype(_BF16),
      w_norm_final.reshape(1, D_MODEL))


def _sample_inproj_kernel(x_ref, wn_ref, w_ref, o_ref):
    u = _rms(x_ref[...], wn_ref[...])
    o_ref[...] = jnp.dot(u, w_ref[...], preferred_element_type=_F32, precision=_HI)


def _sample_inproj(xs2d, w_norm, w_in, tn):
    rows = xs2d.shape[0]
    return pl.pallas_call(
        _sample_inproj_kernel,
        grid=(pl.cdiv(D_IN_PROJ, tn),),
        in_specs=[pl.BlockSpec((rows, D_MODEL), lambda j: (0, 0)), pl.BlockSpec((1, D_MODEL), lambda j: (0, 0)),
                  pl.BlockSpec((D_MODEL, tn), lambda j: (0, j))],
        out_specs=pl.BlockSpec((rows, tn), lambda j: (0, j)),
        out_shape=jax.ShapeDtypeStruct((rows, D_IN_PROJ), _F32),
        compiler_params=_cparams(("arbitrary",)),
        name="sample_inproj",
    )(xs2d, w_norm.reshape(1, D_MODEL), w_in)


def _split2(x):
    hi = x.astype(_BF16)
    lo = (x - hi.astype(_F32)).astype(_BF16)
    return hi, lo


def _split3(x):
    hi = x.astype(_BF16)
    r = x - hi.astype(_F32)
    mid = r.astype(_BF16)
    lo = (r - mid.astype(_F32)).astype(_BF16)
    return hi, mid, lo


def _decode_attn_kernel(pt_ref, qkv_ref, c_ref, s1_ref, s2_ref, lq1_ref, lk1_ref, lq2_ref, lk2_ref, wsub_ref, *rest,
                        n_pages_step, n_steps, n_real):
    k_refs = rest[:n_pages_step]
    v_refs = rest[n_pages_step:2 * n_pages_step]
    o_ref, knew_ref, vnew_ref = rest[2 * n_pages_step:2 * n_pages_step + 3]
    q_sc, m_sc, l_sc, acc_sc, kt_sc, vt_sc = rest[2 * n_pages_step + 3:]
    j = pl.program_id(1)
    R = SAMPLE_ROWS
    hr = 2 * R
    page = kt_sc.shape[0]

    @pl.when(j == 0)
    def _():
        c, s1, s2 = c_ref[...], s1_ref[...], s2_ref[...]
        lane = lax.broadcasted_iota(jnp.int32, (R, LANES), 1)
        kt_sc[...] = jnp.zeros(kt_sc.shape, _F32)
        vt_sc[...] = jnp.zeros(vt_sc.shape, _F32)
        for h in range(N_ATT_HEADS):
            sl = slice(h * LANES, (h + 1) * LANES)
            q = _rope_block(qkv_ref[:, sl], c, s1, s2) * (ATT_HEAD_DIM ** -0.5)
            k = _rope_block(qkv_ref[:, D_ATTN + h * LANES:D_ATTN + (h + 1) * LANES], c, s1, s2)
            v = qkv_ref[:, 2 * D_ATTN + h * LANES:2 * D_ATTN + (h + 1) * LANES]
            knew_ref[:, sl] = k
            vnew_ref[:, sl] = v
            kt_sc[0:R, sl] = k
            vt_sc[0:R, sl] = v
            q2 = jnp.concatenate([jnp.where(lane < ATT_HEAD_DIM, q, 0.0), jnp.where(lane >= ATT_HEAD_DIM, q, 0.0)], axis=0)
            hi, mid, lo = _split3(q2)
            q_sc[h] = jnp.concatenate([hi, mid, lo], axis=0)
        m_sc[...] = jnp.full(m_sc.shape, -jnp.inf, _F32)
        l_sc[...] = jnp.zeros(l_sc.shape, _F32)
        acc_sc[...] = jnp.zeros(acc_sc.shape, _F32)

    def process(k_tile, v_tile, mask):
        s_parts = []
        for h in range(N_ATT_HEADS):
            sl = slice(h * LANES, (h + 1) * LANES)
            k_hi, k_lo = _split2(k_tile[:, sl])
            q3 = q_sc[h]
            a = lax.dot_general(q3, k_hi, _NT, preferred_element_type=_F32)
            b = lax.dot_general(q3[0:2 * hr], k_lo, _NT, preferred_element_type=_F32)
            s_parts.append(a[0:hr] + a[hr:2 * hr] + a[2 * hr:3 * hr] + b[0:hr] + b[hr:2 * hr])
        s = jnp.concatenate(s_parts, axis=0)
        if mask is not None:
            s = jnp.where(mask, s, -jnp.inf)
        m_prev = m_sc[...]
        m_new = jnp.maximum(m_prev, jnp.max(s, axis=1, keepdims=True))
        alpha = jnp.exp(m_prev - m_new)
        p = jnp.exp(s - m_new[:, :1])
        l_sc[...] = alpha * l_sc[...] + jnp.sum(p, axis=1, keepdims=True)
        m_sc[...] = m_new
        pv_parts = []
        for h in range(N_ATT_HEADS):
            sl = slice(h * LANES, (h + 1) * LANES)
            v_hi, v_lo = _split2(v_tile[:, sl])
            p_hi, p_lo = _split2(p[h * hr:(h + 1) * hr])
            a = jnp.dot(jnp.concatenate([p_hi, p_lo], axis=0), v_hi, preferred_element_type=_F32)
            b = jnp.dot(p_hi, v_lo, preferred_element_type=_F32)
            pv_parts.append(a[0:hr] + a[hr:2 * hr] + b)
        acc_sc[...] = alpha * acc_sc[...] + jnp.concatenate(pv_parts, axis=0)

    for i in range(n_pages_step):
        process(k_refs[i], v_refs[i], None)

    @pl.when(j == n_steps - 1)
    def _():
        rows = N_ATT_HEADS * hr
        row = lax.broadcasted_iota(jnp.int32, (rows, page), 0)
        col = lax.broadcasted_iota(jnp.int32, (rows, page), 1)
        qrow = row % R
        process(kt_sc, vt_sc, (col <= qrow) & (col < n_real))
        o = acc_sc[...] / l_sc[...]
        lam = _lambda(lq1_ref, lk1_ref, lq2_ref, lk2_ref)
        wsub = wsub_ref[...]
        for h in range(N_ATT_HEADS):
            o1 = o[h * hr:h * hr + R]
            o2 = o[h * hr + R:(h + 1) * hr]
            o_ref[:, h * LANES:(h + 1) * LANES] = _diff_finalize(o1, o2, lam, wsub)


def _decode_attention(proj3, cache_k2, cache_v2, page_table, pos_rows, lam_params, w_subln, n_real):
    bsz = proj3.shape[0]
    page = cache_k2.shape[1]
    n_pages = page_table.shape[1]
    pg = PAGES_PER_STEP
    n_steps = n_pages // pg
    R = SAMPLE_ROWS
    c, s1, s2 = _rope_tables(pos_rows)
    lq1, lk1, lq2, lk2 = [p.reshape(1, ATT_HEAD_DIM) for p in lam_params]
    fixed = lambda b, j, pt: (0, 0)
    tab = pl.BlockSpec((R, LANES), fixed)
    vec = pl.BlockSpec((1, ATT_HEAD_DIM), fixed)

    def page_spec(i):
        return pl.BlockSpec((None, page, D_ATTN), lambda b, j, pt: (pt[b, j * pg + i], 0, 0))

    out_row = pl.BlockSpec((None, R, D_ATTN), lambda b, j, pt: (b, 0, 0))
    grid_spec = pltpu.PrefetchScalarGridSpec(
        num_scalar_prefetch=1,
        grid=(bsz, n_steps),
        in_specs=[pl.BlockSpec((None, R, 3 * D_ATTN), lambda b, j, pt: (b, 0, 0)), tab, tab, tab, vec, vec, vec, vec,
                  pl.BlockSpec((1, ATT_V_DIM), fixed)]
                 + [page_spec(i) for i in range(pg)] + [page_spec(i) for i in range(pg)],
        out_specs=[out_row, out_row, out_row],
        scratch_shapes=[pltpu.VMEM((N_ATT_HEADS, 6 * R, LANES), _BF16),
                        pltpu.VMEM((N_ATT_HEADS * 2 * R, LANES), _F32), pltpu.VMEM((N_ATT_HEADS * 2 * R, LANES), _F32),
                        pltpu.VMEM((N_ATT_HEADS * 2 * R, LANES), _F32),
                        pltpu.VMEM((page, D_ATTN), _F32), pltpu.VMEM((page, D_ATTN), _F32)],
    )
    return pl.pallas_call(
        functools.partial(_decode_attn_kernel, n_pages_step=pg, n_steps=n_steps, n_real=n_real),
        grid_spec=grid_spec,
        out_shape=[jax.ShapeDtypeStruct((bsz, R, D_ATTN), _F32)] * 3,
        compiler_params=_cparams(("arbitrary", "arbitrary")),
        name="decode_attn",
    )(page_table, proj3, c, s1, s2, lq1, lk1, lq2, lk2, w_subln.reshape(1, ATT_V_DIM),
      *([cache_k2] * pg), *([cache_v2] * pg))


def kernel(x_prompt, x_sample, cache_k, cache_v, state_conv, state_ssm, page_table, p_prompt, p_sample, w_norm_mix, w_in, lambda_q1, lambda_k1, lambda_q2, lambda_k2, w_subln, conv_w, conv_b, dt_bias, A_log, D_skip, w_ssm_norm, w_out, w_norm_ffn, w_group_router, b_group_router, w_expert_router, b_expert_router, w_exp_gate, w_exp_up, w_exp_down, w_norm_ple, w_ple_gate, w_ple_proj, w_norm_final):
    bp, lp, _ = x_prompt.shape
    bs, ls, _ = x_sample.shape
    n_phys, page = cache_k.shape[1], cache_k.shape[2]
    past = page_table.shape[1] * page
    tp = bp * lp
    R = SAMPLE_ROWS
    ts = bs * R
    lam_params = (lambda_q1[0], lambda_k1[0], lambda_q2[0], lambda_k2[0])
    pad_r = LANES - N_EXPERT_GROUPS - N_EXPERTS
    w_router = jnp.pad(jnp.concatenate([w_group_router[0], w_expert_router[0]], axis=1), ((0, 0), (0, pad_r)))
    b_router = jnp.pad(jnp.concatenate([b_group_router[0], b_expert_router[0]]), (0, pad_r)).reshape(1, LANES)

    xp2d = x_prompt.reshape(tp, D_MODEL)
    q, kf, kb, vf, vb, z, xbc, dt = _prompt_inproj(xp2d, w_norm_mix[0], w_in[0], lp, 512)
    attn_p = _prompt_attention(q, kb, vb, lam_params, w_subln[0], bp, lp, 256)
    ssd_p, ssm_p = _ssd(xbc, z, dt, jnp.zeros((bp, CONV_WIDTH - 1, CONV_DIM), _F32),
                        jnp.zeros((bp, N_SSM_HEADS, SSM_HEAD_DIM, SSM_STATE), _F32),
                        conv_w[0], conv_b[0], dt_bias[0], A_log[0], D_skip[0], w_ssm_norm[0],
                        bp, lp // SSM_CHUNK, SSM_CHUNK, False, _BF16)
    h1_p, t_p, route_p = _outproj_router(xp2d, attn_p, ssd_p, w_out[0], w_norm_ffn[0], w_router, b_router, 256, False)

    xs3 = jnp.pad(x_sample, ((0, 0), (0, R - ls), (0, 0)))
    xs2d = xs3.reshape(ts, D_MODEL)
    proj_s = _sample_inproj(xs2d, w_norm_mix[0], w_in[0], 512)
    proj3 = proj_s.reshape(bs, R, D_IN_PROJ)
    pos_rows = past + jnp.arange(R, dtype=jnp.int32)
    attn_s, k_s, v_s = _decode_attention(proj3, cache_k[0].reshape(n_phys, page, D_ATTN),
                                         cache_v[0].reshape(n_phys, page, D_ATTN), page_table, pos_rows,
                                         lam_params, w_subln[0], ls)
    off = 3 * D_ATTN
    pad_rows = ((0, 0), (0, SSM_CHUNK - R), (0, 0))
    z_s = jnp.pad(proj3[:, :, off:off + D_SSM], pad_rows).reshape(bs * SSM_CHUNK, D_SSM)
    xbc_s3 = proj3[:, :, off + D_SSM:off + D_SSM + CONV_DIM]
    xbc_s = jnp.pad(xbc_s3, pad_rows).reshape(bs * SSM_CHUNK, CONV_DIM)
    dt_s = jnp.pad(proj3[:, :, off + D_SSM + CONV_DIM:], ((0, 0), (0, SSM_CHUNK - R), (0, LANES - N_SSM_HEADS)))
    dt_s = dt_s.reshape(bs * SSM_CHUNK, LANES)
    ssd_s, ssm_s = _ssd(xbc_s, z_s, dt_s, state_conv[0], state_ssm[0], conv_w[0], conv_b[0], dt_bias[0], A_log[0],
                        D_skip[0], w_ssm_norm[0], bs, 1, ls, True, _F32)
    ssd_s = ssd_s.reshape(bs, SSM_CHUNK, D_SSM)[:, :R].reshape(ts, D_SSM)
    h1_s, t_s, route_s = _outproj_router(xs2d, attn_s.reshape(ts, D_ATTN), ssd_s, w_out[0], w_norm_ffn[0],
                                         w_router, b_router, ts, True)

    t_all = jnp.concatenate([t_p, t_s], axis=0)
    route_all = jnp.concatenate([route_p, route_s], axis=0)
    n_all = tp + ts
    e_ids = route_all[:, :2].astype(jnp.int32)
    y_rows = _expert_mlp(t_all, e_ids, w_exp_gate[0], w_exp_up[0], w_exp_down[0])
    y_stk = y_rows.reshape(2, n_all, D_MODEL)
    y_p = _combine_ple(h1_p, y_stk, route_p, p_prompt[0].reshape(tp, D_PLE), w_norm_ple[0], w_ple_gate[0],
                       w_ple_proj[0], w_norm_final, 256, 0)
    ps2d = jnp.pad(p_sample[0], ((0, 0), (0, R - ls), (0, 0))).reshape(ts, D_PLE)
    y_s = _combine_ple(h1_s, y_stk, route_s, ps2d, w_norm_ple[0], w_ple_gate[0], w_ple_proj[0], w_norm_final, ts, tp)

    y_prompt = y_p.reshape(bp, lp, D_MODEL)
    y_sample = y_s.reshape(bs, R, D_MODEL)[:, :ls]
    k_prompt = kf.reshape(1, bp, lp, N_ATT_HEADS, 2 * ATT_HEAD_DIM)
    v_prompt = vf.reshape(1, bp, lp, N_ATT_HEADS, ATT_V_DIM)
    conv_prompt = xbc.reshape(bp, lp, CONV_DIM)[:, lp - (CONV_WIDTH - 1):][None]
    ssm_prompt = ssm_p[None]
    k_sample = k_s[:, :ls].reshape(1, bs, ls, N_ATT_HEADS, 2 * ATT_HEAD_DIM)
    v_sample = v_s[:, :ls].reshape(1, bs, ls, N_ATT_HEADS, ATT_V_DIM)
    conv_sample = xbc_s3[:, ls - (CONV_WIDTH - 1):ls][None]
    ssm_sample = ssm_s[None]
    return (y_prompt, y_sample, k_prompt, v_prompt, conv_prompt, ssm_prompt, k_sample, v_sample, conv_sample, ssm_sample)
```

```python
import functools
import math

import jax
import jax.numpy as jnp
from jax import lax
from jax.experimental import pallas as pl
from jax.experimental.pallas import tpu as pltpu

D_MODEL = 2048
D_ATTN = 1024
D_SSM = 1024
ATT_HEAD_DIM = 64
N_ATT_HEADS = 8
ATT_V_DIM = 128
ROT_DIM = 16
ROPE_THETA = 500000.0
SSM_HEAD_DIM = 64
N_SSM_HEADS = 16
SSM_GROUPS = 2
SSM_STATE = 128
SSM_CHUNK = 128
CONV_WIDTH = 4
CONV_DIM = D_SSM + 2 * SSM_GROUPS * SSM_STATE
D_IN_PROJ = 3 * D_ATTN + D_SSM + CONV_DIM + N_SSM_HEADS
N_EXPERT_GROUPS = 4
EXPERTS_PER_GROUP = 8
N_EXPERTS = 32
D_FF_EXPERT = 512
D_PLE = 256
RMS_EPS = 1e-6
LAM_INIT = 0.8 - 0.6 * math.exp(-0.3 * 0)

LANES = 128
SUBLANES = 8
VMEM_LIMIT_BYTES = 56 * 2 ** 20

SAMPLE_ROWS = 8
EXPERT_TILE = 256
PAGES_PER_STEP = 8

_F32 = jnp.float32
_BF16 = jnp.bfloat16
_HI = lax.Precision.HIGHEST
_NT = (((1,), (1,)), ((), ()))


def _cparams(semantics):
    return pltpu.CompilerParams(dimension_semantics=semantics, vmem_limit_bytes=VMEM_LIMIT_BYTES)


def _rms(x, w):
    return x * lax.rsqrt(jnp.mean(x * x, axis=-1, keepdims=True) + RMS_EPS) * w


def _silu(x):
    return x * (1.0 / (1.0 + jnp.exp(-x)))


def _softplus(x):
    return jnp.maximum(x, 0.0) + jnp.log(1.0 + jnp.exp(-jnp.abs(x)))


def _dot(a, b, precise=False):
    if precise:
        return jnp.dot(a.astype(_F32), b.astype(_F32), preferred_element_type=_F32, precision=_HI)
    return jnp.dot(a.astype(_BF16), b.astype(_BF16), preferred_element_type=_F32)


def _dot_nt(a, b, precise=False):
    if precise:
        return lax.dot_general(a.astype(_F32), b.astype(_F32), _NT, preferred_element_type=_F32, precision=_HI)
    return lax.dot_general(a.astype(_BF16), b.astype(_BF16), _NT, preferred_element_type=_F32)


def _rope_block(y, c, s1, s2):
    return y * c + pltpu.roll(y, LANES - ROT_DIM // 2, 1) * s1 + pltpu.roll(y, ROT_DIM // 2, 1) * s2


def _rope_tables(pos):
    half = ROT_DIM // 2
    inv_freq = jnp.power(ROPE_THETA, -jnp.arange(half, dtype=_F32) * (2.0 / ROT_DIM))
    ang = pos.astype(_F32)[:, None] * inv_freq[None, :]
    cos, sin = jnp.cos(ang), jnp.sin(ang)
    n = pos.shape[0]
    ones = jnp.ones((n, ATT_HEAD_DIM - ROT_DIM), _F32)
    zeros = jnp.zeros((n, ATT_HEAD_DIM - ROT_DIM), _F32)
    zh = jnp.zeros((n, half), _F32)
    c = jnp.concatenate([cos, cos, ones], axis=1)
    s1 = jnp.concatenate([-sin, zh, zeros], axis=1)
    s2 = jnp.concatenate([zh, sin, zeros], axis=1)
    return tuple(jnp.concatenate([t, t], axis=1) for t in (c, s1, s2))


def _inproj_qk_kernel(x_ref, wn_ref, w_ref, c_ref, s1_ref, s2_ref, q_ref, kf_ref, kb_ref):
    u = _rms(x_ref[...], wn_ref[...]).astype(_BF16)
    y = jnp.dot(u, w_ref[...], preferred_element_type=_F32)
    c, s1, s2 = c_ref[...], s1_ref[...], s2_ref[...]
    for j in range(2 * N_ATT_HEADS):
        r = _rope_block(y[:, j * LANES:(j + 1) * LANES], c, s1, s2)
        if j < N_ATT_HEADS:
            q_ref[:, j * LANES:(j + 1) * LANES] = (r * (ATT_HEAD_DIM ** -0.5)).astype(_BF16)
        else:
            jj = j - N_ATT_HEADS
            kf_ref[pl.ds(jj, r.shape[0], stride=N_ATT_HEADS), :] = r
            kb_ref[:, jj * LANES:(jj + 1) * LANES] = r.astype(_BF16)


def _inproj_vz_kernel(x_ref, wn_ref, w_ref, vf_ref, vb_ref, z_ref):
    u = _rms(x_ref[...], wn_ref[...]).astype(_BF16)
    y = jnp.dot(u, w_ref[...], preferred_element_type=_F32)
    v = y[:, :D_ATTN]
    for h in range(N_ATT_HEADS):
        vf_ref[pl.ds(h, v.shape[0], stride=N_ATT_HEADS), :] = v[:, h * LANES:(h + 1) * LANES]
    vb_ref[...] = v.astype(_BF16)
    z_ref[...] = y[:, D_ATTN:].astype(_BF16)


def _inproj_xbc_kernel(x_ref, wn_ref, w_ref, xbc_ref, dt_ref):
    u = _rms(x_ref[...], wn_ref[...]).astype(_BF16)
    y = jnp.dot(u, w_ref[...], preferred_element_type=_F32)
    xbc_ref[...] = y[:, :CONV_DIM]
    dt_ref[...] = y[:, CONV_DIM:]


def _prompt_inproj(x2d, w_norm, w_in, seq, tm):
    t = x2d.shape[0]
    nblk = t // tm
    per_seq = seq // tm
    wn = w_norm.reshape(1, D_MODEL)
    wb = w_in.astype(_BF16)
    row = lambda i: (i, 0)
    fixed = lambda i: (0, 0)
    x_spec = pl.BlockSpec((tm, D_MODEL), row)
    wn_spec = pl.BlockSpec((1, D_MODEL), fixed)
    head_spec = pl.BlockSpec((tm * N_ATT_HEADS, LANES), row)

    c, s1, s2 = _rope_tables(jnp.arange(seq, dtype=jnp.int32))
    tab_spec = pl.BlockSpec((tm, LANES), lambda i: (i % per_seq, 0))
    q, kf, kb = pl.pallas_call(
        _inproj_qk_kernel,
        grid=(nblk,),
        in_specs=[x_spec, wn_spec, pl.BlockSpec((D_MODEL, 2 * D_ATTN), fixed), tab_spec, tab_spec, tab_spec],
        out_specs=[pl.BlockSpec((tm, D_ATTN), row), head_spec, pl.BlockSpec((tm, D_ATTN), row)],
        out_shape=[jax.ShapeDtypeStruct((t, D_ATTN), _BF16), jax.ShapeDtypeStruct((t * N_ATT_HEADS, LANES), _F32),
                   jax.ShapeDtypeStruct((t, D_ATTN), _BF16)],
        compiler_params=_cparams(("arbitrary",)),
        name="inproj_qk",
    )(x2d, wn, wb[:, :2 * D_ATTN], c, s1, s2)

    vf, vb, z = pl.pallas_call(
        _inproj_vz_kernel,
        grid=(nblk,),
        in_specs=[x_spec, wn_spec, pl.BlockSpec((D_MODEL, D_ATTN + D_SSM), fixed)],
        out_specs=[head_spec, pl.BlockSpec((tm, D_ATTN), row), pl.BlockSpec((tm, D_SSM), row)],
        out_shape=[jax.ShapeDtypeStruct((t * N_ATT_HEADS, LANES), _F32), jax.ShapeDtypeStruct((t, D_ATTN), _BF16),
                   jax.ShapeDtypeStruct((t, D_SSM), _BF16)],
        compiler_params=_cparams(("arbitrary",)),
        name="inproj_vz",
    )(x2d, wn, wb[:, 2 * D_ATTN:3 * D_ATTN + D_SSM])

    n_tail = CONV_DIM + LANES
    w_tail = jnp.pad(wb[:, 3 * D_ATTN + D_SSM:], ((0, 0), (0, n_tail - (CONV_DIM + N_SSM_HEADS))))
    xbc, dt = pl.pallas_call(
        _inproj_xbc_kernel,
        grid=(nblk,),
        in_specs=[x_spec, wn_spec, pl.BlockSpec((D_MODEL, n_tail), fixed)],
        out_specs=[pl.BlockSpec((tm, CONV_DIM), row), pl.BlockSpec((tm, LANES), row)],
        out_shape=[jax.ShapeDtypeStruct((t, CONV_DIM), _F32), jax.ShapeDtypeStruct((t, LANES), _F32)],
        compiler_params=_cparams(("arbitrary",)),
        name="inproj_xbc",
    )(x2d, wn, w_tail)
    return q, kf, kb, vf, vb, z, xbc, dt


def _lambda(lq1_ref, lk1_ref, lq2_ref, lk2_ref):
    a = jnp.sum(lq1_ref[...] * lk1_ref[...], axis=-1, keepdims=True)
    b = jnp.sum(lq2_ref[...] * lk2_ref[...], axis=-1, keepdims=True)
    return jnp.exp(a) - jnp.exp(b) + LAM_INIT


def _diff_finalize(o1, o2, lam, wsub):
    a = o1 - lam * o2
    return _rms(a, wsub) * (1.0 - LAM_INIT)


def _attn_kernel(q_ref, k_ref, v_ref, lq1_ref, lk1_ref, lq2_ref, lk2_ref, wsub_ref, o_ref,
                 qt_sc, vt_sc, m_sc, l_sc, acc_sc, *, tq, tk):
    qi = pl.program_id(2)
    n_kt = vt_sc.shape[0]

    @pl.when(qi == 0)
    def _():
        for c in range(n_kt):
            vt_sc[c] = v_ref[c * tk:(c + 1) * tk, :].astype(_F32).T.astype(_BF16)

    q = q_ref[...].astype(_F32)
    lane = lax.broadcasted_iota(jnp.int32, (tq, LANES), 1)
    qs = jnp.concatenate([jnp.where(lane < ATT_HEAD_DIM, q, 0.0), jnp.where(lane >= ATT_HEAD_DIM, q, 0.0)], axis=0)
    qt_sc[...] = qs.T.astype(_BF16)
    m_sc[...] = jnp.full(m_sc.shape, -jnp.inf, _F32)
    l_sc[...] = jnp.zeros(l_sc.shape, _F32)
    acc_sc[...] = jnp.zeros(acc_sc.shape, _F32)

    def step(kt, masked):
        start = pl.multiple_of(kt * tk, tk)
        k = k_ref[pl.ds(start, tk), :]
        st = jnp.dot(k, qt_sc[...], preferred_element_type=_F32)
        if masked:
            key = start + lax.broadcasted_iota(jnp.int32, st.shape, 0)
            col = lax.broadcasted_iota(jnp.int32, st.shape, 1)
            qpos = qi * tq + jnp.where(col >= tq, col - tq, col)
            st = jnp.where(key <= qpos, st, -jnp.inf)
        m_prev = m_sc[...]
        m_new = jnp.maximum(m_prev, jnp.max(st, axis=0, keepdims=True))
        alpha = jnp.exp(m_prev - m_new)
        pt = jnp.exp(st - m_new[0:1, :])
        l_sc[...] = alpha * l_sc[...] + jnp.sum(pt, axis=0, keepdims=True)
        acc_sc[...] = alpha[0:1, :] * acc_sc[...] + jnp.dot(vt_sc[kt], pt.astype(_BF16),
                                                            preferred_element_type=_F32)
        m_sc[...] = m_new

    def body(kt, carry):
        step(kt, False)
        return carry

    n_full = (qi * tq) // tk
    lax.fori_loop(0, n_full, body, 0)
    step(n_full, True)

    o = (acc_sc[...] / l_sc[0:1, :]).T
    lam = _lambda(lq1_ref, lk1_ref, lq2_ref, lk2_ref)
    o_ref[...] = _diff_finalize(o[:tq], o[tq:], lam, wsub_ref[...]).astype(o_ref.dtype)


def _prompt_attention(q, kb, vb, lam_params, w_subln, batch, seq, tq, tk):
    t = q.shape[0]
    nq = seq // tq
    lq1, lk1, lq2, lk2 = [p.reshape(1, ATT_HEAD_DIM) for p in lam_params]
    vec = pl.BlockSpec((1, ATT_HEAD_DIM), lambda b, h, i: (0, 0))
    kv_spec = pl.BlockSpec((seq, LANES), lambda b, h, i: (b, h))
    qo_spec = pl.BlockSpec((tq, LANES), lambda b, h, i: (b * nq + i, h))
    return pl.pallas_call(
        functools.partial(_attn_kernel, tq=tq, tk=tk),
        grid=(batch, N_ATT_HEADS, nq),
        in_specs=[qo_spec, kv_spec, kv_spec, vec, vec, vec, vec,
                  pl.BlockSpec((1, ATT_V_DIM), lambda b, h, i: (0, 0))],
        out_specs=qo_spec,
        out_shape=jax.ShapeDtypeStruct((t, D_ATTN), _BF16),
        scratch_shapes=[pltpu.VMEM((LANES, 2 * tq), _BF16), pltpu.VMEM((seq // tk, ATT_V_DIM, tk), _BF16),
                        pltpu.VMEM((SUBLANES, 2 * tq), _F32), pltpu.VMEM((SUBLANES, 2 * tq), _F32),
                        pltpu.VMEM((ATT_V_DIM, 2 * tq), _F32)],
        compiler_params=_cparams(("arbitrary", "arbitrary", "arbitrary")),
        name="prompt_attn",
    )(q, kb, vb, lq1, lk1, lq2, lk2, w_subln.reshape(1, ATT_V_DIM))


def _ssd_kernel(xbc_ref, z_ref, dt_ref, dtt_ref, cprev_ref, h0_ref, convw_ref, convb_ref, dtb_row_ref, dtb_col_ref,
                alog_row_ref, alog_col_ref, dskip_ref, wnorm_ref, y_ref, hout_ref,
                xp_sc, st_sc, yg_sc, *, n_valid, precise):
    c = pl.program_id(1)
    L = SSM_CHUNK
    n_pairs = N_SSM_HEADS // 2
    hp = 2 * SSM_HEAD_DIM

    @pl.when(c == 0)
    def _():
        xp_sc[0:SUBLANES, :] = jnp.zeros((SUBLANES, CONV_DIM), _F32)
        xp_sc[SUBLANES - (CONV_WIDTH - 1):SUBLANES, :] = cprev_ref[...]
        st_sc[...] = h0_ref[...]

    xp_sc[SUBLANES:SUBLANES + L, :] = xbc_ref[...]
    base = SUBLANES - (CONV_WIDTH - 1)
    conv = convb_ref[...] + convw_ref[0:1, :] * xp_sc[base:base + L, :]
    for j in range(1, CONV_WIDTH):
        conv = conv + convw_ref[j:j + 1, :] * xp_sc[base + j:base + j + L, :]
    xp_sc[0:SUBLANES, :] = xp_sc[L:L + SUBLANES, :]
    xc = _silu(conv)
    xs = xc[:, :D_SSM]

    row_i = lax.broadcasted_iota(jnp.int32, (L, LANES), 0)
    col_i = lax.broadcasted_iota(jnp.int32, (L, LANES), 1)
    dt = _softplus(dt_ref[...] + dtb_row_ref[...])
    dtt = _softplus(dtt_ref[...] + dtb_col_ref[...])
    if n_valid < L:
        dt = jnp.where(row_i < n_valid, dt, 0.0)
        dtt = jnp.where(lax.broadcasted_iota(jnp.int32, dtt.shape, 1) < n_valid, dtt, 0.0)
    ad = dt * (-jnp.exp(alog_row_ref[...]))
    adt = dtt * (-jnp.exp(alog_col_ref[...]))
    tril = (col_i <= row_i).astype(_F32)
    triu = (row_i <= col_i).astype(_F32)
    acs = jnp.dot(tril, ad, preferred_element_type=_F32, precision=_HI)
    acst = jnp.dot(adt, triu, preferred_element_type=_F32, precision=_HI)
    causal = col_i <= row_i
    lane_lo = col_i < SSM_HEAD_DIM

    cb = []
    for g in range(SSM_GROUPS):
        bg = xc[:, D_SSM + g * SSM_STATE:D_SSM + (g + 1) * SSM_STATE]
        cg = xc[:, D_SSM + (SSM_GROUPS + g) * SSM_STATE:D_SSM + (SSM_GROUPS + g + 1) * SSM_STATE]
        cb.append((bg, cg, _dot_nt(cg, bg, precise)))

    for p in range(n_pairs):
        ha, hb = 2 * p, 2 * p + 1
        bg, cg, cbg = cb[ha // (N_SSM_HEADS // SSM_GROUPS)]
        col_a, col_b = acs[:, ha:ha + 1], acs[:, hb:hb + 1]
        lm_a = jnp.where(causal, jnp.exp(col_a - acst[ha:ha + 1, :]), 0.0)
        lm_b = jnp.where(causal, jnp.exp(col_b - acst[hb:hb + 1, :]), 0.0)
        dt_pair = jnp.where(lane_lo, dt[:, ha:ha + 1], dt[:, hb:hb + 1])
        x_pair = xs[:, p * hp:(p + 1) * hp] * dt_pair
        y_diag = jnp.where(lane_lo, _dot(cbg * lm_a, x_pair, precise), _dot(cbg * lm_b, x_pair, precise))
        st = st_sc[p]
        y_off = _dot_nt(cg, st, precise) * jnp.where(lane_lo, jnp.exp(col_a), jnp.exp(col_b))
        last_a, last_b = acs[L - 1:L, ha:ha + 1], acs[L - 1:L, hb:hb + 1]
        decay = jnp.where(lane_lo, jnp.exp(last_a - col_a), jnp.exp(last_b - col_b))
        upd = _dot((x_pair * decay).T, bg, precise)
        row_lo = row_i < SSM_HEAD_DIM
        st_sc[p] = jnp.where(row_lo, jnp.exp(last_a), jnp.exp(last_b)) * st + upd
        y_pair = y_diag + y_off + xs[:, p * hp:(p + 1) * hp] * dskip_ref[:, p * hp:(p + 1) * hp]
        zg = z_ref[:, p * hp:(p + 1) * hp].astype(_F32)
        yg_sc[:, p * hp:(p + 1) * hp] = y_pair * _silu(zg)

    gw = D_SSM // SSM_GROUPS
    for g in range(SSM_GROUPS):
        y_ref[:, g * gw:(g + 1) * gw] = _rms(yg_sc[:, g * gw:(g + 1) * gw],
                                             wnorm_ref[:, g * gw:(g + 1) * gw]).astype(y_ref.dtype)

    hout_ref[...] = st_sc[...]


def _ssd(xbc, z, dt, conv_prev, h0, conv_w, conv_b, dt_bias, a_log, d_skip, w_ssm_norm, batch, n_chunks,
         n_valid, precise, out_dtype):
    L = SSM_CHUNK
    rows = xbc.shape[0]
    n_pairs = N_SSM_HEADS // 2
    dtt = dt[:, :N_SSM_HEADS].T
    pad = LANES - N_SSM_HEADS
    dtb_row = jnp.pad(dt_bias, (0, pad)).reshape(1, LANES)
    alog_row = jnp.pad(a_log, (0, pad)).reshape(1, LANES)
    dskip = jnp.repeat(d_skip, SSM_HEAD_DIM).reshape(1, D_SSM)
    blk = lambda b, c: (b * n_chunks + c, 0)
    fixed = lambda b, c: (0, 0)
    y, h_out = pl.pallas_call(
        functools.partial(_ssd_kernel, n_valid=n_valid, precise=precise),
        grid=(batch, n_chunks),
        in_specs=[
            pl.BlockSpec((L, CONV_DIM), blk),
            pl.BlockSpec((L, D_SSM), blk),
            pl.BlockSpec((L, LANES), blk),
            pl.BlockSpec((N_SSM_HEADS, L), lambda b, c: (0, b * n_chunks + c)),
            pl.BlockSpec((None, CONV_WIDTH - 1, CONV_DIM), lambda b, c: (b, 0, 0)),
            pl.BlockSpec((None, n_pairs, 2 * SSM_HEAD_DIM, SSM_STATE), lambda b, c: (b, 0, 0, 0)),
            pl.BlockSpec((CONV_WIDTH, CONV_DIM), fixed),
            pl.BlockSpec((1, CONV_DIM), fixed),
            pl.BlockSpec((1, LANES), fixed),
            pl.BlockSpec((N_SSM_HEADS, 1), fixed),
            pl.BlockSpec((1, LANES), fixed),
            pl.BlockSpec((N_SSM_HEADS, 1), fixed),
            pl.BlockSpec((1, D_SSM), fixed),
            pl.BlockSpec((1, D_SSM), fixed),
        ],
        out_specs=[pl.BlockSpec((L, D_SSM), blk),
                   pl.BlockSpec((None, n_pairs, 2 * SSM_HEAD_DIM, SSM_STATE), lambda b, c: (b, 0, 0, 0))],
        out_shape=[jax.ShapeDtypeStruct((rows, D_SSM), out_dtype),
                   jax.ShapeDtypeStruct((batch, n_pairs, 2 * SSM_HEAD_DIM, SSM_STATE), _F32)],
        scratch_shapes=[pltpu.VMEM((SUBLANES + L, CONV_DIM), _F32),
                        pltpu.VMEM((n_pairs, 2 * SSM_HEAD_DIM, SSM_STATE), _F32),
                        pltpu.VMEM((L, D_SSM), _F32)],
        compiler_params=_cparams(("arbitrary", "arbitrary")),
        name="ssd_precise" if precise else "ssd",
    )(xbc, z, dt, dtt, conv_prev, h0.reshape(batch, n_pairs, 2 * SSM_HEAD_DIM, SSM_STATE),
      conv_w, conv_b.reshape(1, CONV_DIM), dtb_row, dt_bias.reshape(N_SSM_HEADS, 1),
      alog_row, a_log.reshape(N_SSM_HEADS, 1), dskip, w_ssm_norm.reshape(1, D_SSM))
    return y, h_out.reshape(batch, N_SSM_HEADS, SSM_HEAD_DIM, SSM_STATE)


def _outproj_router_kernel(x_ref, attn_ref, ssd_ref, wa_ref, wb_ref, wn_ref, wr_ref, br_ref,
                           h1_ref, t_ref, route_ref, *, precise):
    h1 = x_ref[...] + _dot(attn_ref[...], wa_ref[...], precise) + _dot(ssd_ref[...], wb_ref[...], precise)
    h1_ref[...] = h1
    t = _rms(h1, wn_ref[...])
    t_ref[...] = t
    logits = _dot(t, wr_ref[...], precise) + br_ref[...]
    lane = lax.broadcasted_iota(jnp.int32, logits.shape, 1).astype(_F32)
    neg = -jnp.inf
    big = float(LANES)
    is_g = lane < N_EXPERT_GROUPS
    gl = jnp.where(is_g, logits, neg)
    gmax = jnp.max(gl, axis=1, keepdims=True)
    g_idx = jnp.min(jnp.where(gl == gmax, lane, big), axis=1, keepdims=True)
    g_w = 1.0 / jnp.sum(jnp.where(is_g, jnp.exp(logits - gmax), 0.0), axis=1, keepdims=True)
    lo = N_EXPERT_GROUPS + EXPERTS_PER_GROUP * g_idx
    l1 = jnp.where(lane >= lo, jnp.where(lane < lo + EXPERTS_PER_GROUP, logits, neg), neg)
    m1 = jnp.max(l1, axis=1, keepdims=True)
    i1 = jnp.min(jnp.where(l1 == m1, lane, big), axis=1, keepdims=True)
    l2 = jnp.where(lane == i1, neg, l1)
    m2 = jnp.max(l2, axis=1, keepdims=True)
    i2 = jnp.min(jnp.where(l2 == m2, lane, big), axis=1, keepdims=True)
    r = jnp.exp(m2 - m1)
    p1 = 1.0 / (1.0 + r)
    w1 = g_w * p1
    w2 = g_w * (r * p1)
    e1 = i1 - N_EXPERT_GROUPS
    e2 = i2 - N_EXPERT_GROUPS
    route_ref[...] = jnp.where(lane == 0, e1, jnp.where(lane == 1, e2, jnp.where(lane == 2, w1,
                               jnp.where(lane == 3, w2, 0.0))))


def _outproj_router(x2d, attn, ssd, w_out, w_norm_ffn, w_router, b_router, tm, precise):
    t = x2d.shape[0]
    wdt = _F32 if precise else _BF16
    wa = w_out[:D_ATTN].astype(wdt)
    wb = w_out[D_ATTN:].astype(wdt)
    row = lambda i: (i, 0)
    fixed = lambda i: (0, 0)
    return pl.pallas_call(
        functools.partial(_outproj_router_kernel, precise=precise),
        grid=(t // tm,),
        in_specs=[pl.BlockSpec((tm, D_MODEL), row), pl.BlockSpec((tm, D_ATTN), row), pl.BlockSpec((tm, D_SSM), row),
                  pl.BlockSpec((D_ATTN, D_MODEL), fixed), pl.BlockSpec((D_SSM, D_MODEL), fixed),
                  pl.BlockSpec((1, D_MODEL), fixed), pl.BlockSpec((D_MODEL, LANES), fixed),
                  pl.BlockSpec((1, LANES), fixed)],
        out_specs=[pl.BlockSpec((tm, D_MODEL), row), pl.BlockSpec((tm, D_MODEL), row), pl.BlockSpec((tm, LANES), row)],
        out_shape=[jax.ShapeDtypeStruct((t, D_MODEL), _F32), jax.ShapeDtypeStruct((t, D_MODEL), _F32),
                   jax.ShapeDtypeStruct((t, LANES), _F32)],
        compiler_params=_cparams(("arbitrary",)),
        name="outproj_router_precise" if precise else "outproj_router",
    )(x2d, attn, ssd, wa, wb, w_norm_ffn.reshape(1, D_MODEL), w_router.astype(wdt), b_router)


def _expert_kernel(te_ref, nv_ref, idx_ref, idx_next_ref, t_hbm, wg_ref, wu_ref, wd_ref, y_hbm,
                   xbuf, ybuf, sem_in, sem_out, *, tm, n_tiles):
    i = pl.program_id(0)
    nv = nv_ref[i]
    nv_next = jnp.where(i + 1 < n_tiles, nv_ref[jnp.minimum(i + 1, n_tiles - 1)], 0)
    slot = lax.rem(i, 2)

    def start_gather(idx, s):
        for r in range(tm):
            pltpu.make_async_copy(t_hbm.at[pl.ds(idx[0, 0, r], 1), :], xbuf.at[s, pl.ds(r, 1), :], sem_in.at[s]).start()

    def wait_gather(s):
        pltpu.make_async_copy(t_hbm.at[pl.ds(0, tm), :], xbuf.at[s], sem_in.at[s]).wait()

    def start_scatter():
        for r in range(tm):
            pltpu.make_async_copy(ybuf.at[pl.ds(r, 1), :], y_hbm.at[pl.ds(idx_ref[0, 0, tm + r], 1), :], sem_out).start()

    def wait_scatter():
        pltpu.make_async_copy(ybuf, y_hbm.at[pl.ds(0, tm), :], sem_out).wait()

    @pl.when(i == 0)
    def _():
        start_gather(idx_ref, 0)
        half = tm // 2
        n_pad = y_hbm.shape[0] // 2
        ybuf[0:half, :] = jnp.zeros((half, D_MODEL), _F32)
        for k in range(2):
            fill = pltpu.make_async_copy(ybuf.at[pl.ds(0, half), :],
                                         y_hbm.at[pl.ds((k + 1) * n_pad - half, half), :], sem_out)
            fill.start()
            fill.wait()

    @pl.when(nv_next > 0)
    def _():
        start_gather(idx_next_ref, 1 - slot)

    @pl.when(nv > 0)
    def _():
        wait_gather(slot)
        x = xbuf[slot].astype(_BF16)
        hg = jnp.dot(x, wg_ref[...].astype(_BF16), preferred_element_type=_F32)
        hu = jnp.dot(x, wu_ref[...].astype(_BF16), preferred_element_type=_F32)
        hh = (_silu(hg) * hu).astype(_BF16)
        y = jnp.dot(hh, wd_ref[...].astype(_BF16), preferred_element_type=_F32)

        @pl.when(i > 0)
        def _():
            wait_scatter()

        ybuf[...] = y
        start_scatter()

        @pl.when(nv_next == 0)
        def _():
            wait_scatter()


def _moe_plan(e_ids, tm, n_tiles):
    n = e_ids.shape[0]
    n_assign = 2 * n
    half = tm // 2
    n_pad = n + half
    e_flat = e_ids.reshape(-1)
    onehot = (e_flat[:, None] == jnp.arange(N_EXPERTS, dtype=jnp.int32)[None, :]).astype(jnp.int32)
    csum = jnp.cumsum(onehot, axis=0)
    counts = csum[-1]
    seg_start = jnp.cumsum(counts) - counts
    rank = jnp.sum(onehot * csum, axis=1) - 1
    sorted_pos = seg_start[e_flat] + rank
    order = jnp.zeros((n_assign,), jnp.int32).at[sorted_pos].set(jnp.arange(n_assign, dtype=jnp.int32))
    tiles_per = (counts + tm - 1) // tm
    cum_tiles = jnp.cumsum(tiles_per)
    tile_start = cum_tiles - tiles_per
    n_used = cum_tiles[-1]
    tile_id = jnp.arange(n_tiles, dtype=jnp.int32)
    used = tile_id < n_used
    te = jnp.sum((cum_tiles[None, :] <= jnp.minimum(tile_id, n_used - 1)[:, None]).astype(jnp.int32), axis=1)
    j = tile_id - tile_start[te]
    nvalid = jnp.where(used, jnp.clip(counts[te] - j * tm, 0, tm), 0).astype(jnp.int32)
    r = jnp.arange(tm, dtype=jnp.int32)[None, :]
    pos = seg_start[te][:, None] + j[:, None] * tm + r
    a = order[jnp.clip(pos, 0, n_assign - 1)]
    tok = a // 2
    spare = (r // half) * n_pad + n + (r % half)
    dst = jnp.where(r < nvalid[:, None], (a % 2) * n_pad + tok, spare)
    idx = jnp.concatenate([tok, dst], axis=1).reshape(n_tiles, 1, 2 * tm).astype(jnp.int32)
    return te, nvalid, idx, n_pad


def _expert_mlp(t_all, e_ids, w_gate, w_up, w_down):
    n = t_all.shape[0]
    tm = EXPERT_TILE
    n_tiles = pl.cdiv(2 * n, tm) + N_EXPERTS
    te, nvalid, idx, n_pad = _moe_plan(e_ids, tm, n_tiles)
    wmap = lambda i, te_ref, nv_ref: (te_ref[i], 0, 0)
    grid_spec = pltpu.PrefetchScalarGridSpec(
        num_scalar_prefetch=2,
        grid=(n_tiles,),
        in_specs=[
            pl.BlockSpec((1, 1, 2 * tm), lambda i, te_ref, nv_ref: (i, 0, 0), memory_space=pltpu.SMEM),
            pl.BlockSpec((1, 1, 2 * tm), lambda i, te_ref, nv_ref: (jnp.minimum(i + 1, n_tiles - 1), 0, 0),
                         memory_space=pltpu.SMEM),
            pl.BlockSpec(memory_space=pl.ANY),
            pl.BlockSpec((None, D_MODEL, D_FF_EXPERT), wmap),
            pl.BlockSpec((None, D_MODEL, D_FF_EXPERT), wmap),
            pl.BlockSpec((None, D_FF_EXPERT, D_MODEL), wmap),
        ],
        out_specs=pl.BlockSpec(memory_space=pl.ANY),
        scratch_shapes=[pltpu.VMEM((2, tm, D_MODEL), _F32), pltpu.VMEM((tm, D_MODEL), _F32),
                        pltpu.SemaphoreType.DMA((2,)), pltpu.SemaphoreType.DMA(())],
    )
    y_rows = pl.pallas_call(
        functools.partial(_expert_kernel, tm=tm, n_tiles=n_tiles),
        grid_spec=grid_spec,
        out_shape=jax.ShapeDtypeStruct((2 * n_pad, D_MODEL), _F32),
        compiler_params=_cparams(("arbitrary",)),
        name="expert_mlp",
    )(te, nvalid, idx, idx, t_all, w_gate, w_up, w_down)
    return y_rows.reshape(2, n_pad, D_MODEL)


def _combine_ple_kernel(h1_ref, y_ref, route_ref, p_ref, wn_ref, wg_ref, wp_ref, wf_ref, o_ref):
    route = route_ref[...]
    h2 = h1_ref[...] + route[:, 2:3] * y_ref[0] + route[:, 3:4] * y_ref[1]
    u = _rms(h2, wn_ref[...])
    gate = jnp.dot(u.astype(_BF16), wg_ref[...], preferred_element_type=_F32)
    gate = 1.0 / (1.0 + jnp.exp(-gate))
    pp = jnp.dot(p_ref[...].astype(_BF16), wp_ref[...], preferred_element_type=_F32)
    h3 = h2 + gate * pp
    o_ref[...] = _rms(h3, wf_ref[...])


def _combine_ple(h1, y_stk, route, p2d, w_norm_ple, w_ple_gate, w_ple_proj, w_norm_final, tm, row_off):
    t = h1.shape[0]
    off = row_off // tm
    row = lambda i: (i, 0)
    fixed = lambda i: (0, 0)
    return pl.pallas_call(
        _combine_ple_kernel,
        grid=(t // tm,),
        in_specs=[pl.BlockSpec((tm, D_MODEL), row),
                  pl.BlockSpec((2, tm, D_MODEL), lambda i: (0, i + off, 0)),
                  pl.BlockSpec((tm, LANES), row), pl.BlockSpec((tm, D_PLE), row),
                  pl.BlockSpec((1, D_MODEL), fixed), pl.BlockSpec((D_MODEL, D_MODEL), fixed),
                  pl.BlockSpec((D_PLE, D_MODEL), fixed), pl.BlockSpec((1, D_MODEL), fixed)],
        out_specs=pl.BlockSpec((tm, D_MODEL), row),
        out_shape=jax.ShapeDtypeStruct((t, D_MODEL), _F32),
        compiler_params=_cparams(("arbitrary",)),
        name="combine_ple",
    )(h1, y_stk, route, p2d, w_norm_ple.reshape(1, D_MODEL), w_ple_gate.astype(_BF16), w_ple_proj.astype(_BF16),
      w_norm_final.reshape(1, D_MODEL))


def _sample_inproj_kernel(x_ref, wn_ref, w_ref, o_ref):
    u = _rms(x_ref[...], wn_ref[...])
    o_ref[...] = jnp.dot(u, w_ref[...], preferred_element_type=_F32, precision=_HI)


def _sample_inproj(xs2d, w_norm, w_in, tn):
    rows = xs2d.shape[0]
    return pl.pallas_call(
        _sample_inproj_kernel,
        grid=(pl.cdiv(D_IN_PROJ, tn),),
        in_specs=[pl.BlockSpec((rows, D_MODEL), lambda j: (0, 0)), pl.BlockSpec((1, D_MODEL), lambda j: (0, 0)),
                  pl.BlockSpec((D_MODEL, tn), lambda j: (0, j))],
        out_specs=pl.BlockSpec((rows, tn), lambda j: (0, j)),
        out_shape=jax.ShapeDtypeStruct((rows, D_IN_PROJ), _F32),
        compiler_params=_cparams(("arbitrary",)),
        name="sample_inproj",
    )(xs2d, w_norm.reshape(1, D_MODEL), w_in)


def _split2(x):
    hi = x.astype(_BF16)
    lo = (x - hi.astype(_F32)).astype(_BF16)
    return hi, lo


def _decode_attn_kernel(pt_ref, qkv_ref, c_ref, s1_ref, s2_ref, lq1_ref, lk1_ref, lq2_ref, lk2_ref, wsub_ref, *rest,
                        n_pages_step, n_steps, n_real):
    k_refs = rest[:n_pages_step]
    v_refs = rest[n_pages_step:2 * n_pages_step]
    o_ref, knew_ref, vnew_ref = rest[2 * n_pages_step:2 * n_pages_step + 3]
    q_sc, m_sc, l_sc, acc_sc, kt_sc, vt_sc = rest[2 * n_pages_step + 3:]
    j = pl.program_id(1)
    R = SAMPLE_ROWS
    hr = 2 * R
    page = kt_sc.shape[0] // N_ATT_HEADS

    @pl.when(j == 0)
    def _():
        c, s1, s2 = c_ref[...], s1_ref[...], s2_ref[...]
        lane = lax.broadcasted_iota(jnp.int32, (R, LANES), 1)
        kt_sc[...] = jnp.zeros(kt_sc.shape, _F32)
        vt_sc[...] = jnp.zeros(vt_sc.shape, _F32)
        for h in range(N_ATT_HEADS):
            sl = slice(h * LANES, (h + 1) * LANES)
            q = _rope_block(qkv_ref[:, sl], c, s1, s2) * (ATT_HEAD_DIM ** -0.5)
            k = _rope_block(qkv_ref[:, D_ATTN + h * LANES:D_ATTN + (h + 1) * LANES], c, s1, s2)
            v = qkv_ref[:, 2 * D_ATTN + h * LANES:2 * D_ATTN + (h + 1) * LANES]
            knew_ref[:, sl] = k
            vnew_ref[:, sl] = v
            kt_sc[pl.ds(h, R, stride=N_ATT_HEADS), :] = k
            vt_sc[pl.ds(h, R, stride=N_ATT_HEADS), :] = v
            q2 = jnp.concatenate([jnp.where(lane < ATT_HEAD_DIM, q, 0.0), jnp.where(lane >= ATT_HEAD_DIM, q, 0.0)], axis=0)
            hi, lo = _split2(q2)
            q_sc[h] = jnp.concatenate([hi, lo], axis=0)
        m_sc[...] = jnp.full(m_sc.shape, -jnp.inf, _F32)
        l_sc[...] = jnp.zeros(l_sc.shape, _F32)
        acc_sc[...] = jnp.zeros(acc_sc.shape, _F32)

    def head_rows(tiles, h):
        x = jnp.concatenate([t[pl.ds(h, page, stride=N_ATT_HEADS), :] for t in tiles], axis=0)
        return _split2(x)

    def process(k_tiles, v_tiles, mask):
        s_parts = []
        for h in range(N_ATT_HEADS):
            k_hi, k_lo = head_rows(k_tiles, h)
            q3 = q_sc[h]
            a = lax.dot_general(q3[0:2 * hr], k_hi, _NT, preferred_element_type=_F32)
            b = lax.dot_general(q3[0:hr], k_lo, _NT, preferred_element_type=_F32)
            s_parts.append(a[0:hr] + a[hr:2 * hr] + b)
        s = jnp.concatenate(s_parts, axis=0)
        if mask is not None:
            s = jnp.where(mask, s, -jnp.inf)
        m_prev = m_sc[...]
        m_new = jnp.maximum(m_prev, jnp.max(s, axis=1, keepdims=True))
        alpha = jnp.exp(m_prev - m_new)
        p = jnp.exp(s - m_new[:, :1])
        l_sc[...] = alpha * l_sc[...] + jnp.sum(p, axis=1, keepdims=True)
        m_sc[...] = m_new
        pv_parts = []
        for h in range(N_ATT_HEADS):
            v_hi, v_lo = head_rows(v_tiles, h)
            p_hi, p_lo = _split2(p[h * hr:(h + 1) * hr])
            a = jnp.dot(jnp.concatenate([p_hi, p_lo], axis=0), v_hi, preferred_element_type=_F32)
            b = jnp.dot(p_hi, v_lo, preferred_element_type=_F32)
            pv_parts.append(a[0:hr] + a[hr:2 * hr] + b)
        acc_sc[...] = alpha * acc_sc[...] + jnp.concatenate(pv_parts, axis=0)

    process(k_refs, v_refs, None)

    @pl.when(j == n_steps - 1)
    def _():
        rows = N_ATT_HEADS * hr
        row = lax.broadcasted_iota(jnp.int32, (rows, page), 0)
        col = lax.broadcasted_iota(jnp.int32, (rows, page), 1)
        qrow = row % R
        process([kt_sc], [vt_sc], (col <= qrow) & (col < n_real))
        o = acc_sc[...] / l_sc[...]
        lam = _lambda(lq1_ref, lk1_ref, lq2_ref, lk2_ref)
        wsub = wsub_ref[...]
        for h in range(N_ATT_HEADS):
            o1 = o[h * hr:h * hr + R]
            o2 = o[h * hr + R:(h + 1) * hr]
            o_ref[:, h * LANES:(h + 1) * LANES] = _diff_finalize(o1, o2, lam, wsub)


def _decode_attention(proj3, cache_k2, cache_v2, page_table, pos_rows, lam_params, w_subln, n_real):
    bsz = proj3.shape[0]
    page = cache_k2.shape[2] // N_ATT_HEADS
    n_pages = page_table.shape[1]
    pg = PAGES_PER_STEP
    n_steps = n_pages // pg
    R = SAMPLE_ROWS
    c, s1, s2 = _rope_tables(pos_rows)
    lq1, lk1, lq2, lk2 = [p.reshape(1, ATT_HEAD_DIM) for p in lam_params]
    fixed = lambda b, j, pt: (0, 0)
    tab = pl.BlockSpec((R, LANES), fixed)
    vec = pl.BlockSpec((1, ATT_HEAD_DIM), fixed)

    def page_spec(i):
        return pl.BlockSpec((None, None, page * N_ATT_HEADS, LANES), lambda b, j, pt: (0, pt[b, j * pg + i], 0, 0))

    out_row = pl.BlockSpec((None, R, D_ATTN), lambda b, j, pt: (b, 0, 0))
    grid_spec = pltpu.PrefetchScalarGridSpec(
        num_scalar_prefetch=1,
        grid=(bsz, n_steps),
        in_specs=[pl.BlockSpec((None, R, 3 * D_ATTN), lambda b, j, pt: (b, 0, 0)), tab, tab, tab, vec, vec, vec, vec,
                  pl.BlockSpec((1, ATT_V_DIM), fixed)]
                 + [page_spec(i) for i in range(pg)] + [page_spec(i) for i in range(pg)],
        out_specs=[out_row, out_row, out_row],
        scratch_shapes=[pltpu.VMEM((N_ATT_HEADS, 4 * R, LANES), _BF16),
                        pltpu.VMEM((N_ATT_HEADS * 2 * R, LANES), _F32), pltpu.VMEM((N_ATT_HEADS * 2 * R, LANES), _F32),
                        pltpu.VMEM((N_ATT_HEADS * 2 * R, LANES), _F32),
                        pltpu.VMEM((page * N_ATT_HEADS, LANES), _F32), pltpu.VMEM((page * N_ATT_HEADS, LANES), _F32)],
    )
    return pl.pallas_call(
        functools.partial(_decode_attn_kernel, n_pages_step=pg, n_steps=n_steps, n_real=n_real),
        grid_spec=grid_spec,
        out_shape=[jax.ShapeDtypeStruct((bsz, R, D_ATTN), _F32)] * 3,
        compiler_params=_cparams(("arbitrary", "arbitrary")),
        name="decode_attn",
    )(page_table, proj3, c, s1, s2, lq1, lk1, lq2, lk2, w_subln.reshape(1, ATT_V_DIM),
      *([cache_k2] * pg), *([cache_v2] * pg))


def kernel(x_prompt, x_sample, cache_k, cache_v, state_conv, state_ssm, page_table, p_prompt, p_sample, w_norm_mix, w_in, lambda_q1, lambda_k1, lambda_q2, lambda_k2, w_subln, conv_w, conv_b, dt_bias, A_log, D_skip, w_ssm_norm, w_out, w_norm_ffn, w_group_router, b_group_router, w_expert_router, b_expert_router, w_exp_gate, w_exp_up, w_exp_down, w_norm_ple, w_ple_gate, w_ple_proj, w_norm_final):
    bp, lp, _ = x_prompt.shape
    bs, ls, _ = x_sample.shape
    past = page_table.shape[1] * cache_k.shape[2]
    tp = bp * lp
    R = SAMPLE_ROWS
    ts = bs * R
    lam_params = (lambda_q1[0], lambda_k1[0], lambda_q2[0], lambda_k2[0])
    pad_r = LANES - N_EXPERT_GROUPS - N_EXPERTS
    w_router = jnp.pad(jnp.concatenate([w_group_router[0], w_expert_router[0]], axis=1), ((0, 0), (0, pad_r)))
    b_router = jnp.pad(jnp.concatenate([b_group_router[0], b_expert_router[0]]), (0, pad_r)).reshape(1, LANES)

    xp2d = x_prompt.reshape(tp, D_MODEL)
    q, kf, kb, vf, vb, z, xbc, dt = _prompt_inproj(xp2d, w_norm_mix[0], w_in[0], lp, 512)
    attn_p = _prompt_attention(q, kb, vb, lam_params, w_subln[0], bp, lp, 256, 1024)
    ssd_p, ssm_p = _ssd(xbc, z, dt, jnp.zeros((bp, CONV_WIDTH - 1, CONV_DIM), _F32),
                        jnp.zeros((bp, N_SSM_HEADS, SSM_HEAD_DIM, SSM_STATE), _F32),
                        conv_w[0], conv_b[0], dt_bias[0], A_log[0], D_skip[0], w_ssm_norm[0],
                        bp, lp // SSM_CHUNK, SSM_CHUNK, False, _BF16)
    h1_p, t_p, route_p = _outproj_router(xp2d, attn_p, ssd_p, w_out[0], w_norm_ffn[0], w_router, b_router, 256, False)

    xs3 = jnp.pad(x_sample, ((0, 0), (0, R - ls), (0, 0)))
    xs2d = xs3.reshape(ts, D_MODEL)
    proj_s = _sample_inproj(xs2d, w_norm_mix[0], w_in[0], 512)
    proj3 = proj_s.reshape(bs, R, D_IN_PROJ)
    pos_rows = past + jnp.arange(R, dtype=jnp.int32)
    cache_rows = cache_k.shape[:2] + (cache_k.shape[2] * N_ATT_HEADS, LANES)
    attn_s, k_s, v_s = _decode_attention(proj3, cache_k.reshape(cache_rows), cache_v.reshape(cache_rows), page_table,
                                         pos_rows, lam_params, w_subln[0], ls)
    off = 3 * D_ATTN
    pad_rows = ((0, 0), (0, SSM_CHUNK - R), (0, 0))
    z_s = jnp.pad(proj3[:, :, off:off + D_SSM], pad_rows).reshape(bs * SSM_CHUNK, D_SSM)
    xbc_s3 = proj3[:, :, off + D_SSM:off + D_SSM + CONV_DIM]
    xbc_s = jnp.pad(xbc_s3, pad_rows).reshape(bs * SSM_CHUNK, CONV_DIM)
    dt_s = jnp.pad(proj3[:, :, off + D_SSM + CONV_DIM:], ((0, 0), (0, SSM_CHUNK - R), (0, LANES - N_SSM_HEADS)))
    dt_s = dt_s.reshape(bs * SSM_CHUNK, LANES)
    ssd_s, ssm_s = _ssd(xbc_s, z_s, dt_s, state_conv[0], state_ssm[0], conv_w[0], conv_b[0], dt_bias[0], A_log[0],
                        D_skip[0], w_ssm_norm[0], bs, 1, ls, True, _F32)
    ssd_s = ssd_s.reshape(bs, SSM_CHUNK, D_SSM)[:, :R].reshape(ts, D_SSM)
    h1_s, t_s, route_s = _outproj_router(xs2d, attn_s.reshape(ts, D_ATTN), ssd_s, w_out[0], w_norm_ffn[0],
                                         w_router, b_router, ts, True)

    t_all = jnp.concatenate([t_p, t_s], axis=0)
    route_all = jnp.concatenate([route_p, route_s], axis=0)
    e_ids = route_all[:, :2].astype(jnp.int32)
    y_stk = _expert_mlp(t_all, e_ids, w_exp_gate[0], w_exp_up[0], w_exp_down[0])
    y_p = _combine_ple(h1_p, y_stk, route_p, p_prompt[0].reshape(tp, D_PLE), w_norm_ple[0], w_ple_gate[0],
                       w_ple_proj[0], w_norm_final, 256, 0)
    ps2d = jnp.pad(p_sample[0], ((0, 0), (0, R - ls), (0, 0))).reshape(ts, D_PLE)
    y_s = _combine_ple(h1_s, y_stk, route_s, ps2d, w_norm_ple[0], w_ple_gate[0], w_ple_proj[0], w_norm_final, ts, tp)

    y_prompt = y_p.reshape(bp, lp, D_MODEL)
    y_sample = y_s.reshape(bs, R, D_MODEL)[:, :ls]
    k_prompt = kf.reshape(1, bp, lp, N_ATT_HEADS, 2 * ATT_HEAD_DIM)
    v_prompt = vf.reshape(1, bp, lp, N_ATT_HEADS, ATT_V_DIM)
    conv_prompt = xbc.reshape(bp, lp, CONV_DIM)[:, lp - (CONV_WIDTH - 1):][None]
    ssm_prompt = ssm_p[None]
    k_sample = k_s[:, :ls].reshape(1, bs, ls, N_ATT_HEADS, 2 * ATT_HEAD_DIM)
    v_sample = v_s[:, :ls].reshape(1, bs, ls, N_ATT_HEADS, ATT_V_DIM)
    conv_sample = xbc_s3[:, ls - (CONV_WIDTH - 1):ls][None]
    ssm_sample = ssm_s[None]
    return (y_prompt, y_sample, k_prompt, v_prompt, conv_prompt, ssm_prompt, k_sample, v_sample, conv_sample, ssm_sample)
```

```python
import functools
import math

import jax
import jax.numpy as jnp
from jax import lax
from jax.experimental import pallas as pl
from jax.experimental.pallas import tpu as pltpu

D_MODEL = 2048
D_ATTN = 1024
D_SSM = 1024
ATT_HEAD_DIM = 64
N_ATT_HEADS = 8
ATT_V_DIM = 128
ROT_DIM = 16
ROPE_THETA = 500000.0
SSM_HEAD_DIM = 64
N_SSM_HEADS = 16
SSM_GROUPS = 2
SSM_STATE = 128
SSM_CHUNK = 128
CONV_WIDTH = 4
CONV_DIM = D_SSM + 2 * SSM_GROUPS * SSM_STATE
D_IN_PROJ = 3 * D_ATTN + D_SSM + CONV_DIM + N_SSM_HEADS
N_EXPERT_GROUPS = 4
EXPERTS_PER_GROUP = 8
N_EXPERTS = 32
D_FF_EXPERT = 512
D_PLE = 256
RMS_EPS = 1e-6
LAM_INIT = 0.8 - 0.6 * math.exp(-0.3 * 0)

LANES = 128
SUBLANES = 8
VMEM_LIMIT_BYTES = 56 * 2 ** 20

SAMPLE_ROWS = 8
EXPERT_TILE = 256
PAGES_PER_STEP = 8

_F32 = jnp.float32
_BF16 = jnp.bfloat16
_HI = lax.Precision.HIGHEST
_NT = (((1,), (1,)), ((), ()))


def _cparams(semantics):
    return pltpu.CompilerParams(dimension_semantics=semantics, vmem_limit_bytes=VMEM_LIMIT_BYTES)


def _rms(x, w):
    return x * lax.rsqrt(jnp.mean(x * x, axis=-1, keepdims=True) + RMS_EPS) * w


def _silu(x):
    return x * (1.0 / (1.0 + jnp.exp(-x)))


def _softplus(x):
    return jnp.maximum(x, 0.0) + jnp.log(1.0 + jnp.exp(-jnp.abs(x)))


def _dot(a, b, precise=False):
    if precise:
        return jnp.dot(a.astype(_F32), b.astype(_F32), preferred_element_type=_F32, precision=_HI)
    return jnp.dot(a.astype(_BF16), b.astype(_BF16), preferred_element_type=_F32)


def _dot_nt(a, b, precise=False):
    if precise:
        return lax.dot_general(a.astype(_F32), b.astype(_F32), _NT, preferred_element_type=_F32, precision=_HI)
    return lax.dot_general(a.astype(_BF16), b.astype(_BF16), _NT, preferred_element_type=_F32)


def _rope_block(y, c, s1, s2):
    return y * c + pltpu.roll(y, LANES - ROT_DIM // 2, 1) * s1 + pltpu.roll(y, ROT_DIM // 2, 1) * s2


def _rope_tables(pos):
    half = ROT_DIM // 2
    inv_freq = jnp.power(ROPE_THETA, -jnp.arange(half, dtype=_F32) * (2.0 / ROT_DIM))
    ang = pos.astype(_F32)[:, None] * inv_freq[None, :]
    cos, sin = jnp.cos(ang), jnp.sin(ang)
    n = pos.shape[0]
    ones = jnp.ones((n, ATT_HEAD_DIM - ROT_DIM), _F32)
    zeros = jnp.zeros((n, ATT_HEAD_DIM - ROT_DIM), _F32)
    zh = jnp.zeros((n, half), _F32)
    c = jnp.concatenate([cos, cos, ones], axis=1)
    s1 = jnp.concatenate([-sin, zh, zeros], axis=1)
    s2 = jnp.concatenate([zh, sin, zeros], axis=1)
    return tuple(jnp.concatenate([t, t], axis=1) for t in (c, s1, s2))


def _cast_weights_once(w_ref, wb_sc):
    @pl.when(pl.program_id(0) == 0)
    def _():
        n = w_ref.shape[1]
        if n < wb_sc.shape[1]:
            wb_sc[...] = jnp.zeros(wb_sc.shape, wb_sc.dtype)
        wb_sc[:, 0:n] = w_ref[...].astype(wb_sc.dtype)


def _resident(shape, index_map):
    return pl.BlockSpec(shape, index_map, pipeline_mode=pl.Buffered(1))


def _inproj_qk_kernel(x_ref, wn_ref, w_ref, c_ref, s1_ref, s2_ref, q_ref, kf_ref, kb_ref, wb_sc):
    _cast_weights_once(w_ref, wb_sc)
    u = _rms(x_ref[...], wn_ref[...]).astype(_BF16)
    y = jnp.dot(u, wb_sc[...], preferred_element_type=_F32)
    c, s1, s2 = c_ref[...], s1_ref[...], s2_ref[...]
    q_scale = (ATT_HEAD_DIM ** -0.5) * math.log2(math.e)
    for j in range(2 * N_ATT_HEADS):
        r = _rope_block(y[:, j * LANES:(j + 1) * LANES], c, s1, s2)
        if j < N_ATT_HEADS:
            q_ref[:, j * LANES:(j + 1) * LANES] = (r * q_scale).astype(_BF16)
        else:
            jj = j - N_ATT_HEADS
            kf_ref[pl.ds(jj, r.shape[0], stride=N_ATT_HEADS), :] = r
            kb_ref[:, jj * LANES:(jj + 1) * LANES] = r.astype(_BF16)


def _inproj_vz_kernel(x_ref, wn_ref, w_ref, vf_ref, vb_ref, z_ref, wb_sc):
    _cast_weights_once(w_ref, wb_sc)
    u = _rms(x_ref[...], wn_ref[...]).astype(_BF16)
    y = jnp.dot(u, wb_sc[...], preferred_element_type=_F32)
    v = y[:, :D_ATTN]
    for h in range(N_ATT_HEADS):
        vf_ref[pl.ds(h, v.shape[0], stride=N_ATT_HEADS), :] = v[:, h * LANES:(h + 1) * LANES]
    vb_ref[...] = v.astype(_BF16)
    z_ref[...] = y[:, D_ATTN:].astype(_BF16)


def _inproj_xbc_kernel(x_ref, wn_ref, w_ref, xbc_ref, dt_ref, wb_sc):
    _cast_weights_once(w_ref, wb_sc)
    u = _rms(x_ref[...], wn_ref[...]).astype(_BF16)
    y = jnp.dot(u, wb_sc[...], preferred_element_type=_F32)
    xbc_ref[...] = y[:, :CONV_DIM]
    dt_ref[...] = y[:, CONV_DIM:]


def _prompt_inproj(x2d, w_norm, w_in, seq, tm):
    t = x2d.shape[0]
    nblk = t // tm
    per_seq = seq // tm
    wn = w_norm.reshape(1, D_MODEL)
    row = lambda i: (i, 0)
    fixed = lambda i: (0, 0)
    x_spec = pl.BlockSpec((tm, D_MODEL), row)
    wn_spec = pl.BlockSpec((1, D_MODEL), fixed)
    head_spec = pl.BlockSpec((tm * N_ATT_HEADS, LANES), row)
    wide = 2 * D_ATTN

    c, s1, s2 = _rope_tables(jnp.arange(seq, dtype=jnp.int32))
    tab_spec = pl.BlockSpec((tm, LANES), lambda i: (i % per_seq, 0))
    q, kf, kb = pl.pallas_call(
        _inproj_qk_kernel,
        grid=(nblk,),
        in_specs=[x_spec, wn_spec, _resident((D_MODEL, wide), fixed), tab_spec, tab_spec, tab_spec],
        out_specs=[pl.BlockSpec((tm, D_ATTN), row), head_spec, pl.BlockSpec((tm, D_ATTN), row)],
        out_shape=[jax.ShapeDtypeStruct((t, D_ATTN), _BF16), jax.ShapeDtypeStruct((t * N_ATT_HEADS, LANES), _F32),
                   jax.ShapeDtypeStruct((t, D_ATTN), _BF16)],
        scratch_shapes=[pltpu.VMEM((D_MODEL, wide), _BF16)],
        compiler_params=_cparams(("arbitrary",)),
        name="inproj_qk",
    )(x2d, wn, w_in, c, s1, s2)

    vf, vb, z = pl.pallas_call(
        _inproj_vz_kernel,
        grid=(nblk,),
        in_specs=[x_spec, wn_spec, _resident((D_MODEL, wide), lambda i: (0, 1))],
        out_specs=[head_spec, pl.BlockSpec((tm, D_ATTN), row), pl.BlockSpec((tm, D_SSM), row)],
        out_shape=[jax.ShapeDtypeStruct((t * N_ATT_HEADS, LANES), _F32), jax.ShapeDtypeStruct((t, D_ATTN), _BF16),
                   jax.ShapeDtypeStruct((t, D_SSM), _BF16)],
        scratch_shapes=[pltpu.VMEM((D_MODEL, wide), _BF16)],
        compiler_params=_cparams(("arbitrary",)),
        name="inproj_vz",
    )(x2d, wn, w_in)

    w_tail = w_in[:, 2 * wide:]
    n_tail = CONV_DIM + LANES
    xbc, dt = pl.pallas_call(
        _inproj_xbc_kernel,
        grid=(nblk,),
        in_specs=[x_spec, wn_spec, _resident(w_tail.shape, fixed)],
        out_specs=[pl.BlockSpec((tm, CONV_DIM), row), pl.BlockSpec((tm, LANES), row)],
        out_shape=[jax.ShapeDtypeStruct((t, CONV_DIM), _F32), jax.ShapeDtypeStruct((t, LANES), _F32)],
        scratch_shapes=[pltpu.VMEM((D_MODEL, n_tail), _BF16)],
        compiler_params=_cparams(("arbitrary",)),
        name="inproj_xbc",
    )(x2d, wn, w_tail)
    return q, kf, kb, vf, vb, z, xbc, dt


def _lambda(lq1_ref, lk1_ref, lq2_ref, lk2_ref):
    a = jnp.sum(lq1_ref[...] * lk1_ref[...], axis=-1, keepdims=True)
    b = jnp.sum(lq2_ref[...] * lk2_ref[...], axis=-1, keepdims=True)
    return jnp.exp(a) - jnp.exp(b) + LAM_INIT


def _diff_finalize(o1, o2, lam, wsub):
    a = o1 - lam * o2
    return _rms(a, wsub) * (1.0 - LAM_INIT)


def _attn_kernel(q_ref, k_ref, v_ref, lq1_ref, lk1_ref, lq2_ref, lk2_ref, wsub_ref, o_ref,
                 qt_sc, vt_sc, bias_sc, sa_sc, sb_sc, m_sc, acc_sc, *, tile):
    qi = pl.program_id(2)
    n_kt = vt_sc.shape[0]
    dv = ATT_V_DIM
    masked_out = -1e30

    @pl.when(qi == 0)
    def _():
        for c in range(n_kt):
            vt_sc[c, 0:dv, :] = v_ref[c * tile:(c + 1) * tile, :].astype(_F32).T.astype(_BF16)
            vt_sc[c, dv:, :] = jnp.ones((vt_sc.shape[1] - dv, tile), _BF16)
        key = lax.broadcasted_iota(jnp.int32, (tile, tile), 0)
        qry = lax.broadcasted_iota(jnp.int32, (tile, tile), 1)
        bias_sc[...] = jnp.where(key <= qry, 0.0, masked_out)

    q = q_ref[...].astype(_F32)
    lane = lax.broadcasted_iota(jnp.int32, (tile, LANES), 1)
    qt_sc[0] = jnp.where(lane < ATT_HEAD_DIM, q, 0.0).T.astype(_BF16)
    qt_sc[1] = jnp.where(lane >= ATT_HEAD_DIM, q, 0.0).T.astype(_BF16)
    m_sc[...] = jnp.full(m_sc.shape, -jnp.inf, _F32)
    acc_sc[...] = jnp.zeros(acc_sc.shape, _F32)

    def scores(j, dst):
        k = k_ref[pl.ds(pl.multiple_of(j * tile, tile), tile), :]
        for mm in range(2):
            dst[mm] = jnp.dot(k, qt_sc[mm], preferred_element_type=_F32)

    def softmax_pv(j, src, diagonal=False):
        vt = vt_sc[j]
        for mm in range(2):
            st = src[mm]
            if diagonal:
                st = st + bias_sc[...]
            m_prev = m_sc[mm]
            m_new = jnp.maximum(m_prev, jnp.max(st, axis=0, keepdims=True))
            alpha = jnp.exp2(m_prev - m_new)
            pt = jnp.exp2(st - m_new[0:1, :]).astype(_BF16)
            acc_sc[mm] = alpha[0:1, :] * acc_sc[mm] + jnp.dot(vt, pt, preferred_element_type=_F32)
            m_sc[mm] = m_new

    def pair(jj, carry):
        j0 = 2 * jj
        scores(j0 + 1, sb_sc)
        softmax_pv(j0, sa_sc)
        scores(j0 + 2, sa_sc)
        softmax_pv(j0 + 1, sb_sc)
        return carry

    scores(0, sa_sc)
    lax.fori_loop(0, qi // 2, pair, 0)
    odd = lax.rem(qi, 2) == 1

    @pl.when(odd)
    def _():
        scores(qi, sb_sc)
        softmax_pv(qi - 1, sa_sc)
        softmax_pv(qi, sb_sc, diagonal=True)

    @pl.when(jnp.logical_not(odd))
    def _():
        softmax_pv(qi, sa_sc, diagonal=True)

    outs = []
    for mm in range(2):
        acc = acc_sc[mm]
        outs.append((acc[0:dv, :] / acc[dv:dv + 1, :]).T)
    lam = _lambda(lq1_ref, lk1_ref, lq2_ref, lk2_ref)
    o_ref[...] = _diff_finalize(outs[0], outs[1], lam, wsub_ref[...]).astype(o_ref.dtype)


def _prompt_attention(q, kb, vb, lam_params, w_subln, batch, seq, tile):
    t = q.shape[0]
    tq = tile
    nq = seq // tq
    lq1, lk1, lq2, lk2 = [p.reshape(1, ATT_HEAD_DIM) for p in lam_params]
    vec = pl.BlockSpec((1, ATT_HEAD_DIM), lambda b, h, i: (0, 0))
    kv_spec = pl.BlockSpec((seq, LANES), lambda b, h, i: (b, h))
    qo_spec = pl.BlockSpec((tq, LANES), lambda b, h, i: (b * nq + i, h))
    return pl.pallas_call(
        functools.partial(_attn_kernel, tile=tile),
        grid=(batch, N_ATT_HEADS, nq),
        in_specs=[qo_spec, kv_spec, kv_spec, vec, vec, vec, vec,
                  pl.BlockSpec((1, ATT_V_DIM), lambda b, h, i: (0, 0))],
        out_specs=qo_spec,
        out_shape=jax.ShapeDtypeStruct((t, D_ATTN), _BF16),
        scratch_shapes=[pltpu.VMEM((2, LANES, tile), _BF16),
                        pltpu.VMEM((seq // tile, ATT_V_DIM + 2 * SUBLANES, tile), _BF16),
                        pltpu.VMEM((tile, tile), _F32),
                        pltpu.VMEM((2, tile, tile), _F32), pltpu.VMEM((2, tile, tile), _F32),
                        pltpu.VMEM((2, SUBLANES, tile), _F32),
                        pltpu.VMEM((2, ATT_V_DIM + 2 * SUBLANES, tile), _F32)],
        compiler_params=_cparams(("arbitrary", "arbitrary", "arbitrary")),
        name="prompt_attn",
    )(q, kb, vb, lq1, lk1, lq2, lk2, w_subln.reshape(1, ATT_V_DIM))


def _ssd_kernel(xbc_ref, z_ref, dt_ref, dtt_ref, cprev_ref, h0_ref, convw_ref, convb_ref, dtb_row_ref, dtb_col_ref,
                alog_row_ref, alog_col_ref, dskip_ref, wnorm_ref, y_ref, hout_ref,
                xp_sc, st_sc, yg_sc, *, n_valid, precise):
    c = pl.program_id(1)
    L = SSM_CHUNK
    n_pairs = N_SSM_HEADS // 2
    hp = 2 * SSM_HEAD_DIM

    @pl.when(c == 0)
    def _():
        xp_sc[0:SUBLANES, :] = jnp.zeros((SUBLANES, CONV_DIM), _F32)
        xp_sc[SUBLANES - (CONV_WIDTH - 1):SUBLANES, :] = cprev_ref[...]
        st_sc[...] = h0_ref[...]

    xp_sc[SUBLANES:SUBLANES + L, :] = xbc_ref[...]
    base = SUBLANES - (CONV_WIDTH - 1)
    conv = convb_ref[...] + convw_ref[0:1, :] * xp_sc[base:base + L, :]
    for j in range(1, CONV_WIDTH):
        conv = conv + convw_ref[j:j + 1, :] * xp_sc[base + j:base + j + L, :]
    xp_sc[0:SUBLANES, :] = xp_sc[L:L + SUBLANES, :]
    xc = _silu(conv)
    xs = xc[:, :D_SSM]

    row_i = lax.broadcasted_iota(jnp.int32, (L, LANES), 0)
    col_i = lax.broadcasted_iota(jnp.int32, (L, LANES), 1)
    dt = _softplus(dt_ref[...] + dtb_row_ref[...])
    dtt = _softplus(dtt_ref[...] + dtb_col_ref[...])
    if n_valid < L:
        dt = jnp.where(row_i < n_valid, dt, 0.0)
        dtt = jnp.where(lax.broadcasted_iota(jnp.int32, dtt.shape, 1) < n_valid, dtt, 0.0)
    ad = dt * (-jnp.exp(alog_row_ref[...]))
    adt = dtt * (-jnp.exp(alog_col_ref[...]))
    tril = (col_i <= row_i).astype(_F32)
    triu = (row_i <= col_i).astype(_F32)
    acs = jnp.dot(tril, ad, preferred_element_type=_F32, precision=_HI)
    acst = jnp.dot(adt, triu, preferred_element_type=_F32, precision=_HI)
    causal = col_i <= row_i
    lane_lo = col_i < SSM_HEAD_DIM

    cb = []
    for g in range(SSM_GROUPS):
        bg = xc[:, D_SSM + g * SSM_STATE:D_SSM + (g + 1) * SSM_STATE]
        cg = xc[:, D_SSM + (SSM_GROUPS + g) * SSM_STATE:D_SSM + (SSM_GROUPS + g + 1) * SSM_STATE]
        cb.append((bg, cg, _dot_nt(cg, bg, precise)))

    for p in range(n_pairs):
        ha, hb = 2 * p, 2 * p + 1
        bg, cg, cbg = cb[ha // (N_SSM_HEADS // SSM_GROUPS)]
        col_a, col_b = acs[:, ha:ha + 1], acs[:, hb:hb + 1]
        lm_a = jnp.where(causal, jnp.exp(col_a - acst[ha:ha + 1, :]), 0.0)
        lm_b = jnp.where(causal, jnp.exp(col_b - acst[hb:hb + 1, :]), 0.0)
        dt_pair = jnp.where(lane_lo, dt[:, ha:ha + 1], dt[:, hb:hb + 1])
        x_pair = xs[:, p * hp:(p + 1) * hp] * dt_pair
        y_diag = jnp.where(lane_lo, _dot(cbg * lm_a, x_pair, precise), _dot(cbg * lm_b, x_pair, precise))
        st = st_sc[p]
        y_off = _dot_nt(cg, st, precise) * jnp.where(lane_lo, jnp.exp(col_a), jnp.exp(col_b))
        last_a, last_b = acs[L - 1:L, ha:ha + 1], acs[L - 1:L, hb:hb + 1]
        decay = jnp.where(lane_lo, jnp.exp(last_a - col_a), jnp.exp(last_b - col_b))
        upd = _dot((x_pair * decay).T, bg, precise)
        row_lo = row_i < SSM_HEAD_DIM
        st_sc[p] = jnp.where(row_lo, jnp.exp(last_a), jnp.exp(last_b)) * st + upd
        y_pair = y_diag + y_off + xs[:, p * hp:(p + 1) * hp] * dskip_ref[:, p * hp:(p + 1) * hp]
        zg = z_ref[:, p * hp:(p + 1) * hp].astype(_F32)
        yg_sc[:, p * hp:(p + 1) * hp] = y_pair * _silu(zg)

    gw = D_SSM // SSM_GROUPS
    for g in range(SSM_GROUPS):
        y_ref[:, g * gw:(g + 1) * gw] = _rms(yg_sc[:, g * gw:(g + 1) * gw],
                                             wnorm_ref[:, g * gw:(g + 1) * gw]).astype(y_ref.dtype)

    hout_ref[...] = st_sc[...]


def _ssd(xbc, z, dt, conv_prev, h0, conv_w, conv_b, dt_bias, a_log, d_skip, w_ssm_norm, batch, n_chunks,
         n_valid, precise, out_dtype):
    L = SSM_CHUNK
    rows = xbc.shape[0]
    n_pairs = N_SSM_HEADS // 2
    dtt = dt[:, :N_SSM_HEADS].T
    pad = LANES - N_SSM_HEADS
    dtb_row = jnp.pad(dt_bias, (0, pad)).reshape(1, LANES)
    alog_row = jnp.pad(a_log, (0, pad)).reshape(1, LANES)
    dskip = jnp.repeat(d_skip, SSM_HEAD_DIM).reshape(1, D_SSM)
    blk = lambda b, c: (b * n_chunks + c, 0)
    fixed = lambda b, c: (0, 0)
    y, h_out = pl.pallas_call(
        functools.partial(_ssd_kernel, n_valid=n_valid, precise=precise),
        grid=(batch, n_chunks),
        in_specs=[
            pl.BlockSpec((L, CONV_DIM), blk),
            pl.BlockSpec((L, D_SSM), blk),
            pl.BlockSpec((L, LANES), blk),
            pl.BlockSpec((N_SSM_HEADS, L), lambda b, c: (0, b * n_chunks + c)),
            pl.BlockSpec((None, CONV_WIDTH - 1, CONV_DIM), lambda b, c: (b, 0, 0)),
            pl.BlockSpec((None, n_pairs, 2 * SSM_HEAD_DIM, SSM_STATE), lambda b, c: (b, 0, 0, 0)),
            pl.BlockSpec((CONV_WIDTH, CONV_DIM), fixed),
            pl.BlockSpec((1, CONV_DIM), fixed),
            pl.BlockSpec((1, LANES), fixed),
            pl.BlockSpec((N_SSM_HEADS, 1), fixed),
            pl.BlockSpec((1, LANES), fixed),
            pl.BlockSpec((N_SSM_HEADS, 1), fixed),
            pl.BlockSpec((1, D_SSM), fixed),
            pl.BlockSpec((1, D_SSM), fixed),
        ],
        out_specs=[pl.BlockSpec((L, D_SSM), blk),
                   pl.BlockSpec((None, n_pairs, 2 * SSM_HEAD_DIM, SSM_STATE), lambda b, c: (b, 0, 0, 0))],
        out_shape=[jax.ShapeDtypeStruct((rows, D_SSM), out_dtype),
                   jax.ShapeDtypeStruct((batch, n_pairs, 2 * SSM_HEAD_DIM, SSM_STATE), _F32)],
        scratch_shapes=[pltpu.VMEM((SUBLANES + L, CONV_DIM), _F32),
                        pltpu.VMEM((n_pairs, 2 * SSM_HEAD_DIM, SSM_STATE), _F32),
                        pltpu.VMEM((L, D_SSM), _F32)],
        compiler_params=_cparams(("arbitrary", "arbitrary")),
        name="ssd_precise" if precise else "ssd",
    )(xbc, z, dt, dtt, conv_prev, h0.reshape(batch, n_pairs, 2 * SSM_HEAD_DIM, SSM_STATE),
      conv_w, conv_b.reshape(1, CONV_DIM), dtb_row, dt_bias.reshape(N_SSM_HEADS, 1),
      alog_row, a_log.reshape(N_SSM_HEADS, 1), dskip, w_ssm_norm.reshape(1, D_SSM))
    return y, h_out.reshape(batch, N_SSM_HEADS, SSM_HEAD_DIM, SSM_STATE)


def _outproj_router_kernel(x_ref, attn_ref, ssd_ref, wa_ref, wb_ref, wn_ref, wr_ref, br_ref,
                           h1_ref, t_ref, route_ref, *scratch, precise):
    if precise:
        wa, wb = wa_ref, wb_ref
    else:
        wa, wb = scratch
        _cast_weights_once(wa_ref, wa)
        _cast_weights_once(wb_ref, wb)
    h1 = x_ref[...] + _dot(attn_ref[...], wa[...], precise) + _dot(ssd_ref[...], wb[...], precise)
    h1_ref[...] = h1
    t = _rms(h1, wn_ref[...])
    t_ref[...] = t
    logits = _dot(t, wr_ref[...], precise) + br_ref[...]
    lane = lax.broadcasted_iota(jnp.int32, logits.shape, 1).astype(_F32)
    neg = -jnp.inf
    big = float(LANES)
    is_g = lane < N_EXPERT_GROUPS
    gl = jnp.where(is_g, logits, neg)
    gmax = jnp.max(gl, axis=1, keepdims=True)
    g_idx = jnp.min(jnp.where(gl == gmax, lane, big), axis=1, keepdims=True)
    g_w = 1.0 / jnp.sum(jnp.where(is_g, jnp.exp(logits - gmax), 0.0), axis=1, keepdims=True)
    lo = N_EXPERT_GROUPS + EXPERTS_PER_GROUP * g_idx
    l1 = jnp.where(lane >= lo, jnp.where(lane < lo + EXPERTS_PER_GROUP, logits, neg), neg)
    m1 = jnp.max(l1, axis=1, keepdims=True)
    i1 = jnp.min(jnp.where(l1 == m1, lane, big), axis=1, keepdims=True)
    l2 = jnp.where(lane == i1, neg, l1)
    m2 = jnp.max(l2, axis=1, keepdims=True)
    i2 = jnp.min(jnp.where(l2 == m2, lane, big), axis=1, keepdims=True)
    r = jnp.exp(m2 - m1)
    p1 = 1.0 / (1.0 + r)
    w1 = g_w * p1
    w2 = g_w * (r * p1)
    e1 = i1 - N_EXPERT_GROUPS
    e2 = i2 - N_EXPERT_GROUPS
    route_ref[...] = jnp.where(lane == 0, e1, jnp.where(lane == 1, e2, jnp.where(lane == 2, w1,
                               jnp.where(lane == 3, w2, 0.0))))


def _outproj_router(x2d, attn, ssd, w_out, w_norm_ffn, w_router, b_router, tm, precise):
    t = x2d.shape[0]
    wdt = _F32 if precise else _BF16
    row = lambda i: (i, 0)
    fixed = lambda i: (0, 0)
    half_w = (D_ATTN, D_MODEL)
    scratch = [] if precise else [pltpu.VMEM(half_w, _BF16), pltpu.VMEM(half_w, _BF16)]
    return pl.pallas_call(
        functools.partial(_outproj_router_kernel, precise=precise),
        grid=(t // tm,),
        in_specs=[pl.BlockSpec((tm, D_MODEL), row), pl.BlockSpec((tm, D_ATTN), row), pl.BlockSpec((tm, D_SSM), row),
                  _resident(half_w, fixed), _resident(half_w, lambda i: (1, 0)),
                  pl.BlockSpec((1, D_MODEL), fixed), pl.BlockSpec((D_MODEL, LANES), fixed),
                  pl.BlockSpec((1, LANES), fixed)],
        out_specs=[pl.BlockSpec((tm, D_MODEL), row), pl.BlockSpec((tm, D_MODEL), row), pl.BlockSpec((tm, LANES), row)],
        out_shape=[jax.ShapeDtypeStruct((t, D_MODEL), _F32), jax.ShapeDtypeStruct((t, D_MODEL), _F32),
                   jax.ShapeDtypeStruct((t, LANES), _F32)],
        scratch_shapes=scratch,
        compiler_params=_cparams(("arbitrary",)),
        name="outproj_router_precise" if precise else "outproj_router",
    )(x2d, attn, ssd, w_out, w_out, w_norm_ffn.reshape(1, D_MODEL), w_router.astype(wdt), b_router)


def _expert_kernel(te_ref, nv_ref, idx_ref, idx_next_ref, t_hbm, wg_ref, wu_ref, wd_ref, y_hbm,
                   xbuf, ybuf, sem_in, sem_out, *, tm, n_tiles):
    i = pl.program_id(0)
    nv = nv_ref[i]
    nv_next = jnp.where(i + 1 < n_tiles, nv_ref[jnp.minimum(i + 1, n_tiles - 1)], 0)
    slot = lax.rem(i, 2)

    def start_gather(idx, s):
        for r in range(tm):
            pltpu.make_async_copy(t_hbm.at[pl.ds(idx[0, 0, r], 1), :], xbuf.at[s, pl.ds(r, 1), :], sem_in.at[s]).start()

    def wait_gather(s):
        pltpu.make_async_copy(t_hbm.at[pl.ds(0, tm), :], xbuf.at[s], sem_in.at[s]).wait()

    def start_scatter():
        for r in range(tm):
            pltpu.make_async_copy(ybuf.at[pl.ds(r, 1), :], y_hbm.at[pl.ds(idx_ref[0, 0, tm + r], 1), :], sem_out).start()

    def wait_scatter():
        pltpu.make_async_copy(ybuf, y_hbm.at[pl.ds(0, tm), :], sem_out).wait()

    @pl.when(i == 0)
    def _():
        start_gather(idx_ref, 0)
        half = tm // 2
        n_pad = y_hbm.shape[0] // 2
        ybuf[0:half, :] = jnp.zeros((half, D_MODEL), _F32)
        for k in range(2):
            fill = pltpu.make_async_copy(ybuf.at[pl.ds(0, half), :],
                                         y_hbm.at[pl.ds((k + 1) * n_pad - half, half), :], sem_out)
            fill.start()
            fill.wait()

    @pl.when(nv_next > 0)
    def _():
        start_gather(idx_next_ref, 1 - slot)

    @pl.when(nv > 0)
    def _():
        wait_gather(slot)
        x = xbuf[slot].astype(_BF16)
        hg = jnp.dot(x, wg_ref[...].astype(_BF16), preferred_element_type=_F32)
        hu = jnp.dot(x, wu_ref[...].astype(_BF16), preferred_element_type=_F32)
        hh = (_silu(hg) * hu).astype(_BF16)
        y = jnp.dot(hh, wd_ref[...].astype(_BF16), preferred_element_type=_F32)

        @pl.when(i > 0)
        def _():
            wait_scatter()

        ybuf[...] = y
        start_scatter()

        @pl.when(nv_next == 0)
        def _():
            wait_scatter()


def _moe_plan(e_ids, tm, n_tiles):
    n = e_ids.shape[0]
    n_assign = 2 * n
    half = tm // 2
    n_pad = n + half
    e_flat = e_ids.reshape(-1)
    order = jnp.argsort(e_flat, stable=True).astype(jnp.int32)
    counts = jnp.sum((e_flat[:, None] == jnp.arange(N_EXPERTS, dtype=jnp.int32)[None, :]).astype(jnp.int32), axis=0)
    seg_start = jnp.cumsum(counts) - counts
    tiles_per = (counts + tm - 1) // tm
    cum_tiles = jnp.cumsum(tiles_per)
    tile_start = cum_tiles - tiles_per
    n_used = cum_tiles[-1]
    tile_id = jnp.arange(n_tiles, dtype=jnp.int32)
    used = tile_id < n_used
    te = jnp.sum((cum_tiles[None, :] <= jnp.minimum(tile_id, n_used - 1)[:, None]).astype(jnp.int32), axis=1)
    j = tile_id - tile_start[te]
    nvalid = jnp.where(used, jnp.clip(counts[te] - j * tm, 0, tm), 0).astype(jnp.int32)
    r = jnp.arange(tm, dtype=jnp.int32)[None, :]
    pos = seg_start[te][:, None] + j[:, None] * tm + r
    a = order[jnp.clip(pos, 0, n_assign - 1)]
    tok = a // 2
    spare = (r // half) * n_pad + n + (r % half)
    dst = jnp.where(r < nvalid[:, None], (a % 2) * n_pad + tok, spare)
    idx = jnp.concatenate([tok, dst], axis=1).reshape(n_tiles, 1, 2 * tm).astype(jnp.int32)
    return te, nvalid, idx, n_pad


def _expert_mlp(t_all, e_ids, w_gate, w_up, w_down):
    n = t_all.shape[0]
    tm = EXPERT_TILE
    n_tiles = pl.cdiv(2 * n, tm) + N_EXPERTS
    te, nvalid, idx, n_pad = _moe_plan(e_ids, tm, n_tiles)
    wmap = lambda i, te_ref, nv_ref: (te_ref[i], 0, 0)
    grid_spec = pltpu.PrefetchScalarGridSpec(
        num_scalar_prefetch=2,
        grid=(n_tiles,),
        in_specs=[
            pl.BlockSpec((1, 1, 2 * tm), lambda i, te_ref, nv_ref: (i, 0, 0), memory_space=pltpu.SMEM),
            pl.BlockSpec((1, 1, 2 * tm), lambda i, te_ref, nv_ref: (jnp.minimum(i + 1, n_tiles - 1), 0, 0),
                         memory_space=pltpu.SMEM),
            pl.BlockSpec(memory_space=pl.ANY),
            pl.BlockSpec((None, D_MODEL, D_FF_EXPERT), wmap),
            pl.BlockSpec((None, D_MODEL, D_FF_EXPERT), wmap),
            pl.BlockSpec((None, D_FF_EXPERT, D_MODEL), wmap),
        ],
        out_specs=pl.BlockSpec(memory_space=pl.ANY),
        scratch_shapes=[pltpu.VMEM((2, tm, D_MODEL), _F32), pltpu.VMEM((tm, D_MODEL), _F32),
                        pltpu.SemaphoreType.DMA((2,)), pltpu.SemaphoreType.DMA(())],
    )
    y_rows = pl.pallas_call(
        functools.partial(_expert_kernel, tm=tm, n_tiles=n_tiles),
        grid_spec=grid_spec,
        out_shape=jax.ShapeDtypeStruct((2 * n_pad, D_MODEL), _F32),
        compiler_params=_cparams(("arbitrary",)),
        name="expert_mlp",
    )(te, nvalid, idx, idx, t_all, w_gate, w_up, w_down)
    return y_rows.reshape(2, n_pad, D_MODEL)


def _combine_ple_kernel(h1_ref, y_ref, route_ref, p_ref, wn_ref, wg_ref, wp_ref, wf_ref, o_ref, wg_sc):
    _cast_weights_once(wg_ref, wg_sc)
    route = route_ref[...]
    h2 = h1_ref[...] + route[:, 2:3] * y_ref[0] + route[:, 3:4] * y_ref[1]
    u = _rms(h2, wn_ref[...])
    gate = jnp.dot(u.astype(_BF16), wg_sc[...], preferred_element_type=_F32)
    gate = 1.0 / (1.0 + jnp.exp(-gate))
    pp = jnp.dot(p_ref[...].astype(_BF16), wp_ref[...], preferred_element_type=_F32)
    h3 = h2 + gate * pp
    o_ref[...] = _rms(h3, wf_ref[...])


def _combine_ple(h1, y_stk, route, p2d, w_norm_ple, w_ple_gate, w_ple_proj, w_norm_final, tm, row_off):
    t = h1.shape[0]
    off = row_off // tm
    row = lambda i: (i, 0)
    fixed = lambda i: (0, 0)
    return pl.pallas_call(
        _combine_ple_kernel,
        grid=(t // tm,),
        in_specs=[pl.BlockSpec((tm, D_MODEL), row),
                  pl.BlockSpec((2, tm, D_MODEL), lambda i: (0, i + off, 0)),
                  pl.BlockSpec((tm, LANES), row), pl.BlockSpec((tm, D_PLE), row),
                  pl.BlockSpec((1, D_MODEL), fixed), _resident((D_MODEL, D_MODEL), fixed),
                  pl.BlockSpec((D_PLE, D_MODEL), fixed), pl.BlockSpec((1, D_MODEL), fixed)],
        out_specs=pl.BlockSpec((tm, D_MODEL), row),
        out_shape=jax.ShapeDtypeStruct((t, D_MODEL), _F32),
        scratch_shapes=[pltpu.VMEM((D_MODEL, D_MODEL), _BF16)],
        compiler_params=_cparams(("arbitrary",)),
        name="combine_ple",
    )(h1, y_stk, route, p2d, w_norm_ple.reshape(1, D_MODEL), w_ple_gate, w_ple_proj.astype(_BF16),
      w_norm_final.reshape(1, D_MODEL))


def _sample_inproj_kernel(x_ref, wn_ref, w_ref, o_ref):
    u = _rms(x_ref[...], wn_ref[...])
    o_ref[...] = jnp.dot(u, w_ref[...], preferred_element_type=_F32, precision=_HI)


def _sample_inproj(xs2d, w_norm, w_in, tn):
    rows = xs2d.shape[0]
    return pl.pallas_call(
        _sample_inproj_kernel,
        grid=(pl.cdiv(D_IN_PROJ, tn),),
        in_specs=[pl.BlockSpec((rows, D_MODEL), lambda j: (0, 0)), pl.BlockSpec((1, D_MODEL), lambda j: (0, 0)),
                  pl.BlockSpec((D_MODEL, tn), lambda j: (0, j))],
        out_specs=pl.BlockSpec((rows, tn), lambda j: (0, j)),
        out_shape=jax.ShapeDtypeStruct((rows, D_IN_PROJ), _F32),
        compiler_params=_cparams(("arbitrary",)),
        name="sample_inproj",
    )(xs2d, w_norm.reshape(1, D_MODEL), w_in)


def _split2(x):
    hi = x.astype(_BF16)
    lo = (x - hi.astype(_F32)).astype(_BF16)
    return hi, lo


def _decode_attn_kernel(pt_ref, qkv_ref, c_ref, s1_ref, s2_ref, lq1_ref, lk1_ref, lq2_ref, lk2_ref, wsub_ref, *rest,
                        n_pages_step, n_steps, n_real):
    k_refs = rest[:n_pages_step]
    v_refs = rest[n_pages_step:2 * n_pages_step]
    o_ref, knew_ref, vnew_ref = rest[2 * n_pages_step:2 * n_pages_step + 3]
    q_sc, m_sc, l_sc, acc_sc, kt_sc, vt_sc = rest[2 * n_pages_step + 3:]
    j = pl.program_id(1)
    R = SAMPLE_ROWS
    hr = 2 * R
    page = kt_sc.shape[0] // N_ATT_HEADS

    @pl.when(j == 0)
    def _():
        c, s1, s2 = c_ref[...], s1_ref[...], s2_ref[...]
        lane = lax.broadcasted_iota(jnp.int32, (R, LANES), 1)
        kt_sc[...] = jnp.zeros(kt_sc.shape, _F32)
        vt_sc[...] = jnp.zeros(vt_sc.shape, _F32)
        for h in range(N_ATT_HEADS):
            sl = slice(h * LANES, (h + 1) * LANES)
            q = _rope_block(qkv_ref[:, sl], c, s1, s2) * (ATT_HEAD_DIM ** -0.5)
            k = _rope_block(qkv_ref[:, D_ATTN + h * LANES:D_ATTN + (h + 1) * LANES], c, s1, s2)
            v = qkv_ref[:, 2 * D_ATTN + h * LANES:2 * D_ATTN + (h + 1) * LANES]
            knew_ref[:, sl] = k
            vnew_ref[:, sl] = v
            kt_sc[pl.ds(h, R, stride=N_ATT_HEADS), :] = k
            vt_sc[pl.ds(h, R, stride=N_ATT_HEADS), :] = v
            q2 = jnp.concatenate([jnp.where(lane < ATT_HEAD_DIM, q, 0.0), jnp.where(lane >= ATT_HEAD_DIM, q, 0.0)], axis=0)
            hi, lo = _split2(q2)
            q_sc[h] = jnp.concatenate([hi, lo], axis=0)
        m_sc[...] = jnp.full(m_sc.shape, -jnp.inf, _F32)
        l_sc[...] = jnp.zeros(l_sc.shape, _F32)
        acc_sc[...] = jnp.zeros(acc_sc.shape, _F32)

    def head_rows(tiles, h):
        x = jnp.concatenate([t[pl.ds(h, page, stride=N_ATT_HEADS), :] for t in tiles], axis=0)
        return _split2(x)

    def process(k_tiles, v_tiles, mask):
        s_parts = []
        for h in range(N_ATT_HEADS):
            k_hi, k_lo = head_rows(k_tiles, h)
            q3 = q_sc[h]
            a = lax.dot_general(q3[0:2 * hr], k_hi, _NT, preferred_element_type=_F32)
            b = lax.dot_general(q3[0:hr], k_lo, _NT, preferred_element_type=_F32)
            s_parts.append(a[0:hr] + a[hr:2 * hr] + b)
        s = jnp.concatenate(s_parts, axis=0)
        if mask is not None:
            s = jnp.where(mask, s, -jnp.inf)
        m_prev = m_sc[...]
        m_new = jnp.maximum(m_prev, jnp.max(s, axis=1, keepdims=True))
        alpha = jnp.exp(m_prev - m_new)
        p = jnp.exp(s - m_new[:, :1])
        l_sc[...] = alpha * l_sc[...] + jnp.sum(p, axis=1, keepdims=True)
        m_sc[...] = m_new
        pv_parts = []
        for h in range(N_ATT_HEADS):
            v_hi, v_lo = head_rows(v_tiles, h)
            p_hi, p_lo = _split2(p[h * hr:(h + 1) * hr])
            a = jnp.dot(jnp.concatenate([p_hi, p_lo], axis=0), v_hi, preferred_element_type=_F32)
            b = jnp.dot(p_hi, v_lo, preferred_element_type=_F32)
            pv_parts.append(a[0:hr] + a[hr:2 * hr] + b)
        acc_sc[...] = alpha * acc_sc[...] + jnp.concatenate(pv_parts, axis=0)

    process(k_refs, v_refs, None)

    @pl.when(j == n_steps - 1)
    def _():
        rows = N_ATT_HEADS * hr
        row = lax.broadcasted_iota(jnp.int32, (rows, page), 0)
        col = lax.broadcasted_iota(jnp.int32, (rows, page), 1)
        qrow = row % R
        process([kt_sc], [vt_sc], (col <= qrow) & (col < n_real))
        o = acc_sc[...] / l_sc[...]
        lam = _lambda(lq1_ref, lk1_ref, lq2_ref, lk2_ref)
        wsub = wsub_ref[...]
        for h in range(N_ATT_HEADS):
            o1 = o[h * hr:h * hr + R]
            o2 = o[h * hr + R:(h + 1) * hr]
            o_ref[:, h * LANES:(h + 1) * LANES] = _diff_finalize(o1, o2, lam, wsub)


def _decode_attention(proj3, cache_k2, cache_v2, page_table, pos_rows, lam_params, w_subln, n_real):
    bsz = proj3.shape[0]
    page = cache_k2.shape[2] // N_ATT_HEADS
    n_pages = page_table.shape[1]
    pg = PAGES_PER_STEP
    n_steps = n_pages // pg
    R = SAMPLE_ROWS
    c, s1, s2 = _rope_tables(pos_rows)
    lq1, lk1, lq2, lk2 = [p.reshape(1, ATT_HEAD_DIM) for p in lam_params]
    fixed = lambda b, j, pt: (0, 0)
    tab = pl.BlockSpec((R, LANES), fixed)
    vec = pl.BlockSpec((1, ATT_HEAD_DIM), fixed)

    def page_spec(i):
        return pl.BlockSpec((None, None, page * N_ATT_HEADS, LANES), lambda b, j, pt: (0, pt[b, j * pg + i], 0, 0))

    out_row = pl.BlockSpec((None, R, D_ATTN), lambda b, j, pt: (b, 0, 0))
    grid_spec = pltpu.PrefetchScalarGridSpec(
        num_scalar_prefetch=1,
        grid=(bsz, n_steps),
        in_specs=[pl.BlockSpec((None, R, 3 * D_ATTN), lambda b, j, pt: (b, 0, 0)), tab, tab, tab, vec, vec, vec, vec,
                  pl.BlockSpec((1, ATT_V_DIM), fixed)]
                 + [page_spec(i) for i in range(pg)] + [page_spec(i) for i in range(pg)],
        out_specs=[out_row, out_row, out_row],
        scratch_shapes=[pltpu.VMEM((N_ATT_HEADS, 4 * R, LANES), _BF16),
                        pltpu.VMEM((N_ATT_HEADS * 2 * R, LANES), _F32), pltpu.VMEM((N_ATT_HEADS * 2 * R, LANES), _F32),
                        pltpu.VMEM((N_ATT_HEADS * 2 * R, LANES), _F32),
                        pltpu.VMEM((page * N_ATT_HEADS, LANES), _F32), pltpu.VMEM((page * N_ATT_HEADS, LANES), _F32)],
    )
    return pl.pallas_call(
        functools.partial(_decode_attn_kernel, n_pages_step=pg, n_steps=n_steps, n_real=n_real),
        grid_spec=grid_spec,
        out_shape=[jax.ShapeDtypeStruct((bsz, R, D_ATTN), _F32)] * 3,
        compiler_params=_cparams(("arbitrary", "arbitrary")),
        name="decode_attn",
    )(page_table, proj3, c, s1, s2, lq1, lk1, lq2, lk2, w_subln.reshape(1, ATT_V_DIM),
      *([cache_k2] * pg), *([cache_v2] * pg))


def kernel(x_prompt, x_sample, cache_k, cache_v, state_conv, state_ssm, page_table, p_prompt, p_sample, w_norm_mix, w_in, lambda_q1, lambda_k1, lambda_q2, lambda_k2, w_subln, conv_w, conv_b, dt_bias, A_log, D_skip, w_ssm_norm, w_out, w_norm_ffn, w_group_router, b_group_router, w_expert_router, b_expert_router, w_exp_gate, w_exp_up, w_exp_down, w_norm_ple, w_ple_gate, w_ple_proj, w_norm_final):
    bp, lp, _ = x_prompt.shape
    bs, ls, _ = x_sample.shape
    past = page_table.shape[1] * cache_k.shape[2]
    tp = bp * lp
    R = SAMPLE_ROWS
    ts = bs * R
    lam_params = (lambda_q1[0], lambda_k1[0], lambda_q2[0], lambda_k2[0])
    pad_r = LANES - N_EXPERT_GROUPS - N_EXPERTS
    w_router = jnp.pad(jnp.concatenate([w_group_router[0], w_expert_router[0]], axis=1), ((0, 0), (0, pad_r)))
    b_router = jnp.pad(jnp.concatenate([b_group_router[0], b_expert_router[0]]), (0, pad_r)).reshape(1, LANES)

    xp2d = x_prompt.reshape(tp, D_MODEL)
    q, kf, kb, vf, vb, z, xbc, dt = _prompt_inproj(xp2d, w_norm_mix[0], w_in[0], lp, 512)
    attn_p = _prompt_attention(q, kb, vb, lam_params, w_subln[0], bp, lp, 512)
    ssd_p, ssm_p = _ssd(xbc, z, dt, jnp.zeros((bp, CONV_WIDTH - 1, CONV_DIM), _F32),
                        jnp.zeros((bp, N_SSM_HEADS, SSM_HEAD_DIM, SSM_STATE), _F32),
                        conv_w[0], conv_b[0], dt_bias[0], A_log[0], D_skip[0], w_ssm_norm[0],
                        bp, lp // SSM_CHUNK, SSM_CHUNK, False, _BF16)
    h1_p, t_p, route_p = _outproj_router(xp2d, attn_p, ssd_p, w_out[0], w_norm_ffn[0], w_router, b_router, 256, False)

    xs3 = jnp.pad(x_sample, ((0, 0), (0, R - ls), (0, 0)))
    xs2d = xs3.reshape(ts, D_MODEL)
    proj_s = _sample_inproj(xs2d, w_norm_mix[0], w_in[0], 512)
    proj3 = proj_s.reshape(bs, R, D_IN_PROJ)
    pos_rows = past + jnp.arange(R, dtype=jnp.int32)
    cache_rows = cache_k.shape[:2] + (cache_k.shape[2] * N_ATT_HEADS, LANES)
    attn_s, k_s, v_s = _decode_attention(proj3, cache_k.reshape(cache_rows), cache_v.reshape(cache_rows), page_table,
                                         pos_rows, lam_params, w_subln[0], ls)
    off = 3 * D_ATTN
    pad_rows = ((0, 0), (0, SSM_CHUNK - R), (0, 0))
    z_s = jnp.pad(proj3[:, :, off:off + D_SSM], pad_rows).reshape(bs * SSM_CHUNK, D_SSM)
    xbc_s3 = proj3[:, :, off + D_SSM:off + D_SSM + CONV_DIM]
    xbc_s = jnp.pad(xbc_s3, pad_rows).reshape(bs * SSM_CHUNK, CONV_DIM)
    dt_s = jnp.pad(proj3[:, :, off + D_SSM + CONV_DIM:], ((0, 0), (0, SSM_CHUNK - R), (0, LANES - N_SSM_HEADS)))
    dt_s = dt_s.reshape(bs * SSM_CHUNK, LANES)
    ssd_s, ssm_s = _ssd(xbc_s, z_s, dt_s, state_conv[0], state_ssm[0], conv_w[0], conv_b[0], dt_bias[0], A_log[0],
                        D_skip[0], w_ssm_norm[0], bs, 1, ls, True, _F32)
    ssd_s = ssd_s.reshape(bs, SSM_CHUNK, D_SSM)[:, :R].reshape(ts, D_SSM)
    h1_s, t_s, route_s = _outproj_router(xs2d, attn_s.reshape(ts, D_ATTN), ssd_s, w_out[0], w_norm_ffn[0],
                                         w_router, b_router, ts, True)

    t_all = jnp.concatenate([t_p, t_s], axis=0)
    route_all = jnp.concatenate([route_p, route_s], axis=0)
    e_ids = route_all[:, :2].astype(jnp.int32)
    y_stk = _expert_mlp(t_all, e_ids, w_exp_gate[0], w_exp_up[0], w_exp_down[0])
    y_p = _combine_ple(h1_p, y_stk, route_p, p_prompt[0].reshape(tp, D_PLE), w_norm_ple[0], w_ple_gate[0],
                       w_ple_proj[0], w_norm_final, 256, 0)
    ps2d = jnp.pad(p_sample[0], ((0, 0), (0, R - ls), (0, 0))).reshape(ts, D_PLE)
    y_s = _combine_ple(h1_s, y_stk, route_s, ps2d, w_norm_ple[0], w_ple_gate[0], w_ple_proj[0], w_norm_final, ts, tp)

    y_prompt = y_p.reshape(bp, lp, D_MODEL)
    y_sample = y_s.reshape(bs, R, D_MODEL)[:, :ls]
    k_prompt = kf.reshape(1, bp, lp, N_ATT_HEADS, 2 * ATT_HEAD_DIM)
    v_prompt = vf.reshape(1, bp, lp, N_ATT_HEADS, ATT_V_DIM)
    conv_prompt = xbc.reshape(bp, lp, CONV_DIM)[:, lp - (CONV_WIDTH - 1):][None]
    ssm_prompt = ssm_p[None]
    k_sample = k_s[:, :ls].reshape(1, bs, ls, N_ATT_HEADS, 2 * ATT_HEAD_DIM)
    v_sample = v_s[:, :ls].reshape(1, bs, ls, N_ATT_HEADS, ATT_V_DIM)
    conv_sample = xbc_s3[:, ls - (CONV_WIDTH - 1):ls][None]
    ssm_sample = ssm_s[None]
    return (y_prompt, y_sample, k_prompt, v_prompt, conv_prompt, ssm_prompt, k_sample, v_sample, conv_sample, ssm_sample)
```

```python
import functools
import math

import jax
import jax.numpy as jnp
from jax import lax
from jax.experimental import pallas as pl
from jax.experimental.pallas import tpu as pltpu

D_MODEL = 2048
D_ATTN = 1024
D_SSM = 1024
ATT_HEAD_DIM = 64
N_ATT_HEADS = 8
ATT_V_DIM = 128
ROT_DIM = 16
ROPE_THETA = 500000.0
SSM_HEAD_DIM = 64
N_SSM_HEADS = 16
SSM_GROUPS = 2
SSM_STATE = 128
SSM_CHUNK = 128
CONV_WIDTH = 4
CONV_DIM = D_SSM + 2 * SSM_GROUPS * SSM_STATE
D_IN_PROJ = 3 * D_ATTN + D_SSM + CONV_DIM + N_SSM_HEADS
N_EXPERT_GROUPS = 4
EXPERTS_PER_GROUP = 8
N_EXPERTS = 32
D_FF_EXPERT = 512
D_PLE = 256
RMS_EPS = 1e-6
LAM_INIT = 0.8 - 0.6 * math.exp(-0.3 * 0)

LANES = 128
SUBLANES = 8
VMEM_LIMIT_BYTES = 56 * 2 ** 20

SAMPLE_ROWS = 8
EXPERT_TILE = 256
PAGES_PER_STEP = 8

_F32 = jnp.float32
_BF16 = jnp.bfloat16
_HI = lax.Precision.HIGHEST
_NT = (((1,), (1,)), ((), ()))


def _cparams(semantics):
    return pltpu.CompilerParams(dimension_semantics=semantics, vmem_limit_bytes=VMEM_LIMIT_BYTES)


def _rms(x, w):
    return x * lax.rsqrt(jnp.mean(x * x, axis=-1, keepdims=True) + RMS_EPS) * w


def _silu(x):
    return x * (1.0 / (1.0 + jnp.exp(-x)))


def _softplus(x):
    return jnp.maximum(x, 0.0) + jnp.log(1.0 + jnp.exp(-jnp.abs(x)))


def _dot(a, b, precise=False):
    if precise:
        return jnp.dot(a.astype(_F32), b.astype(_F32), preferred_element_type=_F32, precision=_HI)
    return jnp.dot(a.astype(_BF16), b.astype(_BF16), preferred_element_type=_F32)


def _dot_nt(a, b, precise=False):
    if precise:
        return lax.dot_general(a.astype(_F32), b.astype(_F32), _NT, preferred_element_type=_F32, precision=_HI)
    return lax.dot_general(a.astype(_BF16), b.astype(_BF16), _NT, preferred_element_type=_F32)


def _rope_block(y, c, s1, s2):
    return y * c + pltpu.roll(y, LANES - ROT_DIM // 2, 1) * s1 + pltpu.roll(y, ROT_DIM // 2, 1) * s2


def _rope_tables(pos):
    half = ROT_DIM // 2
    inv_freq = jnp.power(ROPE_THETA, -jnp.arange(half, dtype=_F32) * (2.0 / ROT_DIM))
    ang = pos.astype(_F32)[:, None] * inv_freq[None, :]
    cos, sin = jnp.cos(ang), jnp.sin(ang)
    n = pos.shape[0]
    ones = jnp.ones((n, ATT_HEAD_DIM - ROT_DIM), _F32)
    zeros = jnp.zeros((n, ATT_HEAD_DIM - ROT_DIM), _F32)
    zh = jnp.zeros((n, half), _F32)
    c = jnp.concatenate([cos, cos, ones], axis=1)
    s1 = jnp.concatenate([-sin, zh, zeros], axis=1)
    s2 = jnp.concatenate([zh, sin, zeros], axis=1)
    return tuple(jnp.concatenate([t, t], axis=1) for t in (c, s1, s2))


def _cast_weights_once(w_ref, wb_sc):
    @pl.when(pl.program_id(0) == 0)
    def _():
        n = w_ref.shape[1]
        if n < wb_sc.shape[1]:
            wb_sc[...] = jnp.zeros(wb_sc.shape, wb_sc.dtype)
        wb_sc[:, 0:n] = w_ref[...].astype(wb_sc.dtype)


def _resident(shape, index_map):
    return pl.BlockSpec(shape, index_map, pipeline_mode=pl.Buffered(1))


def _inproj_qk_kernel(x_ref, wn_ref, w_ref, c_ref, s1_ref, s2_ref, q_ref, kf_ref, kb_ref, wb_sc):
    _cast_weights_once(w_ref, wb_sc)
    u = _rms(x_ref[...], wn_ref[...]).astype(_BF16)
    y = jnp.dot(u, wb_sc[...], preferred_element_type=_F32)
    c, s1, s2 = c_ref[...], s1_ref[...], s2_ref[...]
    q_scale = (ATT_HEAD_DIM ** -0.5) * math.log2(math.e)
    for j in range(2 * N_ATT_HEADS):
        r = _rope_block(y[:, j * LANES:(j + 1) * LANES], c, s1, s2)
        if j < N_ATT_HEADS:
            q_ref[:, j * LANES:(j + 1) * LANES] = (r * q_scale).astype(_BF16)
        else:
            jj = j - N_ATT_HEADS
            kf_ref[pl.ds(jj, r.shape[0], stride=N_ATT_HEADS), :] = r
            kb_ref[:, jj * LANES:(jj + 1) * LANES] = r.astype(_BF16)


def _inproj_vz_kernel(x_ref, wn_ref, w_ref, vf_ref, vb_ref, z_ref, wb_sc):
    _cast_weights_once(w_ref, wb_sc)
    u = _rms(x_ref[...], wn_ref[...]).astype(_BF16)
    y = jnp.dot(u, wb_sc[...], preferred_element_type=_F32)
    v = y[:, :D_ATTN]
    for h in range(N_ATT_HEADS):
        vf_ref[pl.ds(h, v.shape[0], stride=N_ATT_HEADS), :] = v[:, h * LANES:(h + 1) * LANES]
    vb_ref[...] = v.astype(_BF16)
    z_ref[...] = y[:, D_ATTN:].astype(_BF16)


def _inproj_xbc_kernel(x_ref, wn_ref, w_ref, xbc_ref, dt_ref, wb_sc):
    _cast_weights_once(w_ref, wb_sc)
    u = _rms(x_ref[...], wn_ref[...]).astype(_BF16)
    y = jnp.dot(u, wb_sc[...], preferred_element_type=_F32)
    xbc_ref[...] = y[:, :CONV_DIM]
    dt_ref[...] = y[:, CONV_DIM:]


def _prompt_inproj(x2d, w_norm, w_in, seq, tm):
    t = x2d.shape[0]
    nblk = t // tm
    per_seq = seq // tm
    wn = w_norm.reshape(1, D_MODEL)
    row = lambda i: (i, 0)
    fixed = lambda i: (0, 0)
    x_spec = pl.BlockSpec((tm, D_MODEL), row)
    wn_spec = pl.BlockSpec((1, D_MODEL), fixed)
    head_spec = pl.BlockSpec((tm * N_ATT_HEADS, LANES), row)
    wide = 2 * D_ATTN

    c, s1, s2 = _rope_tables(jnp.arange(seq, dtype=jnp.int32))
    tab_spec = pl.BlockSpec((tm, LANES), lambda i: (i % per_seq, 0))
    q, kf, kb = pl.pallas_call(
        _inproj_qk_kernel,
        grid=(nblk,),
        in_specs=[x_spec, wn_spec, _resident((D_MODEL, wide), fixed), tab_spec, tab_spec, tab_spec],
        out_specs=[pl.BlockSpec((tm, D_ATTN), row), head_spec, pl.BlockSpec((tm, D_ATTN), row)],
        out_shape=[jax.ShapeDtypeStruct((t, D_ATTN), _BF16), jax.ShapeDtypeStruct((t * N_ATT_HEADS, LANES), _F32),
                   jax.ShapeDtypeStruct((t, D_ATTN), _BF16)],
        scratch_shapes=[pltpu.VMEM((D_MODEL, wide), _BF16)],
        compiler_params=_cparams(("arbitrary",)),
        name="inproj_qk",
    )(x2d, wn, w_in, c, s1, s2)

    vf, vb, z = pl.pallas_call(
        _inproj_vz_kernel,
        grid=(nblk,),
        in_specs=[x_spec, wn_spec, _resident((D_MODEL, wide), lambda i: (0, 1))],
        out_specs=[head_spec, pl.BlockSpec((tm, D_ATTN), row), pl.BlockSpec((tm, D_SSM), row)],
        out_shape=[jax.ShapeDtypeStruct((t * N_ATT_HEADS, LANES), _F32), jax.ShapeDtypeStruct((t, D_ATTN), _BF16),
                   jax.ShapeDtypeStruct((t, D_SSM), _BF16)],
        scratch_shapes=[pltpu.VMEM((D_MODEL, wide), _BF16)],
        compiler_params=_cparams(("arbitrary",)),
        name="inproj_vz",
    )(x2d, wn, w_in)

    w_tail = w_in[:, 2 * wide:]
    n_tail = CONV_DIM + LANES
    xbc, dt = pl.pallas_call(
        _inproj_xbc_kernel,
        grid=(nblk,),
        in_specs=[x_spec, wn_spec, _resident(w_tail.shape, fixed)],
        out_specs=[pl.BlockSpec((tm, CONV_DIM), row), pl.BlockSpec((tm, LANES), row)],
        out_shape=[jax.ShapeDtypeStruct((t, CONV_DIM), _F32), jax.ShapeDtypeStruct((t, LANES), _F32)],
        scratch_shapes=[pltpu.VMEM((D_MODEL, n_tail), _BF16)],
        compiler_params=_cparams(("arbitrary",)),
        name="inproj_xbc",
    )(x2d, wn, w_tail)
    return q, kf, kb, vf, vb, z, xbc, dt


def _lambda(lq1_ref, lk1_ref, lq2_ref, lk2_ref):
    a = jnp.sum(lq1_ref[...] * lk1_ref[...], axis=-1, keepdims=True)
    b = jnp.sum(lq2_ref[...] * lk2_ref[...], axis=-1, keepdims=True)
    return jnp.exp(a) - jnp.exp(b) + LAM_INIT


def _diff_finalize(o1, o2, lam, wsub):
    a = o1 - lam * o2
    return _rms(a, wsub) * (1.0 - LAM_INIT)


def _attn_kernel(q_ref, k_ref, v_ref, lq1_ref, lk1_ref, lq2_ref, lk2_ref, wsub_ref, o_ref,
                 qt_sc, vt_sc, bias_sc, sa_sc, sb_sc, m_sc, acc_sc, *, tile):
    qi = pl.program_id(2)
    n_kt = vt_sc.shape[0]
    dv = ATT_V_DIM
    masked_out = -1e30

    @pl.when(qi == 0)
    def _():
        for c in range(n_kt):
            vt_sc[c, 0:dv, :] = v_ref[c * tile:(c + 1) * tile, :].astype(_F32).T.astype(_BF16)
            vt_sc[c, dv:, :] = jnp.ones((vt_sc.shape[1] - dv, tile), _BF16)
        key = lax.broadcasted_iota(jnp.int32, (tile, tile), 0)
        qry = lax.broadcasted_iota(jnp.int32, (tile, tile), 1)
        bias_sc[...] = jnp.where(key <= qry, 0.0, masked_out)

    q = q_ref[...].astype(_F32)
    lane = lax.broadcasted_iota(jnp.int32, (tile, LANES), 1)
    qt_sc[0] = jnp.where(lane < ATT_HEAD_DIM, q, 0.0).T.astype(_BF16)
    qt_sc[1] = jnp.where(lane >= ATT_HEAD_DIM, q, 0.0).T.astype(_BF16)
    m_sc[...] = jnp.full(m_sc.shape, -jnp.inf, _F32)
    acc_sc[...] = jnp.zeros(acc_sc.shape, _F32)

    def scores(j, dst):
        k = k_ref[pl.ds(pl.multiple_of(j * tile, tile), tile), :]
        for mm in range(2):
            dst[mm] = jnp.dot(k, qt_sc[mm], preferred_element_type=_F32)

    def softmax_pv(j, src, diagonal=False):
        vt = vt_sc[j]
        for mm in range(2):
            st = src[mm]
            if diagonal:
                st = st + bias_sc[...]
            m_prev = m_sc[mm]
            m_new = jnp.maximum(m_prev, jnp.max(st, axis=0, keepdims=True))
            alpha = jnp.exp2(m_prev - m_new)
            pt = jnp.exp2(st - m_new[0:1, :]).astype(_BF16)
            acc_sc[mm] = alpha[0:1, :] * acc_sc[mm] + jnp.dot(vt, pt, preferred_element_type=_F32)
            m_sc[mm] = m_new

    def pair(jj, carry):
        j0 = 2 * jj
        scores(j0 + 1, sb_sc)
        softmax_pv(j0, sa_sc)
        scores(j0 + 2, sa_sc)
        softmax_pv(j0 + 1, sb_sc)
        return carry

    scores(0, sa_sc)
    lax.fori_loop(0, qi // 2, pair, 0)
    odd = lax.rem(qi, 2) == 1

    @pl.when(odd)
    def _():
        scores(qi, sb_sc)
        softmax_pv(qi - 1, sa_sc)
        softmax_pv(qi, sb_sc, diagonal=True)

    @pl.when(jnp.logical_not(odd))
    def _():
        softmax_pv(qi, sa_sc, diagonal=True)

    outs = []
    for mm in range(2):
        acc = acc_sc[mm]
        outs.append((acc[0:dv, :] / acc[dv:dv + 1, :]).T)
    lam = _lambda(lq1_ref, lk1_ref, lq2_ref, lk2_ref)
    o_ref[...] = _diff_finalize(outs[0], outs[1], lam, wsub_ref[...]).astype(o_ref.dtype)


def _prompt_attention(q, kb, vb, lam_params, w_subln, batch, seq, tile):
    t = q.shape[0]
    tq = tile
    nq = seq // tq
    lq1, lk1, lq2, lk2 = [p.reshape(1, ATT_HEAD_DIM) for p in lam_params]
    vec = pl.BlockSpec((1, ATT_HEAD_DIM), lambda b, h, i: (0, 0))
    kv_spec = pl.BlockSpec((seq, LANES), lambda b, h, i: (b, h))
    qo_spec = pl.BlockSpec((tq, LANES), lambda b, h, i: (b * nq + i, h))
    return pl.pallas_call(
        functools.partial(_attn_kernel, tile=tile),
        grid=(batch, N_ATT_HEADS, nq),
        in_specs=[qo_spec, kv_spec, kv_spec, vec, vec, vec, vec,
                  pl.BlockSpec((1, ATT_V_DIM), lambda b, h, i: (0, 0))],
        out_specs=qo_spec,
        out_shape=jax.ShapeDtypeStruct((t, D_ATTN), _BF16),
        scratch_shapes=[pltpu.VMEM((2, LANES, tile), _BF16),
                        pltpu.VMEM((seq // tile, ATT_V_DIM + 2 * SUBLANES, tile), _BF16),
                        pltpu.VMEM((tile, tile), _F32),
                        pltpu.VMEM((2, tile, tile), _F32), pltpu.VMEM((2, tile, tile), _F32),
                        pltpu.VMEM((2, SUBLANES, tile), _F32),
                        pltpu.VMEM((2, ATT_V_DIM + 2 * SUBLANES, tile), _F32)],
        compiler_params=_cparams(("arbitrary", "arbitrary", "arbitrary")),
        name="prompt_attn",
    )(q, kb, vb, lq1, lk1, lq2, lk2, w_subln.reshape(1, ATT_V_DIM))


def _ssd_kernel(xbc_ref, z_ref, dt_ref, dtt_ref, cprev_ref, h0_ref, convw_ref, convb_ref, dtb_row_ref, dtb_col_ref,
                alog_row_ref, alog_col_ref, dskip_ref, wnorm_ref, y_ref, hout_ref,
                xp_sc, st_sc, yg_sc, *, n_valid, precise):
    c = pl.program_id(1)
    L = SSM_CHUNK
    n_pairs = N_SSM_HEADS // 2
    hp = 2 * SSM_HEAD_DIM

    @pl.when(c == 0)
    def _():
        xp_sc[0:SUBLANES, :] = jnp.zeros((SUBLANES, CONV_DIM), _F32)
        xp_sc[SUBLANES - (CONV_WIDTH - 1):SUBLANES, :] = cprev_ref[...]
        st_sc[...] = h0_ref[...]

    xp_sc[SUBLANES:SUBLANES + L, :] = xbc_ref[...]
    base = SUBLANES - (CONV_WIDTH - 1)
    conv = convb_ref[...] + convw_ref[0:1, :] * xp_sc[base:base + L, :]
    for j in range(1, CONV_WIDTH):
        conv = conv + convw_ref[j:j + 1, :] * xp_sc[base + j:base + j + L, :]
    xp_sc[0:SUBLANES, :] = xp_sc[L:L + SUBLANES, :]
    xc = _silu(conv)
    xs = xc[:, :D_SSM]

    row_i = lax.broadcasted_iota(jnp.int32, (L, LANES), 0)
    col_i = lax.broadcasted_iota(jnp.int32, (L, LANES), 1)
    dt = _softplus(dt_ref[...] + dtb_row_ref[...])
    dtt = _softplus(dtt_ref[...] + dtb_col_ref[...])
    if n_valid < L:
        dt = jnp.where(row_i < n_valid, dt, 0.0)
        dtt = jnp.where(lax.broadcasted_iota(jnp.int32, dtt.shape, 1) < n_valid, dtt, 0.0)
    ad = dt * (-jnp.exp(alog_row_ref[...]))
    adt = dtt * (-jnp.exp(alog_col_ref[...]))
    tril = (col_i <= row_i).astype(_F32)
    triu = (row_i <= col_i).astype(_F32)
    acs = jnp.dot(tril, ad, preferred_element_type=_F32, precision=_HI)
    acst = jnp.dot(adt, triu, preferred_element_type=_F32, precision=_HI)
    causal = col_i <= row_i
    lane_lo = col_i < SSM_HEAD_DIM

    cb = []
    for g in range(SSM_GROUPS):
        bg = xc[:, D_SSM + g * SSM_STATE:D_SSM + (g + 1) * SSM_STATE]
        cg = xc[:, D_SSM + (SSM_GROUPS + g) * SSM_STATE:D_SSM + (SSM_GROUPS + g + 1) * SSM_STATE]
        cb.append((bg, cg, _dot_nt(cg, bg, precise)))

    for p in range(n_pairs):
        ha, hb = 2 * p, 2 * p + 1
        bg, cg, cbg = cb[ha // (N_SSM_HEADS // SSM_GROUPS)]
        col_a, col_b = acs[:, ha:ha + 1], acs[:, hb:hb + 1]
        lm_a = jnp.where(causal, jnp.exp(col_a - acst[ha:ha + 1, :]), 0.0)
        lm_b = jnp.where(causal, jnp.exp(col_b - acst[hb:hb + 1, :]), 0.0)
        dt_pair = jnp.where(lane_lo, dt[:, ha:ha + 1], dt[:, hb:hb + 1])
        x_pair = xs[:, p * hp:(p + 1) * hp] * dt_pair
        y_diag = jnp.where(lane_lo, _dot(cbg * lm_a, x_pair, precise), _dot(cbg * lm_b, x_pair, precise))
        st = st_sc[p]
        y_off = _dot_nt(cg, st, precise) * jnp.where(lane_lo, jnp.exp(col_a), jnp.exp(col_b))
        last_a, last_b = acs[L - 1:L, ha:ha + 1], acs[L - 1:L, hb:hb + 1]
        decay = jnp.where(lane_lo, jnp.exp(last_a - col_a), jnp.exp(last_b - col_b))
        upd = _dot((x_pair * decay).T, bg, precise)
        row_lo = row_i < SSM_HEAD_DIM
        st_sc[p] = jnp.where(row_lo, jnp.exp(last_a), jnp.exp(last_b)) * st + upd
        y_pair = y_diag + y_off + xs[:, p * hp:(p + 1) * hp] * dskip_ref[:, p * hp:(p + 1) * hp]
        zg = z_ref[:, p * hp:(p + 1) * hp].astype(_F32)
        yg_sc[:, p * hp:(p + 1) * hp] = y_pair * _silu(zg)

    gw = D_SSM // SSM_GROUPS
    for g in range(SSM_GROUPS):
        y_ref[:, g * gw:(g + 1) * gw] = _rms(yg_sc[:, g * gw:(g + 1) * gw],
                                             wnorm_ref[:, g * gw:(g + 1) * gw]).astype(y_ref.dtype)

    hout_ref[...] = st_sc[...]


def _ssd(xbc, z, dt, conv_prev, h0, conv_w, conv_b, dt_bias, a_log, d_skip, w_ssm_norm, batch, n_chunks,
         n_valid, precise, out_dtype):
    L = SSM_CHUNK
    rows = xbc.shape[0]
    n_pairs = N_SSM_HEADS // 2
    dtt = dt[:, :N_SSM_HEADS].T
    pad = LANES - N_SSM_HEADS
    dtb_row = jnp.pad(dt_bias, (0, pad)).reshape(1, LANES)
    alog_row = jnp.pad(a_log, (0, pad)).reshape(1, LANES)
    dskip = jnp.repeat(d_skip, SSM_HEAD_DIM).reshape(1, D_SSM)
    blk = lambda b, c: (b * n_chunks + c, 0)
    fixed = lambda b, c: (0, 0)
    y, h_out = pl.pallas_call(
        functools.partial(_ssd_kernel, n_valid=n_valid, precise=precise),
        grid=(batch, n_chunks),
        in_specs=[
            pl.BlockSpec((L, CONV_DIM), blk),
            pl.BlockSpec((L, D_SSM), blk),
            pl.BlockSpec((L, LANES), blk),
            pl.BlockSpec((N_SSM_HEADS, L), lambda b, c: (0, b * n_chunks + c)),
            pl.BlockSpec((None, CONV_WIDTH - 1, CONV_DIM), lambda b, c: (b, 0, 0)),
            pl.BlockSpec((None, n_pairs, 2 * SSM_HEAD_DIM, SSM_STATE), lambda b, c: (b, 0, 0, 0)),
            pl.BlockSpec((CONV_WIDTH, CONV_DIM), fixed),
            pl.BlockSpec((1, CONV_DIM), fixed),
            pl.BlockSpec((1, LANES), fixed),
            pl.BlockSpec((N_SSM_HEADS, 1), fixed),
            pl.BlockSpec((1, LANES), fixed),
            pl.BlockSpec((N_SSM_HEADS, 1), fixed),
            pl.BlockSpec((1, D_SSM), fixed),
            pl.BlockSpec((1, D_SSM), fixed),
        ],
        out_specs=[pl.BlockSpec((L, D_SSM), blk),
                   pl.BlockSpec((None, n_pairs, 2 * SSM_HEAD_DIM, SSM_STATE), lambda b, c: (b, 0, 0, 0))],
        out_shape=[jax.ShapeDtypeStruct((rows, D_SSM), out_dtype),
                   jax.ShapeDtypeStruct((batch, n_pairs, 2 * SSM_HEAD_DIM, SSM_STATE), _F32)],
        scratch_shapes=[pltpu.VMEM((SUBLANES + L, CONV_DIM), _F32),
                        pltpu.VMEM((n_pairs, 2 * SSM_HEAD_DIM, SSM_STATE), _F32),
                        pltpu.VMEM((L, D_SSM), _F32)],
        compiler_params=_cparams(("arbitrary", "arbitrary")),
        name="ssd_precise" if precise else "ssd",
    )(xbc, z, dt, dtt, conv_prev, h0.reshape(batch, n_pairs, 2 * SSM_HEAD_DIM, SSM_STATE),
      conv_w, conv_b.reshape(1, CONV_DIM), dtb_row, dt_bias.reshape(N_SSM_HEADS, 1),
      alog_row, a_log.reshape(N_SSM_HEADS, 1), dskip, w_ssm_norm.reshape(1, D_SSM))
    return y, h_out.reshape(batch, N_SSM_HEADS, SSM_HEAD_DIM, SSM_STATE)


def _outproj_router_kernel(x_ref, attn_ref, ssd_ref, wa_ref, wb_ref, wn_ref, wr_ref, br_ref, *rest, precise, n_blocks):
    if precise:
        h1_ref, t_ref, route_ref = rest[1:4]
        wa, wb = wa_ref, wb_ref
    else:
        h1_ref, t_ref, route_ref, wa, wb = rest
        _cast_weights_once(wa_ref, wa)
        _cast_weights_once(wb_ref, wb)

    @pl.when(pl.program_id(0) >= n_blocks)
    def _():
        t_ref[...] = jnp.zeros(t_ref.shape, _F32)

    @pl.when(pl.program_id(0) < n_blocks)
    def _():
        _outproj_router_body(x_ref, attn_ref, ssd_ref, wa, wb, wn_ref, wr_ref, br_ref, h1_ref, t_ref, route_ref, precise)


def _outproj_router_body(x_ref, attn_ref, ssd_ref, wa, wb, wn_ref, wr_ref, br_ref, h1_ref, t_ref, route_ref, precise):
    h1 = x_ref[...] + _dot(attn_ref[...], wa[...], precise) + _dot(ssd_ref[...], wb[...], precise)
    h1_ref[...] = h1
    t = _rms(h1, wn_ref[...])
    t_ref[...] = t
    logits = _dot(t, wr_ref[...], precise) + br_ref[...]
    lane = lax.broadcasted_iota(jnp.int32, logits.shape, 1).astype(_F32)
    neg = -jnp.inf
    big = float(LANES)
    is_g = lane < N_EXPERT_GROUPS
    gl = jnp.where(is_g, logits, neg)
    gmax = jnp.max(gl, axis=1, keepdims=True)
    g_idx = jnp.min(jnp.where(gl == gmax, lane, big), axis=1, keepdims=True)
    g_w = 1.0 / jnp.sum(jnp.where(is_g, jnp.exp(logits - gmax), 0.0), axis=1, keepdims=True)
    lo = N_EXPERT_GROUPS + EXPERTS_PER_GROUP * g_idx
    l1 = jnp.where(lane >= lo, jnp.where(lane < lo + EXPERTS_PER_GROUP, logits, neg), neg)
    m1 = jnp.max(l1, axis=1, keepdims=True)
    i1 = jnp.min(jnp.where(l1 == m1, lane, big), axis=1, keepdims=True)
    l2 = jnp.where(lane == i1, neg, l1)
    m2 = jnp.max(l2, axis=1, keepdims=True)
    i2 = jnp.min(jnp.where(l2 == m2, lane, big), axis=1, keepdims=True)
    r = jnp.exp(m2 - m1)
    p1 = 1.0 / (1.0 + r)
    w1 = g_w * p1
    w2 = g_w * (r * p1)
    e1 = i1 - N_EXPERT_GROUPS
    e2 = i2 - N_EXPERT_GROUPS
    route_ref[...] = jnp.where(lane == 0, e1, jnp.where(lane == 1, e2, jnp.where(lane == 2, w1,
                               jnp.where(lane == 3, w2, 0.0))))


def _outproj_router(x2d, attn, ssd, w_out, w_norm_ffn, w_router, b_router, tm, precise, t_rows, t_shared=None):
    t = x2d.shape[0]
    n_blocks = t // tm
    wdt = _F32 if precise else _BF16
    half_w = (D_ATTN, D_MODEL)
    fixed = lambda i: (0, 0)
    if precise:
        assert t_shared is not None and (t_rows - t) % tm == 0
        grid = (n_blocks,)
        row = lambda i: (i, 0)
        t_row = lambda i: (i + (t_rows - t) // tm, 0)
        extra_in, extra_args, aliases, scratch = [pl.BlockSpec(memory_space=pl.ANY)], [t_shared], {8: 1}, []
    else:
        grid = (n_blocks + pl.cdiv(t_rows - t, tm),)
        row = lambda i: (jnp.minimum(i, n_blocks - 1), 0)
        t_row = lambda i: (i, 0)
        extra_in, extra_args, aliases = [], [], {}
        scratch = [pltpu.VMEM(half_w, _BF16), pltpu.VMEM(half_w, _BF16)]
    return pl.pallas_call(
        functools.partial(_outproj_router_kernel, precise=precise, n_blocks=n_blocks),
        grid=grid,
        in_specs=[pl.BlockSpec((tm, D_MODEL), row), pl.BlockSpec((tm, D_ATTN), row), pl.BlockSpec((tm, D_SSM), row),
                  _resident(half_w, fixed), _resident(half_w, lambda i: (1, 0)),
                  pl.BlockSpec((1, D_MODEL), fixed), pl.BlockSpec((D_MODEL, LANES), fixed),
                  pl.BlockSpec((1, LANES), fixed)] + extra_in,
        out_specs=[pl.BlockSpec((tm, D_MODEL), row), pl.BlockSpec((tm, D_MODEL), t_row), pl.BlockSpec((tm, LANES), row)],
        out_shape=[jax.ShapeDtypeStruct((t, D_MODEL), _F32), jax.ShapeDtypeStruct((t_rows, D_MODEL), _F32),
                   jax.ShapeDtypeStruct((t, LANES), _F32)],
        scratch_shapes=scratch,
        input_output_aliases=aliases,
        compiler_params=_cparams(("arbitrary",)),
        name="outproj_router_precise" if precise else "outproj_router",
    )(x2d, attn, ssd, w_out, w_out, w_norm_ffn.reshape(1, D_MODEL), w_router.astype(wdt), b_router, *extra_args)


def _expert_kernel(te_ref, nv_ref, idx_prev_ref, idx_ref, idx_next_ref, t_hbm, wg_ref, wu_ref, wd_ref, y_hbm,
                   xbuf, ybuf, sem_in, sem_out, *, tm, n_tiles):
    i = pl.program_id(0)
    nv = nv_ref[i]
    nv_next = jnp.where(i + 1 < n_tiles, nv_ref[jnp.minimum(i + 1, n_tiles - 1)], 0)
    f_half = D_FF_EXPERT // 2
    n_issue_chunks = 4
    per_chunk = tm // n_issue_chunks

    def gather_rows(idx, s, rows):
        for r in rows:
            pltpu.make_async_copy(t_hbm.at[pl.ds(idx[0, 0, r], 1), :], xbuf.at[s, pl.ds(r, 1), :], sem_in.at[s]).start()

    def wait_gather(s):
        pltpu.make_async_copy(t_hbm.at[pl.ds(0, tm), :], xbuf.at[s], sem_in.at[s]).wait()

    def scatter_rows(idx, s, rows):
        for r in rows:
            pltpu.make_async_copy(ybuf.at[s, pl.ds(r, 1), :], y_hbm.at[pl.ds(idx[0, 0, tm + r], 1), :],
                                  sem_out.at[s]).start()

    def wait_scatter(s):
        pltpu.make_async_copy(ybuf.at[s], y_hbm.at[pl.ds(0, tm), :], sem_out.at[s]).wait()

    @pl.when(i == 0)
    def _():
        gather_rows(idx_ref, 0, range(tm))
        half = tm // 2
        n_pad = y_hbm.shape[0] // 2
        ybuf[0, 0:half, :] = jnp.zeros((half, D_MODEL), _F32)
        for k in range(2):
            fill = pltpu.make_async_copy(ybuf.at[0, pl.ds(0, half), :],
                                         y_hbm.at[pl.ds((k + 1) * n_pad - half, half), :], sem_out.at[0])
            fill.start()
            fill.wait()

    def tile(s, has_prev):
        o = 1 - s

        def issue(c):
            rows = range(c * per_chunk, (c + 1) * per_chunk)
            gather_rows(idx_next_ref, o, rows)
            if has_prev:
                scatter_rows(idx_prev_ref, o, rows)

        wait_gather(s)
        x = xbuf[s].astype(_BF16)
        parts = []
        for c in range(2):
            cols = slice(c * f_half, (c + 1) * f_half)
            hg = jnp.dot(x, wg_ref[:, cols].astype(_BF16), preferred_element_type=_F32)
            issue(2 * c)
            hu = jnp.dot(x, wu_ref[:, cols].astype(_BF16), preferred_element_type=_F32)
            issue(2 * c + 1)
            parts.append((_silu(hg) * hu).astype(_BF16))
        hh = jnp.concatenate(parts, axis=1)
        ybuf[s] = jnp.dot(hh, wd_ref[...].astype(_BF16), preferred_element_type=_F32)
        if has_prev:
            wait_scatter(o)

        @pl.when(nv_next == 0)
        def _():
            wait_gather(o)
            scatter_rows(idx_ref, s, range(tm))
            wait_scatter(s)

    used = nv > 0
    odd = lax.rem(i, 2) == 1

    @pl.when(i == 0)
    def _():
        tile(0, False)

    @pl.when(used & (i > 0) & jnp.logical_not(odd))
    def _():
        tile(0, True)

    @pl.when(used & odd)
    def _():
        tile(1, True)


def _moe_plan(e_ids, tm, n_tiles):
    n = e_ids.shape[0]
    n_assign = 2 * n
    half = tm // 2
    n_pad = n + half
    e_flat = e_ids.reshape(-1)
    order = jnp.argsort(e_flat, stable=True).astype(jnp.int32)
    counts = jnp.sum((e_flat[:, None] == jnp.arange(N_EXPERTS, dtype=jnp.int32)[None, :]).astype(jnp.int32), axis=0)
    seg_start = jnp.cumsum(counts) - counts
    tiles_per = (counts + tm - 1) // tm
    cum_tiles = jnp.cumsum(tiles_per)
    tile_start = cum_tiles - tiles_per
    n_used = cum_tiles[-1]
    tile_id = jnp.arange(n_tiles, dtype=jnp.int32)
    used = tile_id < n_used
    te = jnp.sum((cum_tiles[None, :] <= jnp.minimum(tile_id, n_used - 1)[:, None]).astype(jnp.int32), axis=1)
    j = tile_id - tile_start[te]
    nvalid = jnp.where(used, jnp.clip(counts[te] - j * tm, 0, tm), 0).astype(jnp.int32)
    r = jnp.arange(tm, dtype=jnp.int32)[None, :]
    pos = seg_start[te][:, None] + j[:, None] * tm + r
    a = order[jnp.clip(pos, 0, n_assign - 1)]
    tok = a // 2
    spare = (r // half) * n_pad + n + (r % half)
    dst = jnp.where(r < nvalid[:, None], (a % 2) * n_pad + tok, spare)
    idx = jnp.concatenate([tok, dst], axis=1).reshape(n_tiles, 1, 2 * tm).astype(jnp.int32)
    return te, nvalid, idx, n_pad


def _expert_mlp(t_all, e_ids, w_gate, w_up, w_down):
    n = t_all.shape[0]
    tm = EXPERT_TILE
    n_tiles = pl.cdiv(2 * n, tm) + N_EXPERTS
    te, nvalid, idx, n_pad = _moe_plan(e_ids, tm, n_tiles)
    wmap = lambda i, te_ref, nv_ref: (te_ref[i], 0, 0)
    grid_spec = pltpu.PrefetchScalarGridSpec(
        num_scalar_prefetch=2,
        grid=(n_tiles,),
        in_specs=[
            pl.BlockSpec((1, 1, 2 * tm), lambda i, te_ref, nv_ref: (jnp.maximum(i - 1, 0), 0, 0),
                         memory_space=pltpu.SMEM),
            pl.BlockSpec((1, 1, 2 * tm), lambda i, te_ref, nv_ref: (i, 0, 0), memory_space=pltpu.SMEM),
            pl.BlockSpec((1, 1, 2 * tm), lambda i, te_ref, nv_ref: (jnp.minimum(i + 1, n_tiles - 1), 0, 0),
                         memory_space=pltpu.SMEM),
            pl.BlockSpec(memory_space=pl.ANY),
            pl.BlockSpec((None, D_MODEL, D_FF_EXPERT), wmap),
            pl.BlockSpec((None, D_MODEL, D_FF_EXPERT), wmap),
            pl.BlockSpec((None, D_FF_EXPERT, D_MODEL), wmap),
        ],
        out_specs=pl.BlockSpec(memory_space=pl.ANY),
        scratch_shapes=[pltpu.VMEM((2, tm, D_MODEL), _F32), pltpu.VMEM((2, tm, D_MODEL), _F32),
                        pltpu.SemaphoreType.DMA((2,)), pltpu.SemaphoreType.DMA((2,))],
    )
    y_rows = pl.pallas_call(
        functools.partial(_expert_kernel, tm=tm, n_tiles=n_tiles),
        grid_spec=grid_spec,
        out_shape=jax.ShapeDtypeStruct((2 * n_pad, D_MODEL), _F32),
        compiler_params=_cparams(("arbitrary",)),
        name="expert_mlp",
    )(te, nvalid, idx, idx, idx, t_all, w_gate, w_up, w_down)
    return y_rows.reshape(2, n_pad, D_MODEL)


def _combine_ple_kernel(h1_ref, y_ref, route_ref, p_ref, wn_ref, wg_ref, wp_ref, wf_ref, o_ref, wg_sc):
    _cast_weights_once(wg_ref, wg_sc)
    route = route_ref[...]
    h2 = h1_ref[...] + route[:, 2:3] * y_ref[0] + route[:, 3:4] * y_ref[1]
    u = _rms(h2, wn_ref[...])
    gate = jnp.dot(u.astype(_BF16), wg_sc[...], preferred_element_type=_F32)
    gate = 1.0 / (1.0 + jnp.exp(-gate))
    pp = jnp.dot(p_ref[...].astype(_BF16), wp_ref[...], preferred_element_type=_F32)
    h3 = h2 + gate * pp
    o_ref[...] = _rms(h3, wf_ref[...])


def _combine_ple(h1, y_stk, route, p2d, w_norm_ple, w_ple_gate, w_ple_proj, w_norm_final, tm, row_off):
    t = h1.shape[0]
    off = row_off // tm
    row = lambda i: (i, 0)
    fixed = lambda i: (0, 0)
    return pl.pallas_call(
        _combine_ple_kernel,
        grid=(t // tm,),
        in_specs=[pl.BlockSpec((tm, D_MODEL), row),
                  pl.BlockSpec((2, tm, D_MODEL), lambda i: (0, i + off, 0)),
                  pl.BlockSpec((tm, LANES), row), pl.BlockSpec((tm, D_PLE), row),
                  pl.BlockSpec((1, D_MODEL), fixed), _resident((D_MODEL, D_MODEL), fixed),
                  pl.BlockSpec((D_PLE, D_MODEL), fixed), pl.BlockSpec((1, D_MODEL), fixed)],
        out_specs=pl.BlockSpec((tm, D_MODEL), row),
        out_shape=jax.ShapeDtypeStruct((t, D_MODEL), _F32),
        scratch_shapes=[pltpu.VMEM((D_MODEL, D_MODEL), _BF16)],
        compiler_params=_cparams(("arbitrary",)),
        name="combine_ple",
    )(h1, y_stk, route, p2d, w_norm_ple.reshape(1, D_MODEL), w_ple_gate, w_ple_proj.astype(_BF16),
      w_norm_final.reshape(1, D_MODEL))


def _sample_inproj_kernel(x_ref, wn_ref, w_ref, o_ref):
    u = _rms(x_ref[...], wn_ref[...])
    o_ref[...] = jnp.dot(u, w_ref[...], preferred_element_type=_F32, precision=_HI)


def _sample_inproj(xs2d, w_norm, w_in, tn):
    rows = xs2d.shape[0]
    return pl.pallas_call(
        _sample_inproj_kernel,
        grid=(pl.cdiv(D_IN_PROJ, tn),),
        in_specs=[pl.BlockSpec((rows, D_MODEL), lambda j: (0, 0)), pl.BlockSpec((1, D_MODEL), lambda j: (0, 0)),
                  pl.BlockSpec((D_MODEL, tn), lambda j: (0, j))],
        out_specs=pl.BlockSpec((rows, tn), lambda j: (0, j)),
        out_shape=jax.ShapeDtypeStruct((rows, D_IN_PROJ), _F32),
        compiler_params=_cparams(("arbitrary",)),
        name="sample_inproj",
    )(xs2d, w_norm.reshape(1, D_MODEL), w_in)


def _split2(x):
    hi = x.astype(_BF16)
    lo = (x - hi.astype(_F32)).astype(_BF16)
    return hi, lo


def _decode_attn_kernel(pt_ref, qkv_ref, c_ref, s1_ref, s2_ref, lq1_ref, lk1_ref, lq2_ref, lk2_ref, wsub_ref, *rest,
                        n_pages_step, n_steps, n_real):
    k_refs = rest[:n_pages_step]
    v_refs = rest[n_pages_step:2 * n_pages_step]
    o_ref, knew_ref, vnew_ref = rest[2 * n_pages_step:2 * n_pages_step + 3]
    q_sc, m_sc, l_sc, acc_sc, kt_sc, vt_sc = rest[2 * n_pages_step + 3:]
    j = pl.program_id(1)
    R = SAMPLE_ROWS
    hr = 2 * R
    page = kt_sc.shape[0] // N_ATT_HEADS

    @pl.when(j == 0)
    def _():
        c, s1, s2 = c_ref[...], s1_ref[...], s2_ref[...]
        lane = lax.broadcasted_iota(jnp.int32, (R, LANES), 1)
        kt_sc[...] = jnp.zeros(kt_sc.shape, _F32)
        vt_sc[...] = jnp.zeros(vt_sc.shape, _F32)
        for h in range(N_ATT_HEADS):
            sl = slice(h * LANES, (h + 1) * LANES)
            q = _rope_block(qkv_ref[:, sl], c, s1, s2) * (ATT_HEAD_DIM ** -0.5)
            k = _rope_block(qkv_ref[:, D_ATTN + h * LANES:D_ATTN + (h + 1) * LANES], c, s1, s2)
            v = qkv_ref[:, 2 * D_ATTN + h * LANES:2 * D_ATTN + (h + 1) * LANES]
            knew_ref[:, sl] = k
            vnew_ref[:, sl] = v
            kt_sc[pl.ds(h, R, stride=N_ATT_HEADS), :] = k
            vt_sc[pl.ds(h, R, stride=N_ATT_HEADS), :] = v
            q2 = jnp.concatenate([jnp.where(lane < ATT_HEAD_DIM, q, 0.0), jnp.where(lane >= ATT_HEAD_DIM, q, 0.0)], axis=0)
            hi, lo = _split2(q2)
            q_sc[h] = jnp.concatenate([hi, lo], axis=0)
        m_sc[...] = jnp.full(m_sc.shape, -jnp.inf, _F32)
        l_sc[...] = jnp.zeros(l_sc.shape, _F32)
        acc_sc[...] = jnp.zeros(acc_sc.shape, _F32)

    def head_rows(tiles, h):
        x = jnp.concatenate([t[pl.ds(h, page, stride=N_ATT_HEADS), :] for t in tiles], axis=0)
        return _split2(x)

    def process(k_tiles, v_tiles, mask):
        s_parts = []
        for h in range(N_ATT_HEADS):
            k_hi, k_lo = head_rows(k_tiles, h)
            q3 = q_sc[h]
            a = lax.dot_general(q3[0:2 * hr], k_hi, _NT, preferred_element_type=_F32)
            b = lax.dot_general(q3[0:hr], k_lo, _NT, preferred_element_type=_F32)
            s_parts.append(a[0:hr] + a[hr:2 * hr] + b)
        s = jnp.concatenate(s_parts, axis=0)
        if mask is not None:
            s = jnp.where(mask, s, -jnp.inf)
        m_prev = m_sc[...]
        m_new = jnp.maximum(m_prev, jnp.max(s, axis=1, keepdims=True))
        alpha = jnp.exp(m_prev - m_new)
        p = jnp.exp(s - m_new[:, :1])
        l_sc[...] = alpha * l_sc[...] + jnp.sum(p, axis=1, keepdims=True)
        m_sc[...] = m_new
        pv_parts = []
        for h in range(N_ATT_HEADS):
            v_hi, v_lo = head_rows(v_tiles, h)
            p_hi, p_lo = _split2(p[h * hr:(h + 1) * hr])
            a = jnp.dot(jnp.concatenate([p_hi, p_lo], axis=0), v_hi, preferred_element_type=_F32)
            b = jnp.dot(p_hi, v_lo, preferred_element_type=_F32)
            pv_parts.append(a[0:hr] + a[hr:2 * hr] + b)
        acc_sc[...] = alpha * acc_sc[...] + jnp.concatenate(pv_parts, axis=0)

    process(k_refs, v_refs, None)

    @pl.when(j == n_steps - 1)
    def _():
        rows = N_ATT_HEADS * hr
        row = lax.broadcasted_iota(jnp.int32, (rows, page), 0)
        col = lax.broadcasted_iota(jnp.int32, (rows, page), 1)
        qrow = row % R
        process([kt_sc], [vt_sc], (col <= qrow) & (col < n_real))
        o = acc_sc[...] / l_sc[...]
        lam = _lambda(lq1_ref, lk1_ref, lq2_ref, lk2_ref)
        wsub = wsub_ref[...]
        for h in range(N_ATT_HEADS):
            o1 = o[h * hr:h * hr + R]
            o2 = o[h * hr + R:(h + 1) * hr]
            o_ref[:, h * LANES:(h + 1) * LANES] = _diff_finalize(o1, o2, lam, wsub)


def _decode_attention(proj3, cache_k2, cache_v2, page_table, pos_rows, lam_params, w_subln, n_real):
    bsz = proj3.shape[0]
    page = cache_k2.shape[2] // N_ATT_HEADS
    n_pages = page_table.shape[1]
    pg = PAGES_PER_STEP
    n_steps = n_pages // pg
    R = SAMPLE_ROWS
    c, s1, s2 = _rope_tables(pos_rows)
    lq1, lk1, lq2, lk2 = [p.reshape(1, ATT_HEAD_DIM) for p in lam_params]
    fixed = lambda b, j, pt: (0, 0)
    tab = pl.BlockSpec((R, LANES), fixed)
    vec = pl.BlockSpec((1, ATT_HEAD_DIM), fixed)

    def page_spec(i):
        return pl.BlockSpec((None, None, page * N_ATT_HEADS, LANES), lambda b, j, pt: (0, pt[b, j * pg + i], 0, 0))

    out_row = pl.BlockSpec((None, R, D_ATTN), lambda b, j, pt: (b, 0, 0))
    grid_spec = pltpu.PrefetchScalarGridSpec(
        num_scalar_prefetch=1,
        grid=(bsz, n_steps),
        in_specs=[pl.BlockSpec((None, R, 3 * D_ATTN), lambda b, j, pt: (b, 0, 0)), tab, tab, tab, vec, vec, vec, vec,
                  pl.BlockSpec((1, ATT_V_DIM), fixed)]
                 + [page_spec(i) for i in range(pg)] + [page_spec(i) for i in range(pg)],
        out_specs=[out_row, out_row, out_row],
        scratch_shapes=[pltpu.VMEM((N_ATT_HEADS, 4 * R, LANES), _BF16),
                        pltpu.VMEM((N_ATT_HEADS * 2 * R, LANES), _F32), pltpu.VMEM((N_ATT_HEADS * 2 * R, LANES), _F32),
                        pltpu.VMEM((N_ATT_HEADS * 2 * R, LANES), _F32),
                        pltpu.VMEM((page * N_ATT_HEADS, LANES), _F32), pltpu.VMEM((page * N_ATT_HEADS, LANES), _F32)],
    )
    return pl.pallas_call(
        functools.partial(_decode_attn_kernel, n_pages_step=pg, n_steps=n_steps, n_real=n_real),
        grid_spec=grid_spec,
        out_shape=[jax.ShapeDtypeStruct((bsz, R, D_ATTN), _F32)] * 3,
        compiler_params=_cparams(("arbitrary", "arbitrary")),
        name="decode_attn",
    )(page_table, proj3, c, s1, s2, lq1, lk1, lq2, lk2, w_subln.reshape(1, ATT_V_DIM),
      *([cache_k2] * pg), *([cache_v2] * pg))


def kernel(x_prompt, x_sample, cache_k, cache_v, state_conv, state_ssm, page_table, p_prompt, p_sample, w_norm_mix, w_in, lambda_q1, lambda_k1, lambda_q2, lambda_k2, w_subln, conv_w, conv_b, dt_bias, A_log, D_skip, w_ssm_norm, w_out, w_norm_ffn, w_group_router, b_group_router, w_expert_router, b_expert_router, w_exp_gate, w_exp_up, w_exp_down, w_norm_ple, w_ple_gate, w_ple_proj, w_norm_final):
    bp, lp, _ = x_prompt.shape
    bs, ls, _ = x_sample.shape
    past = page_table.shape[1] * cache_k.shape[2]
    tp = bp * lp
    R = SAMPLE_ROWS
    ts = bs * R
    lam_params = (lambda_q1[0], lambda_k1[0], lambda_q2[0], lambda_k2[0])
    pad_r = LANES - N_EXPERT_GROUPS - N_EXPERTS
    w_router = jnp.pad(jnp.concatenate([w_group_router[0], w_expert_router[0]], axis=1), ((0, 0), (0, pad_r)))
    b_router = jnp.pad(jnp.concatenate([b_group_router[0], b_expert_router[0]]), (0, pad_r)).reshape(1, LANES)

    xp2d = x_prompt.reshape(tp, D_MODEL)
    q, kf, kb, vf, vb, z, xbc, dt = _prompt_inproj(xp2d, w_norm_mix[0], w_in[0], lp, 512)
    attn_p = _prompt_attention(q, kb, vb, lam_params, w_subln[0], bp, lp, 512)
    ssd_p, ssm_p = _ssd(xbc, z, dt, jnp.zeros((bp, CONV_WIDTH - 1, CONV_DIM), _F32),
                        jnp.zeros((bp, N_SSM_HEADS, SSM_HEAD_DIM, SSM_STATE), _F32),
                        conv_w[0], conv_b[0], dt_bias[0], A_log[0], D_skip[0], w_ssm_norm[0],
                        bp, lp // SSM_CHUNK, SSM_CHUNK, False, _BF16)
    h1_p, t_all, route_p = _outproj_router(xp2d, attn_p, ssd_p, w_out[0], w_norm_ffn[0], w_router, b_router, 256, False,
                                           tp + ts)

    xs3 = jnp.pad(x_sample, ((0, 0), (0, R - ls), (0, 0)))
    xs2d = xs3.reshape(ts, D_MODEL)
    proj_s = _sample_inproj(xs2d, w_norm_mix[0], w_in[0], 512)
    proj3 = proj_s.reshape(bs, R, D_IN_PROJ)
    pos_rows = past + jnp.arange(R, dtype=jnp.int32)
    cache_rows = cache_k.shape[:2] + (cache_k.shape[2] * N_ATT_HEADS, LANES)
    attn_s, k_s, v_s = _decode_attention(proj3, cache_k.reshape(cache_rows), cache_v.reshape(cache_rows), page_table,
                                         pos_rows, lam_params, w_subln[0], ls)
    off = 3 * D_ATTN
    pad_rows = ((0, 0), (0, SSM_CHUNK - R), (0, 0))
    z_s = jnp.pad(proj3[:, :, off:off + D_SSM], pad_rows).reshape(bs * SSM_CHUNK, D_SSM)
    xbc_s3 = proj3[:, :, off + D_SSM:off + D_SSM + CONV_DIM]
    xbc_s = jnp.pad(xbc_s3, pad_rows).reshape(bs * SSM_CHUNK, CONV_DIM)
    dt_s = jnp.pad(proj3[:, :, off + D_SSM + CONV_DIM:], ((0, 0), (0, SSM_CHUNK - R), (0, LANES - N_SSM_HEADS)))
    dt_s = dt_s.reshape(bs * SSM_CHUNK, LANES)
    ssd_s, ssm_s = _ssd(xbc_s, z_s, dt_s, state_conv[0], state_ssm[0], conv_w[0], conv_b[0], dt_bias[0], A_log[0],
                        D_skip[0], w_ssm_norm[0], bs, 1, ls, True, _F32)
    ssd_s = ssd_s.reshape(bs, SSM_CHUNK, D_SSM)[:, :R].reshape(ts, D_SSM)
    h1_s, t_all, route_s = _outproj_router(xs2d, attn_s.reshape(ts, D_ATTN), ssd_s, w_out[0], w_norm_ffn[0],
                                           w_router, b_router, ts, True, tp + ts, t_shared=t_all)

    e_ids = jnp.concatenate([route_p[:, :2], route_s[:, :2]], axis=0).astype(jnp.int32)
    y_stk = _expert_mlp(t_all, e_ids, w_exp_gate[0], w_exp_up[0], w_exp_down[0])
    y_p = _combine_ple(h1_p, y_stk, route_p, p_prompt[0].reshape(tp, D_PLE), w_norm_ple[0], w_ple_gate[0],
                       w_ple_proj[0], w_norm_final, 256, 0)
    ps2d = jnp.pad(p_sample[0], ((0, 0), (0, R - ls), (0, 0))).reshape(ts, D_PLE)
    y_s = _combine_ple(h1_s, y_stk, route_s, ps2d, w_norm_ple[0], w_ple_gate[0], w_ple_proj[0], w_norm_final, ts, tp)

    y_prompt = y_p.reshape(bp, lp, D_MODEL)
    y_sample = y_s.reshape(bs, R, D_MODEL)[:, :ls]
    k_prompt = kf.reshape(1, bp, lp, N_ATT_HEADS, 2 * ATT_HEAD_DIM)
    v_prompt = vf.reshape(1, bp, lp, N_ATT_HEADS, ATT_V_DIM)
    conv_prompt = xbc.reshape(bp, lp, CONV_DIM)[:, lp - (CONV_WIDTH - 1):][None]
    ssm_prompt = ssm_p[None]
    k_sample = k_s[:, :ls].reshape(1, bs, ls, N_ATT_HEADS, 2 * ATT_HEAD_DIM)
    v_sample = v_s[:, :ls].reshape(1, bs, ls, N_ATT_HEADS, ATT_V_DIM)
    conv_sample = xbc_s3[:, ls - (CONV_WIDTH - 1):ls][None]
    ssm_sample = ssm_s[None]
    return (y_prompt, y_sample, k_prompt, v_prompt, conv_prompt, ssm_prompt, k_sample, v_sample, conv_sample, ssm_sample)
```

```python
import functools
import math

import jax
import jax.numpy as jnp
from jax import lax
from jax.experimental import pallas as pl
from jax.experimental.pallas import tpu as pltpu

D_MODEL = 2048
D_ATTN = 1024
D_SSM = 1024
ATT_HEAD_DIM = 64
N_ATT_HEADS = 8
ATT_V_DIM = 128
ROT_DIM = 16
ROPE_THETA = 500000.0
SSM_HEAD_DIM = 64
N_SSM_HEADS = 16
SSM_GROUPS = 2
SSM_STATE = 128
SSM_CHUNK = 128
CONV_WIDTH = 4
CONV_DIM = D_SSM + 2 * SSM_GROUPS * SSM_STATE
D_IN_PROJ = 3 * D_ATTN + D_SSM + CONV_DIM + N_SSM_HEADS
N_EXPERT_GROUPS = 4
EXPERTS_PER_GROUP = 8
N_EXPERTS = 32
D_FF_EXPERT = 512
D_PLE = 256
RMS_EPS = 1e-6
LAM_INIT = 0.8 - 0.6 * math.exp(-0.3 * 0)

LANES = 128
SUBLANES = 8
VMEM_LIMIT_BYTES = 56 * 2 ** 20

SAMPLE_ROWS = 8
EXPERT_TILE = 256
PAGES_PER_STEP = 8

_F32 = jnp.float32
_BF16 = jnp.bfloat16
_HI = lax.Precision.HIGHEST
_NT = (((1,), (1,)), ((), ()))


def _cparams(semantics):
    return pltpu.CompilerParams(dimension_semantics=semantics, vmem_limit_bytes=VMEM_LIMIT_BYTES)


def _rms(x, w):
    return x * lax.rsqrt(jnp.mean(x * x, axis=-1, keepdims=True) + RMS_EPS) * w


def _silu(x):
    return x * (1.0 / (1.0 + jnp.exp(-x)))


def _softplus(x):
    return jnp.maximum(x, 0.0) + jnp.log(1.0 + jnp.exp(-jnp.abs(x)))


def _dot(a, b, precise=False):
    if precise:
        return jnp.dot(a.astype(_F32), b.astype(_F32), preferred_element_type=_F32, precision=_HI)
    return jnp.dot(a.astype(_BF16), b.astype(_BF16), preferred_element_type=_F32)


def _dot_nt(a, b, precise=False):
    if precise:
        return lax.dot_general(a.astype(_F32), b.astype(_F32), _NT, preferred_element_type=_F32, precision=_HI)
    return lax.dot_general(a.astype(_BF16), b.astype(_BF16), _NT, preferred_element_type=_F32)


def _rope_block(y, c, s1, s2):
    return y * c + pltpu.roll(y, LANES - ROT_DIM // 2, 1) * s1 + pltpu.roll(y, ROT_DIM // 2, 1) * s2


def _rope_tables(pos):
    half = ROT_DIM // 2
    inv_freq = jnp.power(ROPE_THETA, -jnp.arange(half, dtype=_F32) * (2.0 / ROT_DIM))
    ang = pos.astype(_F32)[:, None] * inv_freq[None, :]
    cos, sin = jnp.cos(ang), jnp.sin(ang)
    n = pos.shape[0]
    ones = jnp.ones((n, ATT_HEAD_DIM - ROT_DIM), _F32)
    zeros = jnp.zeros((n, ATT_HEAD_DIM - ROT_DIM), _F32)
    zh = jnp.zeros((n, half), _F32)
    c = jnp.concatenate([cos, cos, ones], axis=1)
    s1 = jnp.concatenate([-sin, zh, zeros], axis=1)
    s2 = jnp.concatenate([zh, sin, zeros], axis=1)
    return tuple(jnp.concatenate([t, t], axis=1) for t in (c, s1, s2))


def _cast_weights_once(w_ref, wb_sc, transposed=False):
    @pl.when(pl.program_id(0) == 0)
    def _():
        w = w_ref[...]
        wb_sc[...] = (w.T if transposed else w).astype(wb_sc.dtype)


def _resident(shape, index_map):
    return pl.BlockSpec(shape, index_map, pipeline_mode=pl.Buffered(1))


def _inproj_qk_kernel(x_ref, wn_ref, w_ref, c_ref, s1_ref, s2_ref, q_ref, kf_ref, kb_ref, wb_sc):
    _cast_weights_once(w_ref, wb_sc, transposed=True)
    u = _rms(x_ref[...], wn_ref[...]).astype(_BF16)
    y = jnp.dot(u, wb_sc[...], preferred_element_type=_F32)
    c, s1, s2 = c_ref[...], s1_ref[...], s2_ref[...]
    q_scale = (ATT_HEAD_DIM ** -0.5) * math.log2(math.e)
    for j in range(2 * N_ATT_HEADS):
        r = _rope_block(y[:, j * LANES:(j + 1) * LANES], c, s1, s2)
        if j < N_ATT_HEADS:
            q_ref[:, j * LANES:(j + 1) * LANES] = (r * q_scale).astype(_BF16)
        else:
            jj = j - N_ATT_HEADS
            kf_ref[pl.ds(jj, r.shape[0], stride=N_ATT_HEADS), :] = r
            kb_ref[:, jj * LANES:(jj + 1) * LANES] = r.astype(_BF16)


def _inproj_vz_kernel(x_ref, wn_ref, w_ref, vf_ref, vb_ref, z_ref, wb_sc):
    _cast_weights_once(w_ref, wb_sc, transposed=True)
    u = _rms(x_ref[...], wn_ref[...]).astype(_BF16)
    y = jnp.dot(u, wb_sc[...], preferred_element_type=_F32)
    v = y[:, :D_ATTN]
    for h in range(N_ATT_HEADS):
        vf_ref[pl.ds(h, v.shape[0], stride=N_ATT_HEADS), :] = v[:, h * LANES:(h + 1) * LANES]
    vb_ref[...] = v.astype(_BF16)
    z_ref[...] = y[:, D_ATTN:].astype(_BF16)


def _inproj_xbc_kernel(x_ref, wn_ref, w_ref, xbc_ref, dt_ref, wb_sc):
    _cast_weights_once(w_ref, wb_sc, transposed=True)
    u = _rms(x_ref[...], wn_ref[...]).astype(_BF16)
    y = jnp.dot(u, wb_sc[...], preferred_element_type=_F32)
    xbc_ref[...] = y[:, :CONV_DIM]
    dt_ref[...] = y[:, CONV_DIM:]


def _prompt_inproj(x2d, w_norm, w_in_t, seq, tm):
    t = x2d.shape[0]
    nblk = t // tm
    per_seq = seq // tm
    wn = w_norm.reshape(1, D_MODEL)
    row = lambda i: (i, 0)
    fixed = lambda i: (0, 0)
    x_spec = pl.BlockSpec((tm, D_MODEL), row)
    wn_spec = pl.BlockSpec((1, D_MODEL), fixed)
    head_spec = pl.BlockSpec((tm * N_ATT_HEADS, LANES), row)
    wide = 2 * D_ATTN

    c, s1, s2 = _rope_tables(jnp.arange(seq, dtype=jnp.int32))
    tab_spec = pl.BlockSpec((tm, LANES), lambda i: (i % per_seq, 0))
    q, kf, kb = pl.pallas_call(
        _inproj_qk_kernel,
        grid=(nblk,),
        in_specs=[x_spec, wn_spec, _resident((wide, D_MODEL), fixed), tab_spec, tab_spec, tab_spec],
        out_specs=[pl.BlockSpec((tm, D_ATTN), row), head_spec, pl.BlockSpec((tm, D_ATTN), row)],
        out_shape=[jax.ShapeDtypeStruct((t, D_ATTN), _BF16), jax.ShapeDtypeStruct((t * N_ATT_HEADS, LANES), _F32),
                   jax.ShapeDtypeStruct((t, D_ATTN), _BF16)],
        scratch_shapes=[pltpu.VMEM((D_MODEL, wide), _BF16)],
        compiler_params=_cparams(("arbitrary",)),
        name="inproj_qk",
    )(x2d, wn, w_in_t, c, s1, s2)

    vf, vb, z = pl.pallas_call(
        _inproj_vz_kernel,
        grid=(nblk,),
        in_specs=[x_spec, wn_spec, _resident((wide, D_MODEL), lambda i: (1, 0))],
        out_specs=[head_spec, pl.BlockSpec((tm, D_ATTN), row), pl.BlockSpec((tm, D_SSM), row)],
        out_shape=[jax.ShapeDtypeStruct((t * N_ATT_HEADS, LANES), _F32), jax.ShapeDtypeStruct((t, D_ATTN), _BF16),
                   jax.ShapeDtypeStruct((t, D_SSM), _BF16)],
        scratch_shapes=[pltpu.VMEM((D_MODEL, wide), _BF16)],
        compiler_params=_cparams(("arbitrary",)),
        name="inproj_vz",
    )(x2d, wn, w_in_t)

    n_tail = CONV_DIM + LANES
    w_tail = w_in_t[2 * wide:]
    w_tail = jnp.pad(w_tail, ((0, n_tail - w_tail.shape[0]), (0, 0)))
    xbc, dt = pl.pallas_call(
        _inproj_xbc_kernel,
        grid=(nblk,),
        in_specs=[x_spec, wn_spec, _resident(w_tail.shape, fixed)],
        out_specs=[pl.BlockSpec((tm, CONV_DIM), row), pl.BlockSpec((tm, LANES), row)],
        out_shape=[jax.ShapeDtypeStruct((t, CONV_DIM), _F32), jax.ShapeDtypeStruct((t, LANES), _F32)],
        scratch_shapes=[pltpu.VMEM((D_MODEL, n_tail), _BF16)],
        compiler_params=_cparams(("arbitrary",)),
        name="inproj_xbc",
    )(x2d, wn, w_tail)
    return q, kf, kb, vf, vb, z, xbc, dt


def _lambda(lq1_ref, lk1_ref, lq2_ref, lk2_ref):
    a = jnp.sum(lq1_ref[...] * lk1_ref[...], axis=-1, keepdims=True)
    b = jnp.sum(lq2_ref[...] * lk2_ref[...], axis=-1, keepdims=True)
    return jnp.exp(a) - jnp.exp(b) + LAM_INIT


def _diff_finalize(o1, o2, lam, wsub):
    a = o1 - lam * o2
    return _rms(a, wsub) * (1.0 - LAM_INIT)


def _attn_kernel(q_ref, k_ref, v_ref, lq1_ref, lk1_ref, lq2_ref, lk2_ref, wsub_ref, o_ref,
                 qt_sc, vt_sc, bias_sc, sa_sc, sb_sc, m_sc, acc_sc, *, tile):
    qi = pl.program_id(2)
    n_kt = vt_sc.shape[0]
    dv = ATT_V_DIM
    masked_out = -1e30

    @pl.when(qi == 0)
    def _():
        for c in range(n_kt):
            vt_sc[c, 0:dv, :] = v_ref[c * tile:(c + 1) * tile, :].astype(_F32).T.astype(_BF16)
            vt_sc[c, dv:, :] = jnp.ones((vt_sc.shape[1] - dv, tile), _BF16)
        key = lax.broadcasted_iota(jnp.int32, (tile, tile), 0)
        qry = lax.broadcasted_iota(jnp.int32, (tile, tile), 1)
        bias_sc[...] = jnp.where(key <= qry, 0.0, masked_out)

    q = q_ref[...].astype(_F32)
    lane = lax.broadcasted_iota(jnp.int32, (tile, LANES), 1)
    qt_sc[0] = jnp.where(lane < ATT_HEAD_DIM, q, 0.0).T.astype(_BF16)
    qt_sc[1] = jnp.where(lane >= ATT_HEAD_DIM, q, 0.0).T.astype(_BF16)
    m_sc[...] = jnp.full(m_sc.shape, -jnp.inf, _F32)
    acc_sc[...] = jnp.zeros(acc_sc.shape, _F32)

    def scores(j, dst):
        k = k_ref[pl.ds(pl.multiple_of(j * tile, tile), tile), :]
        for mm in range(2):
            dst[mm] = jnp.dot(k, qt_sc[mm], preferred_element_type=_F32)

    def softmax_pv(j, src, diagonal=False):
        vt = vt_sc[j]
        for mm in range(2):
            st = src[mm]
            if diagonal:
                st = st + bias_sc[...]
            m_prev = m_sc[mm]
            m_new = jnp.maximum(m_prev, jnp.max(st, axis=0, keepdims=True))
            alpha = jnp.exp2(m_prev - m_new)
            pt = jnp.exp2(st - m_new[0:1, :]).astype(_BF16)
            acc_sc[mm] = alpha[0:1, :] * acc_sc[mm] + jnp.dot(vt, pt, preferred_element_type=_F32)
            m_sc[mm] = m_new

    def pair(jj, carry):
        j0 = 2 * jj
        scores(j0 + 1, sb_sc)
        softmax_pv(j0, sa_sc)
        scores(j0 + 2, sa_sc)
        softmax_pv(j0 + 1, sb_sc)
        return carry

    scores(0, sa_sc)
    lax.fori_loop(0, qi // 2, pair, 0)
    odd = lax.rem(qi, 2) == 1

    @pl.when(odd)
    def _():
        scores(qi, sb_sc)
        softmax_pv(qi - 1, sa_sc)
        softmax_pv(qi, sb_sc, diagonal=True)

    @pl.when(jnp.logical_not(odd))
    def _():
        softmax_pv(qi, sa_sc, diagonal=True)

    outs = []
    for mm in range(2):
        acc = acc_sc[mm]
        outs.append((acc[0:dv, :] / acc[dv:dv + 1, :]).T)
    lam = _lambda(lq1_ref, lk1_ref, lq2_ref, lk2_ref)
    o_ref[...] = _diff_finalize(outs[0], outs[1], lam, wsub_ref[...]).astype(o_ref.dtype)


def _prompt_attention(q, kb, vb, lam_params, w_subln, batch, seq, tile):
    t = q.shape[0]
    tq = tile
    nq = seq // tq
    lq1, lk1, lq2, lk2 = [p.reshape(1, ATT_HEAD_DIM) for p in lam_params]
    vec = pl.BlockSpec((1, ATT_HEAD_DIM), lambda b, h, i: (0, 0))
    kv_spec = pl.BlockSpec((seq, LANES), lambda b, h, i: (b, h))
    qo_spec = pl.BlockSpec((tq, LANES), lambda b, h, i: (b * nq + i, h))
    return pl.pallas_call(
        functools.partial(_attn_kernel, tile=tile),
        grid=(batch, N_ATT_HEADS, nq),
        in_specs=[qo_spec, kv_spec, kv_spec, vec, vec, vec, vec,
                  pl.BlockSpec((1, ATT_V_DIM), lambda b, h, i: (0, 0))],
        out_specs=qo_spec,
        out_shape=jax.ShapeDtypeStruct((t, D_ATTN), _BF16),
        scratch_shapes=[pltpu.VMEM((2, LANES, tile), _BF16),
                        pltpu.VMEM((seq // tile, ATT_V_DIM + 2 * SUBLANES, tile), _BF16),
                        pltpu.VMEM((tile, tile), _F32),
                        pltpu.VMEM((2, tile, tile), _F32), pltpu.VMEM((2, tile, tile), _F32),
                        pltpu.VMEM((2, SUBLANES, tile), _F32),
                        pltpu.VMEM((2, ATT_V_DIM + 2 * SUBLANES, tile), _F32)],
        compiler_params=_cparams(("arbitrary", "arbitrary", "arbitrary")),
        name="prompt_attn",
    )(q, kb, vb, lq1, lk1, lq2, lk2, w_subln.reshape(1, ATT_V_DIM))


def _ssd_kernel(xbc_ref, z_ref, dt_ref, dtt_ref, cprev_ref, h0_ref, convw_ref, convb_ref, dtb_row_ref, dtb_col_ref,
                alog_row_ref, alog_col_ref, dskip_ref, wnorm_ref, y_ref, hout_ref,
                xp_sc, st_sc, yg_sc, *, n_valid, precise):
    c = pl.program_id(1)
    L = SSM_CHUNK
    n_pairs = N_SSM_HEADS // 2
    hp = 2 * SSM_HEAD_DIM

    @pl.when(c == 0)
    def _():
        xp_sc[0:SUBLANES, :] = jnp.zeros((SUBLANES, CONV_DIM), _F32)
        xp_sc[SUBLANES - (CONV_WIDTH - 1):SUBLANES, :] = cprev_ref[...]
        st_sc[...] = h0_ref[...]

    xp_sc[SUBLANES:SUBLANES + L, :] = xbc_ref[...]
    base = SUBLANES - (CONV_WIDTH - 1)
    conv = convb_ref[...] + convw_ref[0:1, :] * xp_sc[base:base + L, :]
    for j in range(1, CONV_WIDTH):
        conv = conv + convw_ref[j:j + 1, :] * xp_sc[base + j:base + j + L, :]
    xp_sc[0:SUBLANES, :] = xp_sc[L:L + SUBLANES, :]
    xc = _silu(conv)
    xs = xc[:, :D_SSM]

    row_i = lax.broadcasted_iota(jnp.int32, (L, LANES), 0)
    col_i = lax.broadcasted_iota(jnp.int32, (L, LANES), 1)
    dt = _softplus(dt_ref[...] + dtb_row_ref[...])
    dtt = _softplus(dtt_ref[...] + dtb_col_ref[...])
    if n_valid < L:
        dt = jnp.where(row_i < n_valid, dt, 0.0)
        dtt = jnp.where(lax.broadcasted_iota(jnp.int32, dtt.shape, 1) < n_valid, dtt, 0.0)
    ad = dt * (-jnp.exp(alog_row_ref[...]))
    adt = dtt * (-jnp.exp(alog_col_ref[...]))
    tril = (col_i <= row_i).astype(_F32)
    triu = (row_i <= col_i).astype(_F32)
    acs = jnp.dot(tril, ad, preferred_element_type=_F32, precision=_HI)
    acst = jnp.dot(adt, triu, preferred_element_type=_F32, precision=_HI)
    causal = col_i <= row_i
    lane_lo = col_i < SSM_HEAD_DIM

    cb = []
    for g in range(SSM_GROUPS):
        bg = xc[:, D_SSM + g * SSM_STATE:D_SSM + (g + 1) * SSM_STATE]
        cg = xc[:, D_SSM + (SSM_GROUPS + g) * SSM_STATE:D_SSM + (SSM_GROUPS + g + 1) * SSM_STATE]
        cb.append((bg, cg, _dot_nt(cg, bg, precise)))

    for p in range(n_pairs):
        ha, hb = 2 * p, 2 * p + 1
        bg, cg, cbg = cb[ha // (N_SSM_HEADS // SSM_GROUPS)]
        col_a, col_b = acs[:, ha:ha + 1], acs[:, hb:hb + 1]
        lm_a = jnp.where(causal, jnp.exp(col_a - acst[ha:ha + 1, :]), 0.0)
        lm_b = jnp.where(causal, jnp.exp(col_b - acst[hb:hb + 1, :]), 0.0)
        dt_pair = jnp.where(lane_lo, dt[:, ha:ha + 1], dt[:, hb:hb + 1])
        x_pair = xs[:, p * hp:(p + 1) * hp] * dt_pair
        y_diag = jnp.where(lane_lo, _dot(cbg * lm_a, x_pair, precise), _dot(cbg * lm_b, x_pair, precise))
        st = st_sc[p]
        y_off = _dot_nt(cg, st, precise) * jnp.where(lane_lo, jnp.exp(col_a), jnp.exp(col_b))
        last_a, last_b = acs[L - 1:L, ha:ha + 1], acs[L - 1:L, hb:hb + 1]
        decay = jnp.where(lane_lo, jnp.exp(last_a - col_a), jnp.exp(last_b - col_b))
        upd = _dot((x_pair * decay).T, bg, precise)
        row_lo = row_i < SSM_HEAD_DIM
        st_sc[p] = jnp.where(row_lo, jnp.exp(last_a), jnp.exp(last_b)) * st + upd
        y_pair = y_diag + y_off + xs[:, p * hp:(p + 1) * hp] * dskip_ref[:, p * hp:(p + 1) * hp]
        zg = z_ref[:, p * hp:(p + 1) * hp].astype(_F32)
        yg_sc[:, p * hp:(p + 1) * hp] = y_pair * _silu(zg)

    gw = D_SSM // SSM_GROUPS
    for g in range(SSM_GROUPS):
        y_ref[:, g * gw:(g + 1) * gw] = _rms(yg_sc[:, g * gw:(g + 1) * gw],
                                             wnorm_ref[:, g * gw:(g + 1) * gw]).astype(y_ref.dtype)

    hout_ref[...] = st_sc[...]


def _ssd(xbc, z, dt, conv_prev, h0, conv_w, conv_b, dt_bias, a_log, d_skip, w_ssm_norm, batch, n_chunks,
         n_valid, precise, out_dtype):
    L = SSM_CHUNK
    rows = xbc.shape[0]
    n_pairs = N_SSM_HEADS // 2
    dtt = dt[:, :N_SSM_HEADS].T
    pad = LANES - N_SSM_HEADS
    dtb_row = jnp.pad(dt_bias, (0, pad)).reshape(1, LANES)
    alog_row = jnp.pad(a_log, (0, pad)).reshape(1, LANES)
    dskip = jnp.repeat(d_skip, SSM_HEAD_DIM).reshape(1, D_SSM)
    blk = lambda b, c: (b * n_chunks + c, 0)
    fixed = lambda b, c: (0, 0)
    y, h_out = pl.pallas_call(
        functools.partial(_ssd_kernel, n_valid=n_valid, precise=precise),
        grid=(batch, n_chunks),
        in_specs=[
            pl.BlockSpec((L, CONV_DIM), blk),
            pl.BlockSpec((L, D_SSM), blk),
            pl.BlockSpec((L, LANES), blk),
            pl.BlockSpec((N_SSM_HEADS, L), lambda b, c: (0, b * n_chunks + c)),
            pl.BlockSpec((None, CONV_WIDTH - 1, CONV_DIM), lambda b, c: (b, 0, 0)),
            pl.BlockSpec((None, n_pairs, 2 * SSM_HEAD_DIM, SSM_STATE), lambda b, c: (b, 0, 0, 0)),
            pl.BlockSpec((CONV_WIDTH, CONV_DIM), fixed),
            pl.BlockSpec((1, CONV_DIM), fixed),
            pl.BlockSpec((1, LANES), fixed),
            pl.BlockSpec((N_SSM_HEADS, 1), fixed),
            pl.BlockSpec((1, LANES), fixed),
            pl.BlockSpec((N_SSM_HEADS, 1), fixed),
            pl.BlockSpec((1, D_SSM), fixed),
            pl.BlockSpec((1, D_SSM), fixed),
        ],
        out_specs=[pl.BlockSpec((L, D_SSM), blk),
                   pl.BlockSpec((None, n_pairs, 2 * SSM_HEAD_DIM, SSM_STATE), lambda b, c: (b, 0, 0, 0))],
        out_shape=[jax.ShapeDtypeStruct((rows, D_SSM), out_dtype),
                   jax.ShapeDtypeStruct((batch, n_pairs, 2 * SSM_HEAD_DIM, SSM_STATE), _F32)],
        scratch_shapes=[pltpu.VMEM((SUBLANES + L, CONV_DIM), _F32),
                        pltpu.VMEM((n_pairs, 2 * SSM_HEAD_DIM, SSM_STATE), _F32),
                        pltpu.VMEM((L, D_SSM), _F32)],
        compiler_params=_cparams(("arbitrary", "arbitrary")),
        name="ssd_precise" if precise else "ssd",
    )(xbc, z, dt, dtt, conv_prev, h0.reshape(batch, n_pairs, 2 * SSM_HEAD_DIM, SSM_STATE),
      conv_w, conv_b.reshape(1, CONV_DIM), dtb_row, dt_bias.reshape(N_SSM_HEADS, 1),
      alog_row, a_log.reshape(N_SSM_HEADS, 1), dskip, w_ssm_norm.reshape(1, D_SSM))
    return y, h_out.reshape(batch, N_SSM_HEADS, SSM_HEAD_DIM, SSM_STATE)


def _outproj_router_kernel(x_ref, attn_ref, ssd_ref, wa_ref, wb_ref, wn_ref, wr_ref, br_ref, *rest, precise, n_blocks):
    if precise:
        h1_ref, t_ref, route_ref = rest[1:4]
        wa, wb = wa_ref, wb_ref
    else:
        h1_ref, t_ref, route_ref, wa, wb = rest
        _cast_weights_once(wa_ref, wa)
        _cast_weights_once(wb_ref, wb)

    @pl.when(pl.program_id(0) >= n_blocks)
    def _():
        t_ref[...] = jnp.zeros(t_ref.shape, _F32)

    @pl.when(pl.program_id(0) < n_blocks)
    def _():
        _outproj_router_body(x_ref, attn_ref, ssd_ref, wa, wb, wn_ref, wr_ref, br_ref, h1_ref, t_ref, route_ref, precise)


def _outproj_router_body(x_ref, attn_ref, ssd_ref, wa, wb, wn_ref, wr_ref, br_ref, h1_ref, t_ref, route_ref, precise):
    h1 = x_ref[...] + _dot(attn_ref[...], wa[...], precise) + _dot(ssd_ref[...], wb[...], precise)
    h1_ref[...] = h1
    t = _rms(h1, wn_ref[...])
    t_ref[...] = t
    logits = _dot(t, wr_ref[...], precise) + br_ref[...]
    lane = lax.broadcasted_iota(jnp.int32, logits.shape, 1).astype(_F32)
    neg = -jnp.inf
    big = float(LANES)
    is_g = lane < N_EXPERT_GROUPS
    gl = jnp.where(is_g, logits, neg)
    gmax = jnp.max(gl, axis=1, keepdims=True)
    g_idx = jnp.min(jnp.where(gl == gmax, lane, big), axis=1, keepdims=True)
    g_w = 1.0 / jnp.sum(jnp.where(is_g, jnp.exp(logits - gmax), 0.0), axis=1, keepdims=True)
    lo = N_EXPERT_GROUPS + EXPERTS_PER_GROUP * g_idx
    l1 = jnp.where(lane >= lo, jnp.where(lane < lo + EXPERTS_PER_GROUP, logits, neg), neg)
    m1 = jnp.max(l1, axis=1, keepdims=True)
    i1 = jnp.min(jnp.where(l1 == m1, lane, big), axis=1, keepdims=True)
    l2 = jnp.where(lane == i1, neg, l1)
    m2 = jnp.max(l2, axis=1, keepdims=True)
    i2 = jnp.min(jnp.where(l2 == m2, lane, big), axis=1, keepdims=True)
    r = jnp.exp(m2 - m1)
    p1 = 1.0 / (1.0 + r)
    w1 = g_w * p1
    w2 = g_w * (r * p1)
    e1 = i1 - N_EXPERT_GROUPS
    e2 = i2 - N_EXPERT_GROUPS
    route_ref[...] = jnp.where(lane == 0, e1, jnp.where(lane == 1, e2, jnp.where(lane == 2, w1,
                               jnp.where(lane == 3, w2, 0.0))))


def _outproj_router(x2d, attn, ssd, w_out, w_norm_ffn, w_router, b_router, tm, precise, t_rows, t_shared=None):
    t = x2d.shape[0]
    n_blocks = t // tm
    wdt = _F32 if precise else _BF16
    half_w = (D_ATTN, D_MODEL)
    fixed = lambda i: (0, 0)
    if precise:
        assert t_shared is not None and (t_rows - t) % tm == 0
        grid = (n_blocks,)
        row = lambda i: (i, 0)
        t_row = lambda i: (i + (t_rows - t) // tm, 0)
        extra_in, extra_args, aliases, scratch = [pl.BlockSpec(memory_space=pl.ANY)], [t_shared], {8: 1}, []
    else:
        grid = (n_blocks + pl.cdiv(t_rows - t, tm),)
        row = lambda i: (jnp.minimum(i, n_blocks - 1), 0)
        t_row = lambda i: (i, 0)
        extra_in, extra_args, aliases = [], [], {}
        scratch = [pltpu.VMEM(half_w, _BF16), pltpu.VMEM(half_w, _BF16)]
    return pl.pallas_call(
        functools.partial(_outproj_router_kernel, precise=precise, n_blocks=n_blocks),
        grid=grid,
        in_specs=[pl.BlockSpec((tm, D_MODEL), row), pl.BlockSpec((tm, D_ATTN), row), pl.BlockSpec((tm, D_SSM), row),
                  _resident(half_w, fixed), _resident(half_w, lambda i: (1, 0)),
                  pl.BlockSpec((1, D_MODEL), fixed), pl.BlockSpec((D_MODEL, LANES), fixed),
                  pl.BlockSpec((1, LANES), fixed)] + extra_in,
        out_specs=[pl.BlockSpec((tm, D_MODEL), row), pl.BlockSpec((tm, D_MODEL), t_row), pl.BlockSpec((tm, LANES), row)],
        out_shape=[jax.ShapeDtypeStruct((t, D_MODEL), _F32), jax.ShapeDtypeStruct((t_rows, D_MODEL), _F32),
                   jax.ShapeDtypeStruct((t, LANES), _F32)],
        scratch_shapes=scratch,
        input_output_aliases=aliases,
        compiler_params=_cparams(("arbitrary",)),
        name="outproj_router_precise" if precise else "outproj_router",
    )(x2d, attn, ssd, w_out, w_out, w_norm_ffn.reshape(1, D_MODEL), w_router.astype(wdt), b_router, *extra_args)


def _expert_kernel(te_ref, nv_ref, idx_prev_ref, idx_ref, idx_next_ref, t_hbm, wg_ref, wu_ref, wd_ref, y_hbm,
                   xbuf, ybuf, sem_in, sem_out, *, tm, n_tiles):
    i = pl.program_id(0)
    nv = nv_ref[i]
    nv_next = jnp.where(i + 1 < n_tiles, nv_ref[jnp.minimum(i + 1, n_tiles - 1)], 0)
    f_half = D_FF_EXPERT // 2
    n_issue_chunks = 4
    per_chunk = tm // n_issue_chunks

    def gather_rows(idx, s, rows):
        for r in rows:
            pltpu.make_async_copy(t_hbm.at[pl.ds(idx[0, 0, r], 1), :], xbuf.at[s, pl.ds(r, 1), :], sem_in.at[s]).start()

    def wait_gather(s):
        pltpu.make_async_copy(t_hbm.at[pl.ds(0, tm), :], xbuf.at[s], sem_in.at[s]).wait()

    def scatter_rows(idx, s, rows):
        for r in rows:
            pltpu.make_async_copy(ybuf.at[s, pl.ds(r, 1), :], y_hbm.at[pl.ds(idx[0, 0, tm + r], 1), :],
                                  sem_out.at[s]).start(priority=1)

    def wait_scatter(s):
        pltpu.make_async_copy(ybuf.at[s], y_hbm.at[pl.ds(0, tm), :], sem_out.at[s]).wait()

    @pl.when(i == 0)
    def _():
        gather_rows(idx_ref, 0, range(tm))
        half = tm // 2
        n_pad = y_hbm.shape[0] // 2
        ybuf[0, 0:half, :] = jnp.zeros((half, D_MODEL), _F32)
        for k in range(2):
            fill = pltpu.make_async_copy(ybuf.at[0, pl.ds(0, half), :],
                                         y_hbm.at[pl.ds((k + 1) * n_pad - half, half), :], sem_out.at[0])
            fill.start()
            fill.wait()

    def tile(s, has_prev):
        o = 1 - s

        def issue(c):
            rows = range(c * per_chunk, (c + 1) * per_chunk)
            gather_rows(idx_next_ref, o, rows)
            if has_prev:
                scatter_rows(idx_prev_ref, o, rows)

        wait_gather(s)
        x = xbuf[s].astype(_BF16)
        parts = []
        for c in range(2):
            cols = slice(c * f_half, (c + 1) * f_half)
            hg = jnp.dot(x, wg_ref[:, cols].astype(_BF16), preferred_element_type=_F32)
            issue(2 * c)
            hu = jnp.dot(x, wu_ref[:, cols].astype(_BF16), preferred_element_type=_F32)
            issue(2 * c + 1)
            parts.append((_silu(hg) * hu).astype(_BF16))
        hh = jnp.concatenate(parts, axis=1)
        ybuf[s] = jnp.dot(hh, wd_ref[...].astype(_BF16), preferred_element_type=_F32)
        if has_prev:
            wait_scatter(o)

        @pl.when(nv_next == 0)
        def _():
            wait_gather(o)
            scatter_rows(idx_ref, s, range(tm))
            wait_scatter(s)

    used = nv > 0
    odd = lax.rem(i, 2) == 1

    @pl.when(i == 0)
    def _():
        tile(0, False)

    @pl.when(used & (i > 0) & jnp.logical_not(odd))
    def _():
        tile(0, True)

    @pl.when(used & odd)
    def _():
        tile(1, True)


def _moe_plan(e_ids, tm, n_tiles):
    n = e_ids.shape[0]
    n_assign = 2 * n
    half = tm // 2
    n_pad = n + half
    e_flat = e_ids.reshape(-1)
    order = jnp.argsort(e_flat, stable=True).astype(jnp.int32)
    counts = jnp.sum((e_flat[:, None] == jnp.arange(N_EXPERTS, dtype=jnp.int32)[None, :]).astype(jnp.int32), axis=0)
    seg_start = jnp.cumsum(counts) - counts
    tiles_per = (counts + tm - 1) // tm
    cum_tiles = jnp.cumsum(tiles_per)
    tile_start = cum_tiles - tiles_per
    n_used = cum_tiles[-1]
    tile_id = jnp.arange(n_tiles, dtype=jnp.int32)
    used = tile_id < n_used
    te = jnp.sum((cum_tiles[None, :] <= jnp.minimum(tile_id, n_used - 1)[:, None]).astype(jnp.int32), axis=1)
    j = tile_id - tile_start[te]
    nvalid = jnp.where(used, jnp.clip(counts[te] - j * tm, 0, tm), 0).astype(jnp.int32)
    r = jnp.arange(tm, dtype=jnp.int32)[None, :]
    pos = seg_start[te][:, None] + j[:, None] * tm + r
    a = order[jnp.clip(pos, 0, n_assign - 1)]
    tok = a // 2
    spare = (r // half) * n_pad + n + (r % half)
    dst = jnp.where(r < nvalid[:, None], (a % 2) * n_pad + tok, spare)
    idx = jnp.concatenate([tok, dst], axis=1).reshape(n_tiles, 1, 2 * tm).astype(jnp.int32)
    return te, nvalid, idx, n_pad


def _expert_mlp(t_all, e_ids, w_gate, w_up, w_down):
    n = t_all.shape[0]
    tm = EXPERT_TILE
    n_tiles = pl.cdiv(2 * n, tm) + N_EXPERTS
    te, nvalid, idx, n_pad = _moe_plan(e_ids, tm, n_tiles)
    wmap = lambda i, te_ref, nv_ref: (te_ref[i], 0, 0)
    grid_spec = pltpu.PrefetchScalarGridSpec(
        num_scalar_prefetch=2,
        grid=(n_tiles,),
        in_specs=[
            pl.BlockSpec((1, 1, 2 * tm), lambda i, te_ref, nv_ref: (jnp.maximum(i - 1, 0), 0, 0),
                         memory_space=pltpu.SMEM),
            pl.BlockSpec((1, 1, 2 * tm), lambda i, te_ref, nv_ref: (i, 0, 0), memory_space=pltpu.SMEM),
            pl.BlockSpec((1, 1, 2 * tm), lambda i, te_ref, nv_ref: (jnp.minimum(i + 1, n_tiles - 1), 0, 0),
                         memory_space=pltpu.SMEM),
            pl.BlockSpec(memory_space=pl.ANY),
            pl.BlockSpec((None, D_MODEL, D_FF_EXPERT), wmap),
            pl.BlockSpec((None, D_MODEL, D_FF_EXPERT), wmap),
            pl.BlockSpec((None, D_FF_EXPERT, D_MODEL), wmap),
        ],
        out_specs=pl.BlockSpec(memory_space=pl.ANY),
        scratch_shapes=[pltpu.VMEM((2, tm, D_MODEL), _F32), pltpu.VMEM((2, tm, D_MODEL), _F32),
                        pltpu.SemaphoreType.DMA((2,)), pltpu.SemaphoreType.DMA((2,))],
    )
    y_rows = pl.pallas_call(
        functools.partial(_expert_kernel, tm=tm, n_tiles=n_tiles),
        grid_spec=grid_spec,
        out_shape=jax.ShapeDtypeStruct((2 * n_pad, D_MODEL), _F32),
        compiler_params=_cparams(("arbitrary",)),
        name="expert_mlp",
    )(te, nvalid, idx, idx, idx, t_all, w_gate, w_up, w_down)
    return y_rows.reshape(2, n_pad, D_MODEL)


def _combine_ple_kernel(h1_ref, y_ref, route_ref, p_ref, wn_ref, wg_ref, wp_ref, wf_ref, o_ref, wg_sc):
    _cast_weights_once(wg_ref, wg_sc)
    route = route_ref[...]
    h2 = h1_ref[...] + route[:, 2:3] * y_ref[0] + route[:, 3:4] * y_ref[1]
    u = _rms(h2, wn_ref[...])
    gate = jnp.dot(u.astype(_BF16), wg_sc[...], preferred_element_type=_F32)
    gate = 1.0 / (1.0 + jnp.exp(-gate))
    pp = jnp.dot(p_ref[...].astype(_BF16), wp_ref[...], preferred_element_type=_F32)
    h3 = h2 + gate * pp
    o_ref[...] = _rms(h3, wf_ref[...])


def _combine_ple(h1, y_stk, route, p2d, w_norm_ple, w_ple_gate, w_ple_proj, w_norm_final, tm, row_off):
    t = h1.shape[0]
    off = row_off // tm
    row = lambda i: (i, 0)
    fixed = lambda i: (0, 0)
    return pl.pallas_call(
        _combine_ple_kernel,
        grid=(t // tm,),
        in_specs=[pl.BlockSpec((tm, D_MODEL), row),
                  pl.BlockSpec((2, tm, D_MODEL), lambda i: (0, i + off, 0)),
                  pl.BlockSpec((tm, LANES), row), pl.BlockSpec((tm, D_PLE), row),
                  pl.BlockSpec((1, D_MODEL), fixed), _resident((D_MODEL, D_MODEL), fixed),
                  pl.BlockSpec((D_PLE, D_MODEL), fixed), pl.BlockSpec((1, D_MODEL), fixed)],
        out_specs=pl.BlockSpec((tm, D_MODEL), row),
        out_shape=jax.ShapeDtypeStruct((t, D_MODEL), _F32),
        scratch_shapes=[pltpu.VMEM((D_MODEL, D_MODEL), _BF16)],
        compiler_params=_cparams(("arbitrary",)),
        name="combine_ple",
    )(h1, y_stk, route, p2d, w_norm_ple.reshape(1, D_MODEL), w_ple_gate, w_ple_proj.astype(_BF16),
      w_norm_final.reshape(1, D_MODEL))


def _sample_inproj_kernel(x_ref, wn_ref, w_ref, o_ref):
    u = _rms(x_ref[...], wn_ref[...])
    o_ref[...] = lax.dot_general(u, w_ref[...], _NT, preferred_element_type=_F32, precision=_HI)


def _sample_inproj(xs2d, w_norm, w_in_t, tn):
    rows = xs2d.shape[0]
    return pl.pallas_call(
        _sample_inproj_kernel,
        grid=(pl.cdiv(D_IN_PROJ, tn),),
        in_specs=[pl.BlockSpec((rows, D_MODEL), lambda j: (0, 0)), pl.BlockSpec((1, D_MODEL), lambda j: (0, 0)),
                  pl.BlockSpec((tn, D_MODEL), lambda j: (j, 0))],
        out_specs=pl.BlockSpec((rows, tn), lambda j: (0, j)),
        out_shape=jax.ShapeDtypeStruct((rows, D_IN_PROJ), _F32),
        compiler_params=_cparams(("arbitrary",)),
        name="sample_inproj",
    )(xs2d, w_norm.reshape(1, D_MODEL), w_in_t)


def _split2(x):
    hi = x.astype(_BF16)
    lo = (x - hi.astype(_F32)).astype(_BF16)
    return hi, lo


def _decode_attn_kernel(pt_ref, qkv_ref, c_ref, s1_ref, s2_ref, lq1_ref, lk1_ref, lq2_ref, lk2_ref, wsub_ref, *rest,
                        n_pages_step, n_steps, n_real):
    k_refs = rest[:n_pages_step]
    v_refs = rest[n_pages_step:2 * n_pages_step]
    o_ref, knew_ref, vnew_ref = rest[2 * n_pages_step:2 * n_pages_step + 3]
    q_sc, m_sc, l_sc, acc_sc, kt_sc, vt_sc = rest[2 * n_pages_step + 3:]
    j = pl.program_id(1)
    R = SAMPLE_ROWS
    hr = 2 * R
    page = kt_sc.shape[0] // N_ATT_HEADS

    @pl.when(j == 0)
    def _():
        c, s1, s2 = c_ref[...], s1_ref[...], s2_ref[...]
        lane = lax.broadcasted_iota(jnp.int32, (R, LANES), 1)
        kt_sc[...] = jnp.zeros(kt_sc.shape, _F32)
        vt_sc[...] = jnp.zeros(vt_sc.shape, _F32)
        for h in range(N_ATT_HEADS):
            sl = slice(h * LANES, (h + 1) * LANES)
            q = _rope_block(qkv_ref[:, sl], c, s1, s2) * (ATT_HEAD_DIM ** -0.5)
            k = _rope_block(qkv_ref[:, D_ATTN + h * LANES:D_ATTN + (h + 1) * LANES], c, s1, s2)
            v = qkv_ref[:, 2 * D_ATTN + h * LANES:2 * D_ATTN + (h + 1) * LANES]
            knew_ref[:, sl] = k
            vnew_ref[:, sl] = v
            kt_sc[pl.ds(h, R, stride=N_ATT_HEADS), :] = k
            vt_sc[pl.ds(h, R, stride=N_ATT_HEADS), :] = v
            q2 = jnp.concatenate([jnp.where(lane < ATT_HEAD_DIM, q, 0.0), jnp.where(lane >= ATT_HEAD_DIM, q, 0.0)], axis=0)
            hi, lo = _split2(q2)
            q_sc[h] = jnp.concatenate([hi, lo], axis=0)
        m_sc[...] = jnp.full(m_sc.shape, -jnp.inf, _F32)
        l_sc[...] = jnp.zeros(l_sc.shape, _F32)
        acc_sc[...] = jnp.zeros(acc_sc.shape, _F32)

    def head_rows(tiles, h):
        x = jnp.concatenate([t[pl.ds(h, page, stride=N_ATT_HEADS), :] for t in tiles], axis=0)
        return _split2(x)

    def process(k_tiles, v_tiles, mask):
        s_parts = []
        for h in range(N_ATT_HEADS):
            k_hi, k_lo = head_rows(k_tiles, h)
            q3 = q_sc[h]
            a = lax.dot_general(q3[0:2 * hr], k_hi, _NT, preferred_element_type=_F32)
            b = lax.dot_general(q3[0:hr], k_lo, _NT, preferred_element_type=_F32)
            s_parts.append(a[0:hr] + a[hr:2 * hr] + b)
        s = jnp.concatenate(s_parts, axis=0)
        if mask is not None:
            s = jnp.where(mask, s, -jnp.inf)
        m_prev = m_sc[...]
        m_new = jnp.maximum(m_prev, jnp.max(s, axis=1, keepdims=True))
        alpha = jnp.exp(m_prev - m_new)
        p = jnp.exp(s - m_new[:, :1])
        l_sc[...] = alpha * l_sc[...] + jnp.sum(p, axis=1, keepdims=True)
        m_sc[...] = m_new
        pv_parts = []
        for h in range(N_ATT_HEADS):
            v_hi, v_lo = head_rows(v_tiles, h)
            p_hi, p_lo = _split2(p[h * hr:(h + 1) * hr])
            a = jnp.dot(jnp.concatenate([p_hi, p_lo], axis=0), v_hi, preferred_element_type=_F32)
            b = jnp.dot(p_hi, v_lo, preferred_element_type=_F32)
            pv_parts.append(a[0:hr] + a[hr:2 * hr] + b)
        acc_sc[...] = alpha * acc_sc[...] + jnp.concatenate(pv_parts, axis=0)

    process(k_refs, v_refs, None)

    @pl.when(j == n_steps - 1)
    def _():
        rows = N_ATT_HEADS * hr
        row = lax.broadcasted_iota(jnp.int32, (rows, page), 0)
        col = lax.broadcasted_iota(jnp.int32, (rows, page), 1)
        qrow = row % R
        process([kt_sc], [vt_sc], (col <= qrow) & (col < n_real))
        o = acc_sc[...] / l_sc[...]
        lam = _lambda(lq1_ref, lk1_ref, lq2_ref, lk2_ref)
        wsub = wsub_ref[...]
        for h in range(N_ATT_HEADS):
            o1 = o[h * hr:h * hr + R]
            o2 = o[h * hr + R:(h + 1) * hr]
            o_ref[:, h * LANES:(h + 1) * LANES] = _diff_finalize(o1, o2, lam, wsub)


def _decode_attention(proj3, cache_k2, cache_v2, page_table, pos_rows, lam_params, w_subln, n_real):
    bsz = proj3.shape[0]
    page = cache_k2.shape[2] // N_ATT_HEADS
    n_pages = page_table.shape[1]
    pg = PAGES_PER_STEP
    n_steps = n_pages // pg
    R = SAMPLE_ROWS
    c, s1, s2 = _rope_tables(pos_rows)
    lq1, lk1, lq2, lk2 = [p.reshape(1, ATT_HEAD_DIM) for p in lam_params]
    fixed = lambda b, j, pt: (0, 0)
    tab = pl.BlockSpec((R, LANES), fixed)
    vec = pl.BlockSpec((1, ATT_HEAD_DIM), fixed)

    def page_spec(i):
        return pl.BlockSpec((None, None, page * N_ATT_HEADS, LANES), lambda b, j, pt: (0, pt[b, j * pg + i], 0, 0))

    out_row = pl.BlockSpec((None, R, D_ATTN), lambda b, j, pt: (b, 0, 0))
    grid_spec = pltpu.PrefetchScalarGridSpec(
        num_scalar_prefetch=1,
        grid=(bsz, n_steps),
        in_specs=[pl.BlockSpec((None, R, 3 * D_ATTN), lambda b, j, pt: (b, 0, 0)), tab, tab, tab, vec, vec, vec, vec,
                  pl.BlockSpec((1, ATT_V_DIM), fixed)]
                 + [page_spec(i) for i in range(pg)] + [page_spec(i) for i in range(pg)],
        out_specs=[out_row, out_row, out_row],
        scratch_shapes=[pltpu.VMEM((N_ATT_HEADS, 4 * R, LANES), _BF16),
                        pltpu.VMEM((N_ATT_HEADS * 2 * R, LANES), _F32), pltpu.VMEM((N_ATT_HEADS * 2 * R, LANES), _F32),
                        pltpu.VMEM((N_ATT_HEADS * 2 * R, LANES), _F32),
                        pltpu.VMEM((page * N_ATT_HEADS, LANES), _F32), pltpu.VMEM((page * N_ATT_HEADS, LANES), _F32)],
    )
    return pl.pallas_call(
        functools.partial(_decode_attn_kernel, n_pages_step=pg, n_steps=n_steps, n_real=n_real),
        grid_spec=grid_spec,
        out_shape=[jax.ShapeDtypeStruct((bsz, R, D_ATTN), _F32)] * 3,
        compiler_params=_cparams(("arbitrary", "arbitrary")),
        name="decode_attn",
    )(page_table, proj3, c, s1, s2, lq1, lk1, lq2, lk2, w_subln.reshape(1, ATT_V_DIM),
      *([cache_k2] * pg), *([cache_v2] * pg))


def kernel(x_prompt, x_sample, cache_k, cache_v, state_conv, state_ssm, page_table, p_prompt, p_sample, w_norm_mix, w_in, lambda_q1, lambda_k1, lambda_q2, lambda_k2, w_subln, conv_w, conv_b, dt_bias, A_log, D_skip, w_ssm_norm, w_out, w_norm_ffn, w_group_router, b_group_router, w_expert_router, b_expert_router, w_exp_gate, w_exp_up, w_exp_down, w_norm_ple, w_ple_gate, w_ple_proj, w_norm_final):
    bp, lp, _ = x_prompt.shape
    bs, ls, _ = x_sample.shape
    past = page_table.shape[1] * cache_k.shape[2]
    tp = bp * lp
    R = SAMPLE_ROWS
    ts = bs * R
    lam_params = (lambda_q1[0], lambda_k1[0], lambda_q2[0], lambda_k2[0])
    pad_r = LANES - N_EXPERT_GROUPS - N_EXPERTS
    w_router = jnp.pad(jnp.concatenate([w_group_router[0], w_expert_router[0]], axis=1), ((0, 0), (0, pad_r)))
    b_router = jnp.pad(jnp.concatenate([b_group_router[0], b_expert_router[0]]), (0, pad_r)).reshape(1, LANES)

    xp2d = x_prompt.reshape(tp, D_MODEL)
    w_in_t = w_in[0].T
    q, kf, kb, vf, vb, z, xbc, dt = _prompt_inproj(xp2d, w_norm_mix[0], w_in_t, lp, 512)
    attn_p = _prompt_attention(q, kb, vb, lam_params, w_subln[0], bp, lp, 512)
    ssd_p, ssm_p = _ssd(xbc, z, dt, jnp.zeros((bp, CONV_WIDTH - 1, CONV_DIM), _F32),
                        jnp.zeros((bp, N_SSM_HEADS, SSM_HEAD_DIM, SSM_STATE), _F32),
                        conv_w[0], conv_b[0], dt_bias[0], A_log[0], D_skip[0], w_ssm_norm[0],
                        bp, lp // SSM_CHUNK, SSM_CHUNK, False, _BF16)
    h1_p, t_all, route_p = _outproj_router(xp2d, attn_p, ssd_p, w_out[0], w_norm_ffn[0], w_router, b_router, 256, False,
                                           tp + ts)

    xs3 = jnp.pad(x_sample, ((0, 0), (0, R - ls), (0, 0)))
    xs2d = xs3.reshape(ts, D_MODEL)
    proj_s = _sample_inproj(xs2d, w_norm_mix[0], w_in_t, 512)
    proj3 = proj_s.reshape(bs, R, D_IN_PROJ)
    pos_rows = past + jnp.arange(R, dtype=jnp.int32)
    cache_rows = cache_k.shape[:2] + (cache_k.shape[2] * N_ATT_HEADS, LANES)
    attn_s, k_s, v_s = _decode_attention(proj3, cache_k.reshape(cache_rows), cache_v.reshape(cache_rows), page_table,
                                         pos_rows, lam_params, w_subln[0], ls)
    off = 3 * D_ATTN
    pad_rows = ((0, 0), (0, SSM_CHUNK - R), (0, 0))
    z_s = jnp.pad(proj3[:, :, off:off + D_SSM], pad_rows).reshape(bs * SSM_CHUNK, D_SSM)
    xbc_s3 = proj3[:, :, off + D_SSM:off + D_SSM + CONV_DIM]
    xbc_s = jnp.pad(xbc_s3, pad_rows).reshape(bs * SSM_CHUNK, CONV_DIM)
    dt_s = jnp.pad(proj3[:, :, off + D_SSM + CONV_DIM:], ((0, 0), (0, SSM_CHUNK - R), (0, LANES - N_SSM_HEADS)))
    dt_s = dt_s.reshape(bs * SSM_CHUNK, LANES)
    ssd_s, ssm_s = _ssd(xbc_s, z_s, dt_s, state_conv[0], state_ssm[0], conv_w[0], conv_b[0], dt_bias[0], A_log[0],
                        D_skip[0], w_ssm_norm[0], bs, 1, ls, True, _F32)
    ssd_s = ssd_s.reshape(bs, SSM_CHUNK, D_SSM)[:, :R].reshape(ts, D_SSM)
    h1_s, t_all, route_s = _outproj_router(xs2d, attn_s.reshape(ts, D_ATTN), ssd_s, w_out[0], w_norm_ffn[0],
                                           w_router, b_router, ts, True, tp + ts, t_shared=t_all)

    e_ids = jnp.concatenate([route_p[:, :2], route_s[:, :2]], axis=0).astype(jnp.int32)
    y_stk = _expert_mlp(t_all, e_ids, w_exp_gate[0], w_exp_up[0], w_exp_down[0])
    y_p = _combine_ple(h1_p, y_stk, route_p, p_prompt[0].reshape(tp, D_PLE), w_norm_ple[0], w_ple_gate[0],
                       w_ple_proj[0], w_norm_final, 256, 0)
    ps2d = jnp.pad(p_sample[0], ((0, 0), (0, R - ls), (0, 0))).reshape(ts, D_PLE)
    y_s = _combine_ple(h1_s, y_stk, route_s, ps2d, w_norm_ple[0], w_ple_gate[0], w_ple_proj[0], w_norm_final, ts, tp)

    y_prompt = y_p.reshape(bp, lp, D_MODEL)
    y_sample = y_s.reshape(bs, R, D_MODEL)[:, :ls]
    k_prompt = kf.reshape(1, bp, lp, N_ATT_HEADS, 2 * ATT_HEAD_DIM)
    v_prompt = vf.reshape(1, bp, lp, N_ATT_HEADS, ATT_V_DIM)
    conv_prompt = xbc.reshape(bp, lp, CONV_DIM)[:, lp - (CONV_WIDTH - 1):][None]
    ssm_prompt = ssm_p[None]
    k_sample = k_s[:, :ls].reshape(1, bs, ls, N_ATT_HEADS, 2 * ATT_HEAD_DIM)
    v_sample = v_s[:, :ls].reshape(1, bs, ls, N_ATT_HEADS, ATT_V_DIM)
    conv_sample = xbc_s3[:, ls - (CONV_WIDTH - 1):ls][None]
    ssm_sample = ssm_s[None]
    return (y_prompt, y_sample, k_prompt, v_prompt, conv_prompt, ssm_prompt, k_sample, v_sample, conv_sample, ssm_sample)
```

```python
import functools
import math

import jax
import jax.numpy as jnp
from jax import lax
from jax.experimental import pallas as pl
from jax.experimental.pallas import tpu as pltpu

D_MODEL = 2048
D_ATTN = 1024
D_SSM = 1024
ATT_HEAD_DIM = 64
N_ATT_HEADS = 8
ATT_V_DIM = 128
ROT_DIM = 16
ROPE_THETA = 500000.0
SSM_HEAD_DIM = 64
N_SSM_HEADS = 16
SSM_GROUPS = 2
SSM_STATE = 128
SSM_CHUNK = 128
CONV_WIDTH = 4
CONV_DIM = D_SSM + 2 * SSM_GROUPS * SSM_STATE
D_IN_PROJ = 3 * D_ATTN + D_SSM + CONV_DIM + N_SSM_HEADS
N_EXPERT_GROUPS = 4
EXPERTS_PER_GROUP = 8
N_EXPERTS = 32
D_FF_EXPERT = 512
D_PLE = 256
RMS_EPS = 1e-6
LAM_INIT = 0.8 - 0.6 * math.exp(-0.3 * 0)

LANES = 128
SUBLANES = 8
VMEM_LIMIT_BYTES = 56 * 2 ** 20

SAMPLE_ROWS = 8
EXPERT_TILE = 256
PAGES_PER_STEP = 8

_F32 = jnp.float32
_BF16 = jnp.bfloat16
_HI = lax.Precision.HIGHEST
_NT = (((1,), (1,)), ((), ()))


def _cparams(semantics):
    return pltpu.CompilerParams(dimension_semantics=semantics, vmem_limit_bytes=VMEM_LIMIT_BYTES)


def _rms(x, w):
    return x * lax.rsqrt(jnp.mean(x * x, axis=-1, keepdims=True) + RMS_EPS) * w


def _silu(x):
    return x * (1.0 / (1.0 + jnp.exp(-x)))


def _softplus(x):
    return jnp.maximum(x, 0.0) + jnp.log(1.0 + jnp.exp(-jnp.abs(x)))


def _dot(a, b, precise=False):
    if precise:
        return jnp.dot(a.astype(_F32), b.astype(_F32), preferred_element_type=_F32, precision=_HI)
    return jnp.dot(a.astype(_BF16), b.astype(_BF16), preferred_element_type=_F32)


def _dot_nt(a, b, precise=False):
    if precise:
        return lax.dot_general(a.astype(_F32), b.astype(_F32), _NT, preferred_element_type=_F32, precision=_HI)
    return lax.dot_general(a.astype(_BF16), b.astype(_BF16), _NT, preferred_element_type=_F32)


def _rope_block(y, c, s1, s2):
    return y * c + pltpu.roll(y, LANES - ROT_DIM // 2, 1) * s1 + pltpu.roll(y, ROT_DIM // 2, 1) * s2


def _rope_tables(pos):
    half = ROT_DIM // 2
    inv_freq = jnp.power(ROPE_THETA, -jnp.arange(half, dtype=_F32) * (2.0 / ROT_DIM))
    ang = pos.astype(_F32)[:, None] * inv_freq[None, :]
    cos, sin = jnp.cos(ang), jnp.sin(ang)
    n = pos.shape[0]
    ones = jnp.ones((n, ATT_HEAD_DIM - ROT_DIM), _F32)
    zeros = jnp.zeros((n, ATT_HEAD_DIM - ROT_DIM), _F32)
    zh = jnp.zeros((n, half), _F32)
    c = jnp.concatenate([cos, cos, ones], axis=1)
    s1 = jnp.concatenate([-sin, zh, zeros], axis=1)
    s2 = jnp.concatenate([zh, sin, zeros], axis=1)
    return tuple(jnp.concatenate([t, t], axis=1) for t in (c, s1, s2))


def _cast_weights_once(w_ref, wb_sc, transposed=False, valid_out=None):
    @pl.when(pl.program_id(0) == 0)
    def _():
        if transposed:
            w = w_ref[0:wb_sc.shape[1], :]
            if valid_out is not None:
                w = jnp.where(lax.broadcasted_iota(jnp.int32, w.shape, 0) < valid_out, w, 0.0)
            w = w.T
        else:
            w = w_ref[...]
        wb_sc[...] = w.astype(wb_sc.dtype)


def _resident(shape, index_map):
    return pl.BlockSpec(shape, index_map, pipeline_mode=pl.Buffered(1))


def _inproj_qk_kernel(x_ref, wn_ref, w_ref, c_ref, s1_ref, s2_ref, q_ref, kf_ref, kb_ref, wb_sc):
    _cast_weights_once(w_ref, wb_sc, transposed=True)
    u = _rms(x_ref[...], wn_ref[...]).astype(_BF16)
    y = jnp.dot(u, wb_sc[...], preferred_element_type=_F32)
    c, s1, s2 = c_ref[...], s1_ref[...], s2_ref[...]
    q_scale = (ATT_HEAD_DIM ** -0.5) * math.log2(math.e)
    for j in range(2 * N_ATT_HEADS):
        r = _rope_block(y[:, j * LANES:(j + 1) * LANES], c, s1, s2)
        if j < N_ATT_HEADS:
            q_ref[:, j * LANES:(j + 1) * LANES] = (r * q_scale).astype(_BF16)
        else:
            jj = j - N_ATT_HEADS
            kf_ref[pl.ds(jj, r.shape[0], stride=N_ATT_HEADS), :] = r
            kb_ref[:, jj * LANES:(jj + 1) * LANES] = r.astype(_BF16)


def _inproj_vz_kernel(x_ref, wn_ref, w_ref, vf_ref, vb_ref, z_ref, wb_sc):
    _cast_weights_once(w_ref, wb_sc, transposed=True)
    u = _rms(x_ref[...], wn_ref[...]).astype(_BF16)
    y = jnp.dot(u, wb_sc[...], preferred_element_type=_F32)
    v = y[:, :D_ATTN]
    for h in range(N_ATT_HEADS):
        vf_ref[pl.ds(h, v.shape[0], stride=N_ATT_HEADS), :] = v[:, h * LANES:(h + 1) * LANES]
    vb_ref[...] = v.astype(_BF16)
    z_ref[...] = y[:, D_ATTN:].astype(_BF16)


def _inproj_xbc_kernel(x_ref, wn_ref, w_ref, xbc_ref, dt_ref, wb_sc):
    _cast_weights_once(w_ref, wb_sc, transposed=True, valid_out=CONV_DIM + N_SSM_HEADS)
    u = _rms(x_ref[...], wn_ref[...]).astype(_BF16)
    y = jnp.dot(u, wb_sc[...], preferred_element_type=_F32)
    xbc_ref[...] = y[:, :CONV_DIM]
    dt_ref[...] = y[:, CONV_DIM:]


def _prompt_inproj(x2d, w_norm, w_in_t, seq, tm):
    t = x2d.shape[0]
    nblk = t // tm
    per_seq = seq // tm
    wn = w_norm.reshape(1, D_MODEL)
    row = lambda i: (i, 0)
    fixed = lambda i: (0, 0)
    x_spec = pl.BlockSpec((tm, D_MODEL), row)
    wn_spec = pl.BlockSpec((1, D_MODEL), fixed)
    head_spec = pl.BlockSpec((tm * N_ATT_HEADS, LANES), row)
    wide = 2 * D_ATTN

    c, s1, s2 = _rope_tables(jnp.arange(seq, dtype=jnp.int32))
    tab_spec = pl.BlockSpec((tm, LANES), lambda i: (i % per_seq, 0))
    q, kf, kb = pl.pallas_call(
        _inproj_qk_kernel,
        grid=(nblk,),
        in_specs=[x_spec, wn_spec, _resident((wide, D_MODEL), fixed), tab_spec, tab_spec, tab_spec],
        out_specs=[pl.BlockSpec((tm, D_ATTN), row), head_spec, pl.BlockSpec((tm, D_ATTN), row)],
        out_shape=[jax.ShapeDtypeStruct((t, D_ATTN), _BF16), jax.ShapeDtypeStruct((t * N_ATT_HEADS, LANES), _F32),
                   jax.ShapeDtypeStruct((t, D_ATTN), _BF16)],
        scratch_shapes=[pltpu.VMEM((D_MODEL, wide), _BF16)],
        compiler_params=_cparams(("arbitrary",)),
        name="inproj_qk",
    )(x2d, wn, w_in_t, c, s1, s2)

    vf, vb, z = pl.pallas_call(
        _inproj_vz_kernel,
        grid=(nblk,),
        in_specs=[x_spec, wn_spec, _resident((wide, D_MODEL), lambda i: (1, 0))],
        out_specs=[head_spec, pl.BlockSpec((tm, D_ATTN), row), pl.BlockSpec((tm, D_SSM), row)],
        out_shape=[jax.ShapeDtypeStruct((t * N_ATT_HEADS, LANES), _F32), jax.ShapeDtypeStruct((t, D_ATTN), _BF16),
                   jax.ShapeDtypeStruct((t, D_SSM), _BF16)],
        scratch_shapes=[pltpu.VMEM((D_MODEL, wide), _BF16)],
        compiler_params=_cparams(("arbitrary",)),
        name="inproj_vz",
    )(x2d, wn, w_in_t)

    n_tail = CONV_DIM + LANES
    xbc, dt = pl.pallas_call(
        _inproj_xbc_kernel,
        grid=(nblk,),
        in_specs=[x_spec, wn_spec, _resident((wide, D_MODEL), lambda i: (2, 0))],
        out_specs=[pl.BlockSpec((tm, CONV_DIM), row), pl.BlockSpec((tm, LANES), row)],
        out_shape=[jax.ShapeDtypeStruct((t, CONV_DIM), _F32), jax.ShapeDtypeStruct((t, LANES), _F32)],
        scratch_shapes=[pltpu.VMEM((D_MODEL, n_tail), _BF16)],
        compiler_params=_cparams(("arbitrary",)),
        name="inproj_xbc",
    )(x2d, wn, w_in_t)
    return q, kf, kb, vf, vb, z, xbc, dt


def _lambda(lq1_ref, lk1_ref, lq2_ref, lk2_ref):
    a = jnp.sum(lq1_ref[...] * lk1_ref[...], axis=-1, keepdims=True)
    b = jnp.sum(lq2_ref[...] * lk2_ref[...], axis=-1, keepdims=True)
    return jnp.exp(a) - jnp.exp(b) + LAM_INIT


def _diff_finalize(o1, o2, lam, wsub):
    a = o1 - lam * o2
    return _rms(a, wsub) * (1.0 - LAM_INIT)


def _attn_kernel(q_ref, k_ref, v_ref, lq1_ref, lk1_ref, lq2_ref, lk2_ref, wsub_ref, o_ref,
                 qt_sc, vt_sc, bias_sc, sa_sc, sb_sc, m_sc, acc_sc, *, tile):
    qi = pl.program_id(2)
    n_kt = vt_sc.shape[0]
    dv = ATT_V_DIM
    masked_out = -1e30

    @pl.when(qi == 0)
    def _():
        for c in range(n_kt):
            vt_sc[c, 0:dv, :] = v_ref[c * tile:(c + 1) * tile, :].astype(_F32).T.astype(_BF16)
            vt_sc[c, dv:, :] = jnp.ones((vt_sc.shape[1] - dv, tile), _BF16)
        key = lax.broadcasted_iota(jnp.int32, (tile, tile), 0)
        qry = lax.broadcasted_iota(jnp.int32, (tile, tile), 1)
        bias_sc[...] = jnp.where(key <= qry, 0.0, masked_out)

    q = q_ref[...].astype(_F32)
    lane = lax.broadcasted_iota(jnp.int32, (tile, LANES), 1)
    qt_sc[0] = jnp.where(lane < ATT_HEAD_DIM, q, 0.0).T.astype(_BF16)
    qt_sc[1] = jnp.where(lane >= ATT_HEAD_DIM, q, 0.0).T.astype(_BF16)
    m_sc[...] = jnp.full(m_sc.shape, -jnp.inf, _F32)
    acc_sc[...] = jnp.zeros(acc_sc.shape, _F32)

    def scores(j, dst):
        k = k_ref[pl.ds(pl.multiple_of(j * tile, tile), tile), :]
        for mm in range(2):
            dst[mm] = jnp.dot(k, qt_sc[mm], preferred_element_type=_F32)

    def softmax_pv(j, src, diagonal=False):
        vt = vt_sc[j]
        for mm in range(2):
            st = src[mm]
            if diagonal:
                st = st + bias_sc[...]
            m_prev = m_sc[mm]
            m_new = jnp.maximum(m_prev, jnp.max(st, axis=0, keepdims=True))
            alpha = jnp.exp2(m_prev - m_new)
            pt = jnp.exp2(st - m_new[0:1, :]).astype(_BF16)
            acc_sc[mm] = alpha[0:1, :] * acc_sc[mm] + jnp.dot(vt, pt, preferred_element_type=_F32)
            m_sc[mm] = m_new

    def pair(jj, carry):
        j0 = 2 * jj
        scores(j0 + 1, sb_sc)
        softmax_pv(j0, sa_sc)
        scores(j0 + 2, sa_sc)
        softmax_pv(j0 + 1, sb_sc)
        return carry

    scores(0, sa_sc)
    lax.fori_loop(0, qi // 2, pair, 0)
    odd = lax.rem(qi, 2) == 1

    @pl.when(odd)
    def _():
        scores(qi, sb_sc)
        softmax_pv(qi - 1, sa_sc)
        softmax_pv(qi, sb_sc, diagonal=True)

    @pl.when(jnp.logical_not(odd))
    def _():
        softmax_pv(qi, sa_sc, diagonal=True)

    outs = []
    for mm in range(2):
        acc = acc_sc[mm]
        outs.append((acc[0:dv, :] / acc[dv:dv + 1, :]).T)
    lam = _lambda(lq1_ref, lk1_ref, lq2_ref, lk2_ref)
    o_ref[...] = _diff_finalize(outs[0], outs[1], lam, wsub_ref[...]).astype(o_ref.dtype)


def _prompt_attention(q, kb, vb, lam_params, w_subln, batch, seq, tile):
    t = q.shape[0]
    tq = tile
    nq = seq // tq
    lq1, lk1, lq2, lk2 = [p.reshape(1, ATT_HEAD_DIM) for p in lam_params]
    vec = pl.BlockSpec((1, ATT_HEAD_DIM), lambda b, h, i: (0, 0))
    kv_spec = pl.BlockSpec((seq, LANES), lambda b, h, i: (b, h))
    qo_spec = pl.BlockSpec((tq, LANES), lambda b, h, i: (b * nq + i, h))
    return pl.pallas_call(
        functools.partial(_attn_kernel, tile=tile),
        grid=(batch, N_ATT_HEADS, nq),
        in_specs=[qo_spec, kv_spec, kv_spec, vec, vec, vec, vec,
                  pl.BlockSpec((1, ATT_V_DIM), lambda b, h, i: (0, 0))],
        out_specs=qo_spec,
        out_shape=jax.ShapeDtypeStruct((t, D_ATTN), _BF16),
        scratch_shapes=[pltpu.VMEM((2, LANES, tile), _BF16),
                        pltpu.VMEM((seq // tile, ATT_V_DIM + 2 * SUBLANES, tile), _BF16),
                        pltpu.VMEM((tile, tile), _F32),
                        pltpu.VMEM((2, tile, tile), _F32), pltpu.VMEM((2, tile, tile), _F32),
                        pltpu.VMEM((2, SUBLANES, tile), _F32),
                        pltpu.VMEM((2, ATT_V_DIM + 2 * SUBLANES, tile), _F32)],
        compiler_params=_cparams(("arbitrary", "arbitrary", "arbitrary")),
        name="prompt_attn",
    )(q, kb, vb, lq1, lk1, lq2, lk2, w_subln.reshape(1, ATT_V_DIM))


def _ssd_kernel(xbc_ref, z_ref, dt_ref, dtt_ref, cprev_ref, h0_ref, convw_ref, convb_ref, dtb_row_ref, dtb_col_ref,
                alog_row_ref, alog_col_ref, dskip_ref, wnorm_ref, y_ref, hout_ref,
                xp_sc, st_sc, yg_sc, *, n_valid, precise):
    c = pl.program_id(1)
    L = SSM_CHUNK
    n_pairs = N_SSM_HEADS // 2
    hp = 2 * SSM_HEAD_DIM

    @pl.when(c == 0)
    def _():
        xp_sc[0:SUBLANES, :] = jnp.zeros((SUBLANES, CONV_DIM), _F32)
        xp_sc[SUBLANES - (CONV_WIDTH - 1):SUBLANES, :] = cprev_ref[...]
        st_sc[...] = h0_ref[...]

    xp_sc[SUBLANES:SUBLANES + L, :] = xbc_ref[...]
    base = SUBLANES - (CONV_WIDTH - 1)
    conv = convb_ref[...] + convw_ref[0:1, :] * xp_sc[base:base + L, :]
    for j in range(1, CONV_WIDTH):
        conv = conv + convw_ref[j:j + 1, :] * xp_sc[base + j:base + j + L, :]
    xp_sc[0:SUBLANES, :] = xp_sc[L:L + SUBLANES, :]
    xc = _silu(conv)
    xs = xc[:, :D_SSM]

    row_i = lax.broadcasted_iota(jnp.int32, (L, LANES), 0)
    col_i = lax.broadcasted_iota(jnp.int32, (L, LANES), 1)
    dt = _softplus(dt_ref[...] + dtb_row_ref[...])
    dtt = _softplus(dtt_ref[...] + dtb_col_ref[...])
    if n_valid < L:
        dt = jnp.where(row_i < n_valid, dt, 0.0)
        dtt = jnp.where(lax.broadcasted_iota(jnp.int32, dtt.shape, 1) < n_valid, dtt, 0.0)
    ad = dt * (-jnp.exp(alog_row_ref[...]))
    adt = dtt * (-jnp.exp(alog_col_ref[...]))
    tril = (col_i <= row_i).astype(_F32)
    triu = (row_i <= col_i).astype(_F32)
    acs = jnp.dot(tril, ad, preferred_element_type=_F32, precision=_HI)
    acst = jnp.dot(adt, triu, preferred_element_type=_F32, precision=_HI)
    causal = col_i <= row_i
    lane_lo = col_i < SSM_HEAD_DIM

    cb = []
    for g in range(SSM_GROUPS):
        bg = xc[:, D_SSM + g * SSM_STATE:D_SSM + (g + 1) * SSM_STATE]
        cg = xc[:, D_SSM + (SSM_GROUPS + g) * SSM_STATE:D_SSM + (SSM_GROUPS + g + 1) * SSM_STATE]
        cb.append((bg, cg, _dot_nt(cg, bg, precise)))

    for p in range(n_pairs):
        ha, hb = 2 * p, 2 * p + 1
        bg, cg, cbg = cb[ha // (N_SSM_HEADS // SSM_GROUPS)]
        col_a, col_b = acs[:, ha:ha + 1], acs[:, hb:hb + 1]
        lm_a = jnp.where(causal, jnp.exp(col_a - acst[ha:ha + 1, :]), 0.0)
        lm_b = jnp.where(causal, jnp.exp(col_b - acst[hb:hb + 1, :]), 0.0)
        dt_pair = jnp.where(lane_lo, dt[:, ha:ha + 1], dt[:, hb:hb + 1])
        x_pair = xs[:, p * hp:(p + 1) * hp] * dt_pair
        y_diag = jnp.where(lane_lo, _dot(cbg * lm_a, x_pair, precise), _dot(cbg * lm_b, x_pair, precise))
        st = st_sc[p]
        y_off = _dot_nt(cg, st, precise) * jnp.where(lane_lo, jnp.exp(col_a), jnp.exp(col_b))
        last_a, last_b = acs[L - 1:L, ha:ha + 1], acs[L - 1:L, hb:hb + 1]
        decay = jnp.where(lane_lo, jnp.exp(last_a - col_a), jnp.exp(last_b - col_b))
        upd = _dot((x_pair * decay).T, bg, precise)
        row_lo = row_i < SSM_HEAD_DIM
        st_sc[p] = jnp.where(row_lo, jnp.exp(last_a), jnp.exp(last_b)) * st + upd
        y_pair = y_diag + y_off + xs[:, p * hp:(p + 1) * hp] * dskip_ref[:, p * hp:(p + 1) * hp]
        zg = z_ref[:, p * hp:(p + 1) * hp].astype(_F32)
        yg_sc[:, p * hp:(p + 1) * hp] = y_pair * _silu(zg)

    gw = D_SSM // SSM_GROUPS
    for g in range(SSM_GROUPS):
        y_ref[:, g * gw:(g + 1) * gw] = _rms(yg_sc[:, g * gw:(g + 1) * gw],
                                             wnorm_ref[:, g * gw:(g + 1) * gw]).astype(y_ref.dtype)

    hout_ref[...] = st_sc[...]


def _ssd(xbc, z, dt, conv_prev, h0, conv_w, conv_b, dt_bias, a_log, d_skip, w_ssm_norm, batch, n_chunks,
         n_valid, precise, out_dtype):
    L = SSM_CHUNK
    rows = xbc.shape[0]
    n_pairs = N_SSM_HEADS // 2
    dtt = dt[:, :N_SSM_HEADS].T
    pad = LANES - N_SSM_HEADS
    dtb_row = jnp.pad(dt_bias, (0, pad)).reshape(1, LANES)
    alog_row = jnp.pad(a_log, (0, pad)).reshape(1, LANES)
    dskip = jnp.repeat(d_skip, SSM_HEAD_DIM).reshape(1, D_SSM)
    blk = lambda b, c: (b * n_chunks + c, 0)
    fixed = lambda b, c: (0, 0)
    y, h_out = pl.pallas_call(
        functools.partial(_ssd_kernel, n_valid=n_valid, precise=precise),
        grid=(batch, n_chunks),
        in_specs=[
            pl.BlockSpec((L, CONV_DIM), blk),
            pl.BlockSpec((L, D_SSM), blk),
            pl.BlockSpec((L, LANES), blk),
            pl.BlockSpec((N_SSM_HEADS, L), lambda b, c: (0, b * n_chunks + c)),
            pl.BlockSpec((None, CONV_WIDTH - 1, CONV_DIM), lambda b, c: (b, 0, 0)),
            pl.BlockSpec((None, n_pairs, 2 * SSM_HEAD_DIM, SSM_STATE), lambda b, c: (b, 0, 0, 0)),
            pl.BlockSpec((CONV_WIDTH, CONV_DIM), fixed),
            pl.BlockSpec((1, CONV_DIM), fixed),
            pl.BlockSpec((1, LANES), fixed),
            pl.BlockSpec((N_SSM_HEADS, 1), fixed),
            pl.BlockSpec((1, LANES), fixed),
            pl.BlockSpec((N_SSM_HEADS, 1), fixed),
            pl.BlockSpec((1, D_SSM), fixed),
            pl.BlockSpec((1, D_SSM), fixed),
        ],
        out_specs=[pl.BlockSpec((L, D_SSM), blk),
                   pl.BlockSpec((None, n_pairs, 2 * SSM_HEAD_DIM, SSM_STATE), lambda b, c: (b, 0, 0, 0))],
        out_shape=[jax.ShapeDtypeStruct((rows, D_SSM), out_dtype),
                   jax.ShapeDtypeStruct((batch, n_pairs, 2 * SSM_HEAD_DIM, SSM_STATE), _F32)],
        scratch_shapes=[pltpu.VMEM((SUBLANES + L, CONV_DIM), _F32),
                        pltpu.VMEM((n_pairs, 2 * SSM_HEAD_DIM, SSM_STATE), _F32),
                        pltpu.VMEM((L, D_SSM), _F32)],
        compiler_params=_cparams(("arbitrary", "arbitrary")),
        name="ssd_precise" if precise else "ssd",
    )(xbc, z, dt, dtt, conv_prev, h0.reshape(batch, n_pairs, 2 * SSM_HEAD_DIM, SSM_STATE),
      conv_w, conv_b.reshape(1, CONV_DIM), dtb_row, dt_bias.reshape(N_SSM_HEADS, 1),
      alog_row, a_log.reshape(N_SSM_HEADS, 1), dskip, w_ssm_norm.reshape(1, D_SSM))
    return y, h_out.reshape(batch, N_SSM_HEADS, SSM_HEAD_DIM, SSM_STATE)


def _outproj_router_kernel(x_ref, attn_ref, ssd_ref, wa_ref, wb_ref, wn_ref, wr_ref, br_ref, *rest, precise, n_blocks):
    if precise:
        h1_ref, t_ref, route_ref = rest[1:4]
        wa, wb = wa_ref, wb_ref
    else:
        h1_ref, t_ref, route_ref, wa, wb = rest
        _cast_weights_once(wa_ref, wa)
        _cast_weights_once(wb_ref, wb)

    @pl.when(pl.program_id(0) >= n_blocks)
    def _():
        t_ref[...] = jnp.zeros(t_ref.shape, _F32)

    @pl.when(pl.program_id(0) < n_blocks)
    def _():
        _outproj_router_body(x_ref, attn_ref, ssd_ref, wa, wb, wn_ref, wr_ref, br_ref, h1_ref, t_ref, route_ref, precise)


def _outproj_router_body(x_ref, attn_ref, ssd_ref, wa, wb, wn_ref, wr_ref, br_ref, h1_ref, t_ref, route_ref, precise):
    h1 = x_ref[...] + _dot(attn_ref[...], wa[...], precise) + _dot(ssd_ref[...], wb[...], precise)
    h1_ref[...] = h1
    t = _rms(h1, wn_ref[...])
    t_ref[...] = t
    logits = _dot(t, wr_ref[...], precise) + br_ref[...]
    lane = lax.broadcasted_iota(jnp.int32, logits.shape, 1).astype(_F32)
    neg = -jnp.inf
    big = float(LANES)
    is_g = lane < N_EXPERT_GROUPS
    gl = jnp.where(is_g, logits, neg)
    gmax = jnp.max(gl, axis=1, keepdims=True)
    g_idx = jnp.min(jnp.where(gl == gmax, lane, big), axis=1, keepdims=True)
    g_w = 1.0 / jnp.sum(jnp.where(is_g, jnp.exp(logits - gmax), 0.0), axis=1, keepdims=True)
    lo = N_EXPERT_GROUPS + EXPERTS_PER_GROUP * g_idx
    l1 = jnp.where(lane >= lo, jnp.where(lane < lo + EXPERTS_PER_GROUP, logits, neg), neg)
    m1 = jnp.max(l1, axis=1, keepdims=True)
    i1 = jnp.min(jnp.where(l1 == m1, lane, big), axis=1, keepdims=True)
    l2 = jnp.where(lane == i1, neg, l1)
    m2 = jnp.max(l2, axis=1, keepdims=True)
    i2 = jnp.min(jnp.where(l2 == m2, lane, big), axis=1, keepdims=True)
    r = jnp.exp(m2 - m1)
    p1 = 1.0 / (1.0 + r)
    w1 = g_w * p1
    w2 = g_w * (r * p1)
    e1 = i1 - N_EXPERT_GROUPS
    e2 = i2 - N_EXPERT_GROUPS
    route_ref[...] = jnp.where(lane == 0, e1, jnp.where(lane == 1, e2, jnp.where(lane == 2, w1,
                               jnp.where(lane == 3, w2, 0.0))))


def _outproj_router(x2d, attn, ssd, w_out, w_norm_ffn, w_router, b_router, tm, precise, t_rows, t_shared=None):
    t = x2d.shape[0]
    n_blocks = t // tm
    wdt = _F32 if precise else _BF16
    half_w = (D_ATTN, D_MODEL)
    fixed = lambda i: (0, 0)
    if precise:
        assert t_shared is not None and (t_rows - t) % tm == 0
        grid = (n_blocks,)
        row = lambda i: (i, 0)
        t_row = lambda i: (i + (t_rows - t) // tm, 0)
        extra_in, extra_args, aliases, scratch = [pl.BlockSpec(memory_space=pl.ANY)], [t_shared], {8: 1}, []
    else:
        grid = (n_blocks + pl.cdiv(t_rows - t, tm),)
        row = lambda i: (jnp.minimum(i, n_blocks - 1), 0)
        t_row = lambda i: (i, 0)
        extra_in, extra_args, aliases = [], [], {}
        scratch = [pltpu.VMEM(half_w, _BF16), pltpu.VMEM(half_w, _BF16)]
    return pl.pallas_call(
        functools.partial(_outproj_router_kernel, precise=precise, n_blocks=n_blocks),
        grid=grid,
        in_specs=[pl.BlockSpec((tm, D_MODEL), row), pl.BlockSpec((tm, D_ATTN), row), pl.BlockSpec((tm, D_SSM), row),
                  _resident(half_w, fixed), _resident(half_w, lambda i: (1, 0)),
                  pl.BlockSpec((1, D_MODEL), fixed), pl.BlockSpec((D_MODEL, LANES), fixed),
                  pl.BlockSpec((1, LANES), fixed)] + extra_in,
        out_specs=[pl.BlockSpec((tm, D_MODEL), row), pl.BlockSpec((tm, D_MODEL), t_row), pl.BlockSpec((tm, LANES), row)],
        out_shape=[jax.ShapeDtypeStruct((t, D_MODEL), _F32), jax.ShapeDtypeStruct((t_rows, D_MODEL), _F32),
                   jax.ShapeDtypeStruct((t, LANES), _F32)],
        scratch_shapes=scratch,
        input_output_aliases=aliases,
        compiler_params=_cparams(("arbitrary",)),
        name="outproj_router_precise" if precise else "outproj_router",
    )(x2d, attn, ssd, w_out, w_out, w_norm_ffn.reshape(1, D_MODEL), w_router.astype(wdt), b_router, *extra_args)


def _expert_kernel(te_ref, nv_ref, idx_prev_ref, idx_ref, idx_next_ref, t_hbm, wg_ref, wu_ref, wd_ref, y_hbm,
                   xbuf, ybuf, sem_in, sem_out, *, tm, n_tiles):
    i = pl.program_id(0)
    nv = nv_ref[i]
    nv_next = jnp.where(i + 1 < n_tiles, nv_ref[jnp.minimum(i + 1, n_tiles - 1)], 0)
    f_half = D_FF_EXPERT // 2
    n_issue_chunks = 4
    per_chunk = tm // n_issue_chunks

    def gather_rows(idx, s, rows):
        for r in rows:
            pltpu.make_async_copy(t_hbm.at[pl.ds(idx[0, 0, r], 1), :], xbuf.at[s, pl.ds(r, 1), :], sem_in.at[s]).start()

    def wait_gather(s):
        pltpu.make_async_copy(t_hbm.at[pl.ds(0, tm), :], xbuf.at[s], sem_in.at[s]).wait()

    def scatter_rows(idx, s, rows):
        for r in rows:
            pltpu.make_async_copy(ybuf.at[s, pl.ds(r, 1), :], y_hbm.at[pl.ds(idx[0, 0, tm + r], 1), :],
                                  sem_out.at[s]).start()

    def wait_scatter(s):
        pltpu.make_async_copy(ybuf.at[s], y_hbm.at[pl.ds(0, tm), :], sem_out.at[s]).wait()

    @pl.when(i == 0)
    def _():
        gather_rows(idx_ref, 0, range(tm))
        half = tm // 2
        n_pad = y_hbm.shape[0] // 2
        ybuf[0, 0:half, :] = jnp.zeros((half, D_MODEL), _F32)
        for k in range(2):
            fill = pltpu.make_async_copy(ybuf.at[0, pl.ds(0, half), :],
                                         y_hbm.at[pl.ds((k + 1) * n_pad - half, half), :], sem_out.at[0])
            fill.start()
            fill.wait()

    def tile(s, has_prev):
        o = 1 - s

        def issue(c):
            rows = range(c * per_chunk, (c + 1) * per_chunk)
            gather_rows(idx_next_ref, o, rows)
            if has_prev:
                scatter_rows(idx_prev_ref, o, rows)

        wait_gather(s)
        x = xbuf[s].astype(_BF16)
        parts = []
        for c in range(2):
            cols = slice(c * f_half, (c + 1) * f_half)
            hg = jnp.dot(x, wg_ref[:, cols].astype(_BF16), preferred_element_type=_F32)
            issue(2 * c)
            hu = jnp.dot(x, wu_ref[:, cols].astype(_BF16), preferred_element_type=_F32)
            issue(2 * c + 1)
            parts.append((_silu(hg) * hu).astype(_BF16))
        hh = jnp.concatenate(parts, axis=1)
        ybuf[s] = jnp.dot(hh, wd_ref[...].astype(_BF16), preferred_element_type=_F32)
        if has_prev:
            wait_scatter(o)

        @pl.when(nv_next == 0)
        def _():
            wait_gather(o)
            scatter_rows(idx_ref, s, range(tm))
            wait_scatter(s)

    used = nv > 0
    odd = lax.rem(i, 2) == 1

    @pl.when(i == 0)
    def _():
        tile(0, False)

    @pl.when(used & (i > 0) & jnp.logical_not(odd))
    def _():
        tile(0, True)

    @pl.when(used & odd)
    def _():
        tile(1, True)


def _moe_plan(e_ids, tm, n_tiles):
    n = e_ids.shape[0]
    n_assign = 2 * n
    half = tm // 2
    n_pad = n + half
    e_flat = e_ids.reshape(-1)
    order = jnp.argsort(e_flat, stable=True).astype(jnp.int32)
    counts = jnp.sum((e_flat[:, None] == jnp.arange(N_EXPERTS, dtype=jnp.int32)[None, :]).astype(jnp.int32), axis=0)
    seg_start = jnp.cumsum(counts) - counts
    tiles_per = (counts + tm - 1) // tm
    cum_tiles = jnp.cumsum(tiles_per)
    tile_start = cum_tiles - tiles_per
    n_used = cum_tiles[-1]
    tile_id = jnp.arange(n_tiles, dtype=jnp.int32)
    used = tile_id < n_used
    te = jnp.sum((cum_tiles[None, :] <= jnp.minimum(tile_id, n_used - 1)[:, None]).astype(jnp.int32), axis=1)
    j = tile_id - tile_start[te]
    nvalid = jnp.where(used, jnp.clip(counts[te] - j * tm, 0, tm), 0).astype(jnp.int32)
    r = jnp.arange(tm, dtype=jnp.int32)[None, :]
    pos = seg_start[te][:, None] + j[:, None] * tm + r
    a = order[jnp.clip(pos, 0, n_assign - 1)]
    tok = a // 2
    spare = (r // half) * n_pad + n + (r % half)
    dst = jnp.where(r < nvalid[:, None], (a % 2) * n_pad + tok, spare)
    idx = jnp.concatenate([tok, dst], axis=1).reshape(n_tiles, 1, 2 * tm).astype(jnp.int32)
    return te, nvalid, idx, n_pad


def _expert_mlp(t_all, e_ids, w_gate, w_up, w_down):
    n = t_all.shape[0]
    tm = EXPERT_TILE
    n_tiles = pl.cdiv(2 * n, tm) + N_EXPERTS
    te, nvalid, idx, n_pad = _moe_plan(e_ids, tm, n_tiles)
    wmap = lambda i, te_ref, nv_ref: (te_ref[i], 0, 0)
    grid_spec = pltpu.PrefetchScalarGridSpec(
        num_scalar_prefetch=2,
        grid=(n_tiles,),
        in_specs=[
            pl.BlockSpec((1, 1, 2 * tm), lambda i, te_ref, nv_ref: (jnp.maximum(i - 1, 0), 0, 0),
                         memory_space=pltpu.SMEM),
            pl.BlockSpec((1, 1, 2 * tm), lambda i, te_ref, nv_ref: (i, 0, 0), memory_space=pltpu.SMEM),
            pl.BlockSpec((1, 1, 2 * tm), lambda i, te_ref, nv_ref: (jnp.minimum(i + 1, n_tiles - 1), 0, 0),
                         memory_space=pltpu.SMEM),
            pl.BlockSpec(memory_space=pl.ANY),
            pl.BlockSpec((None, D_MODEL, D_FF_EXPERT), wmap),
            pl.BlockSpec((None, D_MODEL, D_FF_EXPERT), wmap),
            pl.BlockSpec((None, D_FF_EXPERT, D_MODEL), wmap),
        ],
        out_specs=pl.BlockSpec(memory_space=pl.ANY),
        scratch_shapes=[pltpu.VMEM((2, tm, D_MODEL), _F32), pltpu.VMEM((2, tm, D_MODEL), _F32),
                        pltpu.SemaphoreType.DMA((2,)), pltpu.SemaphoreType.DMA((2,))],
    )
    y_rows = pl.pallas_call(
        functools.partial(_expert_kernel, tm=tm, n_tiles=n_tiles),
        grid_spec=grid_spec,
        out_shape=jax.ShapeDtypeStruct((2 * n_pad, D_MODEL), _F32),
        compiler_params=_cparams(("arbitrary",)),
        name="expert_mlp",
    )(te, nvalid, idx, idx, idx, t_all, w_gate, w_up, w_down)
    return y_rows.reshape(2, n_pad, D_MODEL)


def _combine_ple_kernel(h1_ref, y_ref, route_ref, p_ref, wn_ref, wg_ref, wp_ref, wf_ref, o_ref, wg_sc):
    _cast_weights_once(wg_ref, wg_sc)
    route = route_ref[...]
    h2 = h1_ref[...] + route[:, 2:3] * y_ref[0] + route[:, 3:4] * y_ref[1]
    u = _rms(h2, wn_ref[...])
    gate = jnp.dot(u.astype(_BF16), wg_sc[...], preferred_element_type=_F32)
    gate = 1.0 / (1.0 + jnp.exp(-gate))
    pp = jnp.dot(p_ref[...].astype(_BF16), wp_ref[...], preferred_element_type=_F32)
    h3 = h2 + gate * pp
    o_ref[...] = _rms(h3, wf_ref[...])


def _combine_ple(h1, y_stk, route, p2d, w_norm_ple, w_ple_gate, w_ple_proj, w_norm_final, tm, row_off):
    t = h1.shape[0]
    off = row_off // tm
    row = lambda i: (i, 0)
    fixed = lambda i: (0, 0)
    return pl.pallas_call(
        _combine_ple_kernel,
        grid=(t // tm,),
        in_specs=[pl.BlockSpec((tm, D_MODEL), row),
                  pl.BlockSpec((2, tm, D_MODEL), lambda i: (0, i + off, 0)),
                  pl.BlockSpec((tm, LANES), row), pl.BlockSpec((tm, D_PLE), row),
                  pl.BlockSpec((1, D_MODEL), fixed), _resident((D_MODEL, D_MODEL), fixed),
                  pl.BlockSpec((D_PLE, D_MODEL), fixed), pl.BlockSpec((1, D_MODEL), fixed)],
        out_specs=pl.BlockSpec((tm, D_MODEL), row),
        out_shape=jax.ShapeDtypeStruct((t, D_MODEL), _F32),
        scratch_shapes=[pltpu.VMEM((D_MODEL, D_MODEL), _BF16)],
        compiler_params=_cparams(("arbitrary",)),
        name="combine_ple",
    )(h1, y_stk, route, p2d, w_norm_ple.reshape(1, D_MODEL), w_ple_gate, w_ple_proj.astype(_BF16),
      w_norm_final.reshape(1, D_MODEL))


def _sample_inproj_kernel(x_ref, wn_ref, w_ref, o_ref):
    u = _rms(x_ref[...], wn_ref[...])
    o_ref[...] = jnp.dot(u, w_ref[...].T, preferred_element_type=_F32, precision=_HI)


def _sample_inproj(xs2d, w_norm, w_in_t, tn):
    rows = xs2d.shape[0]
    return pl.pallas_call(
        _sample_inproj_kernel,
        grid=(pl.cdiv(D_IN_PROJ, tn),),
        in_specs=[pl.BlockSpec((rows, D_MODEL), lambda j: (0, 0)), pl.BlockSpec((1, D_MODEL), lambda j: (0, 0)),
                  pl.BlockSpec((tn, D_MODEL), lambda j: (j, 0))],
        out_specs=pl.BlockSpec((rows, tn), lambda j: (0, j)),
        out_shape=jax.ShapeDtypeStruct((rows, D_IN_PROJ), _F32),
        compiler_params=_cparams(("arbitrary",)),
        name="sample_inproj",
    )(xs2d, w_norm.reshape(1, D_MODEL), w_in_t)


def _split2(x):
    hi = x.astype(_BF16)
    lo = (x - hi.astype(_F32)).astype(_BF16)
    return hi, lo


def _decode_attn_kernel(pt_ref, qkv_ref, c_ref, s1_ref, s2_ref, lq1_ref, lk1_ref, lq2_ref, lk2_ref, wsub_ref, *rest,
                        n_pages_step, n_steps, n_real):
    k_refs = rest[:n_pages_step]
    v_refs = rest[n_pages_step:2 * n_pages_step]
    o_ref, knew_ref, vnew_ref = rest[2 * n_pages_step:2 * n_pages_step + 3]
    q_sc, m_sc, l_sc, acc_sc, kt_sc, vt_sc = rest[2 * n_pages_step + 3:]
    j = pl.program_id(1)
    R = SAMPLE_ROWS
    hr = 2 * R
    page = kt_sc.shape[0] // N_ATT_HEADS

    @pl.when(j == 0)
    def _():
        c, s1, s2 = c_ref[...], s1_ref[...], s2_ref[...]
        lane = lax.broadcasted_iota(jnp.int32, (R, LANES), 1)
        kt_sc[...] = jnp.zeros(kt_sc.shape, _F32)
        vt_sc[...] = jnp.zeros(vt_sc.shape, _F32)
        for h in range(N_ATT_HEADS):
            sl = slice(h * LANES, (h + 1) * LANES)
            q = _rope_block(qkv_ref[:, sl], c, s1, s2) * (ATT_HEAD_DIM ** -0.5)
            k = _rope_block(qkv_ref[:, D_ATTN + h * LANES:D_ATTN + (h + 1) * LANES], c, s1, s2)
            v = qkv_ref[:, 2 * D_ATTN + h * LANES:2 * D_ATTN + (h + 1) * LANES]
            knew_ref[:, sl] = k
            vnew_ref[:, sl] = v
            kt_sc[pl.ds(h, R, stride=N_ATT_HEADS), :] = k
            vt_sc[pl.ds(h, R, stride=N_ATT_HEADS), :] = v
            q2 = jnp.concatenate([jnp.where(lane < ATT_HEAD_DIM, q, 0.0), jnp.where(lane >= ATT_HEAD_DIM, q, 0.0)], axis=0)
            hi, lo = _split2(q2)
            q_sc[h] = jnp.concatenate([hi, lo], axis=0)
        m_sc[...] = jnp.full(m_sc.shape, -jnp.inf, _F32)
        l_sc[...] = jnp.zeros(l_sc.shape, _F32)
        acc_sc[...] = jnp.zeros(acc_sc.shape, _F32)

    def head_rows(tiles, h):
        x = jnp.concatenate([t[pl.ds(h, page, stride=N_ATT_HEADS), :] for t in tiles], axis=0)
        return _split2(x)

    def process(k_tiles, v_tiles, mask):
        s_parts = []
        for h in range(N_ATT_HEADS):
            k_hi, k_lo = head_rows(k_tiles, h)
            q3 = q_sc[h]
            a = lax.dot_general(q3[0:2 * hr], k_hi, _NT, preferred_element_type=_F32)
            b = lax.dot_general(q3[0:hr], k_lo, _NT, preferred_element_type=_F32)
            s_parts.append(a[0:hr] + a[hr:2 * hr] + b)
        s = jnp.concatenate(s_parts, axis=0)
        if mask is not None:
            s = jnp.where(mask, s, -jnp.inf)
        m_prev = m_sc[...]
        m_new = jnp.maximum(m_prev, jnp.max(s, axis=1, keepdims=True))
        alpha = jnp.exp(m_prev - m_new)
        p = jnp.exp(s - m_new[:, :1])
        l_sc[...] = alpha * l_sc[...] + jnp.sum(p, axis=1, keepdims=True)
        m_sc[...] = m_new
        pv_parts = []
        for h in range(N_ATT_HEADS):
            v_hi, v_lo = head_rows(v_tiles, h)
            p_hi, p_lo = _split2(p[h * hr:(h + 1) * hr])
            a = jnp.dot(jnp.concatenate([p_hi, p_lo], axis=0), v_hi, preferred_element_type=_F32)
            b = jnp.dot(p_hi, v_lo, preferred_element_type=_F32)
            pv_parts.append(a[0:hr] + a[hr:2 * hr] + b)
        acc_sc[...] = alpha * acc_sc[...] + jnp.concatenate(pv_parts, axis=0)

    process(k_refs, v_refs, None)

    @pl.when(j == n_steps - 1)
    def _():
        rows = N_ATT_HEADS * hr
        row = lax.broadcasted_iota(jnp.int32, (rows, page), 0)
        col = lax.broadcasted_iota(jnp.int32, (rows, page), 1)
        qrow = row % R
        process([kt_sc], [vt_sc], (col <= qrow) & (col < n_real))
        o = acc_sc[...] / l_sc[...]
        lam = _lambda(lq1_ref, lk1_ref, lq2_ref, lk2_ref)
        wsub = wsub_ref[...]
        for h in range(N_ATT_HEADS):
            o1 = o[h * hr:h * hr + R]
            o2 = o[h * hr + R:(h + 1) * hr]
            o_ref[:, h * LANES:(h + 1) * LANES] = _diff_finalize(o1, o2, lam, wsub)


def _decode_attention(proj3, cache_k2, cache_v2, page_table, pos_rows, lam_params, w_subln, n_real):
    bsz = proj3.shape[0]
    page = cache_k2.shape[2] // N_ATT_HEADS
    n_pages = page_table.shape[1]
    pg = PAGES_PER_STEP
    n_steps = n_pages // pg
    R = SAMPLE_ROWS
    c, s1, s2 = _rope_tables(pos_rows)
    lq1, lk1, lq2, lk2 = [p.reshape(1, ATT_HEAD_DIM) for p in lam_params]
    fixed = lambda b, j, pt: (0, 0)
    tab = pl.BlockSpec((R, LANES), fixed)
    vec = pl.BlockSpec((1, ATT_HEAD_DIM), fixed)

    def page_spec(i):
        return pl.BlockSpec((None, None, page * N_ATT_HEADS, LANES), lambda b, j, pt: (0, pt[b, j * pg + i], 0, 0))

    out_row = pl.BlockSpec((None, R, D_ATTN), lambda b, j, pt: (b, 0, 0))
    grid_spec = pltpu.PrefetchScalarGridSpec(
        num_scalar_prefetch=1,
        grid=(bsz, n_steps),
        in_specs=[pl.BlockSpec((None, R, 3 * D_ATTN), lambda b, j, pt: (b, 0, 0)), tab, tab, tab, vec, vec, vec, vec,
                  pl.BlockSpec((1, ATT_V_DIM), fixed)]
                 + [page_spec(i) for i in range(pg)] + [page_spec(i) for i in range(pg)],
        out_specs=[out_row, out_row, out_row],
        scratch_shapes=[pltpu.VMEM((N_ATT_HEADS, 4 * R, LANES), _BF16),
                        pltpu.VMEM((N_ATT_HEADS * 2 * R, LANES), _F32), pltpu.VMEM((N_ATT_HEADS * 2 * R, LANES), _F32),
                        pltpu.VMEM((N_ATT_HEADS * 2 * R, LANES), _F32),
                        pltpu.VMEM((page * N_ATT_HEADS, LANES), _F32), pltpu.VMEM((page * N_ATT_HEADS, LANES), _F32)],
    )
    return pl.pallas_call(
        functools.partial(_decode_attn_kernel, n_pages_step=pg, n_steps=n_steps, n_real=n_real),
        grid_spec=grid_spec,
        out_shape=[jax.ShapeDtypeStruct((bsz, R, D_ATTN), _F32)] * 3,
        compiler_params=_cparams(("arbitrary", "arbitrary")),
        name="decode_attn",
    )(page_table, proj3, c, s1, s2, lq1, lk1, lq2, lk2, w_subln.reshape(1, ATT_V_DIM),
      *([cache_k2] * pg), *([cache_v2] * pg))


def kernel(x_prompt, x_sample, cache_k, cache_v, state_conv, state_ssm, page_table, p_prompt, p_sample, w_norm_mix, w_in, lambda_q1, lambda_k1, lambda_q2, lambda_k2, w_subln, conv_w, conv_b, dt_bias, A_log, D_skip, w_ssm_norm, w_out, w_norm_ffn, w_group_router, b_group_router, w_expert_router, b_expert_router, w_exp_gate, w_exp_up, w_exp_down, w_norm_ple, w_ple_gate, w_ple_proj, w_norm_final):
    bp, lp, _ = x_prompt.shape
    bs, ls, _ = x_sample.shape
    past = page_table.shape[1] * cache_k.shape[2]
    tp = bp * lp
    R = SAMPLE_ROWS
    ts = bs * R
    lam_params = (lambda_q1[0], lambda_k1[0], lambda_q2[0], lambda_k2[0])
    pad_r = LANES - N_EXPERT_GROUPS - N_EXPERTS
    w_router = jnp.pad(jnp.concatenate([w_group_router[0], w_expert_router[0]], axis=1), ((0, 0), (0, pad_r)))
    b_router = jnp.pad(jnp.concatenate([b_group_router[0], b_expert_router[0]]), (0, pad_r)).reshape(1, LANES)

    xp2d = x_prompt.reshape(tp, D_MODEL)
    w_in_t = w_in[0].T
    q, kf, kb, vf, vb, z, xbc, dt = _prompt_inproj(xp2d, w_norm_mix[0], w_in_t, lp, 512)
    attn_p = _prompt_attention(q, kb, vb, lam_params, w_subln[0], bp, lp, 512)
    ssd_p, ssm_p = _ssd(xbc, z, dt, jnp.zeros((bp, CONV_WIDTH - 1, CONV_DIM), _F32),
                        jnp.zeros((bp, N_SSM_HEADS, SSM_HEAD_DIM, SSM_STATE), _F32),
                        conv_w[0], conv_b[0], dt_bias[0], A_log[0], D_skip[0], w_ssm_norm[0],
                        bp, lp // SSM_CHUNK, SSM_CHUNK, False, _BF16)
    h1_p, t_all, route_p = _outproj_router(xp2d, attn_p, ssd_p, w_out[0], w_norm_ffn[0], w_router, b_router, 256, False,
                                           tp + ts)

    xs3 = jnp.pad(x_sample, ((0, 0), (0, R - ls), (0, 0)))
    xs2d = xs3.reshape(ts, D_MODEL)
    proj_s = _sample_inproj(xs2d, w_norm_mix[0], w_in_t, 512)
    proj3 = proj_s.reshape(bs, R, D_IN_PROJ)
    pos_rows = past + jnp.arange(R, dtype=jnp.int32)
    cache_rows = cache_k.shape[:2] + (cache_k.shape[2] * N_ATT_HEADS, LANES)
    attn_s, k_s, v_s = _decode_attention(proj3, cache_k.reshape(cache_rows), cache_v.reshape(cache_rows), page_table,
                                         pos_rows, lam_params, w_subln[0], ls)
    off = 3 * D_ATTN
    pad_rows = ((0, 0), (0, SSM_CHUNK - R), (0, 0))
    z_s = jnp.pad(proj3[:, :, off:off + D_SSM], pad_rows).reshape(bs * SSM_CHUNK, D_SSM)
    xbc_s3 = proj3[:, :, off + D_SSM:off + D_SSM + CONV_DIM]
    xbc_s = jnp.pad(xbc_s3, pad_rows).reshape(bs * SSM_CHUNK, CONV_DIM)
    dt_s = jnp.pad(proj3[:, :, off + D_SSM + CONV_DIM:], ((0, 0), (0, SSM_CHUNK - R), (0, LANES - N_SSM_HEADS)))
    dt_s = dt_s.reshape(bs * SSM_CHUNK, LANES)
    ssd_s, ssm_s = _ssd(xbc_s, z_s, dt_s, state_conv[0], state_ssm[0], conv_w[0], conv_b[0], dt_bias[0], A_log[0],
                        D_skip[0], w_ssm_norm[0], bs, 1, ls, True, _F32)
    ssd_s = ssd_s.reshape(bs, SSM_CHUNK, D_SSM)[:, :R].reshape(ts, D_SSM)
    h1_s, t_all, route_s = _outproj_router(xs2d, attn_s.reshape(ts, D_ATTN), ssd_s, w_out[0], w_norm_ffn[0],
                                           w_router, b_router, ts, True, tp + ts, t_shared=t_all)

    e_ids = jnp.concatenate([route_p[:, :2], route_s[:, :2]], axis=0).astype(jnp.int32)
    y_stk = _expert_mlp(t_all, e_ids, w_exp_gate[0], w_exp_up[0], w_exp_down[0])
    y_p = _combine_ple(h1_p, y_stk, route_p, p_prompt[0].reshape(tp, D_PLE), w_norm_ple[0], w_ple_gate[0],
                       w_ple_proj[0], w_norm_final, 256, 0)
    ps2d = jnp.pad(p_sample[0], ((0, 0), (0, R - ls), (0, 0))).reshape(ts, D_PLE)
    y_s = _combine_ple(h1_s, y_stk, route_s, ps2d, w_norm_ple[0], w_ple_gate[0], w_ple_proj[0], w_norm_final, ts, tp)

    y_prompt = y_p.reshape(bp, lp, D_MODEL)
    y_sample = y_s.reshape(bs, R, D_MODEL)[:, :ls]
    k_prompt = kf.reshape(1, bp, lp, N_ATT_HEADS, 2 * ATT_HEAD_DIM)
    v_prompt = vf.reshape(1, bp, lp, N_ATT_HEADS, ATT_V_DIM)
    conv_prompt = xbc.reshape(bp, lp, CONV_DIM)[:, lp - (CONV_WIDTH - 1):][None]
    ssm_prompt = ssm_p[None]
    k_sample = k_s[:, :ls].reshape(1, bs, ls, N_ATT_HEADS, 2 * ATT_HEAD_DIM)
    v_sample = v_s[:, :ls].reshape(1, bs, ls, N_ATT_HEADS, ATT_V_DIM)
    conv_sample = xbc_s3[:, ls - (CONV_WIDTH - 1):ls][None]
    ssm_sample = ssm_s[None]
    return (y_prompt, y_sample, k_prompt, v_prompt, conv_prompt, ssm_prompt, k_sample, v_sample, conv_sample, ssm_sample)
```

```python
import functools
import math

import jax
import jax.numpy as jnp
from jax import lax
from jax.experimental import pallas as pl
from jax.experimental.pallas import tpu as pltpu

D_MODEL = 2048
D_ATTN = 1024
D_SSM = 1024
ATT_HEAD_DIM = 64
N_ATT_HEADS = 8
ATT_V_DIM = 128
ROT_DIM = 16
ROPE_THETA = 500000.0
SSM_HEAD_DIM = 64
N_SSM_HEADS = 16
SSM_GROUPS = 2
SSM_STATE = 128
SSM_CHUNK = 128
CONV_WIDTH = 4
CONV_DIM = D_SSM + 2 * SSM_GROUPS * SSM_STATE
D_IN_PROJ = 3 * D_ATTN + D_SSM + CONV_DIM + N_SSM_HEADS
N_EXPERT_GROUPS = 4
EXPERTS_PER_GROUP = 8
N_EXPERTS = 32
D_FF_EXPERT = 512
D_PLE = 256
RMS_EPS = 1e-6
LAM_INIT = 0.8 - 0.6 * math.exp(-0.3 * 0)

LANES = 128
SUBLANES = 8
VMEM_LIMIT_BYTES = 56 * 2 ** 20

SAMPLE_ROWS = 8
EXPERT_TILE = 256
PAGES_PER_STEP = 8

_F32 = jnp.float32
_BF16 = jnp.bfloat16
_HI = lax.Precision.HIGHEST
_NT = (((1,), (1,)), ((), ()))


def _cparams(semantics):
    return pltpu.CompilerParams(dimension_semantics=semantics, vmem_limit_bytes=VMEM_LIMIT_BYTES)


def _rms(x, w):
    return x * lax.rsqrt(jnp.mean(x * x, axis=-1, keepdims=True) + RMS_EPS) * w


def _silu(x):
    return x * (1.0 / (1.0 + jnp.exp(-x)))


def _softplus(x):
    return jnp.maximum(x, 0.0) + jnp.log(1.0 + jnp.exp(-jnp.abs(x)))


def _dot(a, b, precise=False):
    if precise:
        return jnp.dot(a.astype(_F32), b.astype(_F32), preferred_element_type=_F32, precision=_HI)
    return jnp.dot(a.astype(_BF16), b.astype(_BF16), preferred_element_type=_F32)


def _dot_nt(a, b, precise=False):
    if precise:
        return lax.dot_general(a.astype(_F32), b.astype(_F32), _NT, preferred_element_type=_F32, precision=_HI)
    return lax.dot_general(a.astype(_BF16), b.astype(_BF16), _NT, preferred_element_type=_F32)


def _rope_block(y, c, s1, s2):
    return y * c + pltpu.roll(y, LANES - ROT_DIM // 2, 1) * s1 + pltpu.roll(y, ROT_DIM // 2, 1) * s2


def _rope_tables(first_pos, n):
    half = ROT_DIM // 2
    pos = first_pos + jnp.arange(n, dtype=jnp.int32)
    inv_freq = jnp.power(ROPE_THETA, -jnp.arange(half, dtype=_F32) * (2.0 / ROT_DIM))
    ang = pos.astype(_F32)[:, None] * inv_freq[None, :]
    cos, sin = jnp.cos(ang), jnp.sin(ang)
    ones = jnp.ones((n, ATT_HEAD_DIM - ROT_DIM), _F32)
    zeros = jnp.zeros((n, ATT_HEAD_DIM - ROT_DIM), _F32)
    zh = jnp.zeros((n, half), _F32)
    c = jnp.concatenate([cos, cos, ones], axis=1)
    s1 = jnp.concatenate([-sin, zh, zeros], axis=1)
    s2 = jnp.concatenate([zh, sin, zeros], axis=1)
    return tuple(jnp.concatenate([t, t], axis=1) for t in (c, s1, s2))


def _cast_weights_once(w_ref, wb_sc, transposed=False, valid_out=None):
    @pl.when(pl.program_id(0) == 0)
    def _():
        if transposed:
            w = w_ref[0:wb_sc.shape[1], :]
            if valid_out is not None:
                w = jnp.where(lax.broadcasted_iota(jnp.int32, w.shape, 0) < valid_out, w, 0.0)
            w = w.T
        else:
            w = w_ref[...]
        wb_sc[...] = w.astype(wb_sc.dtype)


def _resident(shape, index_map):
    return pl.BlockSpec(shape, index_map, pipeline_mode=pl.Buffered(1))


def _inproj_qk_kernel(x_ref, wn_ref, w_ref, c_ref, s1_ref, s2_ref, q_ref, kf_ref, kb_ref, wb_sc):
    _cast_weights_once(w_ref, wb_sc, transposed=True)
    u = _rms(x_ref[...], wn_ref[...]).astype(_BF16)
    y = jnp.dot(u, wb_sc[...], preferred_element_type=_F32)
    c, s1, s2 = c_ref[...], s1_ref[...], s2_ref[...]
    q_scale = (ATT_HEAD_DIM ** -0.5) * math.log2(math.e)
    for j in range(2 * N_ATT_HEADS):
        r = _rope_block(y[:, j * LANES:(j + 1) * LANES], c, s1, s2)
        if j < N_ATT_HEADS:
            q_ref[:, j * LANES:(j + 1) * LANES] = (r * q_scale).astype(_BF16)
        else:
            jj = j - N_ATT_HEADS
            kf_ref[pl.ds(jj, r.shape[0], stride=N_ATT_HEADS), :] = r
            kb_ref[:, jj * LANES:(jj + 1) * LANES] = r.astype(_BF16)


def _inproj_vz_kernel(x_ref, wn_ref, w_ref, vf_ref, vb_ref, z_ref, wb_sc):
    _cast_weights_once(w_ref, wb_sc, transposed=True)
    u = _rms(x_ref[...], wn_ref[...]).astype(_BF16)
    y = jnp.dot(u, wb_sc[...], preferred_element_type=_F32)
    v = y[:, :D_ATTN]
    for h in range(N_ATT_HEADS):
        vf_ref[pl.ds(h, v.shape[0], stride=N_ATT_HEADS), :] = v[:, h * LANES:(h + 1) * LANES]
    vb_ref[...] = v.astype(_BF16)
    z_ref[...] = y[:, D_ATTN:].astype(_BF16)


def _inproj_xbc_kernel(x_ref, wn_ref, w_ref, xbc_ref, dt_ref, wb_sc):
    _cast_weights_once(w_ref, wb_sc, transposed=True, valid_out=CONV_DIM + N_SSM_HEADS)
    u = _rms(x_ref[...], wn_ref[...]).astype(_BF16)
    y = jnp.dot(u, wb_sc[...], preferred_element_type=_F32)
    xbc_ref[...] = y[:, :CONV_DIM]
    dt_ref[...] = y[:, CONV_DIM:]


def _prompt_inproj(x2d, w_norm, w_in_t, seq, tm):
    t = x2d.shape[0]
    nblk = t // tm
    per_seq = seq // tm
    wn = w_norm.reshape(1, D_MODEL)
    row = lambda i: (i, 0)
    fixed = lambda i: (0, 0)
    x_spec = pl.BlockSpec((tm, D_MODEL), row)
    wn_spec = pl.BlockSpec((1, D_MODEL), fixed)
    head_spec = pl.BlockSpec((tm * N_ATT_HEADS, LANES), row)
    wide = 2 * D_ATTN

    c, s1, s2 = _rope_tables(0, seq)
    tab_spec = pl.BlockSpec((tm, LANES), lambda i: (i % per_seq, 0))
    q, kf, kb = pl.pallas_call(
        _inproj_qk_kernel,
        grid=(nblk,),
        in_specs=[x_spec, wn_spec, _resident((wide, D_MODEL), fixed), tab_spec, tab_spec, tab_spec],
        out_specs=[pl.BlockSpec((tm, D_ATTN), row), head_spec, pl.BlockSpec((tm, D_ATTN), row)],
        out_shape=[jax.ShapeDtypeStruct((t, D_ATTN), _BF16), jax.ShapeDtypeStruct((t * N_ATT_HEADS, LANES), _F32),
                   jax.ShapeDtypeStruct((t, D_ATTN), _BF16)],
        scratch_shapes=[pltpu.VMEM((D_MODEL, wide), _BF16)],
        compiler_params=_cparams(("arbitrary",)),
        name="inproj_qk",
    )(x2d, wn, w_in_t, c, s1, s2)

    vf, vb, z = pl.pallas_call(
        _inproj_vz_kernel,
        grid=(nblk,),
        in_specs=[x_spec, wn_spec, _resident((wide, D_MODEL), lambda i: (1, 0))],
        out_specs=[head_spec, pl.BlockSpec((tm, D_ATTN), row), pl.BlockSpec((tm, D_SSM), row)],
        out_shape=[jax.ShapeDtypeStruct((t * N_ATT_HEADS, LANES), _F32), jax.ShapeDtypeStruct((t, D_ATTN), _BF16),
                   jax.ShapeDtypeStruct((t, D_SSM), _BF16)],
        scratch_shapes=[pltpu.VMEM((D_MODEL, wide), _BF16)],
        compiler_params=_cparams(("arbitrary",)),
        name="inproj_vz",
    )(x2d, wn, w_in_t)

    n_tail = CONV_DIM + LANES
    xbc, dt = pl.pallas_call(
        _inproj_xbc_kernel,
        grid=(nblk,),
        in_specs=[x_spec, wn_spec, _resident((wide, D_MODEL), lambda i: (2, 0))],
        out_specs=[pl.BlockSpec((tm, CONV_DIM), row), pl.BlockSpec((tm, LANES), row)],
        out_shape=[jax.ShapeDtypeStruct((t, CONV_DIM), _F32), jax.ShapeDtypeStruct((t, LANES), _F32)],
        scratch_shapes=[pltpu.VMEM((D_MODEL, n_tail), _BF16)],
        compiler_params=_cparams(("arbitrary",)),
        name="inproj_xbc",
    )(x2d, wn, w_in_t)
    return q, kf, kb, vf, vb, z, xbc, dt


def _lambda(lq1_ref, lk1_ref, lq2_ref, lk2_ref):
    a = jnp.sum(lq1_ref[...] * lk1_ref[...], axis=-1, keepdims=True)
    b = jnp.sum(lq2_ref[...] * lk2_ref[...], axis=-1, keepdims=True)
    return jnp.exp(a) - jnp.exp(b) + LAM_INIT


def _diff_finalize(o1, o2, lam, wsub):
    a = o1 - lam * o2
    return _rms(a, wsub) * (1.0 - LAM_INIT)


def _attn_kernel(q_ref, k_ref, v_ref, lq1_ref, lk1_ref, lq2_ref, lk2_ref, wsub_ref, o_ref,
                 qt_sc, vt_sc, bias_sc, sa_sc, sb_sc, m_sc, acc_sc, *, tile):
    qi = pl.program_id(2)
    n_kt = vt_sc.shape[0]
    dv = ATT_V_DIM
    masked_out = -1e30

    @pl.when(qi == 0)
    def _():
        for c in range(n_kt):
            vt_sc[c, 0:dv, :] = v_ref[c * tile:(c + 1) * tile, :].astype(_F32).T.astype(_BF16)
            vt_sc[c, dv:, :] = jnp.ones((vt_sc.shape[1] - dv, tile), _BF16)
        key = lax.broadcasted_iota(jnp.int32, (tile, tile), 0)
        qry = lax.broadcasted_iota(jnp.int32, (tile, tile), 1)
        bias_sc[...] = jnp.where(key <= qry, 0.0, masked_out)

    q = q_ref[...].astype(_F32)
    lane = lax.broadcasted_iota(jnp.int32, (tile, LANES), 1)
    qt_sc[0] = jnp.where(lane < ATT_HEAD_DIM, q, 0.0).T.astype(_BF16)
    qt_sc[1] = jnp.where(lane >= ATT_HEAD_DIM, q, 0.0).T.astype(_BF16)
    m_sc[...] = jnp.full(m_sc.shape, -jnp.inf, _F32)
    acc_sc[...] = jnp.zeros(acc_sc.shape, _F32)

    def scores(j, dst):
        k = k_ref[pl.ds(pl.multiple_of(j * tile, tile), tile), :]
        for mm in range(2):
            dst[mm] = jnp.dot(k, qt_sc[mm], preferred_element_type=_F32)

    def softmax_pv(j, src, diagonal=False):
        vt = vt_sc[j]
        for mm in range(2):
            st = src[mm]
            if diagonal:
                st = st + bias_sc[...]
            m_prev = m_sc[mm]
            m_new = jnp.maximum(m_prev, jnp.max(st, axis=0, keepdims=True))
            alpha = jnp.exp2(m_prev - m_new)
            pt = jnp.exp2(st - m_new[0:1, :]).astype(_BF16)
            acc_sc[mm] = alpha[0:1, :] * acc_sc[mm] + jnp.dot(vt, pt, preferred_element_type=_F32)
            m_sc[mm] = m_new

    def pair(jj, carry):
        j0 = 2 * jj
        scores(j0 + 1, sb_sc)
        softmax_pv(j0, sa_sc)
        scores(j0 + 2, sa_sc)
        softmax_pv(j0 + 1, sb_sc)
        return carry

    scores(0, sa_sc)
    lax.fori_loop(0, qi // 2, pair, 0)
    odd = lax.rem(qi, 2) == 1

    @pl.when(odd)
    def _():
        scores(qi, sb_sc)
        softmax_pv(qi - 1, sa_sc)
        softmax_pv(qi, sb_sc, diagonal=True)

    @pl.when(jnp.logical_not(odd))
    def _():
        softmax_pv(qi, sa_sc, diagonal=True)

    outs = []
    for mm in range(2):
        acc = acc_sc[mm]
        outs.append((acc[0:dv, :] / acc[dv:dv + 1, :]).T)
    lam = _lambda(lq1_ref, lk1_ref, lq2_ref, lk2_ref)
    o_ref[...] = _diff_finalize(outs[0], outs[1], lam, wsub_ref[...]).astype(o_ref.dtype)


def _prompt_attention(q, kb, vb, lam_params, w_subln, batch, seq, tile):
    t = q.shape[0]
    tq = tile
    nq = seq // tq
    lq1, lk1, lq2, lk2 = [p.reshape(1, ATT_HEAD_DIM) for p in lam_params]
    vec = pl.BlockSpec((1, ATT_HEAD_DIM), lambda b, h, i: (0, 0))
    kv_spec = pl.BlockSpec((seq, LANES), lambda b, h, i: (b, h))
    qo_spec = pl.BlockSpec((tq, LANES), lambda b, h, i: (b * nq + i, h))
    return pl.pallas_call(
        functools.partial(_attn_kernel, tile=tile),
        grid=(batch, N_ATT_HEADS, nq),
        in_specs=[qo_spec, kv_spec, kv_spec, vec, vec, vec, vec,
                  pl.BlockSpec((1, ATT_V_DIM), lambda b, h, i: (0, 0))],
        out_specs=qo_spec,
        out_shape=jax.ShapeDtypeStruct((t, D_ATTN), _BF16),
        scratch_shapes=[pltpu.VMEM((2, LANES, tile), _BF16),
                        pltpu.VMEM((seq // tile, ATT_V_DIM + 2 * SUBLANES, tile), _BF16),
                        pltpu.VMEM((tile, tile), _F32),
                        pltpu.VMEM((2, tile, tile), _F32), pltpu.VMEM((2, tile, tile), _F32),
                        pltpu.VMEM((2, SUBLANES, tile), _F32),
                        pltpu.VMEM((2, ATT_V_DIM + 2 * SUBLANES, tile), _F32)],
        compiler_params=_cparams(("arbitrary", "arbitrary", "arbitrary")),
        name="prompt_attn",
    )(q, kb, vb, lq1, lk1, lq2, lk2, w_subln.reshape(1, ATT_V_DIM))


def _ssd_kernel(xbc_ref, z_ref, dt_ref, dtt_ref, cprev_ref, h0_ref, convw_ref, convb_ref, dtb_row_ref, dtb_col_ref,
                alog_row_ref, alog_col_ref, dskip_ref, wnorm_ref, y_ref, hout_ref,
                xp_sc, st_sc, yg_sc, *, n_valid, precise):
    c = pl.program_id(1)
    L = SSM_CHUNK
    n_pairs = N_SSM_HEADS // 2
    hp = 2 * SSM_HEAD_DIM

    @pl.when(c == 0)
    def _():
        xp_sc[0:SUBLANES, :] = jnp.zeros((SUBLANES, CONV_DIM), _F32)
        xp_sc[SUBLANES - (CONV_WIDTH - 1):SUBLANES, :] = cprev_ref[...]
        st_sc[...] = h0_ref[...]

    xp_sc[SUBLANES:SUBLANES + L, :] = xbc_ref[...]
    base = SUBLANES - (CONV_WIDTH - 1)
    conv = convb_ref[...] + convw_ref[0:1, :] * xp_sc[base:base + L, :]
    for j in range(1, CONV_WIDTH):
        conv = conv + convw_ref[j:j + 1, :] * xp_sc[base + j:base + j + L, :]
    xp_sc[0:SUBLANES, :] = xp_sc[L:L + SUBLANES, :]
    xc = _silu(conv)
    xs = xc[:, :D_SSM]

    row_i = lax.broadcasted_iota(jnp.int32, (L, LANES), 0)
    col_i = lax.broadcasted_iota(jnp.int32, (L, LANES), 1)
    dt = _softplus(dt_ref[...] + dtb_row_ref[...])
    dtt = _softplus(dtt_ref[...] + dtb_col_ref[...])
    if n_valid < L:
        dt = jnp.where(row_i < n_valid, dt, 0.0)
        dtt = jnp.where(lax.broadcasted_iota(jnp.int32, dtt.shape, 1) < n_valid, dtt, 0.0)
    ad = dt * (-jnp.exp(alog_row_ref[...]))
    adt = dtt * (-jnp.exp(alog_col_ref[...]))
    tril = (col_i <= row_i).astype(_F32)
    triu = (row_i <= col_i).astype(_F32)
    acs = jnp.dot(tril, ad, preferred_element_type=_F32, precision=_HI)
    acst = jnp.dot(adt, triu, preferred_element_type=_F32, precision=_HI)
    causal = col_i <= row_i
    lane_lo = col_i < SSM_HEAD_DIM

    cb = []
    for g in range(SSM_GROUPS):
        bg = xc[:, D_SSM + g * SSM_STATE:D_SSM + (g + 1) * SSM_STATE]
        cg = xc[:, D_SSM + (SSM_GROUPS + g) * SSM_STATE:D_SSM + (SSM_GROUPS + g + 1) * SSM_STATE]
        cb.append((bg, cg, _dot_nt(cg, bg, precise)))

    for p in range(n_pairs):
        ha, hb = 2 * p, 2 * p + 1
        bg, cg, cbg = cb[ha // (N_SSM_HEADS // SSM_GROUPS)]
        col_a, col_b = acs[:, ha:ha + 1], acs[:, hb:hb + 1]
        lm_a = jnp.where(causal, jnp.exp(col_a - acst[ha:ha + 1, :]), 0.0)
        lm_b = jnp.where(causal, jnp.exp(col_b - acst[hb:hb + 1, :]), 0.0)
        dt_pair = jnp.where(lane_lo, dt[:, ha:ha + 1], dt[:, hb:hb + 1])
        x_pair = xs[:, p * hp:(p + 1) * hp] * dt_pair
        y_diag = jnp.where(lane_lo, _dot(cbg * lm_a, x_pair, precise), _dot(cbg * lm_b, x_pair, precise))
        st = st_sc[p]
        y_off = _dot_nt(cg, st, precise) * jnp.where(lane_lo, jnp.exp(col_a), jnp.exp(col_b))
        last_a, last_b = acs[L - 1:L, ha:ha + 1], acs[L - 1:L, hb:hb + 1]
        decay = jnp.where(lane_lo, jnp.exp(last_a - col_a), jnp.exp(last_b - col_b))
        upd = _dot((x_pair * decay).T, bg, precise)
        row_lo = row_i < SSM_HEAD_DIM
        st_sc[p] = jnp.where(row_lo, jnp.exp(last_a), jnp.exp(last_b)) * st + upd
        y_pair = y_diag + y_off + xs[:, p * hp:(p + 1) * hp] * dskip_ref[:, p * hp:(p + 1) * hp]
        zg = z_ref[:, p * hp:(p + 1) * hp].astype(_F32)
        yg_sc[:, p * hp:(p + 1) * hp] = y_pair * _silu(zg)

    gw = D_SSM // SSM_GROUPS
    for g in range(SSM_GROUPS):
        y_ref[:, g * gw:(g + 1) * gw] = _rms(yg_sc[:, g * gw:(g + 1) * gw],
                                             wnorm_ref[:, g * gw:(g + 1) * gw]).astype(y_ref.dtype)

    hout_ref[...] = st_sc[...]


def _ssd(xbc, z, dt, conv_prev, h0, conv_w, conv_b, dt_bias, a_log, d_skip, w_ssm_norm, batch, n_chunks,
         n_valid, precise, out_dtype):
    L = SSM_CHUNK
    rows = xbc.shape[0]
    n_pairs = N_SSM_HEADS // 2
    dtt = dt[:, :N_SSM_HEADS].T
    pad = LANES - N_SSM_HEADS
    dtb_row = jnp.pad(dt_bias, (0, pad)).reshape(1, LANES)
    alog_row = jnp.pad(a_log, (0, pad)).reshape(1, LANES)
    dskip = jnp.repeat(d_skip, SSM_HEAD_DIM).reshape(1, D_SSM)
    blk = lambda b, c: (b * n_chunks + c, 0)
    fixed = lambda b, c: (0, 0)
    y, h_out = pl.pallas_call(
        functools.partial(_ssd_kernel, n_valid=n_valid, precise=precise),
        grid=(batch, n_chunks),
        in_specs=[
            pl.BlockSpec((L, CONV_DIM), blk),
            pl.BlockSpec((L, D_SSM), blk),
            pl.BlockSpec((L, LANES), blk),
            pl.BlockSpec((N_SSM_HEADS, L), lambda b, c: (0, b * n_chunks + c)),
            pl.BlockSpec((None, CONV_WIDTH - 1, CONV_DIM), lambda b, c: (b, 0, 0)),
            pl.BlockSpec((None, n_pairs, 2 * SSM_HEAD_DIM, SSM_STATE), lambda b, c: (b, 0, 0, 0)),
            pl.BlockSpec((CONV_WIDTH, CONV_DIM), fixed),
            pl.BlockSpec((1, CONV_DIM), fixed),
            pl.BlockSpec((1, LANES), fixed),
            pl.BlockSpec((N_SSM_HEADS, 1), fixed),
            pl.BlockSpec((1, LANES), fixed),
            pl.BlockSpec((N_SSM_HEADS, 1), fixed),
            pl.BlockSpec((1, D_SSM), fixed),
            pl.BlockSpec((1, D_SSM), fixed),
        ],
        out_specs=[pl.BlockSpec((L, D_SSM), blk),
                   pl.BlockSpec((None, n_pairs, 2 * SSM_HEAD_DIM, SSM_STATE), lambda b, c: (b, 0, 0, 0))],
        out_shape=[jax.ShapeDtypeStruct((rows, D_SSM), out_dtype),
                   jax.ShapeDtypeStruct((batch, n_pairs, 2 * SSM_HEAD_DIM, SSM_STATE), _F32)],
        scratch_shapes=[pltpu.VMEM((SUBLANES + L, CONV_DIM), _F32),
                        pltpu.VMEM((n_pairs, 2 * SSM_HEAD_DIM, SSM_STATE), _F32),
                        pltpu.VMEM((L, D_SSM), _F32)],
        compiler_params=_cparams(("arbitrary", "arbitrary")),
        name="ssd_precise" if precise else "ssd",
    )(xbc, z, dt, dtt, conv_prev, h0.reshape(batch, n_pairs, 2 * SSM_HEAD_DIM, SSM_STATE),
      conv_w, conv_b.reshape(1, CONV_DIM), dtb_row, dt_bias.reshape(N_SSM_HEADS, 1),
      alog_row, a_log.reshape(N_SSM_HEADS, 1), dskip, w_ssm_norm.reshape(1, D_SSM))
    return y, h_out.reshape(batch, N_SSM_HEADS, SSM_HEAD_DIM, SSM_STATE)


def _outproj_router_kernel(x_ref, attn_ref, ssd_ref, wa_ref, wb_ref, wn_ref, wr_ref, br_ref, *rest, precise, n_blocks):
    if precise:
        h1_ref, t_ref, route_ref = rest[1:4]
        wa, wb = wa_ref, wb_ref
    else:
        h1_ref, t_ref, route_ref, wa, wb = rest
        _cast_weights_once(wa_ref, wa)
        _cast_weights_once(wb_ref, wb)

    @pl.when(pl.program_id(0) >= n_blocks)
    def _():
        t_ref[...] = jnp.zeros(t_ref.shape, _F32)

    @pl.when(pl.program_id(0) < n_blocks)
    def _():
        _outproj_router_body(x_ref, attn_ref, ssd_ref, wa, wb, wn_ref, wr_ref, br_ref, h1_ref, t_ref, route_ref, precise)


def _outproj_router_body(x_ref, attn_ref, ssd_ref, wa, wb, wn_ref, wr_ref, br_ref, h1_ref, t_ref, route_ref, precise):
    h1 = x_ref[...] + _dot(attn_ref[...], wa[...], precise) + _dot(ssd_ref[...], wb[...], precise)
    h1_ref[...] = h1
    t = _rms(h1, wn_ref[...])
    t_ref[...] = t
    logits = _dot(t, wr_ref[...], precise) + br_ref[...]
    lane = lax.broadcasted_iota(jnp.int32, logits.shape, 1).astype(_F32)
    neg = -jnp.inf
    big = float(LANES)
    is_g = lane < N_EXPERT_GROUPS
    gl = jnp.where(is_g, logits, neg)
    gmax = jnp.max(gl, axis=1, keepdims=True)
    g_idx = jnp.min(jnp.where(gl == gmax, lane, big), axis=1, keepdims=True)
    g_w = 1.0 / jnp.sum(jnp.where(is_g, jnp.exp(logits - gmax), 0.0), axis=1, keepdims=True)
    lo = N_EXPERT_GROUPS + EXPERTS_PER_GROUP * g_idx
    l1 = jnp.where(lane >= lo, jnp.where(lane < lo + EXPERTS_PER_GROUP, logits, neg), neg)
    m1 = jnp.max(l1, axis=1, keepdims=True)
    i1 = jnp.min(jnp.where(l1 == m1, lane, big), axis=1, keepdims=True)
    l2 = jnp.where(lane == i1, neg, l1)
    m2 = jnp.max(l2, axis=1, keepdims=True)
    i2 = jnp.min(jnp.where(l2 == m2, lane, big), axis=1, keepdims=True)
    r = jnp.exp(m2 - m1)
    p1 = 1.0 / (1.0 + r)
    w1 = g_w * p1
    w2 = g_w * (r * p1)
    e1 = i1 - N_EXPERT_GROUPS
    e2 = i2 - N_EXPERT_GROUPS
    route_ref[...] = jnp.where(lane == 0, e1, jnp.where(lane == 1, e2, jnp.where(lane == 2, w1,
                               jnp.where(lane == 3, w2, 0.0))))


def _outproj_router(x2d, attn, ssd, w_out, w_norm_ffn, w_router, b_router, tm, precise, t_rows, t_shared=None):
    t = x2d.shape[0]
    n_blocks = t // tm
    wdt = _F32 if precise else _BF16
    half_w = (D_ATTN, D_MODEL)
    fixed = lambda i: (0, 0)
    if precise:
        assert t_shared is not None and (t_rows - t) % tm == 0
        grid = (n_blocks,)
        row = lambda i: (i, 0)
        t_row = lambda i: (i + (t_rows - t) // tm, 0)
        extra_in, extra_args, aliases, scratch = [pl.BlockSpec(memory_space=pl.ANY)], [t_shared], {8: 1}, []
    else:
        grid = (n_blocks + pl.cdiv(t_rows - t, tm),)
        row = lambda i: (jnp.minimum(i, n_blocks - 1), 0)
        t_row = lambda i: (i, 0)
        extra_in, extra_args, aliases = [], [], {}
        scratch = [pltpu.VMEM(half_w, _BF16), pltpu.VMEM(half_w, _BF16)]
    return pl.pallas_call(
        functools.partial(_outproj_router_kernel, precise=precise, n_blocks=n_blocks),
        grid=grid,
        in_specs=[pl.BlockSpec((tm, D_MODEL), row), pl.BlockSpec((tm, D_ATTN), row), pl.BlockSpec((tm, D_SSM), row),
                  _resident(half_w, fixed), _resident(half_w, lambda i: (1, 0)),
                  pl.BlockSpec((1, D_MODEL), fixed), pl.BlockSpec((D_MODEL, LANES), fixed),
                  pl.BlockSpec((1, LANES), fixed)] + extra_in,
        out_specs=[pl.BlockSpec((tm, D_MODEL), row), pl.BlockSpec((tm, D_MODEL), t_row), pl.BlockSpec((tm, LANES), row)],
        out_shape=[jax.ShapeDtypeStruct((t, D_MODEL), _F32), jax.ShapeDtypeStruct((t_rows, D_MODEL), _F32),
                   jax.ShapeDtypeStruct((t, LANES), _F32)],
        scratch_shapes=scratch,
        input_output_aliases=aliases,
        compiler_params=_cparams(("arbitrary",)),
        name="outproj_router_precise" if precise else "outproj_router",
    )(x2d, attn, ssd, w_out, w_out, w_norm_ffn.reshape(1, D_MODEL), w_router.astype(wdt), b_router, *extra_args)


def _expert_kernel(te_ref, nv_ref, idx_prev_ref, idx_ref, idx_next_ref, t_hbm, wg_ref, wu_ref, wd_ref, y_hbm,
                   xbuf, ybuf, sem_in, sem_out, *, tm, n_tiles):
    i = pl.program_id(0)
    nv = nv_ref[i]
    nv_next = jnp.where(i + 1 < n_tiles, nv_ref[jnp.minimum(i + 1, n_tiles - 1)], 0)
    f_half = D_FF_EXPERT // 2
    n_issue_chunks = 4
    per_chunk = tm // n_issue_chunks

    def gather_rows(idx, s, rows):
        for r in rows:
            pltpu.make_async_copy(t_hbm.at[pl.ds(idx[0, 0, r], 1), :], xbuf.at[s, pl.ds(r, 1), :], sem_in.at[s]).start()

    def wait_gather(s):
        pltpu.make_async_copy(t_hbm.at[pl.ds(0, tm), :], xbuf.at[s], sem_in.at[s]).wait()

    def scatter_rows(idx, s, rows):
        for r in rows:
            pltpu.make_async_copy(ybuf.at[s, pl.ds(r, 1), :], y_hbm.at[pl.ds(idx[0, 0, tm + r], 1), :],
                                  sem_out.at[s]).start()

    def wait_scatter(s):
        pltpu.make_async_copy(ybuf.at[s], y_hbm.at[pl.ds(0, tm), :], sem_out.at[s]).wait()

    @pl.when(i == 0)
    def _():
        gather_rows(idx_ref, 0, range(tm))
        half = tm // 2
        n_pad = y_hbm.shape[0] // 2
        ybuf[0, 0:half, :] = jnp.zeros((half, D_MODEL), _F32)
        for k in range(2):
            fill = pltpu.make_async_copy(ybuf.at[0, pl.ds(0, half), :],
                                         y_hbm.at[pl.ds((k + 1) * n_pad - half, half), :], sem_out.at[0])
            fill.start()
            fill.wait()

    def tile(s, has_prev):
        o = 1 - s

        def issue(c):
            rows = range(c * per_chunk, (c + 1) * per_chunk)
            gather_rows(idx_next_ref, o, rows)
            if has_prev:
                scatter_rows(idx_prev_ref, o, rows)

        wait_gather(s)
        x = xbuf[s].astype(_BF16)
        parts = []
        for c in range(2):
            cols = slice(c * f_half, (c + 1) * f_half)
            hg = jnp.dot(x, wg_ref[:, cols].astype(_BF16), preferred_element_type=_F32)
            issue(2 * c)
            hu = jnp.dot(x, wu_ref[:, cols].astype(_BF16), preferred_element_type=_F32)
            issue(2 * c + 1)
            parts.append((_silu(hg) * hu).astype(_BF16))
        hh = jnp.concatenate(parts, axis=1)
        ybuf[s] = jnp.dot(hh, wd_ref[...].astype(_BF16), preferred_element_type=_F32)
        if has_prev:
            wait_scatter(o)

        @pl.when(nv_next == 0)
        def _():
            wait_gather(o)
            scatter_rows(idx_ref, s, range(tm))
            wait_scatter(s)

    used = nv > 0
    odd = lax.rem(i, 2) == 1

    @pl.when(i == 0)
    def _():
        tile(0, False)

    @pl.when(used & (i > 0) & jnp.logical_not(odd))
    def _():
        tile(0, True)

    @pl.when(used & odd)
    def _():
        tile(1, True)


def _moe_plan(e_ids, tm, n_tiles):
    n = e_ids.shape[0]
    n_assign = 2 * n
    half = tm // 2
    n_pad = n + half
    e_flat = e_ids.reshape(-1)
    order = jnp.argsort(e_flat, stable=True).astype(jnp.int32)
    counts = jnp.sum((e_flat[:, None] == jnp.arange(N_EXPERTS, dtype=jnp.int32)[None, :]).astype(jnp.int32), axis=0)
    seg_start = jnp.cumsum(counts) - counts
    tiles_per = (counts + tm - 1) // tm
    cum_tiles = jnp.cumsum(tiles_per)
    tile_start = cum_tiles - tiles_per
    n_used = cum_tiles[-1]
    tile_id = jnp.arange(n_tiles, dtype=jnp.int32)
    used = tile_id < n_used
    te = jnp.sum((cum_tiles[None, :] <= jnp.minimum(tile_id, n_used - 1)[:, None]).astype(jnp.int32), axis=1)
    j = tile_id - tile_start[te]
    nvalid = jnp.where(used, jnp.clip(counts[te] - j * tm, 0, tm), 0).astype(jnp.int32)
    r = jnp.arange(tm, dtype=jnp.int32)[None, :]
    pos = seg_start[te][:, None] + j[:, None] * tm + r
    a = order[jnp.clip(pos, 0, n_assign - 1)]
    tok = a // 2
    spare = (r // half) * n_pad + n + (r % half)
    dst = jnp.where(r < nvalid[:, None], (a % 2) * n_pad + tok, spare)
    idx = jnp.concatenate([tok, dst], axis=1).reshape(n_tiles, 1, 2 * tm).astype(jnp.int32)
    return te, nvalid, idx, n_pad


def _expert_mlp(t_all, e_ids, w_gate, w_up, w_down):
    n = t_all.shape[0]
    tm = EXPERT_TILE
    n_tiles = pl.cdiv(2 * n, tm) + N_EXPERTS
    te, nvalid, idx, n_pad = _moe_plan(e_ids, tm, n_tiles)
    wmap = lambda i, te_ref, nv_ref: (te_ref[i], 0, 0)
    grid_spec = pltpu.PrefetchScalarGridSpec(
        num_scalar_prefetch=2,
        grid=(n_tiles,),
        in_specs=[
            pl.BlockSpec((1, 1, 2 * tm), lambda i, te_ref, nv_ref: (jnp.maximum(i - 1, 0), 0, 0),
                         memory_space=pltpu.SMEM),
            pl.BlockSpec((1, 1, 2 * tm), lambda i, te_ref, nv_ref: (i, 0, 0), memory_space=pltpu.SMEM),
            pl.BlockSpec((1, 1, 2 * tm), lambda i, te_ref, nv_ref: (jnp.minimum(i + 1, n_tiles - 1), 0, 0),
                         memory_space=pltpu.SMEM),
            pl.BlockSpec(memory_space=pl.ANY),
            pl.BlockSpec((None, D_MODEL, D_FF_EXPERT), wmap),
            pl.BlockSpec((None, D_MODEL, D_FF_EXPERT), wmap),
            pl.BlockSpec((None, D_FF_EXPERT, D_MODEL), wmap),
        ],
        out_specs=pl.BlockSpec(memory_space=pl.ANY),
        scratch_shapes=[pltpu.VMEM((2, tm, D_MODEL), _F32), pltpu.VMEM((2, tm, D_MODEL), _F32),
                        pltpu.SemaphoreType.DMA((2,)), pltpu.SemaphoreType.DMA((2,))],
    )
    y_rows = pl.pallas_call(
        functools.partial(_expert_kernel, tm=tm, n_tiles=n_tiles),
        grid_spec=grid_spec,
        out_shape=jax.ShapeDtypeStruct((2 * n_pad, D_MODEL), _F32),
        compiler_params=_cparams(("arbitrary",)),
        name="expert_mlp",
    )(te, nvalid, idx, idx, idx, t_all, w_gate, w_up, w_down)
    return y_rows.reshape(2, n_pad, D_MODEL)


def _combine_ple_kernel(h1_ref, y_ref, route_ref, p_ref, wn_ref, wg_ref, wp_ref, wf_ref, o_ref, wg_sc):
    _cast_weights_once(wg_ref, wg_sc)
    route = route_ref[...]
    h2 = h1_ref[...] + route[:, 2:3] * y_ref[0] + route[:, 3:4] * y_ref[1]
    u = _rms(h2, wn_ref[...])
    gate = jnp.dot(u.astype(_BF16), wg_sc[...], preferred_element_type=_F32)
    gate = 1.0 / (1.0 + jnp.exp(-gate))
    pp = jnp.dot(p_ref[...].astype(_BF16), wp_ref[...], preferred_element_type=_F32)
    h3 = h2 + gate * pp
    o_ref[...] = _rms(h3, wf_ref[...])


def _combine_ple(h1, y_stk, route, p2d, w_norm_ple, w_ple_gate, w_ple_proj, w_norm_final, tm, row_off):
    t = h1.shape[0]
    off = row_off // tm
    row = lambda i: (i, 0)
    fixed = lambda i: (0, 0)
    return pl.pallas_call(
        _combine_ple_kernel,
        grid=(t // tm,),
        in_specs=[pl.BlockSpec((tm, D_MODEL), row),
                  pl.BlockSpec((2, tm, D_MODEL), lambda i: (0, i + off, 0)),
                  pl.BlockSpec((tm, LANES), row), pl.BlockSpec((tm, D_PLE), row),
                  pl.BlockSpec((1, D_MODEL), fixed), _resident((D_MODEL, D_MODEL), fixed),
                  pl.BlockSpec((D_PLE, D_MODEL), fixed), pl.BlockSpec((1, D_MODEL), fixed)],
        out_specs=pl.BlockSpec((tm, D_MODEL), row),
        out_shape=jax.ShapeDtypeStruct((t, D_MODEL), _F32),
        scratch_shapes=[pltpu.VMEM((D_MODEL, D_MODEL), _BF16)],
        compiler_params=_cparams(("arbitrary",)),
        name="combine_ple",
    )(h1, y_stk, route, p2d, w_norm_ple.reshape(1, D_MODEL), w_ple_gate, w_ple_proj.astype(_BF16),
      w_norm_final.reshape(1, D_MODEL))


def _sample_inproj_kernel(x_ref, wn_ref, w_ref, o_ref):
    u = _rms(x_ref[...], wn_ref[...])
    o_ref[...] = lax.dot_general(w_ref[...], u, _NT, preferred_element_type=_F32, precision=_HI)


def _sample_inproj(xs2d, w_norm, w_in_t, tn):
    rows = xs2d.shape[0]
    return pl.pallas_call(
        _sample_inproj_kernel,
        grid=(pl.cdiv(D_IN_PROJ, tn),),
        in_specs=[pl.BlockSpec((rows, D_MODEL), lambda j: (0, 0)), pl.BlockSpec((1, D_MODEL), lambda j: (0, 0)),
                  pl.BlockSpec((tn, D_MODEL), lambda j: (j, 0))],
        out_specs=pl.BlockSpec((tn, rows), lambda j: (j, 0)),
        out_shape=jax.ShapeDtypeStruct((D_IN_PROJ, rows), _F32),
        compiler_params=_cparams(("arbitrary",)),
        name="sample_inproj",
    )(xs2d, w_norm.reshape(1, D_MODEL), w_in_t).T


def _split2(x):
    hi = x.astype(_BF16)
    lo = (x - hi.astype(_F32)).astype(_BF16)
    return hi, lo


def _decode_attn_kernel(pt_ref, qkv_ref, c_ref, s1_ref, s2_ref, lq1_ref, lk1_ref, lq2_ref, lk2_ref, wsub_ref, *rest,
                        n_pages_step, n_steps, n_real):
    k_refs = rest[:n_pages_step]
    v_refs = rest[n_pages_step:2 * n_pages_step]
    o_ref, knew_ref, vnew_ref = rest[2 * n_pages_step:2 * n_pages_step + 3]
    q_sc, m_sc, l_sc, acc_sc, kt_sc, vt_sc = rest[2 * n_pages_step + 3:]
    j = pl.program_id(1)
    R = SAMPLE_ROWS
    hr = 2 * R
    page = kt_sc.shape[0] // N_ATT_HEADS

    @pl.when(j == 0)
    def _():
        c, s1, s2 = c_ref[...], s1_ref[...], s2_ref[...]
        lane = lax.broadcasted_iota(jnp.int32, (R, LANES), 1)
        kt_sc[...] = jnp.zeros(kt_sc.shape, _F32)
        vt_sc[...] = jnp.zeros(vt_sc.shape, _F32)
        for h in range(N_ATT_HEADS):
            sl = slice(h * LANES, (h + 1) * LANES)
            q = _rope_block(qkv_ref[:, sl], c, s1, s2) * (ATT_HEAD_DIM ** -0.5)
            k = _rope_block(qkv_ref[:, D_ATTN + h * LANES:D_ATTN + (h + 1) * LANES], c, s1, s2)
            v = qkv_ref[:, 2 * D_ATTN + h * LANES:2 * D_ATTN + (h + 1) * LANES]
            knew_ref[:, sl] = k
            vnew_ref[:, sl] = v
            kt_sc[pl.ds(h, R, stride=N_ATT_HEADS), :] = k
            vt_sc[pl.ds(h, R, stride=N_ATT_HEADS), :] = v
            q2 = jnp.concatenate([jnp.where(lane < ATT_HEAD_DIM, q, 0.0), jnp.where(lane >= ATT_HEAD_DIM, q, 0.0)], axis=0)
            hi, lo = _split2(q2)
            q_sc[h] = jnp.concatenate([hi, lo], axis=0)
        m_sc[...] = jnp.full(m_sc.shape, -jnp.inf, _F32)
        l_sc[...] = jnp.zeros(l_sc.shape, _F32)
        acc_sc[...] = jnp.zeros(acc_sc.shape, _F32)

    def head_rows(tiles, h):
        x = jnp.concatenate([t[pl.ds(h, page, stride=N_ATT_HEADS), :] for t in tiles], axis=0)
        return _split2(x)

    def process(k_tiles, v_tiles, mask):
        s_parts = []
        for h in range(N_ATT_HEADS):
            k_hi, k_lo = head_rows(k_tiles, h)
            q3 = q_sc[h]
            a = lax.dot_general(q3[0:2 * hr], k_hi, _NT, preferred_element_type=_F32)
            b = lax.dot_general(q3[0:hr], k_lo, _NT, preferred_element_type=_F32)
            s_parts.append(a[0:hr] + a[hr:2 * hr] + b)
        s = jnp.concatenate(s_parts, axis=0)
        if mask is not None:
            s = jnp.where(mask, s, -jnp.inf)
        m_prev = m_sc[...]
        m_new = jnp.maximum(m_prev, jnp.max(s, axis=1, keepdims=True))
        alpha = jnp.exp(m_prev - m_new)
        p = jnp.exp(s - m_new[:, :1])
        l_sc[...] = alpha * l_sc[...] + jnp.sum(p, axis=1, keepdims=True)
        m_sc[...] = m_new
        pv_parts = []
        for h in range(N_ATT_HEADS):
            v_hi, v_lo = head_rows(v_tiles, h)
            p_hi, p_lo = _split2(p[h * hr:(h + 1) * hr])
            a = jnp.dot(jnp.concatenate([p_hi, p_lo], axis=0), v_hi, preferred_element_type=_F32)
            b = jnp.dot(p_hi, v_lo, preferred_element_type=_F32)
            pv_parts.append(a[0:hr] + a[hr:2 * hr] + b)
        acc_sc[...] = alpha * acc_sc[...] + jnp.concatenate(pv_parts, axis=0)

    process(k_refs, v_refs, None)

    @pl.when(j == n_steps - 1)
    def _():
        rows = N_ATT_HEADS * hr
        row = lax.broadcasted_iota(jnp.int32, (rows, page), 0)
        col = lax.broadcasted_iota(jnp.int32, (rows, page), 1)
        qrow = row % R
        process([kt_sc], [vt_sc], (col <= qrow) & (col < n_real))
        o = acc_sc[...] / l_sc[...]
        lam = _lambda(lq1_ref, lk1_ref, lq2_ref, lk2_ref)
        wsub = wsub_ref[...]
        for h in range(N_ATT_HEADS):
            o1 = o[h * hr:h * hr + R]
            o2 = o[h * hr + R:(h + 1) * hr]
            o_ref[:, h * LANES:(h + 1) * LANES] = _diff_finalize(o1, o2, lam, wsub)


def _decode_attention(proj3, cache_k2, cache_v2, page_table, first_pos, lam_params, w_subln, n_real):
    bsz = proj3.shape[0]
    page = cache_k2.shape[2] // N_ATT_HEADS
    n_pages = page_table.shape[1]
    pg = PAGES_PER_STEP
    n_steps = n_pages // pg
    R = SAMPLE_ROWS
    c, s1, s2 = _rope_tables(first_pos, R)
    lq1, lk1, lq2, lk2 = [p.reshape(1, ATT_HEAD_DIM) for p in lam_params]
    fixed = lambda b, j, pt: (0, 0)
    tab = pl.BlockSpec((R, LANES), fixed)
    vec = pl.BlockSpec((1, ATT_HEAD_DIM), fixed)

    def page_spec(i):
        return pl.BlockSpec((None, None, page * N_ATT_HEADS, LANES), lambda b, j, pt: (0, pt[b, j * pg + i], 0, 0))

    out_row = pl.BlockSpec((None, R, D_ATTN), lambda b, j, pt: (b, 0, 0))
    grid_spec = pltpu.PrefetchScalarGridSpec(
        num_scalar_prefetch=1,
        grid=(bsz, n_steps),
        in_specs=[pl.BlockSpec((None, R, 3 * D_ATTN), lambda b, j, pt: (b, 0, 0)), tab, tab, tab, vec, vec, vec, vec,
                  pl.BlockSpec((1, ATT_V_DIM), fixed)]
                 + [page_spec(i) for i in range(pg)] + [page_spec(i) for i in range(pg)],
        out_specs=[out_row, out_row, out_row],
        scratch_shapes=[pltpu.VMEM((N_ATT_HEADS, 4 * R, LANES), _BF16),
                        pltpu.VMEM((N_ATT_HEADS * 2 * R, LANES), _F32), pltpu.VMEM((N_ATT_HEADS * 2 * R, LANES), _F32),
                        pltpu.VMEM((N_ATT_HEADS * 2 * R, LANES), _F32),
                        pltpu.VMEM((page * N_ATT_HEADS, LANES), _F32), pltpu.VMEM((page * N_ATT_HEADS, LANES), _F32)],
    )
    return pl.pallas_call(
        functools.partial(_decode_attn_kernel, n_pages_step=pg, n_steps=n_steps, n_real=n_real),
        grid_spec=grid_spec,
        out_shape=[jax.ShapeDtypeStruct((bsz, R, D_ATTN), _F32)] * 3,
        compiler_params=_cparams(("arbitrary", "arbitrary")),
        name="decode_attn",
    )(page_table, proj3, c, s1, s2, lq1, lk1, lq2, lk2, w_subln.reshape(1, ATT_V_DIM),
      *([cache_k2] * pg), *([cache_v2] * pg))


def kernel(x_prompt, x_sample, cache_k, cache_v, state_conv, state_ssm, page_table, p_prompt, p_sample, w_norm_mix, w_in, lambda_q1, lambda_k1, lambda_q2, lambda_k2, w_subln, conv_w, conv_b, dt_bias, A_log, D_skip, w_ssm_norm, w_out, w_norm_ffn, w_group_router, b_group_router, w_expert_router, b_expert_router, w_exp_gate, w_exp_up, w_exp_down, w_norm_ple, w_ple_gate, w_ple_proj, w_norm_final):
    bp, lp, _ = x_prompt.shape
    bs, ls, _ = x_sample.shape
    past = page_table.shape[1] * cache_k.shape[2]
    tp = bp * lp
    R = SAMPLE_ROWS
    ts = bs * R
    lam_params = (lambda_q1[0], lambda_k1[0], lambda_q2[0], lambda_k2[0])
    pad_r = LANES - N_EXPERT_GROUPS - N_EXPERTS
    w_router = jnp.pad(jnp.concatenate([w_group_router[0], w_expert_router[0]], axis=1), ((0, 0), (0, pad_r)))
    b_router = jnp.pad(jnp.concatenate([b_group_router[0], b_expert_router[0]]), (0, pad_r)).reshape(1, LANES)

    xp2d = x_prompt.reshape(tp, D_MODEL)
    w_in_t = w_in[0].T
    q, kf, kb, vf, vb, z, xbc, dt = _prompt_inproj(xp2d, w_norm_mix[0], w_in_t, lp, 512)
    attn_p = _prompt_attention(q, kb, vb, lam_params, w_subln[0], bp, lp, 512)
    ssd_p, ssm_p = _ssd(xbc, z, dt, jnp.zeros((bp, CONV_WIDTH - 1, CONV_DIM), _F32),
                        jnp.zeros((bp, N_SSM_HEADS, SSM_HEAD_DIM, SSM_STATE), _F32),
                        conv_w[0], conv_b[0], dt_bias[0], A_log[0], D_skip[0], w_ssm_norm[0],
                        bp, lp // SSM_CHUNK, SSM_CHUNK, False, _BF16)
    h1_p, t_all, route_p = _outproj_router(xp2d, attn_p, ssd_p, w_out[0], w_norm_ffn[0], w_router, b_router, 256, False,
                                           tp + ts)

    xs3 = jnp.pad(x_sample, ((0, 0), (0, R - ls), (0, 0)))
    xs2d = xs3.reshape(ts, D_MODEL)
    proj_s = _sample_inproj(xs2d, w_norm_mix[0], w_in_t, 512)
    proj3 = proj_s.reshape(bs, R, D_IN_PROJ)
    cache_rows = cache_k.shape[:2] + (cache_k.shape[2] * N_ATT_HEADS, LANES)
    attn_s, k_s, v_s = _decode_attention(proj3, cache_k.reshape(cache_rows), cache_v.reshape(cache_rows), page_table,
                                         past, lam_params, w_subln[0], ls)
    off = 3 * D_ATTN
    pad_rows = ((0, 0), (0, SSM_CHUNK - R), (0, 0))
    z_s = jnp.pad(proj3[:, :, off:off + D_SSM], pad_rows).reshape(bs * SSM_CHUNK, D_SSM)
    xbc_s3 = proj3[:, :, off + D_SSM:off + D_SSM + CONV_DIM]
    xbc_s = jnp.pad(xbc_s3, pad_rows).reshape(bs * SSM_CHUNK, CONV_DIM)
    dt_s = jnp.pad(proj3[:, :, off + D_SSM + CONV_DIM:], ((0, 0), (0, SSM_CHUNK - R), (0, LANES - N_SSM_HEADS)))
    dt_s = dt_s.reshape(bs * SSM_CHUNK, LANES)
    ssd_s, ssm_s = _ssd(xbc_s, z_s, dt_s, state_conv[0], state_ssm[0], conv_w[0], conv_b[0], dt_bias[0], A_log[0],
                        D_skip[0], w_ssm_norm[0], bs, 1, ls, True, _F32)
    ssd_s = ssd_s.reshape(bs, SSM_CHUNK, D_SSM)[:, :R].reshape(ts, D_SSM)
    h1_s, t_all, route_s = _outproj_router(xs2d, attn_s.reshape(ts, D_ATTN), ssd_s, w_out[0], w_norm_ffn[0],
                                           w_router, b_router, ts, True, tp + ts, t_shared=t_all)

    e_ids = jnp.concatenate([route_p[:, :2], route_s[:, :2]], axis=0).astype(jnp.int32)
    y_stk = _expert_mlp(t_all, e_ids, w_exp_gate[0], w_exp_up[0], w_exp_down[0])
    y_p = _combine_ple(h1_p, y_stk, route_p, p_prompt[0].reshape(tp, D_PLE), w_norm_ple[0], w_ple_gate[0],
                       w_ple_proj[0], w_norm_final, 256, 0)
    ps2d = jnp.pad(p_sample[0], ((0, 0), (0, R - ls), (0, 0))).reshape(ts, D_PLE)
    y_s = _combine_ple(h1_s, y_stk, route_s, ps2d, w_norm_ple[0], w_ple_gate[0], w_ple_proj[0], w_norm_final, ts, tp)

    y_prompt = y_p.reshape(bp, lp, D_MODEL)
    y_sample = y_s.reshape(bs, R, D_MODEL)[:, :ls]
    k_prompt = kf.reshape(1, bp, lp, N_ATT_HEADS, 2 * ATT_HEAD_DIM)
    v_prompt = vf.reshape(1, bp, lp, N_ATT_HEADS, ATT_V_DIM)
    conv_prompt = xbc.reshape(bp, lp, CONV_DIM)[:, lp - (CONV_WIDTH - 1):][None]
    ssm_prompt = ssm_p[None]
    k_sample = k_s[:, :ls].reshape(1, bs, ls, N_ATT_HEADS, 2 * ATT_HEAD_DIM)
    v_sample = v_s[:, :ls].reshape(1, bs, ls, N_ATT_HEADS, ATT_V_DIM)
    conv_sample = xbc_s3[:, ls - (CONV_WIDTH - 1):ls][None]
    ssm_sample = ssm_s[None]
    return (y_prompt, y_sample, k_prompt, v_prompt, conv_prompt, ssm_prompt, k_sample, v_sample, conv_sample, ssm_sample)
```

```python
import functools
import math

import jax
import jax.numpy as jnp
from jax import lax
from jax.experimental import pallas as pl
from jax.experimental.pallas import tpu as pltpu

D_MODEL = 2048
D_ATTN = 1024
D_SSM = 1024
ATT_HEAD_DIM = 64
N_ATT_HEADS = 8
ATT_V_DIM = 128
ROT_DIM = 16
ROPE_THETA = 500000.0
SSM_HEAD_DIM = 64
N_SSM_HEADS = 16
SSM_GROUPS = 2
SSM_STATE = 128
SSM_CHUNK = 128
CONV_WIDTH = 4
CONV_DIM = D_SSM + 2 * SSM_GROUPS * SSM_STATE
D_IN_PROJ = 3 * D_ATTN + D_SSM + CONV_DIM + N_SSM_HEADS
N_EXPERT_GROUPS = 4
EXPERTS_PER_GROUP = 8
N_EXPERTS = 32
D_FF_EXPERT = 512
D_PLE = 256
RMS_EPS = 1e-6
LAM_INIT = 0.8 - 0.6 * math.exp(-0.3 * 0)

LANES = 128
SUBLANES = 8
VMEM_LIMIT_BYTES = 56 * 2 ** 20

SAMPLE_ROWS = 8
EXPERT_TILE = 256
PAGES_PER_STEP = 8

_F32 = jnp.float32
_BF16 = jnp.bfloat16
_HI = lax.Precision.HIGHEST
_NT = (((1,), (1,)), ((), ()))


def _cparams(semantics):
    return pltpu.CompilerParams(dimension_semantics=semantics, vmem_limit_bytes=VMEM_LIMIT_BYTES)


def _rms(x, w):
    return x * lax.rsqrt(jnp.mean(x * x, axis=-1, keepdims=True) + RMS_EPS) * w


def _silu(x):
    return x * (1.0 / (1.0 + jnp.exp(-x)))


def _softplus(x):
    return jnp.maximum(x, 0.0) + jnp.log(1.0 + jnp.exp(-jnp.abs(x)))


def _dot(a, b, precise=False):
    if precise:
        return jnp.dot(a.astype(_F32), b.astype(_F32), preferred_element_type=_F32, precision=_HI)
    return jnp.dot(a.astype(_BF16), b.astype(_BF16), preferred_element_type=_F32)


def _dot_nt(a, b, precise=False):
    if precise:
        return lax.dot_general(a.astype(_F32), b.astype(_F32), _NT, preferred_element_type=_F32, precision=_HI)
    return lax.dot_general(a.astype(_BF16), b.astype(_BF16), _NT, preferred_element_type=_F32)


def _rope_block(y, c, s1, s2):
    return y * c + pltpu.roll(y, LANES - ROT_DIM // 2, 1) * s1 + pltpu.roll(y, ROT_DIM // 2, 1) * s2


def _rope_tables(first_pos, n):
    half = ROT_DIM // 2
    pos = first_pos + jnp.arange(n, dtype=jnp.int32)
    inv_freq = jnp.power(ROPE_THETA, -jnp.arange(half, dtype=_F32) * (2.0 / ROT_DIM))
    ang = pos.astype(_F32)[:, None] * inv_freq[None, :]
    cos, sin = jnp.cos(ang), jnp.sin(ang)
    ones = jnp.ones((n, ATT_HEAD_DIM - ROT_DIM), _F32)
    zeros = jnp.zeros((n, ATT_HEAD_DIM - ROT_DIM), _F32)
    zh = jnp.zeros((n, half), _F32)
    c = jnp.concatenate([cos, cos, ones], axis=1)
    s1 = jnp.concatenate([-sin, zh, zeros], axis=1)
    s2 = jnp.concatenate([zh, sin, zeros], axis=1)
    return tuple(jnp.concatenate([t, t], axis=1) for t in (c, s1, s2))


def _cast_weights_once(w_ref, wb_sc, transposed=False, valid_out=None):
    @pl.when(pl.program_id(0) == 0)
    def _():
        if transposed:
            w = w_ref[0:wb_sc.shape[1], :]
            if valid_out is not None:
                w = jnp.where(lax.broadcasted_iota(jnp.int32, w.shape, 0) < valid_out, w, 0.0)
            w = w.T
        else:
            w = w_ref[...]
        wb_sc[...] = w.astype(wb_sc.dtype)


def _resident(shape, index_map):
    return pl.BlockSpec(shape, index_map, pipeline_mode=pl.Buffered(1))


def _inproj_qk_kernel(x_ref, wn_ref, w_ref, c_ref, s1_ref, s2_ref, q_ref, kf_ref, kb_ref, wb_sc):
    _cast_weights_once(w_ref, wb_sc, transposed=True)
    u = _rms(x_ref[...], wn_ref[...]).astype(_BF16)
    y = jnp.dot(u, wb_sc[...], preferred_element_type=_F32)
    c, s1, s2 = c_ref[...], s1_ref[...], s2_ref[...]
    q_scale = (ATT_HEAD_DIM ** -0.5) * math.log2(math.e)
    for j in range(2 * N_ATT_HEADS):
        r = _rope_block(y[:, j * LANES:(j + 1) * LANES], c, s1, s2)
        if j < N_ATT_HEADS:
            q_ref[:, j * LANES:(j + 1) * LANES] = (r * q_scale).astype(_BF16)
        else:
            jj = j - N_ATT_HEADS
            kf_ref[pl.ds(jj, r.shape[0], stride=N_ATT_HEADS), :] = r
            kb_ref[:, jj * LANES:(jj + 1) * LANES] = r.astype(_BF16)


def _inproj_vz_kernel(x_ref, wn_ref, w_ref, vf_ref, vb_ref, z_ref, wb_sc):
    _cast_weights_once(w_ref, wb_sc, transposed=True)
    u = _rms(x_ref[...], wn_ref[...]).astype(_BF16)
    y = jnp.dot(u, wb_sc[...], preferred_element_type=_F32)
    v = y[:, :D_ATTN]
    for h in range(N_ATT_HEADS):
        vf_ref[pl.ds(h, v.shape[0], stride=N_ATT_HEADS), :] = v[:, h * LANES:(h + 1) * LANES]
    vb_ref[...] = v.astype(_BF16)
    z_ref[...] = y[:, D_ATTN:].astype(_BF16)


def _inproj_xbc_kernel(x_ref, wn_ref, w_ref, xbc_ref, dt_ref, wb_sc):
    _cast_weights_once(w_ref, wb_sc, transposed=True, valid_out=CONV_DIM + N_SSM_HEADS)
    u = _rms(x_ref[...], wn_ref[...]).astype(_BF16)
    y = jnp.dot(u, wb_sc[...], preferred_element_type=_F32)
    xbc_ref[...] = y[:, :CONV_DIM]
    dt_ref[...] = y[:, CONV_DIM:]


def _prompt_inproj(x2d, w_norm, w_in_t, seq, tm):
    t = x2d.shape[0]
    nblk = t // tm
    per_seq = seq // tm
    wn = w_norm.reshape(1, D_MODEL)
    row = lambda i: (i, 0)
    fixed = lambda i: (0, 0)
    x_spec = pl.BlockSpec((tm, D_MODEL), row)
    wn_spec = pl.BlockSpec((1, D_MODEL), fixed)
    head_spec = pl.BlockSpec((tm * N_ATT_HEADS, LANES), row)
    wide = 2 * D_ATTN

    c, s1, s2 = _rope_tables(0, seq)
    tab_spec = pl.BlockSpec((tm, LANES), lambda i: (i % per_seq, 0))
    q, kf, kb = pl.pallas_call(
        _inproj_qk_kernel,
        grid=(nblk,),
        in_specs=[x_spec, wn_spec, _resident((wide, D_MODEL), fixed), tab_spec, tab_spec, tab_spec],
        out_specs=[pl.BlockSpec((tm, D_ATTN), row), head_spec, pl.BlockSpec((tm, D_ATTN), row)],
        out_shape=[jax.ShapeDtypeStruct((t, D_ATTN), _BF16), jax.ShapeDtypeStruct((t * N_ATT_HEADS, LANES), _F32),
                   jax.ShapeDtypeStruct((t, D_ATTN), _BF16)],
        scratch_shapes=[pltpu.VMEM((D_MODEL, wide), _BF16)],
        compiler_params=_cparams(("arbitrary",)),
        name="inproj_qk",
    )(x2d, wn, w_in_t, c, s1, s2)

    vf, vb, z = pl.pallas_call(
        _inproj_vz_kernel,
        grid=(nblk,),
        in_specs=[x_spec, wn_spec, _resident((wide, D_MODEL), lambda i: (1, 0))],
        out_specs=[head_spec, pl.BlockSpec((tm, D_ATTN), row), pl.BlockSpec((tm, D_SSM), row)],
        out_shape=[jax.ShapeDtypeStruct((t * N_ATT_HEADS, LANES), _F32), jax.ShapeDtypeStruct((t, D_ATTN), _BF16),
                   jax.ShapeDtypeStruct((t, D_SSM), _BF16)],
        scratch_shapes=[pltpu.VMEM((D_MODEL, wide), _BF16)],
        compiler_params=_cparams(("arbitrary",)),
        name="inproj_vz",
    )(x2d, wn, w_in_t)

    n_tail = CONV_DIM + LANES
    xbc, dt = pl.pallas_call(
        _inproj_xbc_kernel,
        grid=(nblk,),
        in_specs=[x_spec, wn_spec, _resident((wide, D_MODEL), lambda i: (2, 0))],
        out_specs=[pl.BlockSpec((tm, CONV_DIM), row), pl.BlockSpec((tm, LANES), row)],
        out_shape=[jax.ShapeDtypeStruct((t, CONV_DIM), _F32), jax.ShapeDtypeStruct((t, LANES), _F32)],
        scratch_shapes=[pltpu.VMEM((D_MODEL, n_tail), _BF16)],
        compiler_params=_cparams(("arbitrary",)),
        name="inproj_xbc",
    )(x2d, wn, w_in_t)
    return q, kf, kb, vf, vb, z, xbc, dt


def _lambda(lq1_ref, lk1_ref, lq2_ref, lk2_ref):
    a = jnp.sum(lq1_ref[...] * lk1_ref[...], axis=-1, keepdims=True)
    b = jnp.sum(lq2_ref[...] * lk2_ref[...], axis=-1, keepdims=True)
    return jnp.exp(a) - jnp.exp(b) + LAM_INIT


def _diff_finalize(o1, o2, lam, wsub):
    a = o1 - lam * o2
    return _rms(a, wsub) * (1.0 - LAM_INIT)


def _attn_kernel(q_ref, k_ref, v_ref, lq1_ref, lk1_ref, lq2_ref, lk2_ref, wsub_ref, o_ref,
                 qt_sc, vt_sc, bias_sc, sa_sc, sb_sc, m_sc, acc_sc, *, tile):
    qi = pl.program_id(2)
    n_kt = vt_sc.shape[0]
    dv = ATT_V_DIM
    masked_out = -1e30

    @pl.when(qi == 0)
    def _():
        for c in range(n_kt):
            vt_sc[c, 0:dv, :] = v_ref[c * tile:(c + 1) * tile, :].astype(_F32).T.astype(_BF16)
            vt_sc[c, dv:, :] = jnp.ones((vt_sc.shape[1] - dv, tile), _BF16)
        key = lax.broadcasted_iota(jnp.int32, (tile, tile), 0)
        qry = lax.broadcasted_iota(jnp.int32, (tile, tile), 1)
        bias_sc[...] = jnp.where(key <= qry, 0.0, masked_out)

    q = q_ref[...].astype(_F32)
    lane = lax.broadcasted_iota(jnp.int32, (tile, LANES), 1)
    qt_sc[0] = jnp.where(lane < ATT_HEAD_DIM, q, 0.0).T.astype(_BF16)
    qt_sc[1] = jnp.where(lane >= ATT_HEAD_DIM, q, 0.0).T.astype(_BF16)
    m_sc[...] = jnp.full(m_sc.shape, -jnp.inf, _F32)
    acc_sc[...] = jnp.zeros(acc_sc.shape, _F32)

    def scores(j, dst):
        k = k_ref[pl.ds(pl.multiple_of(j * tile, tile), tile), :]
        for mm in range(2):
            dst[mm] = jnp.dot(k, qt_sc[mm], preferred_element_type=_F32)

    def softmax_pv(j, src, diagonal=False):
        vt = vt_sc[j]
        for mm in range(2):
            st = src[mm]
            if diagonal:
                st = st + bias_sc[...]
            m_prev = m_sc[mm]
            m_new = jnp.maximum(m_prev, jnp.max(st, axis=0, keepdims=True))
            alpha = jnp.exp2(m_prev - m_new)
            pt = jnp.exp2(st - m_new[0:1, :]).astype(_BF16)
            acc_sc[mm] = alpha[0:1, :] * acc_sc[mm] + jnp.dot(vt, pt, preferred_element_type=_F32)
            m_sc[mm] = m_new

    def pair(jj, carry):
        j0 = 2 * jj
        scores(j0 + 1, sb_sc)
        softmax_pv(j0, sa_sc)
        scores(j0 + 2, sa_sc)
        softmax_pv(j0 + 1, sb_sc)
        return carry

    scores(0, sa_sc)
    lax.fori_loop(0, qi // 2, pair, 0)
    odd = lax.rem(qi, 2) == 1

    @pl.when(odd)
    def _():
        scores(qi, sb_sc)
        softmax_pv(qi - 1, sa_sc)
        softmax_pv(qi, sb_sc, diagonal=True)

    @pl.when(jnp.logical_not(odd))
    def _():
        softmax_pv(qi, sa_sc, diagonal=True)

    outs = []
    for mm in range(2):
        acc = acc_sc[mm]
        outs.append((acc[0:dv, :] / acc[dv:dv + 1, :]).T)
    lam = _lambda(lq1_ref, lk1_ref, lq2_ref, lk2_ref)
    o_ref[...] = _diff_finalize(outs[0], outs[1], lam, wsub_ref[...]).astype(o_ref.dtype)


def _prompt_attention(q, kb, vb, lam_params, w_subln, batch, seq, tile):
    t = q.shape[0]
    tq = tile
    nq = seq // tq
    lq1, lk1, lq2, lk2 = [p.reshape(1, ATT_HEAD_DIM) for p in lam_params]
    vec = pl.BlockSpec((1, ATT_HEAD_DIM), lambda b, h, i: (0, 0))
    kv_spec = pl.BlockSpec((seq, LANES), lambda b, h, i: (b, h))
    qo_spec = pl.BlockSpec((tq, LANES), lambda b, h, i: (b * nq + i, h))
    return pl.pallas_call(
        functools.partial(_attn_kernel, tile=tile),
        grid=(batch, N_ATT_HEADS, nq),
        in_specs=[qo_spec, kv_spec, kv_spec, vec, vec, vec, vec,
                  pl.BlockSpec((1, ATT_V_DIM), lambda b, h, i: (0, 0))],
        out_specs=qo_spec,
        out_shape=jax.ShapeDtypeStruct((t, D_ATTN), _BF16),
        scratch_shapes=[pltpu.VMEM((2, LANES, tile), _BF16),
                        pltpu.VMEM((seq // tile, ATT_V_DIM + 2 * SUBLANES, tile), _BF16),
                        pltpu.VMEM((tile, tile), _F32),
                        pltpu.VMEM((2, tile, tile), _F32), pltpu.VMEM((2, tile, tile), _F32),
                        pltpu.VMEM((2, SUBLANES, tile), _F32),
                        pltpu.VMEM((2, ATT_V_DIM + 2 * SUBLANES, tile), _F32)],
        compiler_params=_cparams(("arbitrary", "arbitrary", "arbitrary")),
        name="prompt_attn",
    )(q, kb, vb, lq1, lk1, lq2, lk2, w_subln.reshape(1, ATT_V_DIM))


def _ssd_kernel(xbc_ref, z_ref, dt_ref, dtt_ref, cprev_ref, h0_ref, convw_ref, convb_ref, dtb_row_ref, dtb_col_ref,
                alog_row_ref, alog_col_ref, dskip_ref, wnorm_ref, y_ref, hout_ref,
                xp_sc, st_sc, yg_sc, *, n_valid, precise):
    c = pl.program_id(1)
    L = SSM_CHUNK
    n_pairs = N_SSM_HEADS // 2
    hp = 2 * SSM_HEAD_DIM

    @pl.when(c == 0)
    def _():
        xp_sc[0:SUBLANES, :] = jnp.zeros((SUBLANES, CONV_DIM), _F32)
        xp_sc[SUBLANES - (CONV_WIDTH - 1):SUBLANES, :] = cprev_ref[...]
        st_sc[...] = h0_ref[...]

    xp_sc[SUBLANES:SUBLANES + L, :] = xbc_ref[...]
    base = SUBLANES - (CONV_WIDTH - 1)
    conv = convb_ref[...] + convw_ref[0:1, :] * xp_sc[base:base + L, :]
    for j in range(1, CONV_WIDTH):
        conv = conv + convw_ref[j:j + 1, :] * xp_sc[base + j:base + j + L, :]
    xp_sc[0:SUBLANES, :] = xp_sc[L:L + SUBLANES, :]
    xc = _silu(conv)
    xs = xc[:, :D_SSM]

    row_i = lax.broadcasted_iota(jnp.int32, (L, LANES), 0)
    col_i = lax.broadcasted_iota(jnp.int32, (L, LANES), 1)
    dt = _softplus(dt_ref[...] + dtb_row_ref[...])
    dtt = _softplus(dtt_ref[...] + dtb_col_ref[...])
    if n_valid < L:
        dt = jnp.where(row_i < n_valid, dt, 0.0)
        dtt = jnp.where(lax.broadcasted_iota(jnp.int32, dtt.shape, 1) < n_valid, dtt, 0.0)
    ad = dt * (-jnp.exp(alog_row_ref[...]))
    adt = dtt * (-jnp.exp(alog_col_ref[...]))
    tril = (col_i <= row_i).astype(_F32)
    triu = (row_i <= col_i).astype(_F32)
    acs = jnp.dot(tril, ad, preferred_element_type=_F32, precision=_HI)
    acst = jnp.dot(adt, triu, preferred_element_type=_F32, precision=_HI)
    causal = col_i <= row_i
    lane_lo = col_i < SSM_HEAD_DIM

    cb = []
    for g in range(SSM_GROUPS):
        bg = xc[:, D_SSM + g * SSM_STATE:D_SSM + (g + 1) * SSM_STATE]
        cg = xc[:, D_SSM + (SSM_GROUPS + g) * SSM_STATE:D_SSM + (SSM_GROUPS + g + 1) * SSM_STATE]
        cb.append((bg, cg, _dot_nt(cg, bg, precise)))

    for p in range(n_pairs):
        ha, hb = 2 * p, 2 * p + 1
        bg, cg, cbg = cb[ha // (N_SSM_HEADS // SSM_GROUPS)]
        col_a, col_b = acs[:, ha:ha + 1], acs[:, hb:hb + 1]
        lm_a = jnp.where(causal, jnp.exp(col_a - acst[ha:ha + 1, :]), 0.0)
        lm_b = jnp.where(causal, jnp.exp(col_b - acst[hb:hb + 1, :]), 0.0)
        dt_pair = jnp.where(lane_lo, dt[:, ha:ha + 1], dt[:, hb:hb + 1])
        x_pair = xs[:, p * hp:(p + 1) * hp] * dt_pair
        y_diag = jnp.where(lane_lo, _dot(cbg * lm_a, x_pair, precise), _dot(cbg * lm_b, x_pair, precise))
        st = st_sc[p]
        y_off = _dot_nt(cg, st, precise) * jnp.where(lane_lo, jnp.exp(col_a), jnp.exp(col_b))
        last_a, last_b = acs[L - 1:L, ha:ha + 1], acs[L - 1:L, hb:hb + 1]
        decay = jnp.where(lane_lo, jnp.exp(last_a - col_a), jnp.exp(last_b - col_b))
        upd = _dot((x_pair * decay).T, bg, precise)
        row_lo = row_i < SSM_HEAD_DIM
        st_sc[p] = jnp.where(row_lo, jnp.exp(last_a), jnp.exp(last_b)) * st + upd
        y_pair = y_diag + y_off + xs[:, p * hp:(p + 1) * hp] * dskip_ref[:, p * hp:(p + 1) * hp]
        zg = z_ref[:, p * hp:(p + 1) * hp].astype(_F32)
        yg_sc[:, p * hp:(p + 1) * hp] = y_pair * _silu(zg)

    gw = D_SSM // SSM_GROUPS
    for g in range(SSM_GROUPS):
        y_ref[:, g * gw:(g + 1) * gw] = _rms(yg_sc[:, g * gw:(g + 1) * gw],
                                             wnorm_ref[:, g * gw:(g + 1) * gw]).astype(y_ref.dtype)

    hout_ref[...] = st_sc[...]


def _ssd(xbc, z, dt, conv_prev, h0, conv_w, conv_b, dt_bias, a_log, d_skip, w_ssm_norm, batch, n_chunks,
         n_valid, precise, out_dtype):
    L = SSM_CHUNK
    rows = xbc.shape[0]
    n_pairs = N_SSM_HEADS // 2
    dtt = dt[:, :N_SSM_HEADS].T
    pad = LANES - N_SSM_HEADS
    dtb_row = jnp.pad(dt_bias, (0, pad)).reshape(1, LANES)
    alog_row = jnp.pad(a_log, (0, pad)).reshape(1, LANES)
    dskip = jnp.repeat(d_skip, SSM_HEAD_DIM).reshape(1, D_SSM)
    blk = lambda b, c: (b * n_chunks + c, 0)
    fixed = lambda b, c: (0, 0)
    y, h_out = pl.pallas_call(
        functools.partial(_ssd_kernel, n_valid=n_valid, precise=precise),
        grid=(batch, n_chunks),
        in_specs=[
            pl.BlockSpec((L, CONV_DIM), blk),
            pl.BlockSpec((L, D_SSM), blk),
            pl.BlockSpec((L, LANES), blk),
            pl.BlockSpec((N_SSM_HEADS, L), lambda b, c: (0, b * n_chunks + c)),
            pl.BlockSpec((None, CONV_WIDTH - 1, CONV_DIM), lambda b, c: (b, 0, 0)),
            pl.BlockSpec((None, n_pairs, 2 * SSM_HEAD_DIM, SSM_STATE), lambda b, c: (b, 0, 0, 0)),
            pl.BlockSpec((CONV_WIDTH, CONV_DIM), fixed),
            pl.BlockSpec((1, CONV_DIM), fixed),
            pl.BlockSpec((1, LANES), fixed),
            pl.BlockSpec((N_SSM_HEADS, 1), fixed),
            pl.BlockSpec((1, LANES), fixed),
            pl.BlockSpec((N_SSM_HEADS, 1), fixed),
            pl.BlockSpec((1, D_SSM), fixed),
            pl.BlockSpec((1, D_SSM), fixed),
        ],
        out_specs=[pl.BlockSpec((L, D_SSM), blk),
                   pl.BlockSpec((None, n_pairs, 2 * SSM_HEAD_DIM, SSM_STATE), lambda b, c: (b, 0, 0, 0))],
        out_shape=[jax.ShapeDtypeStruct((rows, D_SSM), out_dtype),
                   jax.ShapeDtypeStruct((batch, n_pairs, 2 * SSM_HEAD_DIM, SSM_STATE), _F32)],
        scratch_shapes=[pltpu.VMEM((SUBLANES + L, CONV_DIM), _F32),
                        pltpu.VMEM((n_pairs, 2 * SSM_HEAD_DIM, SSM_STATE), _F32),
                        pltpu.VMEM((L, D_SSM), _F32)],
        compiler_params=_cparams(("arbitrary", "arbitrary")),
        name="ssd_precise" if precise else "ssd",
    )(xbc, z, dt, dtt, conv_prev, h0.reshape(batch, n_pairs, 2 * SSM_HEAD_DIM, SSM_STATE),
      conv_w, conv_b.reshape(1, CONV_DIM), dtb_row, dt_bias.reshape(N_SSM_HEADS, 1),
      alog_row, a_log.reshape(N_SSM_HEADS, 1), dskip, w_ssm_norm.reshape(1, D_SSM))
    return y, h_out.reshape(batch, N_SSM_HEADS, SSM_HEAD_DIM, SSM_STATE)


def _outproj_router_kernel(x_ref, attn_ref, ssd_ref, wa_ref, wb_ref, wn_ref, wr_ref, br_ref, *rest, precise, n_blocks):
    if precise:
        h1_ref, t_ref, route_ref = rest[1:4]
        wa, wb = wa_ref, wb_ref
    else:
        h1_ref, t_ref, route_ref, wa, wb = rest
        _cast_weights_once(wa_ref, wa)
        _cast_weights_once(wb_ref, wb)

    @pl.when(pl.program_id(0) >= n_blocks)
    def _():
        t_ref[...] = jnp.zeros(t_ref.shape, _F32)

    @pl.when(pl.program_id(0) < n_blocks)
    def _():
        _outproj_router_body(x_ref, attn_ref, ssd_ref, wa, wb, wn_ref, wr_ref, br_ref, h1_ref, t_ref, route_ref, precise)


def _outproj_router_body(x_ref, attn_ref, ssd_ref, wa, wb, wn_ref, wr_ref, br_ref, h1_ref, t_ref, route_ref, precise):
    h1 = x_ref[...] + _dot(attn_ref[...], wa[...], precise) + _dot(ssd_ref[...], wb[...], precise)
    h1_ref[...] = h1
    t = _rms(h1, wn_ref[...])
    t_ref[...] = t
    logits = _dot(t, wr_ref[...], precise) + br_ref[...]
    lane = lax.broadcasted_iota(jnp.int32, logits.shape, 1).astype(_F32)
    neg = -jnp.inf
    big = float(LANES)
    is_g = lane < N_EXPERT_GROUPS
    gl = jnp.where(is_g, logits, neg)
    gmax = jnp.max(gl, axis=1, keepdims=True)
    g_idx = jnp.min(jnp.where(gl == gmax, lane, big), axis=1, keepdims=True)
    g_w = 1.0 / jnp.sum(jnp.where(is_g, jnp.exp(logits - gmax), 0.0), axis=1, keepdims=True)
    lo = N_EXPERT_GROUPS + EXPERTS_PER_GROUP * g_idx
    l1 = jnp.where(lane >= lo, jnp.where(lane < lo + EXPERTS_PER_GROUP, logits, neg), neg)
    m1 = jnp.max(l1, axis=1, keepdims=True)
    i1 = jnp.min(jnp.where(l1 == m1, lane, big), axis=1, keepdims=True)
    l2 = jnp.where(lane == i1, neg, l1)
    m2 = jnp.max(l2, axis=1, keepdims=True)
    i2 = jnp.min(jnp.where(l2 == m2, lane, big), axis=1, keepdims=True)
    r = jnp.exp(m2 - m1)
    p1 = 1.0 / (1.0 + r)
    w1 = g_w * p1
    w2 = g_w * (r * p1)
    e1 = i1 - N_EXPERT_GROUPS
    e2 = i2 - N_EXPERT_GROUPS
    route_ref[...] = jnp.where(lane == 0, e1, jnp.where(lane == 1, e2, jnp.where(lane == 2, w1,
                               jnp.where(lane == 3, w2, 0.0))))


def _outproj_router(x2d, attn, ssd, w_out, w_norm_ffn, w_router, b_router, tm, precise, t_rows, t_shared=None):
    t = x2d.shape[0]
    n_blocks = t // tm
    wdt = _F32 if precise else _BF16
    half_w = (D_ATTN, D_MODEL)
    fixed = lambda i: (0, 0)
    if precise:
        assert t_shared is not None and (t_rows - t) % tm == 0
        grid = (n_blocks,)
        row = lambda i: (i, 0)
        t_row = lambda i: (i + (t_rows - t) // tm, 0)
        extra_in, extra_args, aliases, scratch = [pl.BlockSpec(memory_space=pl.ANY)], [t_shared], {8: 1}, []
    else:
        grid = (n_blocks + pl.cdiv(t_rows - t, tm),)
        row = lambda i: (jnp.minimum(i, n_blocks - 1), 0)
        t_row = lambda i: (i, 0)
        extra_in, extra_args, aliases = [], [], {}
        scratch = [pltpu.VMEM(half_w, _BF16), pltpu.VMEM(half_w, _BF16)]
    return pl.pallas_call(
        functools.partial(_outproj_router_kernel, precise=precise, n_blocks=n_blocks),
        grid=grid,
        in_specs=[pl.BlockSpec((tm, D_MODEL), row), pl.BlockSpec((tm, D_ATTN), row), pl.BlockSpec((tm, D_SSM), row),
                  _resident(half_w, fixed), _resident(half_w, lambda i: (1, 0)),
                  pl.BlockSpec((1, D_MODEL), fixed), pl.BlockSpec((D_MODEL, LANES), fixed),
                  pl.BlockSpec((1, LANES), fixed)] + extra_in,
        out_specs=[pl.BlockSpec((tm, D_MODEL), row), pl.BlockSpec((tm, D_MODEL), t_row), pl.BlockSpec((tm, LANES), row)],
        out_shape=[jax.ShapeDtypeStruct((t, D_MODEL), _F32), jax.ShapeDtypeStruct((t_rows, D_MODEL), _F32),
                   jax.ShapeDtypeStruct((t, LANES), _F32)],
        scratch_shapes=scratch,
        input_output_aliases=aliases,
        compiler_params=_cparams(("arbitrary",)),
        name="outproj_router_precise" if precise else "outproj_router",
    )(x2d, attn, ssd, w_out, w_out, w_norm_ffn.reshape(1, D_MODEL), w_router.astype(wdt), b_router, *extra_args)


def _expert_kernel(te_ref, nv_ref, blk_ref, off_ref, idx_prev_ref, idx_ref, idx_next_ref, t_hbm, wg_ref, wu_ref, wd_ref,
                   y_hbm, xbuf, ybuf, sem_in, sem_out, *, tm, n_tiles):
    i = pl.program_id(0)
    i_prev = jnp.maximum(i - 1, 0)
    i_next = jnp.minimum(i + 1, n_tiles - 1)
    nv = nv_ref[i]
    nv_next = jnp.where(i + 1 < n_tiles, nv_ref[i_next], 0)
    f_half = D_FF_EXPERT // 2
    n_issue_chunks = 4
    per_chunk = tm // n_issue_chunks
    half = tm // 2
    n_pad = y_hbm.shape[0] // 2

    def gather_rows(idx, tile_i, s, rows):
        off = off_ref[tile_i]
        for r in rows:
            pltpu.make_async_copy(t_hbm.at[pl.ds(idx[0, 0, off + r], 1), :], xbuf.at[s, pl.ds(r, 1), :],
                                  sem_in.at[s]).start()

    def wait_gather(s):
        pltpu.make_async_copy(t_hbm.at[pl.ds(0, tm), :], xbuf.at[s], sem_in.at[s]).wait()

    def scatter_rows(idx, tile_i, s, rows):
        off = off_ref[tile_i] + 2 * tm
        valid = nv_ref[tile_i]
        for r in rows:
            spare = (r // half) * n_pad + (n_pad - half) + (r % half)
            dst = jnp.where(r < valid, idx[0, 0, off + r], spare)
            pltpu.make_async_copy(ybuf.at[s, pl.ds(r, 1), :], y_hbm.at[pl.ds(dst, 1), :], sem_out.at[s]).start()

    def wait_scatter(s):
        pltpu.make_async_copy(ybuf.at[s], y_hbm.at[pl.ds(0, tm), :], sem_out.at[s]).wait()

    @pl.when(i == 0)
    def _():
        gather_rows(idx_ref, i, 0, range(tm))
        ybuf[0, 0:half, :] = jnp.zeros((half, D_MODEL), _F32)
        for k in range(2):
            fill = pltpu.make_async_copy(ybuf.at[0, pl.ds(0, half), :],
                                         y_hbm.at[pl.ds((k + 1) * n_pad - half, half), :], sem_out.at[0])
            fill.start()
            fill.wait()

    def tile(s, has_prev):
        o = 1 - s

        def issue(c):
            rows = range(c * per_chunk, (c + 1) * per_chunk)
            gather_rows(idx_next_ref, i_next, o, rows)
            if has_prev:
                scatter_rows(idx_prev_ref, i_prev, o, rows)

        wait_gather(s)
        x = xbuf[s].astype(_BF16)
        parts = []
        for c in range(2):
            cols = slice(c * f_half, (c + 1) * f_half)
            hg = jnp.dot(x, wg_ref[:, cols].astype(_BF16), preferred_element_type=_F32)
            issue(2 * c)
            hu = jnp.dot(x, wu_ref[:, cols].astype(_BF16), preferred_element_type=_F32)
            issue(2 * c + 1)
            parts.append((_silu(hg) * hu).astype(_BF16))
        hh = jnp.concatenate(parts, axis=1)
        ybuf[s] = jnp.dot(hh, wd_ref[...].astype(_BF16), preferred_element_type=_F32)
        if has_prev:
            wait_scatter(o)

        @pl.when(nv_next == 0)
        def _():
            wait_gather(o)
            scatter_rows(idx_ref, i, s, range(tm))
            wait_scatter(s)

    used = nv > 0
    odd = lax.rem(i, 2) == 1

    @pl.when(i == 0)
    def _():
        tile(0, False)

    @pl.when(used & (i > 0) & jnp.logical_not(odd))
    def _():
        tile(0, True)

    @pl.when(used & odd)
    def _():
        tile(1, True)


def _moe_plan(e_ids, tm, n_tiles):
    n = e_ids.shape[0]
    n_assign = 2 * n
    n_pad = n + tm // 2
    e_flat = e_ids.reshape(-1)
    order = jnp.argsort(e_flat, stable=True).astype(jnp.int32)
    experts = jnp.arange(N_EXPERTS, dtype=jnp.int32)
    counts = jnp.sum((e_flat[:, None] == experts[None, :]).astype(jnp.int32), axis=0)
    seg_start = jnp.cumsum(counts) - counts
    tiles_per = (counts + tm - 1) // tm
    cum_tiles = jnp.cumsum(tiles_per)
    tile_start = cum_tiles - tiles_per
    n_used = cum_tiles[-1]
    tile_id = jnp.arange(n_tiles, dtype=jnp.int32)
    used = tile_id < n_used
    te = jnp.sum((cum_tiles[None, :] <= jnp.minimum(tile_id, n_used - 1)[:, None]).astype(jnp.int32), axis=1)
    of_tile = lambda v: jnp.sum(jnp.where(te[:, None] == experts[None, :], v[None, :], 0), axis=1)
    j = tile_id - of_tile(tile_start)
    nvalid = jnp.where(used, jnp.clip(of_tile(counts) - j * tm, 0, tm), 0).astype(jnp.int32)
    first = jnp.minimum(of_tile(seg_start) + j * tm, n_assign - 1)
    n_win = pl.cdiv(n_assign, tm)
    fill = (n_win + 1) * tm - n_assign
    tok = jnp.pad(order // 2, (0, fill)).reshape(n_win + 1, tm)
    dst = jnp.pad((order % 2) * n_pad + order // 2, (0, fill)).reshape(n_win + 1, tm)
    windows = jnp.concatenate([tok[:-1], tok[1:], dst[:-1], dst[1:]], axis=1).reshape(n_win, 1, 4 * tm)
    return te, nvalid, (first // tm).astype(jnp.int32), (first % tm).astype(jnp.int32), windows, n_pad


def _expert_mlp(t_all, e_ids, w_gate, w_up, w_down):
    n = t_all.shape[0]
    tm = EXPERT_TILE
    n_tiles = pl.cdiv(2 * n, tm) + N_EXPERTS
    te, nvalid, blk, off, windows, n_pad = _moe_plan(e_ids, tm, n_tiles)
    wmap = lambda i, te_ref, nv_ref, blk_ref, off_ref: (te_ref[i], 0, 0)

    def window_spec(shift):
        def index_map(i, te_ref, nv_ref, blk_ref, off_ref):
            return (blk_ref[jnp.clip(i + shift, 0, n_tiles - 1)], 0, 0)
        return pl.BlockSpec((1, 1, 4 * tm), index_map, memory_space=pltpu.SMEM)

    grid_spec = pltpu.PrefetchScalarGridSpec(
        num_scalar_prefetch=4,
        grid=(n_tiles,),
        in_specs=[
            window_spec(-1), window_spec(0), window_spec(1),
            pl.BlockSpec(memory_space=pl.ANY),
            pl.BlockSpec((None, D_MODEL, D_FF_EXPERT), wmap),
            pl.BlockSpec((None, D_MODEL, D_FF_EXPERT), wmap),
            pl.BlockSpec((None, D_FF_EXPERT, D_MODEL), wmap),
        ],
        out_specs=pl.BlockSpec(memory_space=pl.ANY),
        scratch_shapes=[pltpu.VMEM((2, tm, D_MODEL), _F32), pltpu.VMEM((2, tm, D_MODEL), _F32),
                        pltpu.SemaphoreType.DMA((2,)), pltpu.SemaphoreType.DMA((2,))],
    )
    y_rows = pl.pallas_call(
        functools.partial(_expert_kernel, tm=tm, n_tiles=n_tiles),
        grid_spec=grid_spec,
        out_shape=jax.ShapeDtypeStruct((2 * n_pad, D_MODEL), _F32),
        compiler_params=_cparams(("arbitrary",)),
        name="expert_mlp",
    )(te, nvalid, blk, off, windows, windows, windows, t_all, w_gate, w_up, w_down)
    return y_rows.reshape(2, n_pad, D_MODEL)


def _combine_ple_kernel(h1_ref, y_ref, route_ref, p_ref, wn_ref, wg_ref, wp_ref, wf_ref, o_ref, wg_sc):
    _cast_weights_once(wg_ref, wg_sc)
    route = route_ref[...]
    h2 = h1_ref[...] + route[:, 2:3] * y_ref[0] + route[:, 3:4] * y_ref[1]
    u = _rms(h2, wn_ref[...])
    gate = jnp.dot(u.astype(_BF16), wg_sc[...], preferred_element_type=_F32)
    gate = 1.0 / (1.0 + jnp.exp(-gate))
    pp = jnp.dot(p_ref[...].astype(_BF16), wp_ref[...], preferred_element_type=_F32)
    h3 = h2 + gate * pp
    o_ref[...] = _rms(h3, wf_ref[...])


def _combine_ple(h1, y_stk, route, p2d, w_norm_ple, w_ple_gate, w_ple_proj, w_norm_final, tm, row_off):
    t = h1.shape[0]
    off = row_off // tm
    row = lambda i: (i, 0)
    fixed = lambda i: (0, 0)
    return pl.pallas_call(
        _combine_ple_kernel,
        grid=(t // tm,),
        in_specs=[pl.BlockSpec((tm, D_MODEL), row),
                  pl.BlockSpec((2, tm, D_MODEL), lambda i: (0, i + off, 0)),
                  pl.BlockSpec((tm, LANES), row), pl.BlockSpec((tm, D_PLE), row),
                  pl.BlockSpec((1, D_MODEL), fixed), _resident((D_MODEL, D_MODEL), fixed),
                  pl.BlockSpec((D_PLE, D_MODEL), fixed), pl.BlockSpec((1, D_MODEL), fixed)],
        out_specs=pl.BlockSpec((tm, D_MODEL), row),
        out_shape=jax.ShapeDtypeStruct((t, D_MODEL), _F32),
        scratch_shapes=[pltpu.VMEM((D_MODEL, D_MODEL), _BF16)],
        compiler_params=_cparams(("arbitrary",)),
        name="combine_ple",
    )(h1, y_stk, route, p2d, w_norm_ple.reshape(1, D_MODEL), w_ple_gate, w_ple_proj.astype(_BF16),
      w_norm_final.reshape(1, D_MODEL))


def _sample_inproj_kernel(x_ref, wn_ref, w_ref, o_ref):
    u = _rms(x_ref[...], wn_ref[...])
    u_hi, u_lo = _split2(u)
    w_hi, w_lo = _split2(w_ref[...])
    nt = lambda a, b: lax.dot_general(a, b, _NT, preferred_element_type=_F32)
    o_ref[...] = nt(w_hi, u_hi) + nt(w_hi, u_lo) + nt(w_lo, u_hi)


def _sample_inproj(xs2d, w_norm, w_in_t, tn):
    rows = xs2d.shape[0]
    return pl.pallas_call(
        _sample_inproj_kernel,
        grid=(pl.cdiv(D_IN_PROJ, tn),),
        in_specs=[pl.BlockSpec((rows, D_MODEL), lambda j: (0, 0)), pl.BlockSpec((1, D_MODEL), lambda j: (0, 0)),
                  pl.BlockSpec((tn, D_MODEL), lambda j: (j, 0))],
        out_specs=pl.BlockSpec((tn, rows), lambda j: (j, 0)),
        out_shape=jax.ShapeDtypeStruct((D_IN_PROJ, rows), _F32),
        compiler_params=_cparams(("arbitrary",)),
        name="sample_inproj",
    )(xs2d, w_norm.reshape(1, D_MODEL), w_in_t).T


def _split2(x):
    hi = x.astype(_BF16)
    lo = (x - hi.astype(_F32)).astype(_BF16)
    return hi, lo


def _decode_attn_kernel(pt_ref, qkv_ref, c_ref, s1_ref, s2_ref, lq1_ref, lk1_ref, lq2_ref, lk2_ref, wsub_ref, *rest,
                        n_pages_step, n_steps, n_real):
    k_refs = rest[:n_pages_step]
    v_refs = rest[n_pages_step:2 * n_pages_step]
    o_ref, knew_ref, vnew_ref = rest[2 * n_pages_step:2 * n_pages_step + 3]
    q_sc, m_sc, l_sc, acc_sc, kt_sc, vt_sc = rest[2 * n_pages_step + 3:]
    j = pl.program_id(1)
    R = SAMPLE_ROWS
    hr = 2 * R
    page = kt_sc.shape[0] // N_ATT_HEADS

    @pl.when(j == 0)
    def _():
        c, s1, s2 = c_ref[...], s1_ref[...], s2_ref[...]
        lane = lax.broadcasted_iota(jnp.int32, (R, LANES), 1)
        kt_sc[...] = jnp.zeros(kt_sc.shape, _F32)
        vt_sc[...] = jnp.zeros(vt_sc.shape, _F32)
        for h in range(N_ATT_HEADS):
            sl = slice(h * LANES, (h + 1) * LANES)
            q = _rope_block(qkv_ref[:, sl], c, s1, s2) * (ATT_HEAD_DIM ** -0.5)
            k = _rope_block(qkv_ref[:, D_ATTN + h * LANES:D_ATTN + (h + 1) * LANES], c, s1, s2)
            v = qkv_ref[:, 2 * D_ATTN + h * LANES:2 * D_ATTN + (h + 1) * LANES]
            knew_ref[:, sl] = k
            vnew_ref[:, sl] = v
            kt_sc[pl.ds(h, R, stride=N_ATT_HEADS), :] = k
            vt_sc[pl.ds(h, R, stride=N_ATT_HEADS), :] = v
            q2 = jnp.concatenate([jnp.where(lane < ATT_HEAD_DIM, q, 0.0), jnp.where(lane >= ATT_HEAD_DIM, q, 0.0)], axis=0)
            hi, lo = _split2(q2)
            q_sc[h] = jnp.concatenate([hi, lo], axis=0)
        m_sc[...] = jnp.full(m_sc.shape, -jnp.inf, _F32)
        l_sc[...] = jnp.zeros(l_sc.shape, _F32)
        acc_sc[...] = jnp.zeros(acc_sc.shape, _F32)

    def head_rows(tiles, h):
        x = jnp.concatenate([t[pl.ds(h, page, stride=N_ATT_HEADS), :] for t in tiles], axis=0)
        return _split2(x)

    def process(k_tiles, v_tiles, mask):
        s_parts = []
        for h in range(N_ATT_HEADS):
            k_hi, k_lo = head_rows(k_tiles, h)
            q3 = q_sc[h]
            a = lax.dot_general(q3[0:2 * hr], k_hi, _NT, preferred_element_type=_F32)
            b = lax.dot_general(q3[0:hr], k_lo, _NT, preferred_element_type=_F32)
            s_parts.append(a[0:hr] + a[hr:2 * hr] + b)
        s = jnp.concatenate(s_parts, axis=0)
        if mask is not None:
            s = jnp.where(mask, s, -jnp.inf)
        m_prev = m_sc[...]
        m_new = jnp.maximum(m_prev, jnp.max(s, axis=1, keepdims=True))
        alpha = jnp.exp(m_prev - m_new)
        p = jnp.exp(s - m_new[:, :1])
        l_sc[...] = alpha * l_sc[...] + jnp.sum(p, axis=1, keepdims=True)
        m_sc[...] = m_new
        pv_parts = []
        for h in range(N_ATT_HEADS):
            v_hi, v_lo = head_rows(v_tiles, h)
            p_hi, p_lo = _split2(p[h * hr:(h + 1) * hr])
            a = jnp.dot(jnp.concatenate([p_hi, p_lo], axis=0), v_hi, preferred_element_type=_F32)
            b = jnp.dot(p_hi, v_lo, preferred_element_type=_F32)
            pv_parts.append(a[0:hr] + a[hr:2 * hr] + b)
        acc_sc[...] = alpha * acc_sc[...] + jnp.concatenate(pv_parts, axis=0)

    process(k_refs, v_refs, None)

    @pl.when(j == n_steps - 1)
    def _():
        rows = N_ATT_HEADS * hr
        row = lax.broadcasted_iota(jnp.int32, (rows, page), 0)
        col = lax.broadcasted_iota(jnp.int32, (rows, page), 1)
        qrow = row % R
        process([kt_sc], [vt_sc], (col <= qrow) & (col < n_real))
        o = acc_sc[...] / l_sc[...]
        lam = _lambda(lq1_ref, lk1_ref, lq2_ref, lk2_ref)
        wsub = wsub_ref[...]
        for h in range(N_ATT_HEADS):
            o1 = o[h * hr:h * hr + R]
            o2 = o[h * hr + R:(h + 1) * hr]
            o_ref[:, h * LANES:(h + 1) * LANES] = _diff_finalize(o1, o2, lam, wsub)


def _decode_attention(proj3, cache_k2, cache_v2, page_table, first_pos, lam_params, w_subln, n_real):
    bsz = proj3.shape[0]
    page = cache_k2.shape[2] // N_ATT_HEADS
    n_pages = page_table.shape[1]
    pg = PAGES_PER_STEP
    n_steps = n_pages // pg
    R = SAMPLE_ROWS
    c, s1, s2 = _rope_tables(first_pos, R)
    lq1, lk1, lq2, lk2 = [p.reshape(1, ATT_HEAD_DIM) for p in lam_params]
    fixed = lambda b, j, pt: (0, 0)
    tab = pl.BlockSpec((R, LANES), fixed)
    vec = pl.BlockSpec((1, ATT_HEAD_DIM), fixed)

    def page_spec(i):
        return pl.BlockSpec((None, None, page * N_ATT_HEADS, LANES), lambda b, j, pt: (0, pt[b, j * pg + i], 0, 0))

    out_row = pl.BlockSpec((None, R, D_ATTN), lambda b, j, pt: (b, 0, 0))
    grid_spec = pltpu.PrefetchScalarGridSpec(
        num_scalar_prefetch=1,
        grid=(bsz, n_steps),
        in_specs=[pl.BlockSpec((None, R, 3 * D_ATTN), lambda b, j, pt: (b, 0, 0)), tab, tab, tab, vec, vec, vec, vec,
                  pl.BlockSpec((1, ATT_V_DIM), fixed)]
                 + [page_spec(i) for i in range(pg)] + [page_spec(i) for i in range(pg)],
        out_specs=[out_row, out_row, out_row],
        scratch_shapes=[pltpu.VMEM((N_ATT_HEADS, 4 * R, LANES), _BF16),
                        pltpu.VMEM((N_ATT_HEADS * 2 * R, LANES), _F32), pltpu.VMEM((N_ATT_HEADS * 2 * R, LANES), _F32),
                        pltpu.VMEM((N_ATT_HEADS * 2 * R, LANES), _F32),
                        pltpu.VMEM((page * N_ATT_HEADS, LANES), _F32), pltpu.VMEM((page * N_ATT_HEADS, LANES), _F32)],
    )
    return pl.pallas_call(
        functools.partial(_decode_attn_kernel, n_pages_step=pg, n_steps=n_steps, n_real=n_real),
        grid_spec=grid_spec,
        out_shape=[jax.ShapeDtypeStruct((bsz, R, D_ATTN), _F32)] * 3,
        compiler_params=_cparams(("arbitrary", "arbitrary")),
        name="decode_attn",
    )(page_table, proj3, c, s1, s2, lq1, lk1, lq2, lk2, w_subln.reshape(1, ATT_V_DIM),
      *([cache_k2] * pg), *([cache_v2] * pg))


def kernel(x_prompt, x_sample, cache_k, cache_v, state_conv, state_ssm, page_table, p_prompt, p_sample, w_norm_mix, w_in, lambda_q1, lambda_k1, lambda_q2, lambda_k2, w_subln, conv_w, conv_b, dt_bias, A_log, D_skip, w_ssm_norm, w_out, w_norm_ffn, w_group_router, b_group_router, w_expert_router, b_expert_router, w_exp_gate, w_exp_up, w_exp_down, w_norm_ple, w_ple_gate, w_ple_proj, w_norm_final):
    bp, lp, _ = x_prompt.shape
    bs, ls, _ = x_sample.shape
    past = page_table.shape[1] * cache_k.shape[2]
    tp = bp * lp
    R = SAMPLE_ROWS
    ts = bs * R
    lam_params = (lambda_q1[0], lambda_k1[0], lambda_q2[0], lambda_k2[0])
    pad_r = LANES - N_EXPERT_GROUPS - N_EXPERTS
    w_router = jnp.pad(jnp.concatenate([w_group_router[0], w_expert_router[0]], axis=1), ((0, 0), (0, pad_r)))
    b_router = jnp.pad(jnp.concatenate([b_group_router[0], b_expert_router[0]]), (0, pad_r)).reshape(1, LANES)

    xp2d = x_prompt.reshape(tp, D_MODEL)
    w_in_t = w_in[0].T
    q, kf, kb, vf, vb, z, xbc, dt = _prompt_inproj(xp2d, w_norm_mix[0], w_in_t, lp, 512)
    attn_p = _prompt_attention(q, kb, vb, lam_params, w_subln[0], bp, lp, 512)
    ssd_p, ssm_p = _ssd(xbc, z, dt, jnp.zeros((bp, CONV_WIDTH - 1, CONV_DIM), _F32),
                        jnp.zeros((bp, N_SSM_HEADS, SSM_HEAD_DIM, SSM_STATE), _F32),
                        conv_w[0], conv_b[0], dt_bias[0], A_log[0], D_skip[0], w_ssm_norm[0],
                        bp, lp // SSM_CHUNK, SSM_CHUNK, False, _BF16)
    h1_p, t_all, route_p = _outproj_router(xp2d, attn_p, ssd_p, w_out[0], w_norm_ffn[0], w_router, b_router, 256, False,
                                           tp + ts)

    xs3 = jnp.pad(x_sample, ((0, 0), (0, R - ls), (0, 0)))
    xs2d = xs3.reshape(ts, D_MODEL)
    proj_s = _sample_inproj(xs2d, w_norm_mix[0], w_in_t, 512)
    proj3 = proj_s.reshape(bs, R, D_IN_PROJ)
    cache_rows = cache_k.shape[:2] + (cache_k.shape[2] * N_ATT_HEADS, LANES)
    attn_s, k_s, v_s = _decode_attention(proj3, cache_k.reshape(cache_rows), cache_v.reshape(cache_rows), page_table,
                                         past, lam_params, w_subln[0], ls)
    off = 3 * D_ATTN
    pad_rows = ((0, 0), (0, SSM_CHUNK - R), (0, 0))
    z_s = jnp.pad(proj3[:, :, off:off + D_SSM], pad_rows).reshape(bs * SSM_CHUNK, D_SSM)
    xbc_s3 = proj3[:, :, off + D_SSM:off + D_SSM + CONV_DIM]
    xbc_s = jnp.pad(xbc_s3, pad_rows).reshape(bs * SSM_CHUNK, CONV_DIM)
    dt_s = jnp.pad(proj3[:, :, off + D_SSM + CONV_DIM:], ((0, 0), (0, SSM_CHUNK - R), (0, LANES - N_SSM_HEADS)))
    dt_s = dt_s.reshape(bs * SSM_CHUNK, LANES)
    ssd_s, ssm_s = _ssd(xbc_s, z_s, dt_s, state_conv[0], state_ssm[0], conv_w[0], conv_b[0], dt_bias[0], A_log[0],
                        D_skip[0], w_ssm_norm[0], bs, 1, ls, True, _F32)
    ssd_s = ssd_s.reshape(bs, SSM_CHUNK, D_SSM)[:, :R].reshape(ts, D_SSM)
    h1_s, t_all, route_s = _outproj_router(xs2d, attn_s.reshape(ts, D_ATTN), ssd_s, w_out[0], w_norm_ffn[0],
                                           w_router, b_router, ts, True, tp + ts, t_shared=t_all)

    e_ids = jnp.concatenate([route_p[:, :2], route_s[:, :2]], axis=0).astype(jnp.int32)
    y_stk = _expert_mlp(t_all, e_ids, w_exp_gate[0], w_exp_up[0], w_exp_down[0])
    y_p = _combine_ple(h1_p, y_stk, route_p, p_prompt[0].reshape(tp, D_PLE), w_norm_ple[0], w_ple_gate[0],
                       w_ple_proj[0], w_norm_final, 256, 0)
    ps2d = jnp.pad(p_sample[0], ((0, 0), (0, R - ls), (0, 0))).reshape(ts, D_PLE)
    y_s = _combine_ple(h1_s, y_stk, route_s, ps2d, w_norm_ple[0], w_ple_gate[0], w_ple_proj[0], w_norm_final, ts, tp)

    y_prompt = y_p.reshape(bp, lp, D_MODEL)
    y_sample = y_s.reshape(bs, R, D_MODEL)[:, :ls]
    k_prompt = kf.reshape(1, bp, lp, N_ATT_HEADS, 2 * ATT_HEAD_DIM)
    v_prompt = vf.reshape(1, bp, lp, N_ATT_HEADS, ATT_V_DIM)
    conv_prompt = xbc.reshape(bp, lp, CONV_DIM)[:, lp - (CONV_WIDTH - 1):][None]
    ssm_prompt = ssm_p[None]
    k_sample = k_s[:, :ls].reshape(1, bs, ls, N_ATT_HEADS, 2 * ATT_HEAD_DIM)
    v_sample = v_s[:, :ls].reshape(1, bs, ls, N_ATT_HEADS, ATT_V_DIM)
    conv_sample = xbc_s3[:, ls - (CONV_WIDTH - 1):ls][None]
    ssm_sample = ssm_s[None]
    return (y_prompt, y_sample, k_prompt, v_prompt, conv_prompt, ssm_prompt, k_sample, v_sample, conv_sample, ssm_sample)
```

```python
import functools
import math

import jax
import jax.numpy as jnp
from jax import lax
from jax.experimental import pallas as pl
from jax.experimental.pallas import tpu as pltpu

D_MODEL = 2048
D_ATTN = 1024
D_SSM = 1024
ATT_HEAD_DIM = 64
N_ATT_HEADS = 8
ATT_V_DIM = 128
ROT_DIM = 16
ROPE_THETA = 500000.0
SSM_HEAD_DIM = 64
N_SSM_HEADS = 16
SSM_GROUPS = 2
SSM_STATE = 128
SSM_CHUNK = 128
CONV_WIDTH = 4
CONV_DIM = D_SSM + 2 * SSM_GROUPS * SSM_STATE
D_IN_PROJ = 3 * D_ATTN + D_SSM + CONV_DIM + N_SSM_HEADS
N_EXPERT_GROUPS = 4
EXPERTS_PER_GROUP = 8
N_EXPERTS = 32
D_FF_EXPERT = 512
D_PLE = 256
RMS_EPS = 1e-6
LAM_INIT = 0.8 - 0.6 * math.exp(-0.3 * 0)

LANES = 128
SUBLANES = 8
VMEM_LIMIT_BYTES = 56 * 2 ** 20

SAMPLE_ROWS = 8
EXPERT_TILE = 256
PAGES_PER_STEP = 16

_F32 = jnp.float32
_BF16 = jnp.bfloat16
_HI = lax.Precision.HIGHEST
_NT = (((1,), (1,)), ((), ()))


def _cparams(semantics):
    return pltpu.CompilerParams(dimension_semantics=semantics, vmem_limit_bytes=VMEM_LIMIT_BYTES)


def _rms(x, w):
    return x * lax.rsqrt(jnp.mean(x * x, axis=-1, keepdims=True) + RMS_EPS) * w


def _silu(x):
    return x * (1.0 / (1.0 + jnp.exp(-x)))


def _softplus(x):
    return jnp.maximum(x, 0.0) + jnp.log(1.0 + jnp.exp(-jnp.abs(x)))


def _dot(a, b, precise=False):
    if precise:
        return jnp.dot(a.astype(_F32), b.astype(_F32), preferred_element_type=_F32, precision=_HI)
    return jnp.dot(a.astype(_BF16), b.astype(_BF16), preferred_element_type=_F32)


def _dot_nt(a, b, precise=False):
    if precise:
        return lax.dot_general(a.astype(_F32), b.astype(_F32), _NT, preferred_element_type=_F32, precision=_HI)
    return lax.dot_general(a.astype(_BF16), b.astype(_BF16), _NT, preferred_element_type=_F32)


def _rope_block(y, c, s1, s2):
    return y * c + pltpu.roll(y, LANES - ROT_DIM // 2, 1) * s1 + pltpu.roll(y, ROT_DIM // 2, 1) * s2


def _rope_tables(first_pos, n):
    half = ROT_DIM // 2
    pos = first_pos + jnp.arange(n, dtype=jnp.int32)
    inv_freq = jnp.power(ROPE_THETA, -jnp.arange(half, dtype=_F32) * (2.0 / ROT_DIM))
    dim = jnp.arange(LANES, dtype=jnp.int32) % ATT_HEAD_DIM
    ang = pos.astype(_F32)[:, None] * inv_freq[dim % half][None, :]
    cos, sin = jnp.cos(ang), jnp.sin(ang)
    first, second = (dim < half)[None, :], ((dim >= half) & (dim < ROT_DIM))[None, :]
    c = jnp.where(first | second, cos, 1.0)
    s1 = jnp.where(first, -sin, 0.0)
    s2 = jnp.where(second, sin, 0.0)
    return c, s1, s2


def _cast_weights_once(w_ref, wb_sc, transposed=False, valid_out=None):
    @pl.when(pl.program_id(0) == 0)
    def _():
        if transposed:
            w = w_ref[0:wb_sc.shape[1], :]
            if valid_out is not None:
                w = jnp.where(lax.broadcasted_iota(jnp.int32, w.shape, 0) < valid_out, w, 0.0)
            w = w.T
        else:
            w = w_ref[...]
        wb_sc[...] = w.astype(wb_sc.dtype)


def _resident(shape, index_map):
    return pl.BlockSpec(shape, index_map, pipeline_mode=pl.Buffered(1))


def _inproj_qk_kernel(x_ref, wn_ref, w_ref, c_ref, s1_ref, s2_ref, q_ref, kf_ref, kb_ref, wb_sc):
    _cast_weights_once(w_ref, wb_sc, transposed=True)
    u = _rms(x_ref[...], wn_ref[...]).astype(_BF16)
    y = jnp.dot(u, wb_sc[...], preferred_element_type=_F32)
    c, s1, s2 = c_ref[...], s1_ref[...], s2_ref[...]
    q_scale = (ATT_HEAD_DIM ** -0.5) * math.log2(math.e)
    for j in range(2 * N_ATT_HEADS):
        r = _rope_block(y[:, j * LANES:(j + 1) * LANES], c, s1, s2)
        if j < N_ATT_HEADS:
            q_ref[:, j * LANES:(j + 1) * LANES] = (r * q_scale).astype(_BF16)
        else:
            jj = j - N_ATT_HEADS
            kf_ref[pl.ds(jj, r.shape[0], stride=N_ATT_HEADS), :] = r
            kb_ref[:, jj * LANES:(jj + 1) * LANES] = r.astype(_BF16)


def _inproj_vz_kernel(x_ref, wn_ref, w_ref, vf_ref, vb_ref, z_ref, wb_sc):
    _cast_weights_once(w_ref, wb_sc, transposed=True)
    u = _rms(x_ref[...], wn_ref[...]).astype(_BF16)
    y = jnp.dot(u, wb_sc[...], preferred_element_type=_F32)
    v = y[:, :D_ATTN]
    for h in range(N_ATT_HEADS):
        vf_ref[pl.ds(h, v.shape[0], stride=N_ATT_HEADS), :] = v[:, h * LANES:(h + 1) * LANES]
    vb_ref[...] = v.astype(_BF16)
    z_ref[...] = y[:, D_ATTN:].astype(_BF16)


def _inproj_xbc_kernel(x_ref, wn_ref, w_ref, xbc_ref, dt_ref, wb_sc):
    _cast_weights_once(w_ref, wb_sc, transposed=True, valid_out=CONV_DIM + N_SSM_HEADS)
    u = _rms(x_ref[...], wn_ref[...]).astype(_BF16)
    y = jnp.dot(u, wb_sc[...], preferred_element_type=_F32)
    xbc_ref[...] = y[:, :CONV_DIM]
    dt_ref[...] = y[:, CONV_DIM:]


def _prompt_inproj(x2d, w_norm, w_in_t, seq, tm):
    t = x2d.shape[0]
    nblk = t // tm
    per_seq = seq // tm
    wn = w_norm.reshape(1, D_MODEL)
    row = lambda i: (i, 0)
    fixed = lambda i: (0, 0)
    x_spec = pl.BlockSpec((tm, D_MODEL), row)
    wn_spec = pl.BlockSpec((1, D_MODEL), fixed)
    head_spec = pl.BlockSpec((tm * N_ATT_HEADS, LANES), row)
    wide = 2 * D_ATTN

    c, s1, s2 = _rope_tables(0, seq)
    tab_spec = pl.BlockSpec((tm, LANES), lambda i: (i % per_seq, 0))
    q, kf, kb = pl.pallas_call(
        _inproj_qk_kernel,
        grid=(nblk,),
        in_specs=[x_spec, wn_spec, _resident((wide, D_MODEL), fixed), tab_spec, tab_spec, tab_spec],
        out_specs=[pl.BlockSpec((tm, D_ATTN), row), head_spec, pl.BlockSpec((tm, D_ATTN), row)],
        out_shape=[jax.ShapeDtypeStruct((t, D_ATTN), _BF16), jax.ShapeDtypeStruct((t * N_ATT_HEADS, LANES), _F32),
                   jax.ShapeDtypeStruct((t, D_ATTN), _BF16)],
        scratch_shapes=[pltpu.VMEM((D_MODEL, wide), _BF16)],
        compiler_params=_cparams(("arbitrary",)),
        name="inproj_qk",
    )(x2d, wn, w_in_t, c, s1, s2)

    vf, vb, z = pl.pallas_call(
        _inproj_vz_kernel,
        grid=(nblk,),
        in_specs=[x_spec, wn_spec, _resident((wide, D_MODEL), lambda i: (1, 0))],
        out_specs=[head_spec, pl.BlockSpec((tm, D_ATTN), row), pl.BlockSpec((tm, D_SSM), row)],
        out_shape=[jax.ShapeDtypeStruct((t * N_ATT_HEADS, LANES), _F32), jax.ShapeDtypeStruct((t, D_ATTN), _BF16),
                   jax.ShapeDtypeStruct((t, D_SSM), _BF16)],
        scratch_shapes=[pltpu.VMEM((D_MODEL, wide), _BF16)],
        compiler_params=_cparams(("arbitrary",)),
        name="inproj_vz",
    )(x2d, wn, w_in_t)

    n_tail = CONV_DIM + LANES
    xbc, dt = pl.pallas_call(
        _inproj_xbc_kernel,
        grid=(nblk,),
        in_specs=[x_spec, wn_spec, _resident((wide, D_MODEL), lambda i: (2, 0))],
        out_specs=[pl.BlockSpec((tm, CONV_DIM), row), pl.BlockSpec((tm, LANES), row)],
        out_shape=[jax.ShapeDtypeStruct((t, CONV_DIM), _F32), jax.ShapeDtypeStruct((t, LANES), _F32)],
        scratch_shapes=[pltpu.VMEM((D_MODEL, n_tail), _BF16)],
        compiler_params=_cparams(("arbitrary",)),
        name="inproj_xbc",
    )(x2d, wn, w_in_t)
    return q, kf, kb, vf, vb, z, xbc, dt


def _lambda(lq1_ref, lk1_ref, lq2_ref, lk2_ref):
    a = jnp.sum(lq1_ref[...] * lk1_ref[...], axis=-1, keepdims=True)
    b = jnp.sum(lq2_ref[...] * lk2_ref[...], axis=-1, keepdims=True)
    return jnp.exp(a) - jnp.exp(b) + LAM_INIT


def _diff_finalize(o1, o2, lam, wsub):
    a = o1 - lam * o2
    return _rms(a, wsub) * (1.0 - LAM_INIT)


def _attn_kernel(q_ref, k_ref, v_ref, lq1_ref, lk1_ref, lq2_ref, lk2_ref, wsub_ref, o_ref,
                 qt_sc, vt_sc, bias_sc, sa_sc, sb_sc, m_sc, acc_sc, *, tile):
    qi = pl.program_id(2)
    n_kt = vt_sc.shape[0]
    dv = ATT_V_DIM
    masked_out = -1e30

    @pl.when(qi == 0)
    def _():
        for c in range(n_kt):
            vt_sc[c, 0:dv, :] = v_ref[c * tile:(c + 1) * tile, :].astype(_F32).T.astype(_BF16)
            vt_sc[c, dv:, :] = jnp.ones((vt_sc.shape[1] - dv, tile), _BF16)
        key = lax.broadcasted_iota(jnp.int32, (tile, tile), 0)
        qry = lax.broadcasted_iota(jnp.int32, (tile, tile), 1)
        bias_sc[...] = jnp.where(key <= qry, 0.0, masked_out)

    q = q_ref[...].astype(_F32)
    lane = lax.broadcasted_iota(jnp.int32, (tile, LANES), 1)
    qt_sc[0] = jnp.where(lane < ATT_HEAD_DIM, q, 0.0).T.astype(_BF16)
    qt_sc[1] = jnp.where(lane >= ATT_HEAD_DIM, q, 0.0).T.astype(_BF16)
    m_sc[...] = jnp.full(m_sc.shape, -jnp.inf, _F32)
    acc_sc[...] = jnp.zeros(acc_sc.shape, _F32)

    def scores(j, dst):
        k = k_ref[pl.ds(pl.multiple_of(j * tile, tile), tile), :]
        for mm in range(2):
            dst[mm] = jnp.dot(k, qt_sc[mm], preferred_element_type=_F32)

    def softmax_pv(j, src, diagonal=False):
        vt = vt_sc[j]
        for mm in range(2):
            st = src[mm]
            if diagonal:
                st = st + bias_sc[...]
            m_prev = m_sc[mm]
            m_new = jnp.maximum(m_prev, jnp.max(st, axis=0, keepdims=True))
            alpha = jnp.exp2(m_prev - m_new)
            pt = jnp.exp2(st - m_new[0:1, :]).astype(_BF16)
            acc_sc[mm] = alpha[0:1, :] * acc_sc[mm] + jnp.dot(vt, pt, preferred_element_type=_F32)
            m_sc[mm] = m_new

    def pair(jj, carry):
        j0 = 2 * jj
        scores(j0 + 1, sb_sc)
        softmax_pv(j0, sa_sc)
        scores(j0 + 2, sa_sc)
        softmax_pv(j0 + 1, sb_sc)
        return carry

    scores(0, sa_sc)
    lax.fori_loop(0, qi // 2, pair, 0)
    odd = lax.rem(qi, 2) == 1

    @pl.when(odd)
    def _():
        scores(qi, sb_sc)
        softmax_pv(qi - 1, sa_sc)
        softmax_pv(qi, sb_sc, diagonal=True)

    @pl.when(jnp.logical_not(odd))
    def _():
        softmax_pv(qi, sa_sc, diagonal=True)

    outs = []
    for mm in range(2):
        acc = acc_sc[mm]
        outs.append((acc[0:dv, :] / acc[dv:dv + 1, :]).T)
    lam = _lambda(lq1_ref, lk1_ref, lq2_ref, lk2_ref)
    o_ref[...] = _diff_finalize(outs[0], outs[1], lam, wsub_ref[...]).astype(o_ref.dtype)


def _prompt_attention(q, kb, vb, lam_params, w_subln, batch, seq, tile):
    t = q.shape[0]
    tq = tile
    nq = seq // tq
    lq1, lk1, lq2, lk2 = [p.reshape(1, ATT_HEAD_DIM) for p in lam_params]
    vec = pl.BlockSpec((1, ATT_HEAD_DIM), lambda b, h, i: (0, 0))
    kv_spec = pl.BlockSpec((seq, LANES), lambda b, h, i: (b, h))
    qo_spec = pl.BlockSpec((tq, LANES), lambda b, h, i: (b * nq + i, h))
    return pl.pallas_call(
        functools.partial(_attn_kernel, tile=tile),
        grid=(batch, N_ATT_HEADS, nq),
        in_specs=[qo_spec, kv_spec, kv_spec, vec, vec, vec, vec,
                  pl.BlockSpec((1, ATT_V_DIM), lambda b, h, i: (0, 0))],
        out_specs=qo_spec,
        out_shape=jax.ShapeDtypeStruct((t, D_ATTN), _BF16),
        scratch_shapes=[pltpu.VMEM((2, LANES, tile), _BF16),
                        pltpu.VMEM((seq // tile, ATT_V_DIM + 2 * SUBLANES, tile), _BF16),
                        pltpu.VMEM((tile, tile), _F32),
                        pltpu.VMEM((2, tile, tile), _F32), pltpu.VMEM((2, tile, tile), _F32),
                        pltpu.VMEM((2, SUBLANES, tile), _F32),
                        pltpu.VMEM((2, ATT_V_DIM + 2 * SUBLANES, tile), _F32)],
        compiler_params=_cparams(("arbitrary", "arbitrary", "arbitrary")),
        name="prompt_attn",
    )(q, kb, vb, lq1, lk1, lq2, lk2, w_subln.reshape(1, ATT_V_DIM))


def _ssd_kernel(xbc_ref, z_ref, dt_ref, dtt_ref, cprev_ref, h0_ref, convw_ref, convb_ref, dtb_row_ref, dtb_col_ref,
                alog_row_ref, alog_col_ref, dskip_ref, wnorm_ref, y_ref, hout_ref,
                xp_sc, st_sc, yg_sc, *, n_valid, precise):
    c = pl.program_id(1)
    L = SSM_CHUNK
    n_pairs = N_SSM_HEADS // 2
    hp = 2 * SSM_HEAD_DIM

    @pl.when(c == 0)
    def _():
        xp_sc[0:SUBLANES, :] = jnp.zeros((SUBLANES, CONV_DIM), _F32)
        xp_sc[SUBLANES - (CONV_WIDTH - 1):SUBLANES, :] = cprev_ref[...]
        st_sc[...] = h0_ref[...]

    xp_sc[SUBLANES:SUBLANES + L, :] = xbc_ref[...]
    base = SUBLANES - (CONV_WIDTH - 1)
    conv = convb_ref[...] + convw_ref[0:1, :] * xp_sc[base:base + L, :]
    for j in range(1, CONV_WIDTH):
        conv = conv + convw_ref[j:j + 1, :] * xp_sc[base + j:base + j + L, :]
    xp_sc[0:SUBLANES, :] = xp_sc[L:L + SUBLANES, :]
    xc = _silu(conv)
    xs = xc[:, :D_SSM]

    row_i = lax.broadcasted_iota(jnp.int32, (L, LANES), 0)
    col_i = lax.broadcasted_iota(jnp.int32, (L, LANES), 1)
    dt = _softplus(dt_ref[...] + dtb_row_ref[...])
    dtt = _softplus(dtt_ref[...] + dtb_col_ref[...])
    if n_valid < L:
        dt = jnp.where(row_i < n_valid, dt, 0.0)
        dtt = jnp.where(lax.broadcasted_iota(jnp.int32, dtt.shape, 1) < n_valid, dtt, 0.0)
    ad = dt * (-jnp.exp(alog_row_ref[...]))
    adt = dtt * (-jnp.exp(alog_col_ref[...]))
    tril = (col_i <= row_i).astype(_F32)
    triu = (row_i <= col_i).astype(_F32)
    acs = jnp.dot(tril, ad, preferred_element_type=_F32, precision=_HI)
    acst = jnp.dot(adt, triu, preferred_element_type=_F32, precision=_HI)
    causal = col_i <= row_i
    lane_lo = col_i < SSM_HEAD_DIM

    cb = []
    for g in range(SSM_GROUPS):
        bg = xc[:, D_SSM + g * SSM_STATE:D_SSM + (g + 1) * SSM_STATE]
        cg = xc[:, D_SSM + (SSM_GROUPS + g) * SSM_STATE:D_SSM + (SSM_GROUPS + g + 1) * SSM_STATE]
        cb.append((bg, cg, _dot_nt(cg, bg, precise)))

    for p in range(n_pairs):
        ha, hb = 2 * p, 2 * p + 1
        bg, cg, cbg = cb[ha // (N_SSM_HEADS // SSM_GROUPS)]
        col_a, col_b = acs[:, ha:ha + 1], acs[:, hb:hb + 1]
        lm_a = jnp.where(causal, jnp.exp(col_a - acst[ha:ha + 1, :]), 0.0)
        lm_b = jnp.where(causal, jnp.exp(col_b - acst[hb:hb + 1, :]), 0.0)
        dt_pair = jnp.where(lane_lo, dt[:, ha:ha + 1], dt[:, hb:hb + 1])
        x_pair = xs[:, p * hp:(p + 1) * hp] * dt_pair
        y_diag = jnp.where(lane_lo, _dot(cbg * lm_a, x_pair, precise), _dot(cbg * lm_b, x_pair, precise))
        st = st_sc[p]
        y_off = _dot_nt(cg, st, precise) * jnp.where(lane_lo, jnp.exp(col_a), jnp.exp(col_b))
        last_a, last_b = acs[L - 1:L, ha:ha + 1], acs[L - 1:L, hb:hb + 1]
        decay = jnp.where(lane_lo, jnp.exp(last_a - col_a), jnp.exp(last_b - col_b))
        upd = _dot((x_pair * decay).T, bg, precise)
        row_lo = row_i < SSM_HEAD_DIM
        st_sc[p] = jnp.where(row_lo, jnp.exp(last_a), jnp.exp(last_b)) * st + upd
        y_pair = y_diag + y_off + xs[:, p * hp:(p + 1) * hp] * dskip_ref[:, p * hp:(p + 1) * hp]
        zg = z_ref[:, p * hp:(p + 1) * hp].astype(_F32)
        yg_sc[:, p * hp:(p + 1) * hp] = y_pair * _silu(zg)

    gw = D_SSM // SSM_GROUPS
    for g in range(SSM_GROUPS):
        y_ref[:, g * gw:(g + 1) * gw] = _rms(yg_sc[:, g * gw:(g + 1) * gw],
                                             wnorm_ref[:, g * gw:(g + 1) * gw]).astype(y_ref.dtype)

    hout_ref[...] = st_sc[...]


def _ssd(xbc, z, dt, conv_prev, h0, conv_w, conv_b, dt_bias, a_log, d_skip, w_ssm_norm, batch, n_chunks,
         n_valid, precise, out_dtype):
    L = SSM_CHUNK
    rows = xbc.shape[0]
    n_pairs = N_SSM_HEADS // 2
    dtt = dt[:, :N_SSM_HEADS].T
    pad = LANES - N_SSM_HEADS
    dtb_row = jnp.pad(dt_bias, (0, pad)).reshape(1, LANES)
    alog_row = jnp.pad(a_log, (0, pad)).reshape(1, LANES)
    dskip = jnp.repeat(d_skip, SSM_HEAD_DIM).reshape(1, D_SSM)
    blk = lambda b, c: (b * n_chunks + c, 0)
    fixed = lambda b, c: (0, 0)
    y, h_out = pl.pallas_call(
        functools.partial(_ssd_kernel, n_valid=n_valid, precise=precise),
        grid=(batch, n_chunks),
        in_specs=[
            pl.BlockSpec((L, CONV_DIM), blk),
            pl.BlockSpec((L, D_SSM), blk),
            pl.BlockSpec((L, LANES), blk),
            pl.BlockSpec((N_SSM_HEADS, L), lambda b, c: (0, b * n_chunks + c)),
            pl.BlockSpec((None, CONV_WIDTH - 1, CONV_DIM), lambda b, c: (b, 0, 0)),
            pl.BlockSpec((None, n_pairs, 2 * SSM_HEAD_DIM, SSM_STATE), lambda b, c: (b, 0, 0, 0)),
            pl.BlockSpec((CONV_WIDTH, CONV_DIM), fixed),
            pl.BlockSpec((1, CONV_DIM), fixed),
            pl.BlockSpec((1, LANES), fixed),
            pl.BlockSpec((N_SSM_HEADS, 1), fixed),
            pl.BlockSpec((1, LANES), fixed),
            pl.BlockSpec((N_SSM_HEADS, 1), fixed),
            pl.BlockSpec((1, D_SSM), fixed),
            pl.BlockSpec((1, D_SSM), fixed),
        ],
        out_specs=[pl.BlockSpec((L, D_SSM), blk),
                   pl.BlockSpec((None, n_pairs, 2 * SSM_HEAD_DIM, SSM_STATE), lambda b, c: (b, 0, 0, 0))],
        out_shape=[jax.ShapeDtypeStruct((rows, D_SSM), out_dtype),
                   jax.ShapeDtypeStruct((batch, n_pairs, 2 * SSM_HEAD_DIM, SSM_STATE), _F32)],
        scratch_shapes=[pltpu.VMEM((SUBLANES + L, CONV_DIM), _F32),
                        pltpu.VMEM((n_pairs, 2 * SSM_HEAD_DIM, SSM_STATE), _F32),
                        pltpu.VMEM((L, D_SSM), _F32)],
        compiler_params=_cparams(("arbitrary", "arbitrary")),
        name="ssd_precise" if precise else "ssd",
    )(xbc, z, dt, dtt, conv_prev, h0.reshape(batch, n_pairs, 2 * SSM_HEAD_DIM, SSM_STATE),
      conv_w, conv_b.reshape(1, CONV_DIM), dtb_row, dt_bias.reshape(N_SSM_HEADS, 1),
      alog_row, a_log.reshape(N_SSM_HEADS, 1), dskip, w_ssm_norm.reshape(1, D_SSM))
    return y, h_out.reshape(batch, N_SSM_HEADS, SSM_HEAD_DIM, SSM_STATE)


def _outproj_router_kernel(x_ref, attn_ref, ssd_ref, wa_ref, wb_ref, wn_ref, wr_ref, br_ref, *rest, precise, n_blocks):
    if precise:
        h1_ref, t_ref, route_ref = rest[1:4]
        wa, wb = wa_ref, wb_ref
    else:
        h1_ref, t_ref, route_ref, wa, wb = rest
        _cast_weights_once(wa_ref, wa)
        _cast_weights_once(wb_ref, wb)

    @pl.when(pl.program_id(0) >= n_blocks)
    def _():
        t_ref[...] = jnp.zeros(t_ref.shape, _F32)

    @pl.when(pl.program_id(0) < n_blocks)
    def _():
        _outproj_router_body(x_ref, attn_ref, ssd_ref, wa, wb, wn_ref, wr_ref, br_ref, h1_ref, t_ref, route_ref, precise)


def _outproj_router_body(x_ref, attn_ref, ssd_ref, wa, wb, wn_ref, wr_ref, br_ref, h1_ref, t_ref, route_ref, precise):
    h1 = x_ref[...] + _dot(attn_ref[...], wa[...], precise) + _dot(ssd_ref[...], wb[...], precise)
    h1_ref[...] = h1
    t = _rms(h1, wn_ref[...])
    t_ref[...] = t
    logits = _dot(t, wr_ref[...], precise) + br_ref[...]
    lane = lax.broadcasted_iota(jnp.int32, logits.shape, 1).astype(_F32)
    neg = -jnp.inf
    big = float(LANES)
    is_g = lane < N_EXPERT_GROUPS
    gl = jnp.where(is_g, logits, neg)
    gmax = jnp.max(gl, axis=1, keepdims=True)
    g_idx = jnp.min(jnp.where(gl == gmax, lane, big), axis=1, keepdims=True)
    g_w = 1.0 / jnp.sum(jnp.where(is_g, jnp.exp(logits - gmax), 0.0), axis=1, keepdims=True)
    lo = N_EXPERT_GROUPS + EXPERTS_PER_GROUP * g_idx
    l1 = jnp.where(lane >= lo, jnp.where(lane < lo + EXPERTS_PER_GROUP, logits, neg), neg)
    m1 = jnp.max(l1, axis=1, keepdims=True)
    i1 = jnp.min(jnp.where(l1 == m1, lane, big), axis=1, keepdims=True)
    l2 = jnp.where(lane == i1, neg, l1)
    m2 = jnp.max(l2, axis=1, keepdims=True)
    i2 = jnp.min(jnp.where(l2 == m2, lane, big), axis=1, keepdims=True)
    r = jnp.exp(m2 - m1)
    p1 = 1.0 / (1.0 + r)
    w1 = g_w * p1
    w2 = g_w * (r * p1)
    e1 = i1 - N_EXPERT_GROUPS
    e2 = i2 - N_EXPERT_GROUPS
    route_ref[...] = jnp.where(lane == 0, e1, jnp.where(lane == 1, e2, jnp.where(lane == 2, w1,
                               jnp.where(lane == 3, w2, 0.0))))


def _outproj_router(x2d, attn, ssd, w_out, w_norm_ffn, w_router, b_router, tm, precise, t_rows, t_shared=None):
    t = x2d.shape[0]
    n_blocks = t // tm
    wdt = _F32 if precise else _BF16
    half_w = (D_ATTN, D_MODEL)
    fixed = lambda i: (0, 0)
    if precise:
        assert t_shared is not None and (t_rows - t) % tm == 0
        grid = (n_blocks,)
        row = lambda i: (i, 0)
        t_row = lambda i: (i + (t_rows - t) // tm, 0)
        extra_in, extra_args, aliases, scratch = [pl.BlockSpec(memory_space=pl.ANY)], [t_shared], {8: 1}, []
    else:
        grid = (n_blocks + pl.cdiv(t_rows - t, tm),)
        row = lambda i: (jnp.minimum(i, n_blocks - 1), 0)
        t_row = lambda i: (i, 0)
        extra_in, extra_args, aliases = [], [], {}
        scratch = [pltpu.VMEM(half_w, _BF16), pltpu.VMEM(half_w, _BF16)]
    return pl.pallas_call(
        functools.partial(_outproj_router_kernel, precise=precise, n_blocks=n_blocks),
        grid=grid,
        in_specs=[pl.BlockSpec((tm, D_MODEL), row), pl.BlockSpec((tm, D_ATTN), row), pl.BlockSpec((tm, D_SSM), row),
                  _resident(half_w, fixed), _resident(half_w, lambda i: (1, 0)),
                  pl.BlockSpec((1, D_MODEL), fixed), pl.BlockSpec((D_MODEL, LANES), fixed),
                  pl.BlockSpec((1, LANES), fixed)] + extra_in,
        out_specs=[pl.BlockSpec((tm, D_MODEL), row), pl.BlockSpec((tm, D_MODEL), t_row), pl.BlockSpec((tm, LANES), row)],
        out_shape=[jax.ShapeDtypeStruct((t, D_MODEL), _F32), jax.ShapeDtypeStruct((t_rows, D_MODEL), _F32),
                   jax.ShapeDtypeStruct((t, LANES), _F32)],
        scratch_shapes=scratch,
        input_output_aliases=aliases,
        compiler_params=_cparams(("arbitrary",)),
        name="outproj_router_precise" if precise else "outproj_router",
    )(x2d, attn, ssd, w_out, w_out, w_norm_ffn.reshape(1, D_MODEL), w_router.astype(wdt), b_router, *extra_args)


def _expert_kernel(te_ref, nv_ref, blk_ref, off_ref, idx_prev_ref, idx_ref, idx_next_ref, t_hbm, wg_ref, wu_ref, wd_ref,
                   y_hbm, xbuf, ybuf, sem_in, sem_out, *, tm, n_tiles):
    i = pl.program_id(0)
    i_prev = jnp.maximum(i - 1, 0)
    i_next = jnp.minimum(i + 1, n_tiles - 1)
    nv = nv_ref[i]
    nv_next = jnp.where(i + 1 < n_tiles, nv_ref[i_next], 0)
    f_half = D_FF_EXPERT // 2
    n_issue_chunks = 4
    per_chunk = tm // n_issue_chunks
    half = tm // 2
    n_pad = y_hbm.shape[0] // 2

    def gather_rows(idx, tile_i, s, rows):
        off = off_ref[tile_i]
        for r in rows:
            pltpu.make_async_copy(t_hbm.at[pl.ds(idx[0, 0, off + r], 1), :], xbuf.at[s, pl.ds(r, 1), :],
                                  sem_in.at[s]).start()

    def wait_gather(s):
        pltpu.make_async_copy(t_hbm.at[pl.ds(0, tm), :], xbuf.at[s], sem_in.at[s]).wait()

    def scatter_rows(idx, tile_i, s, rows):
        off = off_ref[tile_i] + 2 * tm
        valid = nv_ref[tile_i]
        for r in rows:
            spare = (r // half) * n_pad + (n_pad - half) + (r % half)
            dst = jnp.where(r < valid, idx[0, 0, off + r], spare)
            pltpu.make_async_copy(ybuf.at[s, pl.ds(r, 1), :], y_hbm.at[pl.ds(dst, 1), :], sem_out.at[s]).start()

    def wait_scatter(s):
        pltpu.make_async_copy(ybuf.at[s], y_hbm.at[pl.ds(0, tm), :], sem_out.at[s]).wait()

    @pl.when(i == 0)
    def _():
        gather_rows(idx_ref, i, 0, range(tm))
        ybuf[0, 0:half, :] = jnp.zeros((half, D_MODEL), _F32)
        for k in range(2):
            fill = pltpu.make_async_copy(ybuf.at[0, pl.ds(0, half), :],
                                         y_hbm.at[pl.ds((k + 1) * n_pad - half, half), :], sem_out.at[0])
            fill.start()
            fill.wait()

    def tile(s, has_prev):
        o = 1 - s

        def issue(c):
            rows = range(c * per_chunk, (c + 1) * per_chunk)
            gather_rows(idx_next_ref, i_next, o, rows)
            if has_prev:
                scatter_rows(idx_prev_ref, i_prev, o, rows)

        wait_gather(s)
        x = xbuf[s].astype(_BF16)
        parts = []
        for c in range(2):
            cols = slice(c * f_half, (c + 1) * f_half)
            hg = jnp.dot(x, wg_ref[:, cols].astype(_BF16), preferred_element_type=_F32)
            issue(2 * c)
            hu = jnp.dot(x, wu_ref[:, cols].astype(_BF16), preferred_element_type=_F32)
            issue(2 * c + 1)
            parts.append((_silu(hg) * hu).astype(_BF16))
        hh = jnp.concatenate(parts, axis=1)
        ybuf[s] = jnp.dot(hh, wd_ref[...].astype(_BF16), preferred_element_type=_F32)
        if has_prev:
            wait_scatter(o)

        @pl.when(nv_next == 0)
        def _():
            wait_gather(o)
            scatter_rows(idx_ref, i, s, range(tm))
            wait_scatter(s)

    used = nv > 0
    odd = lax.rem(i, 2) == 1

    @pl.when(i == 0)
    def _():
        tile(0, False)

    @pl.when(used & (i > 0) & jnp.logical_not(odd))
    def _():
        tile(0, True)

    @pl.when(used & odd)
    def _():
        tile(1, True)


def _moe_plan(e_ids, tm, n_tiles):
    n = e_ids.shape[0]
    n_assign = 2 * n
    n_pad = n + tm // 2
    e_flat = e_ids.reshape(-1)
    order = jnp.argsort(e_flat, stable=True).astype(jnp.int32)
    experts = jnp.arange(N_EXPERTS, dtype=jnp.int32)
    counts = jnp.sum((e_flat[:, None] == experts[None, :]).astype(jnp.int32), axis=0)
    seg_start = jnp.cumsum(counts) - counts
    tiles_per = (counts + tm - 1) // tm
    cum_tiles = jnp.cumsum(tiles_per)
    tile_start = cum_tiles - tiles_per
    n_used = cum_tiles[-1]
    tile_id = jnp.arange(n_tiles, dtype=jnp.int32)
    used = tile_id < n_used
    te = jnp.sum((cum_tiles[None, :] <= jnp.minimum(tile_id, n_used - 1)[:, None]).astype(jnp.int32), axis=1)
    of_tile = lambda v: jnp.sum(jnp.where(te[:, None] == experts[None, :], v[None, :], 0), axis=1)
    j = tile_id - of_tile(tile_start)
    nvalid = jnp.where(used, jnp.clip(of_tile(counts) - j * tm, 0, tm), 0).astype(jnp.int32)
    first = jnp.minimum(of_tile(seg_start) + j * tm, n_assign - 1)
    n_win = pl.cdiv(n_assign, tm)
    fill = (n_win + 1) * tm - n_assign
    tok = jnp.pad(order // 2, (0, fill)).reshape(n_win + 1, tm)
    dst = jnp.pad((order % 2) * n_pad + order // 2, (0, fill)).reshape(n_win + 1, tm)
    windows = jnp.concatenate([tok[:-1], tok[1:], dst[:-1], dst[1:]], axis=1).reshape(n_win, 1, 4 * tm)
    return te, nvalid, (first // tm).astype(jnp.int32), (first % tm).astype(jnp.int32), windows, n_pad


def _expert_mlp(t_all, e_ids, w_gate, w_up, w_down):
    n = t_all.shape[0]
    tm = EXPERT_TILE
    n_tiles = pl.cdiv(2 * n, tm) + N_EXPERTS
    te, nvalid, blk, off, windows, n_pad = _moe_plan(e_ids, tm, n_tiles)
    wmap = lambda i, te_ref, nv_ref, blk_ref, off_ref: (te_ref[i], 0, 0)

    def window_spec(shift):
        def index_map(i, te_ref, nv_ref, blk_ref, off_ref):
            return (blk_ref[jnp.clip(i + shift, 0, n_tiles - 1)], 0, 0)
        return pl.BlockSpec((1, 1, 4 * tm), index_map, memory_space=pltpu.SMEM)

    grid_spec = pltpu.PrefetchScalarGridSpec(
        num_scalar_prefetch=4,
        grid=(n_tiles,),
        in_specs=[
            window_spec(-1), window_spec(0), window_spec(1),
            pl.BlockSpec(memory_space=pl.ANY),
            pl.BlockSpec((None, D_MODEL, D_FF_EXPERT), wmap),
            pl.BlockSpec((None, D_MODEL, D_FF_EXPERT), wmap),
            pl.BlockSpec((None, D_FF_EXPERT, D_MODEL), wmap),
        ],
        out_specs=pl.BlockSpec(memory_space=pl.ANY),
        scratch_shapes=[pltpu.VMEM((2, tm, D_MODEL), _F32), pltpu.VMEM((2, tm, D_MODEL), _F32),
                        pltpu.SemaphoreType.DMA((2,)), pltpu.SemaphoreType.DMA((2,))],
    )
    y_rows = pl.pallas_call(
        functools.partial(_expert_kernel, tm=tm, n_tiles=n_tiles),
        grid_spec=grid_spec,
        out_shape=jax.ShapeDtypeStruct((2 * n_pad, D_MODEL), _F32),
        compiler_params=_cparams(("arbitrary",)),
        name="expert_mlp",
    )(te, nvalid, blk, off, windows, windows, windows, t_all, w_gate, w_up, w_down)
    return y_rows.reshape(2, n_pad, D_MODEL)


def _combine_ple_kernel(h1_ref, y_ref, route_ref, p_ref, wn_ref, wg_ref, wp_ref, wf_ref, o_ref, wg_sc):
    _cast_weights_once(wg_ref, wg_sc)
    route = route_ref[...]
    h2 = h1_ref[...] + route[:, 2:3] * y_ref[0] + route[:, 3:4] * y_ref[1]
    u = _rms(h2, wn_ref[...])
    gate = jnp.dot(u.astype(_BF16), wg_sc[...], preferred_element_type=_F32)
    gate = 1.0 / (1.0 + jnp.exp(-gate))
    pp = jnp.dot(p_ref[...].astype(_BF16), wp_ref[...], preferred_element_type=_F32)
    h3 = h2 + gate * pp
    o_ref[...] = _rms(h3, wf_ref[...])


def _combine_ple(h1, y_stk, route, p2d, w_norm_ple, w_ple_gate, w_ple_proj, w_norm_final, tm, row_off):
    t = h1.shape[0]
    off = row_off // tm
    row = lambda i: (i, 0)
    fixed = lambda i: (0, 0)
    return pl.pallas_call(
        _combine_ple_kernel,
        grid=(t // tm,),
        in_specs=[pl.BlockSpec((tm, D_MODEL), row),
                  pl.BlockSpec((2, tm, D_MODEL), lambda i: (0, i + off, 0)),
                  pl.BlockSpec((tm, LANES), row), pl.BlockSpec((tm, D_PLE), row),
                  pl.BlockSpec((1, D_MODEL), fixed), _resident((D_MODEL, D_MODEL), fixed),
                  pl.BlockSpec((D_PLE, D_MODEL), fixed), pl.BlockSpec((1, D_MODEL), fixed)],
        out_specs=pl.BlockSpec((tm, D_MODEL), row),
        out_shape=jax.ShapeDtypeStruct((t, D_MODEL), _F32),
        scratch_shapes=[pltpu.VMEM((D_MODEL, D_MODEL), _BF16)],
        compiler_params=_cparams(("arbitrary",)),
        name="combine_ple",
    )(h1, y_stk, route, p2d, w_norm_ple.reshape(1, D_MODEL), w_ple_gate, w_ple_proj.astype(_BF16),
      w_norm_final.reshape(1, D_MODEL))


def _sample_inproj_kernel(x_ref, wn_ref, w_ref, o_ref):
    u = _rms(x_ref[...], wn_ref[...])
    u_hi, u_lo = _split2(u)
    w_hi, w_lo = _split2(w_ref[...])
    nt = lambda a, b: lax.dot_general(a, b, _NT, preferred_element_type=_F32)
    o_ref[...] = nt(w_hi, u_hi) + nt(w_hi, u_lo) + nt(w_lo, u_hi)


def _sample_inproj(xs2d, w_norm, w_in_t, tn):
    rows = xs2d.shape[0]
    return pl.pallas_call(
        _sample_inproj_kernel,
        grid=(pl.cdiv(D_IN_PROJ, tn),),
        in_specs=[pl.BlockSpec((rows, D_MODEL), lambda j: (0, 0)), pl.BlockSpec((1, D_MODEL), lambda j: (0, 0)),
                  pl.BlockSpec((tn, D_MODEL), lambda j: (j, 0))],
        out_specs=pl.BlockSpec((tn, rows), lambda j: (j, 0)),
        out_shape=jax.ShapeDtypeStruct((D_IN_PROJ, rows), _F32),
        compiler_params=_cparams(("arbitrary",)),
        name="sample_inproj",
    )(xs2d, w_norm.reshape(1, D_MODEL), w_in_t).T


def _split2(x):
    hi = x.astype(_BF16)
    lo = (x - hi.astype(_F32)).astype(_BF16)
    return hi, lo


def _decode_attn_kernel(pt_ref, qkv_ref, c_ref, s1_ref, s2_ref, lq1_ref, lk1_ref, lq2_ref, lk2_ref, wsub_ref, *rest,
                        n_pages_step, n_steps, n_real):
    k_refs = rest[:n_pages_step]
    v_refs = rest[n_pages_step:2 * n_pages_step]
    o_ref, knew_ref, vnew_ref = rest[2 * n_pages_step:2 * n_pages_step + 3]
    q_sc, m_sc, l_sc, acc_sc, kt_sc, vt_sc = rest[2 * n_pages_step + 3:]
    j = pl.program_id(1)
    R = SAMPLE_ROWS
    hr = 2 * R
    page = kt_sc.shape[0] // N_ATT_HEADS

    @pl.when(j == 0)
    def _():
        c, s1, s2 = c_ref[...], s1_ref[...], s2_ref[...]
        lane = lax.broadcasted_iota(jnp.int32, (R, LANES), 1)
        kt_sc[...] = jnp.zeros(kt_sc.shape, _F32)
        vt_sc[...] = jnp.zeros(vt_sc.shape, _F32)
        for h in range(N_ATT_HEADS):
            sl = slice(h * LANES, (h + 1) * LANES)
            q = _rope_block(qkv_ref[:, sl], c, s1, s2) * (ATT_HEAD_DIM ** -0.5)
            k = _rope_block(qkv_ref[:, D_ATTN + h * LANES:D_ATTN + (h + 1) * LANES], c, s1, s2)
            v = qkv_ref[:, 2 * D_ATTN + h * LANES:2 * D_ATTN + (h + 1) * LANES]
            knew_ref[:, sl] = k
            vnew_ref[:, sl] = v
            kt_sc[pl.ds(h, R, stride=N_ATT_HEADS), :] = k
            vt_sc[pl.ds(h, R, stride=N_ATT_HEADS), :] = v
            q2 = jnp.concatenate([jnp.where(lane < ATT_HEAD_DIM, q, 0.0), jnp.where(lane >= ATT_HEAD_DIM, q, 0.0)], axis=0)
            hi, lo = _split2(q2)
            q_sc[h] = jnp.concatenate([hi, lo], axis=0)
        m_sc[...] = jnp.full(m_sc.shape, -jnp.inf, _F32)
        l_sc[...] = jnp.zeros(l_sc.shape, _F32)
        acc_sc[...] = jnp.zeros(acc_sc.shape, _F32)

    def head_rows(tiles, h):
        x = jnp.concatenate([t[pl.ds(h, page, stride=N_ATT_HEADS), :] for t in tiles], axis=0)
        return _split2(x)

    def process(k_tiles, v_tiles, mask):
        s_parts = []
        for h in range(N_ATT_HEADS):
            k_hi, k_lo = head_rows(k_tiles, h)
            q3 = q_sc[h]
            a = lax.dot_general(q3[0:2 * hr], k_hi, _NT, preferred_element_type=_F32)
            b = lax.dot_general(q3[0:hr], k_lo, _NT, preferred_element_type=_F32)
            s_parts.append(a[0:hr] + a[hr:2 * hr] + b)
        s = jnp.concatenate(s_parts, axis=0)
        if mask is not None:
            s = jnp.where(mask, s, -jnp.inf)
        m_prev = m_sc[...]
        m_new = jnp.maximum(m_prev, jnp.max(s, axis=1, keepdims=True))
        alpha = jnp.exp(m_prev - m_new)
        p = jnp.exp(s - m_new[:, :1])
        l_sc[...] = alpha * l_sc[...] + jnp.sum(p, axis=1, keepdims=True)
        m_sc[...] = m_new
        pv_parts = []
        for h in range(N_ATT_HEADS):
            v_hi, v_lo = head_rows(v_tiles, h)
            p_hi, p_lo = _split2(p[h * hr:(h + 1) * hr])
            a = jnp.dot(jnp.concatenate([p_hi, p_lo], axis=0), v_hi, preferred_element_type=_F32)
            b = jnp.dot(p_hi, v_lo, preferred_element_type=_F32)
            pv_parts.append(a[0:hr] + a[hr:2 * hr] + b)
        acc_sc[...] = alpha * acc_sc[...] + jnp.concatenate(pv_parts, axis=0)

    process(k_refs, v_refs, None)

    @pl.when(j == n_steps - 1)
    def _():
        rows = N_ATT_HEADS * hr
        row = lax.broadcasted_iota(jnp.int32, (rows, page), 0)
        col = lax.broadcasted_iota(jnp.int32, (rows, page), 1)
        qrow = row % R
        process([kt_sc], [vt_sc], (col <= qrow) & (col < n_real))
        o = acc_sc[...] / l_sc[...]
        lam = _lambda(lq1_ref, lk1_ref, lq2_ref, lk2_ref)
        wsub = wsub_ref[...]
        for h in range(N_ATT_HEADS):
            o1 = o[h * hr:h * hr + R]
            o2 = o[h * hr + R:(h + 1) * hr]
            o_ref[:, h * LANES:(h + 1) * LANES] = _diff_finalize(o1, o2, lam, wsub)


def _decode_attention(proj3, cache_k2, cache_v2, page_table, first_pos, lam_params, w_subln, n_real):
    bsz = proj3.shape[0]
    page = cache_k2.shape[2] // N_ATT_HEADS
    n_pages = page_table.shape[1]
    pg = PAGES_PER_STEP
    n_steps = n_pages // pg
    R = SAMPLE_ROWS
    c, s1, s2 = _rope_tables(first_pos, R)
    lq1, lk1, lq2, lk2 = [p.reshape(1, ATT_HEAD_DIM) for p in lam_params]
    fixed = lambda b, j, pt: (0, 0)
    tab = pl.BlockSpec((R, LANES), fixed)
    vec = pl.BlockSpec((1, ATT_HEAD_DIM), fixed)

    def page_spec(i):
        return pl.BlockSpec((None, None, page * N_ATT_HEADS, LANES), lambda b, j, pt: (0, pt[b, j * pg + i], 0, 0))

    out_row = pl.BlockSpec((None, R, D_ATTN), lambda b, j, pt: (b, 0, 0))
    grid_spec = pltpu.PrefetchScalarGridSpec(
        num_scalar_prefetch=1,
        grid=(bsz, n_steps),
        in_specs=[pl.BlockSpec((None, R, 3 * D_ATTN), lambda b, j, pt: (b, 0, 0)), tab, tab, tab, vec, vec, vec, vec,
                  pl.BlockSpec((1, ATT_V_DIM), fixed)]
                 + [page_spec(i) for i in range(pg)] + [page_spec(i) for i in range(pg)],
        out_specs=[out_row, out_row, out_row],
        scratch_shapes=[pltpu.VMEM((N_ATT_HEADS, 4 * R, LANES), _BF16),
                        pltpu.VMEM((N_ATT_HEADS * 2 * R, LANES), _F32), pltpu.VMEM((N_ATT_HEADS * 2 * R, LANES), _F32),
                        pltpu.VMEM((N_ATT_HEADS * 2 * R, LANES), _F32),
                        pltpu.VMEM((page * N_ATT_HEADS, LANES), _F32), pltpu.VMEM((page * N_ATT_HEADS, LANES), _F32)],
    )
    return pl.pallas_call(
        functools.partial(_decode_attn_kernel, n_pages_step=pg, n_steps=n_steps, n_real=n_real),
        grid_spec=grid_spec,
        out_shape=[jax.ShapeDtypeStruct((bsz, R, D_ATTN), _F32)] * 3,
        compiler_params=_cparams(("arbitrary", "arbitrary")),
        name="decode_attn",
    )(page_table, proj3, c, s1, s2, lq1, lk1, lq2, lk2, w_subln.reshape(1, ATT_V_DIM),
      *([cache_k2] * pg), *([cache_v2] * pg))


def kernel(x_prompt, x_sample, cache_k, cache_v, state_conv, state_ssm, page_table, p_prompt, p_sample, w_norm_mix, w_in, lambda_q1, lambda_k1, lambda_q2, lambda_k2, w_subln, conv_w, conv_b, dt_bias, A_log, D_skip, w_ssm_norm, w_out, w_norm_ffn, w_group_router, b_group_router, w_expert_router, b_expert_router, w_exp_gate, w_exp_up, w_exp_down, w_norm_ple, w_ple_gate, w_ple_proj, w_norm_final):
    bp, lp, _ = x_prompt.shape
    bs, ls, _ = x_sample.shape
    past = page_table.shape[1] * cache_k.shape[2]
    tp = bp * lp
    R = SAMPLE_ROWS
    ts = bs * R
    lam_params = (lambda_q1[0], lambda_k1[0], lambda_q2[0], lambda_k2[0])
    pad_r = LANES - N_EXPERT_GROUPS - N_EXPERTS
    w_router = jnp.pad(jnp.concatenate([w_group_router[0], w_expert_router[0]], axis=1), ((0, 0), (0, pad_r)))
    b_router = jnp.pad(jnp.concatenate([b_group_router[0], b_expert_router[0]]), (0, pad_r)).reshape(1, LANES)

    xp2d = x_prompt.reshape(tp, D_MODEL)
    w_in_t = w_in[0].T
    q, kf, kb, vf, vb, z, xbc, dt = _prompt_inproj(xp2d, w_norm_mix[0], w_in_t, lp, 512)
    attn_p = _prompt_attention(q, kb, vb, lam_params, w_subln[0], bp, lp, 512)
    ssd_p, ssm_p = _ssd(xbc, z, dt, jnp.zeros((bp, CONV_WIDTH - 1, CONV_DIM), _F32),
                        jnp.zeros((bp, N_SSM_HEADS, SSM_HEAD_DIM, SSM_STATE), _F32),
                        conv_w[0], conv_b[0], dt_bias[0], A_log[0], D_skip[0], w_ssm_norm[0],
                        bp, lp // SSM_CHUNK, SSM_CHUNK, False, _BF16)
    h1_p, t_all, route_p = _outproj_router(xp2d, attn_p, ssd_p, w_out[0], w_norm_ffn[0], w_router, b_router, 256, False,
                                           tp + ts)

    xs3 = jnp.pad(x_sample, ((0, 0), (0, R - ls), (0, 0)))
    xs2d = xs3.reshape(ts, D_MODEL)
    proj_s = _sample_inproj(xs2d, w_norm_mix[0], w_in_t, 512)
    proj3 = proj_s.reshape(bs, R, D_IN_PROJ)
    cache_rows = cache_k.shape[:2] + (cache_k.shape[2] * N_ATT_HEADS, LANES)
    attn_s, k_s, v_s = _decode_attention(proj3, cache_k.reshape(cache_rows), cache_v.reshape(cache_rows), page_table,
                                         past, lam_params, w_subln[0], ls)
    off = 3 * D_ATTN
    pad_rows = ((0, 0), (0, SSM_CHUNK - R), (0, 0))
    z_s = jnp.pad(proj3[:, :, off:off + D_SSM], pad_rows).reshape(bs * SSM_CHUNK, D_SSM)
    xbc_s3 = proj3[:, :, off + D_SSM:off + D_SSM + CONV_DIM]
    xbc_s = jnp.pad(xbc_s3, pad_rows).reshape(bs * SSM_CHUNK, CONV_DIM)
    dt_s = jnp.pad(proj3[:, :, off + D_SSM + CONV_DIM:], ((0, 0), (0, SSM_CHUNK - R), (0, LANES - N_SSM_HEADS)))
    dt_s = dt_s.reshape(bs * SSM_CHUNK, LANES)
    ssd_s, ssm_s = _ssd(xbc_s, z_s, dt_s, state_conv[0], state_ssm[0], conv_w[0], conv_b[0], dt_bias[0], A_log[0],
                        D_skip[0], w_ssm_norm[0], bs, 1, ls, True, _F32)
    ssd_s = ssd_s.reshape(bs, SSM_CHUNK, D_SSM)[:, :R].reshape(ts, D_SSM)
    h1_s, t_all, route_s = _outproj_router(xs2d, attn_s.reshape(ts, D_ATTN), ssd_s, w_out[0], w_norm_ffn[0],
                                           w_router, b_router, ts, True, tp + ts, t_shared=t_all)

    e_ids = jnp.concatenate([route_p[:, :2], route_s[:, :2]], axis=0).astype(jnp.int32)
    y_stk = _expert_mlp(t_all, e_ids, w_exp_gate[0], w_exp_up[0], w_exp_down[0])
    y_p = _combine_ple(h1_p, y_stk, route_p, p_prompt[0].reshape(tp, D_PLE), w_norm_ple[0], w_ple_gate[0],
                       w_ple_proj[0], w_norm_final, 256, 0)
    ps2d = jnp.pad(p_sample[0], ((0, 0), (0, R - ls), (0, 0))).reshape(ts, D_PLE)
    y_s = _combine_ple(h1_s, y_stk, route_s, ps2d, w_norm_ple[0], w_ple_gate[0], w_ple_proj[0], w_norm_final, ts, tp)

    y_prompt = y_p.reshape(bp, lp, D_MODEL)
    y_sample = y_s.reshape(bs, R, D_MODEL)[:, :ls]
    k_prompt = kf.reshape(1, bp, lp, N_ATT_HEADS, 2 * ATT_HEAD_DIM)
    v_prompt = vf.reshape(1, bp, lp, N_ATT_HEADS, ATT_V_DIM)
    conv_prompt = xbc.reshape(bp, lp, CONV_DIM)[:, lp - (CONV_WIDTH - 1):][None]
    ssm_prompt = ssm_p[None]
    k_sample = k_s[:, :ls].reshape(1, bs, ls, N_ATT_HEADS, 2 * ATT_HEAD_DIM)
    v_sample = v_s[:, :ls].reshape(1, bs, ls, N_ATT_HEADS, ATT_V_DIM)
    conv_sample = xbc_s3[:, ls - (CONV_WIDTH - 1):ls][None]
    ssm_sample = ssm_s[None]
    return (y_prompt, y_sample, k_prompt, v_prompt, conv_prompt, ssm_prompt, k_sample, v_sample, conv_sample, ssm_sample)
```

```python
import functools
import math

import jax
import jax.numpy as jnp
from jax import lax
from jax.experimental import pallas as pl
from jax.experimental.pallas import tpu as pltpu

D_MODEL = 2048
D_ATTN = 1024
D_SSM = 1024
ATT_HEAD_DIM = 64
N_ATT_HEADS = 8
ATT_V_DIM = 128
ROT_DIM = 16
ROPE_THETA = 500000.0
SSM_HEAD_DIM = 64
N_SSM_HEADS = 16
SSM_GROUPS = 2
SSM_STATE = 128
SSM_CHUNK = 128
CONV_WIDTH = 4
CONV_DIM = D_SSM + 2 * SSM_GROUPS * SSM_STATE
D_IN_PROJ = 3 * D_ATTN + D_SSM + CONV_DIM + N_SSM_HEADS
N_EXPERT_GROUPS = 4
EXPERTS_PER_GROUP = 8
N_EXPERTS = 32
D_FF_EXPERT = 512
D_PLE = 256
RMS_EPS = 1e-6
LAM_INIT = 0.8 - 0.6 * math.exp(-0.3 * 0)

LANES = 128
SUBLANES = 8
VMEM_LIMIT_BYTES = 56 * 2 ** 20

SAMPLE_ROWS = 8
EXPERT_TILE = 256
PAGES_PER_STEP = 8

_F32 = jnp.float32
_BF16 = jnp.bfloat16
_HI = lax.Precision.HIGHEST
_NT = (((1,), (1,)), ((), ()))


def _cparams(semantics):
    return pltpu.CompilerParams(dimension_semantics=semantics, vmem_limit_bytes=VMEM_LIMIT_BYTES)


def _rms(x, w):
    return x * lax.rsqrt(jnp.mean(x * x, axis=-1, keepdims=True) + RMS_EPS) * w


def _silu(x):
    return x * (1.0 / (1.0 + jnp.exp(-x)))


def _softplus(x):
    return jnp.maximum(x, 0.0) + jnp.log(1.0 + jnp.exp(-jnp.abs(x)))


def _dot(a, b, precise=False):
    if precise:
        return jnp.dot(a.astype(_F32), b.astype(_F32), preferred_element_type=_F32, precision=_HI)
    return jnp.dot(a.astype(_BF16), b.astype(_BF16), preferred_element_type=_F32)


def _dot_nt(a, b, precise=False):
    if precise:
        return lax.dot_general(a.astype(_F32), b.astype(_F32), _NT, preferred_element_type=_F32, precision=_HI)
    return lax.dot_general(a.astype(_BF16), b.astype(_BF16), _NT, preferred_element_type=_F32)


def _rope_block(y, c, s1, s2):
    return y * c + pltpu.roll(y, LANES - ROT_DIM // 2, 1) * s1 + pltpu.roll(y, ROT_DIM // 2, 1) * s2


def _rope_tables(first_pos, n):
    half = ROT_DIM // 2
    pos = first_pos + jnp.arange(n, dtype=jnp.int32)
    inv_freq = jnp.power(ROPE_THETA, -jnp.arange(half, dtype=_F32) * (2.0 / ROT_DIM))
    dim = jnp.arange(LANES, dtype=jnp.int32) % ATT_HEAD_DIM
    ang = pos.astype(_F32)[:, None] * inv_freq[dim % half][None, :]
    cos, sin = jnp.cos(ang), jnp.sin(ang)
    first, second = (dim < half)[None, :], ((dim >= half) & (dim < ROT_DIM))[None, :]
    c = jnp.where(first | second, cos, 1.0)
    s1 = jnp.where(first, -sin, 0.0)
    s2 = jnp.where(second, sin, 0.0)
    return c, s1, s2


def _cast_weights_once(w_ref, wb_sc, transposed=False, valid_out=None):
    @pl.when(pl.program_id(0) == 0)
    def _():
        if transposed:
            w = w_ref[0:wb_sc.shape[1], :]
            if valid_out is not None:
                w = jnp.where(lax.broadcasted_iota(jnp.int32, w.shape, 0) < valid_out, w, 0.0)
            w = w.T
        else:
            w = w_ref[...]
        wb_sc[...] = w.astype(wb_sc.dtype)


def _resident(shape, index_map):
    return pl.BlockSpec(shape, index_map, pipeline_mode=pl.Buffered(1))


def _inproj_qk_kernel(x_ref, wn_ref, w_ref, c_ref, s1_ref, s2_ref, q_ref, kf_ref, kb_ref, wb_sc):
    _cast_weights_once(w_ref, wb_sc, transposed=True)
    u = _rms(x_ref[...], wn_ref[...]).astype(_BF16)
    y = jnp.dot(u, wb_sc[...], preferred_element_type=_F32)
    c, s1, s2 = c_ref[...], s1_ref[...], s2_ref[...]
    q_scale = (ATT_HEAD_DIM ** -0.5) * math.log2(math.e)
    for j in range(2 * N_ATT_HEADS):
        r = _rope_block(y[:, j * LANES:(j + 1) * LANES], c, s1, s2)
        if j < N_ATT_HEADS:
            q_ref[:, j * LANES:(j + 1) * LANES] = (r * q_scale).astype(_BF16)
        else:
            jj = j - N_ATT_HEADS
            kf_ref[pl.ds(jj, r.shape[0], stride=N_ATT_HEADS), :] = r
            kb_ref[:, jj * LANES:(jj + 1) * LANES] = r.astype(_BF16)


def _inproj_vz_kernel(x_ref, wn_ref, w_ref, vf_ref, vb_ref, z_ref, wb_sc):
    _cast_weights_once(w_ref, wb_sc, transposed=True)
    u = _rms(x_ref[...], wn_ref[...]).astype(_BF16)
    y = jnp.dot(u, wb_sc[...], preferred_element_type=_F32)
    v = y[:, :D_ATTN]
    for h in range(N_ATT_HEADS):
        vf_ref[pl.ds(h, v.shape[0], stride=N_ATT_HEADS), :] = v[:, h * LANES:(h + 1) * LANES]
    vb_ref[...] = v.astype(_BF16)
    z_ref[...] = y[:, D_ATTN:].astype(_BF16)


def _inproj_xbc_kernel(x_ref, wn_ref, w_ref, xbc_ref, dt_ref, wb_sc):
    _cast_weights_once(w_ref, wb_sc, transposed=True, valid_out=CONV_DIM + N_SSM_HEADS)
    u = _rms(x_ref[...], wn_ref[...]).astype(_BF16)
    y = jnp.dot(u, wb_sc[...], preferred_element_type=_F32)
    xbc_ref[...] = y[:, :CONV_DIM]
    dt_ref[...] = y[:, CONV_DIM:]


def _prompt_inproj(x2d, w_norm, w_in_t, seq, tm):
    t = x2d.shape[0]
    nblk = t // tm
    per_seq = seq // tm
    wn = w_norm.reshape(1, D_MODEL)
    row = lambda i: (i, 0)
    fixed = lambda i: (0, 0)
    x_spec = pl.BlockSpec((tm, D_MODEL), row)
    wn_spec = pl.BlockSpec((1, D_MODEL), fixed)
    head_spec = pl.BlockSpec((tm * N_ATT_HEADS, LANES), row)
    wide = 2 * D_ATTN

    c, s1, s2 = _rope_tables(0, seq)
    tab_spec = pl.BlockSpec((tm, LANES), lambda i: (i % per_seq, 0))
    q, kf, kb = pl.pallas_call(
        _inproj_qk_kernel,
        grid=(nblk,),
        in_specs=[x_spec, wn_spec, _resident((wide, D_MODEL), fixed), tab_spec, tab_spec, tab_spec],
        out_specs=[pl.BlockSpec((tm, D_ATTN), row), head_spec, pl.BlockSpec((tm, D_ATTN), row)],
        out_shape=[jax.ShapeDtypeStruct((t, D_ATTN), _BF16), jax.ShapeDtypeStruct((t * N_ATT_HEADS, LANES), _F32),
                   jax.ShapeDtypeStruct((t, D_ATTN), _BF16)],
        scratch_shapes=[pltpu.VMEM((D_MODEL, wide), _BF16)],
        compiler_params=_cparams(("arbitrary",)),
        name="inproj_qk",
    )(x2d, wn, w_in_t, c, s1, s2)

    vf, vb, z = pl.pallas_call(
        _inproj_vz_kernel,
        grid=(nblk,),
        in_specs=[x_spec, wn_spec, _resident((wide, D_MODEL), lambda i: (1, 0))],
        out_specs=[head_spec, pl.BlockSpec((tm, D_ATTN), row), pl.BlockSpec((tm, D_SSM), row)],
        out_shape=[jax.ShapeDtypeStruct((t * N_ATT_HEADS, LANES), _F32), jax.ShapeDtypeStruct((t, D_ATTN), _BF16),
                   jax.ShapeDtypeStruct((t, D_SSM), _BF16)],
        scratch_shapes=[pltpu.VMEM((D_MODEL, wide), _BF16)],
        compiler_params=_cparams(("arbitrary",)),
        name="inproj_vz",
    )(x2d, wn, w_in_t)

    n_tail = CONV_DIM + LANES
    xbc, dt = pl.pallas_call(
        _inproj_xbc_kernel,
        grid=(nblk,),
        in_specs=[x_spec, wn_spec, _resident((wide, D_MODEL), lambda i: (2, 0))],
        out_specs=[pl.BlockSpec((tm, CONV_DIM), row), pl.BlockSpec((tm, LANES), row)],
        out_shape=[jax.ShapeDtypeStruct((t, CONV_DIM), _F32), jax.ShapeDtypeStruct((t, LANES), _F32)],
        scratch_shapes=[pltpu.VMEM((D_MODEL, n_tail), _BF16)],
        compiler_params=_cparams(("arbitrary",)),
        name="inproj_xbc",
    )(x2d, wn, w_in_t)
    return q, kf, kb, vf, vb, z, xbc, dt


def _lambda(lq1_ref, lk1_ref, lq2_ref, lk2_ref):
    a = jnp.sum(lq1_ref[...] * lk1_ref[...], axis=-1, keepdims=True)
    b = jnp.sum(lq2_ref[...] * lk2_ref[...], axis=-1, keepdims=True)
    return jnp.exp(a) - jnp.exp(b) + LAM_INIT


def _diff_finalize(o1, o2, lam, wsub):
    a = o1 - lam * o2
    return _rms(a, wsub) * (1.0 - LAM_INIT)


def _attn_kernel(q_ref, k_ref, v_ref, lq1_ref, lk1_ref, lq2_ref, lk2_ref, wsub_ref, o_ref,
                 qt_sc, vt_sc, bias_sc, sa_sc, sb_sc, m_sc, acc_sc, *, tile):
    n_kt = vt_sc.shape[0]
    dv = ATT_V_DIM
    masked_out = -1e30

    for c in range(n_kt):
        vt_sc[c, 0:dv, :] = v_ref[c * tile:(c + 1) * tile, :].astype(_F32).T.astype(_BF16)
        vt_sc[c, dv:, :] = jnp.ones((vt_sc.shape[1] - dv, tile), _BF16)
    key = lax.broadcasted_iota(jnp.int32, (tile, tile), 0)
    qry = lax.broadcasted_iota(jnp.int32, (tile, tile), 1)
    bias_sc[...] = jnp.where(key <= qry, 0.0, masked_out)

    def start_query_tile(qi):
        q = q_ref[pl.ds(pl.multiple_of(qi * tile, tile), tile), :].astype(_F32)
        lane = lax.broadcasted_iota(jnp.int32, (tile, LANES), 1)
        qt_sc[0] = jnp.where(lane < ATT_HEAD_DIM, q, 0.0).T.astype(_BF16)
        qt_sc[1] = jnp.where(lane >= ATT_HEAD_DIM, q, 0.0).T.astype(_BF16)
        m_sc[...] = jnp.full(m_sc.shape, -jnp.inf, _F32)
        acc_sc[...] = jnp.zeros(acc_sc.shape, _F32)
        scores(0, sa_sc)

    def scores(j, dst):
        k = k_ref[pl.ds(pl.multiple_of(j * tile, tile), tile), :]
        for mm in range(2):
            dst[mm] = jnp.dot(k, qt_sc[mm], preferred_element_type=_F32)

    def softmax_pv(j, src, diagonal=False):
        vt = vt_sc[j]
        for mm in range(2):
            st = src[mm]
            if diagonal:
                st = st + bias_sc[...]
            m_prev = m_sc[mm]
            m_new = jnp.maximum(m_prev, jnp.max(st, axis=0, keepdims=True))
            alpha = jnp.exp2(m_prev - m_new)
            pt = jnp.exp2(st - m_new[0:1, :]).astype(_BF16)
            acc_sc[mm] = alpha[0:1, :] * acc_sc[mm] + jnp.dot(vt, pt, preferred_element_type=_F32)
            m_sc[mm] = m_new

    def pair(jj, carry):
        j0 = 2 * jj
        scores(j0 + 1, sb_sc)
        softmax_pv(j0, sa_sc)
        scores(j0 + 2, sa_sc)
        softmax_pv(j0 + 1, sb_sc)
        return carry

    n_q = q_ref.shape[0] // tile
    lam = _lambda(lq1_ref, lk1_ref, lq2_ref, lk2_ref)

    def query_tile(qi, carry):
        lax.fori_loop(0, qi // 2, pair, 0)
        odd = lax.rem(qi, 2) == 1

        @pl.when(odd)
        def _():
            scores(qi, sb_sc)
            softmax_pv(qi - 1, sa_sc)
            softmax_pv(qi, sb_sc, diagonal=True)

        @pl.when(jnp.logical_not(odd))
        def _():
            softmax_pv(qi, sa_sc, diagonal=True)

        outs = []
        for mm in range(2):
            acc = acc_sc[mm]
            outs.append((acc[0:dv, :] / acc[dv:dv + 1, :]).T)
        rows = pl.ds(pl.multiple_of(qi * tile, tile), tile)
        o_ref[rows, :] = _diff_finalize(outs[0], outs[1], lam, wsub_ref[...]).astype(o_ref.dtype)
        start_query_tile(jnp.minimum(qi + 1, n_q - 1))
        return carry

    start_query_tile(0)
    lax.fori_loop(0, n_q, query_tile, 0)


def _prompt_attention(q, kb, vb, lam_params, w_subln, batch, seq, tile):
    t = q.shape[0]
    lq1, lk1, lq2, lk2 = [p.reshape(1, ATT_HEAD_DIM) for p in lam_params]
    vec = pl.BlockSpec((1, ATT_HEAD_DIM), lambda b, h: (0, 0))
    head_seq = pl.BlockSpec((seq, LANES), lambda b, h: (b, h))
    return pl.pallas_call(
        functools.partial(_attn_kernel, tile=tile),
        grid=(batch, N_ATT_HEADS),
        in_specs=[head_seq, head_seq, head_seq, vec, vec, vec, vec,
                  pl.BlockSpec((1, ATT_V_DIM), lambda b, h: (0, 0))],
        out_specs=head_seq,
        out_shape=jax.ShapeDtypeStruct((t, D_ATTN), _BF16),
        scratch_shapes=[pltpu.VMEM((2, LANES, tile), _BF16),
                        pltpu.VMEM((seq // tile, ATT_V_DIM + 2 * SUBLANES, tile), _BF16),
                        pltpu.VMEM((tile, tile), _F32),
                        pltpu.VMEM((2, tile, tile), _F32), pltpu.VMEM((2, tile, tile), _F32),
                        pltpu.VMEM((2, SUBLANES, tile), _F32),
                        pltpu.VMEM((2, ATT_V_DIM + 2 * SUBLANES, tile), _F32)],
        compiler_params=_cparams(("arbitrary", "arbitrary")),
        name="prompt_attn",
    )(q, kb, vb, lq1, lk1, lq2, lk2, w_subln.reshape(1, ATT_V_DIM))


def _ssd_kernel(xbc_ref, z_ref, dt_ref, dtt_ref, cprev_ref, h0_ref, convw_ref, convb_ref, dtb_row_ref, dtb_col_ref,
                alog_row_ref, alog_col_ref, dskip_ref, wnorm_ref, y_ref, hout_ref,
                xp_sc, st_sc, yg_sc, *, n_valid, precise):
    c = pl.program_id(1)
    L = SSM_CHUNK
    n_pairs = N_SSM_HEADS // 2
    hp = 2 * SSM_HEAD_DIM

    @pl.when(c == 0)
    def _():
        xp_sc[0:SUBLANES, :] = jnp.zeros((SUBLANES, CONV_DIM), _F32)
        xp_sc[SUBLANES - (CONV_WIDTH - 1):SUBLANES, :] = cprev_ref[...]
        st_sc[...] = h0_ref[...]

    xp_sc[SUBLANES:SUBLANES + L, :] = xbc_ref[...]
    base = SUBLANES - (CONV_WIDTH - 1)
    conv = convb_ref[...] + convw_ref[0:1, :] * xp_sc[base:base + L, :]
    for j in range(1, CONV_WIDTH):
        conv = conv + convw_ref[j:j + 1, :] * xp_sc[base + j:base + j + L, :]
    xp_sc[0:SUBLANES, :] = xp_sc[L:L + SUBLANES, :]
    xc = _silu(conv)
    xs = xc[:, :D_SSM]

    row_i = lax.broadcasted_iota(jnp.int32, (L, LANES), 0)
    col_i = lax.broadcasted_iota(jnp.int32, (L, LANES), 1)
    dt = _softplus(dt_ref[...] + dtb_row_ref[...])
    dtt = _softplus(dtt_ref[...] + dtb_col_ref[...])
    if n_valid < L:
        dt = jnp.where(row_i < n_valid, dt, 0.0)
        dtt = jnp.where(lax.broadcasted_iota(jnp.int32, dtt.shape, 1) < n_valid, dtt, 0.0)
    ad = dt * (-jnp.exp(alog_row_ref[...]))
    adt = dtt * (-jnp.exp(alog_col_ref[...]))
    tril = (col_i <= row_i).astype(_F32)
    triu = (row_i <= col_i).astype(_F32)
    acs = jnp.dot(tril, ad, preferred_element_type=_F32, precision=_HI)
    acst = jnp.dot(adt, triu, preferred_element_type=_F32, precision=_HI)
    causal = col_i <= row_i
    lane_lo = col_i < SSM_HEAD_DIM

    cb = []
    for g in range(SSM_GROUPS):
        bg = xc[:, D_SSM + g * SSM_STATE:D_SSM + (g + 1) * SSM_STATE]
        cg = xc[:, D_SSM + (SSM_GROUPS + g) * SSM_STATE:D_SSM + (SSM_GROUPS + g + 1) * SSM_STATE]
        cb.append((bg, cg, _dot_nt(cg, bg, precise)))

    for p in range(n_pairs):
        ha, hb = 2 * p, 2 * p + 1
        bg, cg, cbg = cb[ha // (N_SSM_HEADS // SSM_GROUPS)]
        col_a, col_b = acs[:, ha:ha + 1], acs[:, hb:hb + 1]
        lm_a = jnp.where(causal, jnp.exp(col_a - acst[ha:ha + 1, :]), 0.0)
        lm_b = jnp.where(causal, jnp.exp(col_b - acst[hb:hb + 1, :]), 0.0)
        dt_pair = jnp.where(lane_lo, dt[:, ha:ha + 1], dt[:, hb:hb + 1])
        x_pair = xs[:, p * hp:(p + 1) * hp] * dt_pair
        y_diag = jnp.where(lane_lo, _dot(cbg * lm_a, x_pair, precise), _dot(cbg * lm_b, x_pair, precise))
        st = st_sc[p]
        y_off = _dot_nt(cg, st, precise) * jnp.where(lane_lo, jnp.exp(col_a), jnp.exp(col_b))
        last_a, last_b = acs[L - 1:L, ha:ha + 1], acs[L - 1:L, hb:hb + 1]
        decay = jnp.where(lane_lo, jnp.exp(last_a - col_a), jnp.exp(last_b - col_b))
        upd = _dot((x_pair * decay).T, bg, precise)
        row_lo = row_i < SSM_HEAD_DIM
        st_sc[p] = jnp.where(row_lo, jnp.exp(last_a), jnp.exp(last_b)) * st + upd
        y_pair = y_diag + y_off + xs[:, p * hp:(p + 1) * hp] * dskip_ref[:, p * hp:(p + 1) * hp]
        zg = z_ref[:, p * hp:(p + 1) * hp].astype(_F32)
        yg_sc[:, p * hp:(p + 1) * hp] = y_pair * _silu(zg)

    gw = D_SSM // SSM_GROUPS
    for g in range(SSM_GROUPS):
        y_ref[:, g * gw:(g + 1) * gw] = _rms(yg_sc[:, g * gw:(g + 1) * gw],
                                             wnorm_ref[:, g * gw:(g + 1) * gw]).astype(y_ref.dtype)

    hout_ref[...] = st_sc[...]


def _ssd(xbc, z, dt, conv_prev, h0, conv_w, conv_b, dt_bias, a_log, d_skip, w_ssm_norm, batch, n_chunks,
         n_valid, precise, out_dtype):
    L = SSM_CHUNK
    rows = xbc.shape[0]
    n_pairs = N_SSM_HEADS // 2
    dtt = dt[:, :N_SSM_HEADS].T
    pad = LANES - N_SSM_HEADS
    dtb_row = jnp.pad(dt_bias, (0, pad)).reshape(1, LANES)
    alog_row = jnp.pad(a_log, (0, pad)).reshape(1, LANES)
    dskip = jnp.repeat(d_skip, SSM_HEAD_DIM).reshape(1, D_SSM)
    blk = lambda b, c: (b * n_chunks + c, 0)
    fixed = lambda b, c: (0, 0)
    y, h_out = pl.pallas_call(
        functools.partial(_ssd_kernel, n_valid=n_valid, precise=precise),
        grid=(batch, n_chunks),
        in_specs=[
            pl.BlockSpec((L, CONV_DIM), blk),
            pl.BlockSpec((L, D_SSM), blk),
            pl.BlockSpec((L, LANES), blk),
            pl.BlockSpec((N_SSM_HEADS, L), lambda b, c: (0, b * n_chunks + c)),
            pl.BlockSpec((None, CONV_WIDTH - 1, CONV_DIM), lambda b, c: (b, 0, 0)),
            pl.BlockSpec((None, n_pairs, 2 * SSM_HEAD_DIM, SSM_STATE), lambda b, c: (b, 0, 0, 0)),
            pl.BlockSpec((CONV_WIDTH, CONV_DIM), fixed),
            pl.BlockSpec((1, CONV_DIM), fixed),
            pl.BlockSpec((1, LANES), fixed),
            pl.BlockSpec((N_SSM_HEADS, 1), fixed),
            pl.BlockSpec((1, LANES), fixed),
            pl.BlockSpec((N_SSM_HEADS, 1), fixed),
            pl.BlockSpec((1, D_SSM), fixed),
            pl.BlockSpec((1, D_SSM), fixed),
        ],
        out_specs=[pl.BlockSpec((L, D_SSM), blk),
                   pl.BlockSpec((None, n_pairs, 2 * SSM_HEAD_DIM, SSM_STATE), lambda b, c: (b, 0, 0, 0))],
        out_shape=[jax.ShapeDtypeStruct((rows, D_SSM), out_dtype),
                   jax.ShapeDtypeStruct((batch, n_pairs, 2 * SSM_HEAD_DIM, SSM_STATE), _F32)],
        scratch_shapes=[pltpu.VMEM((SUBLANES + L, CONV_DIM), _F32),
                        pltpu.VMEM((n_pairs, 2 * SSM_HEAD_DIM, SSM_STATE), _F32),
                        pltpu.VMEM((L, D_SSM), _F32)],
        compiler_params=_cparams(("arbitrary", "arbitrary")),
        name="ssd_precise" if precise else "ssd",
    )(xbc, z, dt, dtt, conv_prev, h0.reshape(batch, n_pairs, 2 * SSM_HEAD_DIM, SSM_STATE),
      conv_w, conv_b.reshape(1, CONV_DIM), dtb_row, dt_bias.reshape(N_SSM_HEADS, 1),
      alog_row, a_log.reshape(N_SSM_HEADS, 1), dskip, w_ssm_norm.reshape(1, D_SSM))
    return y, h_out.reshape(batch, N_SSM_HEADS, SSM_HEAD_DIM, SSM_STATE)


def _outproj_router_kernel(x_ref, attn_ref, ssd_ref, wa_ref, wb_ref, wn_ref, wr_ref, br_ref, *rest, precise, n_blocks):
    if precise:
        h1_ref, t_ref, route_ref = rest[1:4]
        wa, wb = wa_ref, wb_ref
    else:
        h1_ref, t_ref, route_ref, wa, wb = rest
        _cast_weights_once(wa_ref, wa)
        _cast_weights_once(wb_ref, wb)

    @pl.when(pl.program_id(0) >= n_blocks)
    def _():
        t_ref[...] = jnp.zeros(t_ref.shape, _F32)

    @pl.when(pl.program_id(0) < n_blocks)
    def _():
        _outproj_router_body(x_ref, attn_ref, ssd_ref, wa, wb, wn_ref, wr_ref, br_ref, h1_ref, t_ref, route_ref, precise)


def _outproj_router_body(x_ref, attn_ref, ssd_ref, wa, wb, wn_ref, wr_ref, br_ref, h1_ref, t_ref, route_ref, precise):
    h1 = x_ref[...] + _dot(attn_ref[...], wa[...], precise) + _dot(ssd_ref[...], wb[...], precise)
    h1_ref[...] = h1
    t = _rms(h1, wn_ref[...])
    t_ref[...] = t
    logits = _dot(t, wr_ref[...], precise) + br_ref[...]
    lane = lax.broadcasted_iota(jnp.int32, logits.shape, 1).astype(_F32)
    neg = -jnp.inf
    big = float(LANES)
    is_g = lane < N_EXPERT_GROUPS
    gl = jnp.where(is_g, logits, neg)
    gmax = jnp.max(gl, axis=1, keepdims=True)
    g_idx = jnp.min(jnp.where(gl == gmax, lane, big), axis=1, keepdims=True)
    g_w = 1.0 / jnp.sum(jnp.where(is_g, jnp.exp(logits - gmax), 0.0), axis=1, keepdims=True)
    lo = N_EXPERT_GROUPS + EXPERTS_PER_GROUP * g_idx
    l1 = jnp.where(lane >= lo, jnp.where(lane < lo + EXPERTS_PER_GROUP, logits, neg), neg)
    m1 = jnp.max(l1, axis=1, keepdims=True)
    i1 = jnp.min(jnp.where(l1 == m1, lane, big), axis=1, keepdims=True)
    l2 = jnp.where(lane == i1, neg, l1)
    m2 = jnp.max(l2, axis=1, keepdims=True)
    i2 = jnp.min(jnp.where(l2 == m2, lane, big), axis=1, keepdims=True)
    r = jnp.exp(m2 - m1)
    p1 = 1.0 / (1.0 + r)
    w1 = g_w * p1
    w2 = g_w * (r * p1)
    e1 = i1 - N_EXPERT_GROUPS
    e2 = i2 - N_EXPERT_GROUPS
    route_ref[...] = jnp.where(lane == 0, e1, jnp.where(lane == 1, e2, jnp.where(lane == 2, w1,
                               jnp.where(lane == 3, w2, 0.0))))


def _outproj_router(x2d, attn, ssd, w_out, w_norm_ffn, w_router, b_router, tm, precise, t_rows, t_shared=None):
    t = x2d.shape[0]
    n_blocks = t // tm
    wdt = _F32 if precise else _BF16
    half_w = (D_ATTN, D_MODEL)
    fixed = lambda i: (0, 0)
    if precise:
        assert t_shared is not None and (t_rows - t) % tm == 0
        grid = (n_blocks,)
        row = lambda i: (i, 0)
        t_row = lambda i: (i + (t_rows - t) // tm, 0)
        extra_in, extra_args, aliases, scratch = [pl.BlockSpec(memory_space=pl.ANY)], [t_shared], {8: 1}, []
    else:
        grid = (n_blocks + pl.cdiv(t_rows - t, tm),)
        row = lambda i: (jnp.minimum(i, n_blocks - 1), 0)
        t_row = lambda i: (i, 0)
        extra_in, extra_args, aliases = [], [], {}
        scratch = [pltpu.VMEM(half_w, _BF16), pltpu.VMEM(half_w, _BF16)]
    return pl.pallas_call(
        functools.partial(_outproj_router_kernel, precise=precise, n_blocks=n_blocks),
        grid=grid,
        in_specs=[pl.BlockSpec((tm, D_MODEL), row), pl.BlockSpec((tm, D_ATTN), row), pl.BlockSpec((tm, D_SSM), row),
                  _resident(half_w, fixed), _resident(half_w, lambda i: (1, 0)),
                  pl.BlockSpec((1, D_MODEL), fixed), pl.BlockSpec((D_MODEL, LANES), fixed),
                  pl.BlockSpec((1, LANES), fixed)] + extra_in,
        out_specs=[pl.BlockSpec((tm, D_MODEL), row), pl.BlockSpec((tm, D_MODEL), t_row), pl.BlockSpec((tm, LANES), row)],
        out_shape=[jax.ShapeDtypeStruct((t, D_MODEL), _F32), jax.ShapeDtypeStruct((t_rows, D_MODEL), _F32),
                   jax.ShapeDtypeStruct((t, LANES), _F32)],
        scratch_shapes=scratch,
        input_output_aliases=aliases,
        compiler_params=_cparams(("arbitrary",)),
        name="outproj_router_precise" if precise else "outproj_router",
    )(x2d, attn, ssd, w_out, w_out, w_norm_ffn.reshape(1, D_MODEL), w_router.astype(wdt), b_router, *extra_args)


def _expert_kernel(te_ref, nv_ref, blk_ref, off_ref, idx_prev_ref, idx_ref, idx_next_ref, t_hbm, wg_ref, wu_ref, wd_ref,
                   y_hbm, xbuf, ybuf, sem_in, sem_out, *, tm, n_tiles):
    i = pl.program_id(0)
    i_prev = jnp.maximum(i - 1, 0)
    i_next = jnp.minimum(i + 1, n_tiles - 1)
    nv = nv_ref[i]
    nv_next = jnp.where(i + 1 < n_tiles, nv_ref[i_next], 0)
    f_half = D_FF_EXPERT // 2
    n_issue_chunks = 4
    per_chunk = tm // n_issue_chunks
    half = tm // 2
    n_pad = y_hbm.shape[0] // 2

    def gather_rows(idx, tile_i, s, rows):
        off = off_ref[tile_i]
        for r in rows:
            pltpu.make_async_copy(t_hbm.at[pl.ds(idx[0, 0, off + r], 1), :], xbuf.at[s, pl.ds(r, 1), :],
                                  sem_in.at[s]).start()

    def wait_gather(s):
        pltpu.make_async_copy(t_hbm.at[pl.ds(0, tm), :], xbuf.at[s], sem_in.at[s]).wait()

    def scatter_rows(idx, tile_i, s, rows):
        off = off_ref[tile_i] + 2 * tm
        valid = nv_ref[tile_i]
        for r in rows:
            spare = (r // half) * n_pad + (n_pad - half) + (r % half)
            dst = jnp.where(r < valid, idx[0, 0, off + r], spare)
            pltpu.make_async_copy(ybuf.at[s, pl.ds(r, 1), :], y_hbm.at[pl.ds(dst, 1), :], sem_out.at[s]).start()

    def wait_scatter(s):
        pltpu.make_async_copy(ybuf.at[s], y_hbm.at[pl.ds(0, tm), :], sem_out.at[s]).wait()

    @pl.when(i == 0)
    def _():
        gather_rows(idx_ref, i, 0, range(tm))
        ybuf[0, 0:half, :] = jnp.zeros((half, D_MODEL), _F32)
        for k in range(2):
            fill = pltpu.make_async_copy(ybuf.at[0, pl.ds(0, half), :],
                                         y_hbm.at[pl.ds((k + 1) * n_pad - half, half), :], sem_out.at[0])
            fill.start()
            fill.wait()

    def tile(s, has_prev):
        o = 1 - s

        def issue(c):
            rows = range(c * per_chunk, (c + 1) * per_chunk)
            gather_rows(idx_next_ref, i_next, o, rows)
            if has_prev:
                scatter_rows(idx_prev_ref, i_prev, o, rows)

        wait_gather(s)
        x = xbuf[s].astype(_BF16)
        parts = []
        for c in range(2):
            cols = slice(c * f_half, (c + 1) * f_half)
            hg = jnp.dot(x, wg_ref[:, cols].astype(_BF16), preferred_element_type=_F32)
            issue(2 * c)
            hu = jnp.dot(x, wu_ref[:, cols].astype(_BF16), preferred_element_type=_F32)
            issue(2 * c + 1)
            parts.append((_silu(hg) * hu).astype(_BF16))
        hh = jnp.concatenate(parts, axis=1)
        ybuf[s] = jnp.dot(hh, wd_ref[...].astype(_BF16), preferred_element_type=_F32)
        if has_prev:
            wait_scatter(o)

        @pl.when(nv_next == 0)
        def _():
            wait_gather(o)
            scatter_rows(idx_ref, i, s, range(tm))
            wait_scatter(s)

    used = nv > 0
    odd = lax.rem(i, 2) == 1

    @pl.when(i == 0)
    def _():
        tile(0, False)

    @pl.when(used & (i > 0) & jnp.logical_not(odd))
    def _():
        tile(0, True)

    @pl.when(used & odd)
    def _():
        tile(1, True)


def _moe_plan(e_ids, tm, n_tiles):
    n = e_ids.shape[0]
    n_assign = 2 * n
    n_pad = n + tm // 2
    e_flat = e_ids.reshape(-1)
    order = jnp.argsort(e_flat, stable=True).astype(jnp.int32)
    experts = jnp.arange(N_EXPERTS, dtype=jnp.int32)
    counts = jnp.sum((e_flat[:, None] == experts[None, :]).astype(jnp.int32), axis=0)
    seg_start = jnp.cumsum(counts) - counts
    tiles_per = (counts + tm - 1) // tm
    cum_tiles = jnp.cumsum(tiles_per)
    tile_start = cum_tiles - tiles_per
    n_used = cum_tiles[-1]
    tile_id = jnp.arange(n_tiles, dtype=jnp.int32)
    used = tile_id < n_used
    te = jnp.sum((cum_tiles[None, :] <= jnp.minimum(tile_id, n_used - 1)[:, None]).astype(jnp.int32), axis=1)
    of_tile = lambda v: jnp.sum(jnp.where(te[:, None] == experts[None, :], v[None, :], 0), axis=1)
    j = tile_id - of_tile(tile_start)
    nvalid = jnp.where(used, jnp.clip(of_tile(counts) - j * tm, 0, tm), 0).astype(jnp.int32)
    first = jnp.minimum(of_tile(seg_start) + j * tm, n_assign - 1)
    n_win = pl.cdiv(n_assign, tm)
    fill = (n_win + 1) * tm - n_assign
    tok = jnp.pad(order // 2, (0, fill)).reshape(n_win + 1, tm)
    dst = jnp.pad((order % 2) * n_pad + order // 2, (0, fill)).reshape(n_win + 1, tm)
    windows = jnp.concatenate([tok[:-1], tok[1:], dst[:-1], dst[1:]], axis=1).reshape(n_win, 1, 4 * tm)
    return te, nvalid, (first // tm).astype(jnp.int32), (first % tm).astype(jnp.int32), windows, n_pad


def _expert_mlp(t_all, e_ids, w_gate, w_up, w_down):
    n = t_all.shape[0]
    tm = EXPERT_TILE
    n_tiles = pl.cdiv(2 * n, tm) + N_EXPERTS
    te, nvalid, blk, off, windows, n_pad = _moe_plan(e_ids, tm, n_tiles)
    wmap = lambda i, te_ref, nv_ref, blk_ref, off_ref: (te_ref[i], 0, 0)

    def window_spec(shift):
        def index_map(i, te_ref, nv_ref, blk_ref, off_ref):
            return (blk_ref[jnp.clip(i + shift, 0, n_tiles - 1)], 0, 0)
        return pl.BlockSpec((1, 1, 4 * tm), index_map, memory_space=pltpu.SMEM)

    grid_spec = pltpu.PrefetchScalarGridSpec(
        num_scalar_prefetch=4,
        grid=(n_tiles,),
        in_specs=[
            window_spec(-1), window_spec(0), window_spec(1),
            pl.BlockSpec(memory_space=pl.ANY),
            pl.BlockSpec((None, D_MODEL, D_FF_EXPERT), wmap),
            pl.BlockSpec((None, D_MODEL, D_FF_EXPERT), wmap),
            pl.BlockSpec((None, D_FF_EXPERT, D_MODEL), wmap),
        ],
        out_specs=pl.BlockSpec(memory_space=pl.ANY),
        scratch_shapes=[pltpu.VMEM((2, tm, D_MODEL), _F32), pltpu.VMEM((2, tm, D_MODEL), _F32),
                        pltpu.SemaphoreType.DMA((2,)), pltpu.SemaphoreType.DMA((2,))],
    )
    y_rows = pl.pallas_call(
        functools.partial(_expert_kernel, tm=tm, n_tiles=n_tiles),
        grid_spec=grid_spec,
        out_shape=jax.ShapeDtypeStruct((2 * n_pad, D_MODEL), _F32),
        compiler_params=_cparams(("arbitrary",)),
        name="expert_mlp",
    )(te, nvalid, blk, off, windows, windows, windows, t_all, w_gate, w_up, w_down)
    return y_rows.reshape(2, n_pad, D_MODEL)


def _combine_ple_kernel(h1_ref, y_ref, route_ref, p_ref, wn_ref, wg_ref, wp_ref, wf_ref, o_ref, wg_sc):
    _cast_weights_once(wg_ref, wg_sc)
    route = route_ref[...]
    h2 = h1_ref[...] + route[:, 2:3] * y_ref[0] + route[:, 3:4] * y_ref[1]
    u = _rms(h2, wn_ref[...])
    gate = jnp.dot(u.astype(_BF16), wg_sc[...], preferred_element_type=_F32)
    gate = 1.0 / (1.0 + jnp.exp(-gate))
    pp = jnp.dot(p_ref[...].astype(_BF16), wp_ref[...], preferred_element_type=_F32)
    h3 = h2 + gate * pp
    o_ref[...] = _rms(h3, wf_ref[...])


def _combine_ple(h1, y_stk, route, p2d, w_norm_ple, w_ple_gate, w_ple_proj, w_norm_final, tm, row_off):
    t = h1.shape[0]
    off = row_off // tm
    row = lambda i: (i, 0)
    fixed = lambda i: (0, 0)
    return pl.pallas_call(
        _combine_ple_kernel,
        grid=(t // tm,),
        in_specs=[pl.BlockSpec((tm, D_MODEL), row),
                  pl.BlockSpec((2, tm, D_MODEL), lambda i: (0, i + off, 0)),
                  pl.BlockSpec((tm, LANES), row), pl.BlockSpec((tm, D_PLE), row),
                  pl.BlockSpec((1, D_MODEL), fixed), _resident((D_MODEL, D_MODEL), fixed),
                  pl.BlockSpec((D_PLE, D_MODEL), fixed), pl.BlockSpec((1, D_MODEL), fixed)],
        out_specs=pl.BlockSpec((tm, D_MODEL), row),
        out_shape=jax.ShapeDtypeStruct((t, D_MODEL), _F32),
        scratch_shapes=[pltpu.VMEM((D_MODEL, D_MODEL), _BF16)],
        compiler_params=_cparams(("arbitrary",)),
        name="combine_ple",
    )(h1, y_stk, route, p2d, w_norm_ple.reshape(1, D_MODEL), w_ple_gate, w_ple_proj.astype(_BF16),
      w_norm_final.reshape(1, D_MODEL))


def _sample_inproj_kernel(x_ref, wn_ref, w_ref, o_ref):
    u = _rms(x_ref[...], wn_ref[...])
    u_hi, u_lo = _split2(u)
    w_hi, w_lo = _split2(w_ref[...])
    nt = lambda a, b: lax.dot_general(a, b, _NT, preferred_element_type=_F32)
    o_ref[...] = nt(w_hi, u_hi) + nt(w_hi, u_lo) + nt(w_lo, u_hi)


def _sample_inproj(xs2d, w_norm, w_in_t, tn):
    rows = xs2d.shape[0]
    return pl.pallas_call(
        _sample_inproj_kernel,
        grid=(pl.cdiv(D_IN_PROJ, tn),),
        in_specs=[pl.BlockSpec((rows, D_MODEL), lambda j: (0, 0)), pl.BlockSpec((1, D_MODEL), lambda j: (0, 0)),
                  pl.BlockSpec((tn, D_MODEL), lambda j: (j, 0))],
        out_specs=pl.BlockSpec((tn, rows), lambda j: (j, 0)),
        out_shape=jax.ShapeDtypeStruct((D_IN_PROJ, rows), _F32),
        compiler_params=_cparams(("arbitrary",)),
        name="sample_inproj",
    )(xs2d, w_norm.reshape(1, D_MODEL), w_in_t).T


def _split2(x):
    hi = x.astype(_BF16)
    lo = (x - hi.astype(_F32)).astype(_BF16)
    return hi, lo


def _decode_attn_kernel(pt_ref, qkv_ref, c_ref, s1_ref, s2_ref, lq1_ref, lk1_ref, lq2_ref, lk2_ref, wsub_ref, *rest,
                        n_pages_step, n_steps, n_real):
    k_refs = rest[:n_pages_step]
    v_refs = rest[n_pages_step:2 * n_pages_step]
    o_ref, knew_ref, vnew_ref = rest[2 * n_pages_step:2 * n_pages_step + 3]
    q_sc, m_sc, l_sc, acc_sc, kt_sc, vt_sc = rest[2 * n_pages_step + 3:]
    j = pl.program_id(1)
    R = SAMPLE_ROWS
    hr = 2 * R
    page = kt_sc.shape[0] // N_ATT_HEADS

    @pl.when(j == 0)
    def _():
        c, s1, s2 = c_ref[...], s1_ref[...], s2_ref[...]
        lane = lax.broadcasted_iota(jnp.int32, (R, LANES), 1)
        kt_sc[...] = jnp.zeros(kt_sc.shape, _F32)
        vt_sc[...] = jnp.zeros(vt_sc.shape, _F32)
        for h in range(N_ATT_HEADS):
            sl = slice(h * LANES, (h + 1) * LANES)
            q = _rope_block(qkv_ref[:, sl], c, s1, s2) * (ATT_HEAD_DIM ** -0.5)
            k = _rope_block(qkv_ref[:, D_ATTN + h * LANES:D_ATTN + (h + 1) * LANES], c, s1, s2)
            v = qkv_ref[:, 2 * D_ATTN + h * LANES:2 * D_ATTN + (h + 1) * LANES]
            knew_ref[:, sl] = k
            vnew_ref[:, sl] = v
            kt_sc[pl.ds(h, R, stride=N_ATT_HEADS), :] = k
            vt_sc[pl.ds(h, R, stride=N_ATT_HEADS), :] = v
            q2 = jnp.concatenate([jnp.where(lane < ATT_HEAD_DIM, q, 0.0), jnp.where(lane >= ATT_HEAD_DIM, q, 0.0)], axis=0)
            hi, lo = _split2(q2)
            q_sc[h] = jnp.concatenate([hi, lo], axis=0)
        m_sc[...] = jnp.full(m_sc.shape, -jnp.inf, _F32)
        l_sc[...] = jnp.zeros(l_sc.shape, _F32)
        acc_sc[...] = jnp.zeros(acc_sc.shape, _F32)

    def head_rows(tiles, h):
        x = jnp.concatenate([t[pl.ds(h, page, stride=N_ATT_HEADS), :] for t in tiles], axis=0)
        return _split2(x)

    def process(k_tiles, v_tiles, mask):
        s_parts = []
        for h in range(N_ATT_HEADS):
            k_hi, k_lo = head_rows(k_tiles, h)
            q3 = q_sc[h]
            a = lax.dot_general(q3[0:2 * hr], k_hi, _NT, preferred_element_type=_F32)
            b = lax.dot_general(q3[0:hr], k_lo, _NT, preferred_element_type=_F32)
            s_parts.append(a[0:hr] + a[hr:2 * hr] + b)
        s = jnp.concatenate(s_parts, axis=0)
        if mask is not None:
            s = jnp.where(mask, s, -jnp.inf)
        m_prev = m_sc[...]
        m_new = jnp.maximum(m_prev, jnp.max(s, axis=1, keepdims=True))
        alpha = jnp.exp(m_prev - m_new)
        p = jnp.exp(s - m_new[:, :1])
        l_sc[...] = alpha * l_sc[...] + jnp.sum(p, axis=1, keepdims=True)
        m_sc[...] = m_new
        pv_parts = []
        for h in range(N_ATT_HEADS):
            v_hi, v_lo = head_rows(v_tiles, h)
            p_hi, p_lo = _split2(p[h * hr:(h + 1) * hr])
            a = jnp.dot(jnp.concatenate([p_hi, p_lo], axis=0), v_hi, preferred_element_type=_F32)
            b = jnp.dot(p_hi, v_lo, preferred_element_type=_F32)
            pv_parts.append(a[0:hr] + a[hr:2 * hr] + b)
        acc_sc[...] = alpha * acc_sc[...] + jnp.concatenate(pv_parts, axis=0)

    process(k_refs, v_refs, None)

    @pl.when(j == n_steps - 1)
    def _():
        rows = N_ATT_HEADS * hr
        row = lax.broadcasted_iota(jnp.int32, (rows, page), 0)
        col = lax.broadcasted_iota(jnp.int32, (rows, page), 1)
        qrow = row % R
        process([kt_sc], [vt_sc], (col <= qrow) & (col < n_real))
        o = acc_sc[...] / l_sc[...]
        lam = _lambda(lq1_ref, lk1_ref, lq2_ref, lk2_ref)
        wsub = wsub_ref[...]
        for h in range(N_ATT_HEADS):
            o1 = o[h * hr:h * hr + R]
            o2 = o[h * hr + R:(h + 1) * hr]
            o_ref[:, h * LANES:(h + 1) * LANES] = _diff_finalize(o1, o2, lam, wsub)


def _decode_attention(proj3, cache_k2, cache_v2, page_table, first_pos, lam_params, w_subln, n_real):
    bsz = proj3.shape[0]
    page = cache_k2.shape[2] // N_ATT_HEADS
    n_pages = page_table.shape[1]
    pg = PAGES_PER_STEP
    n_steps = n_pages // pg
    R = SAMPLE_ROWS
    c, s1, s2 = _rope_tables(first_pos, R)
    lq1, lk1, lq2, lk2 = [p.reshape(1, ATT_HEAD_DIM) for p in lam_params]
    fixed = lambda b, j, pt: (0, 0)
    tab = pl.BlockSpec((R, LANES), fixed)
    vec = pl.BlockSpec((1, ATT_HEAD_DIM), fixed)

    def page_spec(i):
        return pl.BlockSpec((None, None, page * N_ATT_HEADS, LANES), lambda b, j, pt: (0, pt[b, j * pg + i], 0, 0))

    out_row = pl.BlockSpec((None, R, D_ATTN), lambda b, j, pt: (b, 0, 0))
    grid_spec = pltpu.PrefetchScalarGridSpec(
        num_scalar_prefetch=1,
        grid=(bsz, n_steps),
        in_specs=[pl.BlockSpec((None, R, 3 * D_ATTN), lambda b, j, pt: (b, 0, 0)), tab, tab, tab, vec, vec, vec, vec,
                  pl.BlockSpec((1, ATT_V_DIM), fixed)]
                 + [page_spec(i) for i in range(pg)] + [page_spec(i) for i in range(pg)],
        out_specs=[out_row, out_row, out_row],
        scratch_shapes=[pltpu.VMEM((N_ATT_HEADS, 4 * R, LANES), _BF16),
                        pltpu.VMEM((N_ATT_HEADS * 2 * R, LANES), _F32), pltpu.VMEM((N_ATT_HEADS * 2 * R, LANES), _F32),
                        pltpu.VMEM((N_ATT_HEADS * 2 * R, LANES), _F32),
                        pltpu.VMEM((page * N_ATT_HEADS, LANES), _F32), pltpu.VMEM((page * N_ATT_HEADS, LANES), _F32)],
    )
    return pl.pallas_call(
        functools.partial(_decode_attn_kernel, n_pages_step=pg, n_steps=n_steps, n_real=n_real),
        grid_spec=grid_spec,
        out_shape=[jax.ShapeDtypeStruct((bsz, R, D_ATTN), _F32)] * 3,
        compiler_params=_cparams(("arbitrary", "arbitrary")),
        name="decode_attn",
    )(page_table, proj3, c, s1, s2, lq1, lk1, lq2, lk2, w_subln.reshape(1, ATT_V_DIM),
      *([cache_k2] * pg), *([cache_v2] * pg))


def kernel(x_prompt, x_sample, cache_k, cache_v, state_conv, state_ssm, page_table, p_prompt, p_sample, w_norm_mix, w_in, lambda_q1, lambda_k1, lambda_q2, lambda_k2, w_subln, conv_w, conv_b, dt_bias, A_log, D_skip, w_ssm_norm, w_out, w_norm_ffn, w_group_router, b_group_router, w_expert_router, b_expert_router, w_exp_gate, w_exp_up, w_exp_down, w_norm_ple, w_ple_gate, w_ple_proj, w_norm_final):
    bp, lp, _ = x_prompt.shape
    bs, ls, _ = x_sample.shape
    past = page_table.shape[1] * cache_k.shape[2]
    tp = bp * lp
    R = SAMPLE_ROWS
    ts = bs * R
    lam_params = (lambda_q1[0], lambda_k1[0], lambda_q2[0], lambda_k2[0])
    pad_r = LANES - N_EXPERT_GROUPS - N_EXPERTS
    w_router = jnp.pad(jnp.concatenate([w_group_router[0], w_expert_router[0]], axis=1), ((0, 0), (0, pad_r)))
    b_router = jnp.pad(jnp.concatenate([b_group_router[0], b_expert_router[0]]), (0, pad_r)).reshape(1, LANES)

    xp2d = x_prompt.reshape(tp, D_MODEL)
    w_in_t = w_in[0].T
    q, kf, kb, vf, vb, z, xbc, dt = _prompt_inproj(xp2d, w_norm_mix[0], w_in_t, lp, 512)
    attn_p = _prompt_attention(q, kb, vb, lam_params, w_subln[0], bp, lp, 512)
    ssd_p, ssm_p = _ssd(xbc, z, dt, jnp.zeros((bp, CONV_WIDTH - 1, CONV_DIM), _F32),
                        jnp.zeros((bp, N_SSM_HEADS, SSM_HEAD_DIM, SSM_STATE), _F32),
                        conv_w[0], conv_b[0], dt_bias[0], A_log[0], D_skip[0], w_ssm_norm[0],
                        bp, lp // SSM_CHUNK, SSM_CHUNK, False, _BF16)
    h1_p, t_all, route_p = _outproj_router(xp2d, attn_p, ssd_p, w_out[0], w_norm_ffn[0], w_router, b_router, 256, False,
                                           tp + ts)

    xs3 = jnp.pad(x_sample, ((0, 0), (0, R - ls), (0, 0)))
    xs2d = xs3.reshape(ts, D_MODEL)
    proj_s = _sample_inproj(xs2d, w_norm_mix[0], w_in_t, 512)
    proj3 = proj_s.reshape(bs, R, D_IN_PROJ)
    cache_rows = cache_k.shape[:2] + (cache_k.shape[2] * N_ATT_HEADS, LANES)
    attn_s, k_s, v_s = _decode_attention(proj3, cache_k.reshape(cache_rows), cache_v.reshape(cache_rows), page_table,
                                         past, lam_params, w_subln[0], ls)
    off = 3 * D_ATTN
    pad_rows = ((0, 0), (0, SSM_CHUNK - R), (0, 0))
    z_s = jnp.pad(proj3[:, :, off:off + D_SSM], pad_rows).reshape(bs * SSM_CHUNK, D_SSM)
    xbc_s3 = proj3[:, :, off + D_SSM:off + D_SSM + CONV_DIM]
    xbc_s = jnp.pad(xbc_s3, pad_rows).reshape(bs * SSM_CHUNK, CONV_DIM)
    dt_s = jnp.pad(proj3[:, :, off + D_SSM + CONV_DIM:], ((0, 0), (0, SSM_CHUNK - R), (0, LANES - N_SSM_HEADS)))
    dt_s = dt_s.reshape(bs * SSM_CHUNK, LANES)
    ssd_s, ssm_s = _ssd(xbc_s, z_s, dt_s, state_conv[0], state_ssm[0], conv_w[0], conv_b[0], dt_bias[0], A_log[0],
                        D_skip[0], w_ssm_norm[0], bs, 1, ls, True, _F32)
    ssd_s = ssd_s.reshape(bs, SSM_CHUNK, D_SSM)[:, :R].reshape(ts, D_SSM)
    h1_s, t_all, route_s = _outproj_router(xs2d, attn_s.reshape(ts, D_ATTN), ssd_s, w_out[0], w_norm_ffn[0],
                                           w_router, b_router, ts, True, tp + ts, t_shared=t_all)

    e_ids = jnp.concatenate([route_p[:, :2], route_s[:, :2]], axis=0).astype(jnp.int32)
    y_stk = _expert_mlp(t_all, e_ids, w_exp_gate[0], w_exp_up[0], w_exp_down[0])
    y_p = _combine_ple(h1_p, y_stk, route_p, p_prompt[0].reshape(tp, D_PLE), w_norm_ple[0], w_ple_gate[0],
                       w_ple_proj[0], w_norm_final, 256, 0)
    ps2d = jnp.pad(p_sample[0], ((0, 0), (0, R - ls), (0, 0))).reshape(ts, D_PLE)
    y_s = _combine_ple(h1_s, y_stk, route_s, ps2d, w_norm_ple[0], w_ple_gate[0], w_ple_proj[0], w_norm_final, ts, tp)

    y_prompt = y_p.reshape(bp, lp, D_MODEL)
    y_sample = y_s.reshape(bs, R, D_MODEL)[:, :ls]
    k_prompt = kf.reshape(1, bp, lp, N_ATT_HEADS, 2 * ATT_HEAD_DIM)
    v_prompt = vf.reshape(1, bp, lp, N_ATT_HEADS, ATT_V_DIM)
    conv_prompt = xbc.reshape(bp, lp, CONV_DIM)[:, lp - (CONV_WIDTH - 1):][None]
    ssm_prompt = ssm_p[None]
    k_sample = k_s[:, :ls].reshape(1, bs, ls, N_ATT_HEADS, 2 * ATT_HEAD_DIM)
    v_sample = v_s[:, :ls].reshape(1, bs, ls, N_ATT_HEADS, ATT_V_DIM)
    conv_sample = xbc_s3[:, ls - (CONV_WIDTH - 1):ls][None]
    ssm_sample = ssm_s[None]
    return (y_prompt, y_sample, k_prompt, v_prompt, conv_prompt, ssm_prompt, k_sample, v_sample, conv_sample, ssm_sample)
```

```python
import functools
import math

import jax
import jax.numpy as jnp
from jax import lax
from jax.experimental import pallas as pl
from jax.experimental.pallas import tpu as pltpu

D_MODEL = 2048
D_ATTN = 1024
D_SSM = 1024
ATT_HEAD_DIM = 64
N_ATT_HEADS = 8
ATT_V_DIM = 128
ROT_DIM = 16
ROPE_THETA = 500000.0
SSM_HEAD_DIM = 64
N_SSM_HEADS = 16
SSM_GROUPS = 2
SSM_STATE = 128
SSM_CHUNK = 128
CONV_WIDTH = 4
CONV_DIM = D_SSM + 2 * SSM_GROUPS * SSM_STATE
D_IN_PROJ = 3 * D_ATTN + D_SSM + CONV_DIM + N_SSM_HEADS
N_EXPERT_GROUPS = 4
EXPERTS_PER_GROUP = 8
N_EXPERTS = 32
D_FF_EXPERT = 512
D_PLE = 256
RMS_EPS = 1e-6
LAM_INIT = 0.8 - 0.6 * math.exp(-0.3 * 0)

LANES = 128
SUBLANES = 8
VMEM_LIMIT_BYTES = 56 * 2 ** 20

SAMPLE_ROWS = 8
EXPERT_TILE = 256
PAGES_PER_STEP = 8

_F32 = jnp.float32
_BF16 = jnp.bfloat16
_HI = lax.Precision.HIGHEST
_NT = (((1,), (1,)), ((), ()))


def _cparams(semantics):
    return pltpu.CompilerParams(dimension_semantics=semantics, vmem_limit_bytes=VMEM_LIMIT_BYTES)


def _rms(x, w):
    return x * lax.rsqrt(jnp.mean(x * x, axis=-1, keepdims=True) + RMS_EPS) * w


def _silu(x):
    return x * (1.0 / (1.0 + jnp.exp(-x)))


def _softplus(x):
    return jnp.maximum(x, 0.0) + jnp.log(1.0 + jnp.exp(-jnp.abs(x)))


def _dot(a, b, precise=False):
    if precise:
        return jnp.dot(a.astype(_F32), b.astype(_F32), preferred_element_type=_F32, precision=_HI)
    return jnp.dot(a.astype(_BF16), b.astype(_BF16), preferred_element_type=_F32)


def _dot_nt(a, b, precise=False):
    if precise:
        return lax.dot_general(a.astype(_F32), b.astype(_F32), _NT, preferred_element_type=_F32, precision=_HI)
    return lax.dot_general(a.astype(_BF16), b.astype(_BF16), _NT, preferred_element_type=_F32)


def _rope_block(y, c, s1, s2):
    return y * c + pltpu.roll(y, LANES - ROT_DIM // 2, 1) * s1 + pltpu.roll(y, ROT_DIM // 2, 1) * s2


def _rope_tables(first_pos, n):
    half = ROT_DIM // 2
    pos = first_pos + jnp.arange(n, dtype=jnp.int32)
    inv_freq = jnp.power(ROPE_THETA, -jnp.arange(half, dtype=_F32) * (2.0 / ROT_DIM))
    dim = jnp.arange(LANES, dtype=jnp.int32) % ATT_HEAD_DIM
    ang = pos.astype(_F32)[:, None] * inv_freq[dim % half][None, :]
    cos, sin = jnp.cos(ang), jnp.sin(ang)
    first, second = (dim < half)[None, :], ((dim >= half) & (dim < ROT_DIM))[None, :]
    c = jnp.where(first | second, cos, 1.0)
    s1 = jnp.where(first, -sin, 0.0)
    s2 = jnp.where(second, sin, 0.0)
    return c, s1, s2


def _cast_weights_once(w_ref, wb_sc, transposed=False, valid_out=None):
    @pl.when(pl.program_id(0) == 0)
    def _():
        if transposed:
            w = w_ref[0:wb_sc.shape[1], :]
            if valid_out is not None:
                w = jnp.where(lax.broadcasted_iota(jnp.int32, w.shape, 0) < valid_out, w, 0.0)
            w = w.T
        else:
            w = w_ref[...]
        wb_sc[...] = w.astype(wb_sc.dtype)


def _resident(shape, index_map):
    return pl.BlockSpec(shape, index_map, pipeline_mode=pl.Buffered(1))


def _inproj_qk_kernel(x_ref, wn_ref, w_ref, c_ref, s1_ref, s2_ref, q_ref, kf_ref, kb_ref, wb_sc):
    _cast_weights_once(w_ref, wb_sc, transposed=True)
    u = _rms(x_ref[...], wn_ref[...]).astype(_BF16)
    y = jnp.dot(u, wb_sc[...], preferred_element_type=_F32)
    c, s1, s2 = c_ref[...], s1_ref[...], s2_ref[...]
    q_scale = (ATT_HEAD_DIM ** -0.5) * math.log2(math.e)
    for j in range(2 * N_ATT_HEADS):
        r = _rope_block(y[:, j * LANES:(j + 1) * LANES], c, s1, s2)
        if j < N_ATT_HEADS:
            q_ref[:, j * LANES:(j + 1) * LANES] = (r * q_scale).astype(_BF16)
        else:
            jj = j - N_ATT_HEADS
            kf_ref[pl.ds(jj, r.shape[0], stride=N_ATT_HEADS), :] = r
            kb_ref[:, jj * LANES:(jj + 1) * LANES] = r.astype(_BF16)


def _inproj_vz_kernel(x_ref, wn_ref, w_ref, vf_ref, vb_ref, z_ref, wb_sc):
    _cast_weights_once(w_ref, wb_sc, transposed=True)
    u = _rms(x_ref[...], wn_ref[...]).astype(_BF16)
    y = jnp.dot(u, wb_sc[...], preferred_element_type=_F32)
    v = y[:, :D_ATTN]
    for h in range(N_ATT_HEADS):
        vf_ref[pl.ds(h, v.shape[0], stride=N_ATT_HEADS), :] = v[:, h * LANES:(h + 1) * LANES]
    vb_ref[...] = v.astype(_BF16)
    z_ref[...] = y[:, D_ATTN:].astype(_BF16)


def _inproj_xbc_kernel(x_ref, wn_ref, w_ref, xbc_ref, dt_ref, wb_sc):
    _cast_weights_once(w_ref, wb_sc, transposed=True, valid_out=CONV_DIM + N_SSM_HEADS)
    u = _rms(x_ref[...], wn_ref[...]).astype(_BF16)
    y = jnp.dot(u, wb_sc[...], preferred_element_type=_F32)
    xbc_ref[...] = y[:, :CONV_DIM]
    dt_ref[...] = y[:, CONV_DIM:]


def _prompt_inproj(x2d, w_norm, w_in_t, seq, tm):
    t = x2d.shape[0]
    nblk = t // tm
    per_seq = seq // tm
    wn = w_norm.reshape(1, D_MODEL)
    row = lambda i: (i, 0)
    fixed = lambda i: (0, 0)
    x_spec = pl.BlockSpec((tm, D_MODEL), row)
    wn_spec = pl.BlockSpec((1, D_MODEL), fixed)
    head_spec = pl.BlockSpec((tm * N_ATT_HEADS, LANES), row)
    wide = 2 * D_ATTN

    c, s1, s2 = _rope_tables(0, seq)
    tab_spec = pl.BlockSpec((tm, LANES), lambda i: (i % per_seq, 0))
    q, kf, kb = pl.pallas_call(
        _inproj_qk_kernel,
        grid=(nblk,),
        in_specs=[x_spec, wn_spec, _resident((wide, D_MODEL), fixed), tab_spec, tab_spec, tab_spec],
        out_specs=[pl.BlockSpec((tm, D_ATTN), row), head_spec, pl.BlockSpec((tm, D_ATTN), row)],
        out_shape=[jax.ShapeDtypeStruct((t, D_ATTN), _BF16), jax.ShapeDtypeStruct((t * N_ATT_HEADS, LANES), _F32),
                   jax.ShapeDtypeStruct((t, D_ATTN), _BF16)],
        scratch_shapes=[pltpu.VMEM((D_MODEL, wide), _BF16)],
        compiler_params=_cparams(("arbitrary",)),
        name="inproj_qk",
    )(x2d, wn, w_in_t, c, s1, s2)

    vf, vb, z = pl.pallas_call(
        _inproj_vz_kernel,
        grid=(nblk,),
        in_specs=[x_spec, wn_spec, _resident((wide, D_MODEL), lambda i: (1, 0))],
        out_specs=[head_spec, pl.BlockSpec((tm, D_ATTN), row), pl.BlockSpec((tm, D_SSM), row)],
        out_shape=[jax.ShapeDtypeStruct((t * N_ATT_HEADS, LANES), _F32), jax.ShapeDtypeStruct((t, D_ATTN), _BF16),
                   jax.ShapeDtypeStruct((t, D_SSM), _BF16)],
        scratch_shapes=[pltpu.VMEM((D_MODEL, wide), _BF16)],
        compiler_params=_cparams(("arbitrary",)),
        name="inproj_vz",
    )(x2d, wn, w_in_t)

    n_tail = CONV_DIM + LANES
    xbc, dt = pl.pallas_call(
        _inproj_xbc_kernel,
        grid=(nblk,),
        in_specs=[x_spec, wn_spec, _resident((wide, D_MODEL), lambda i: (2, 0))],
        out_specs=[pl.BlockSpec((tm, CONV_DIM), row), pl.BlockSpec((tm, LANES), row)],
        out_shape=[jax.ShapeDtypeStruct((t, CONV_DIM), _F32), jax.ShapeDtypeStruct((t, LANES), _F32)],
        scratch_shapes=[pltpu.VMEM((D_MODEL, n_tail), _BF16)],
        compiler_params=_cparams(("arbitrary",)),
        name="inproj_xbc",
    )(x2d, wn, w_in_t)
    return q, kf, kb, vf, vb, z, xbc, dt


def _lambda(lq1_ref, lk1_ref, lq2_ref, lk2_ref):
    a = jnp.sum(lq1_ref[...] * lk1_ref[...], axis=-1, keepdims=True)
    b = jnp.sum(lq2_ref[...] * lk2_ref[...], axis=-1, keepdims=True)
    return jnp.exp(a) - jnp.exp(b) + LAM_INIT


def _diff_finalize(o1, o2, lam, wsub):
    a = o1 - lam * o2
    return _rms(a, wsub) * (1.0 - LAM_INIT)


def _attn_kernel(q_ref, k_ref, v_ref, lq1_ref, lk1_ref, lq2_ref, lk2_ref, wsub_ref, o_ref,
                 qt_sc, vt_sc, bias_sc, sa_sc, sb_sc, m_sc, acc_sc, *, tile):
    n_kt = vt_sc.shape[0]
    dv = ATT_V_DIM
    masked_out = -1e30

    for c in range(n_kt):
        vt_sc[c, 0:dv, :] = v_ref[c * tile:(c + 1) * tile, :].astype(_F32).T.astype(_BF16)
        vt_sc[c, dv:, :] = jnp.ones((vt_sc.shape[1] - dv, tile), _BF16)
    key = lax.broadcasted_iota(jnp.int32, (tile, tile), 0)
    qry = lax.broadcasted_iota(jnp.int32, (tile, tile), 1)
    bias_sc[...] = jnp.where(key <= qry, 0.0, masked_out)

    def start_query_tile(qi):
        q = q_ref[pl.ds(pl.multiple_of(qi * tile, tile), tile), :].astype(_F32)
        lane = lax.broadcasted_iota(jnp.int32, (tile, LANES), 1)
        qt_sc[0] = jnp.where(lane < ATT_HEAD_DIM, q, 0.0).T.astype(_BF16)
        qt_sc[1] = jnp.where(lane >= ATT_HEAD_DIM, q, 0.0).T.astype(_BF16)
        m_sc[...] = jnp.full(m_sc.shape, -jnp.inf, _F32)
        acc_sc[...] = jnp.zeros(acc_sc.shape, _F32)
        scores(0, sa_sc)

    def scores(j, dst):
        k = k_ref[pl.ds(pl.multiple_of(j * tile, tile), tile), :]
        for mm in range(2):
            dst[mm] = jnp.dot(k, qt_sc[mm], preferred_element_type=_F32)

    def softmax_pv(j, src, diagonal=False):
        vt = vt_sc[j]
        for mm in range(2):
            st = src[mm]
            if diagonal:
                st = st + bias_sc[...]
            m_prev = m_sc[mm]
            m_new = jnp.maximum(m_prev, jnp.max(st, axis=0, keepdims=True))
            alpha = jnp.exp2(m_prev - m_new)
            pt = jnp.exp2(st - m_new[0:1, :]).astype(_BF16)
            acc_sc[mm] = alpha[0:1, :] * acc_sc[mm] + jnp.dot(vt, pt, preferred_element_type=_F32)
            m_sc[mm] = m_new

    def pair(jj, carry):
        j0 = 2 * jj
        scores(j0 + 1, sb_sc)
        softmax_pv(j0, sa_sc)
        scores(j0 + 2, sa_sc)
        softmax_pv(j0 + 1, sb_sc)
        return carry

    n_q = q_ref.shape[0] // tile
    lam = _lambda(lq1_ref, lk1_ref, lq2_ref, lk2_ref)

    def query_tile(qi, carry):
        lax.fori_loop(0, qi // 2, pair, 0)
        odd = lax.rem(qi, 2) == 1

        def finish_and_start_next():
            outs = []
            for mm in range(2):
                acc = acc_sc[mm]
                outs.append((acc[0:dv, :] / acc[dv:dv + 1, :]).T)
            rows = pl.ds(pl.multiple_of(qi * tile, tile), tile)
            o_ref[rows, :] = _diff_finalize(outs[0], outs[1], lam, wsub_ref[...]).astype(o_ref.dtype)
            start_query_tile(jnp.minimum(qi + 1, n_q - 1))

        @pl.when(odd)
        def _():
            scores(qi, sb_sc)
            softmax_pv(qi - 1, sa_sc)
            softmax_pv(qi, sb_sc, diagonal=True)
            finish_and_start_next()

        @pl.when(jnp.logical_not(odd))
        def _():
            softmax_pv(qi, sa_sc, diagonal=True)
            finish_and_start_next()

        return carry

    start_query_tile(0)
    lax.fori_loop(0, n_q, query_tile, 0)


def _prompt_attention(q, kb, vb, lam_params, w_subln, batch, seq, tile):
    t = q.shape[0]
    lq1, lk1, lq2, lk2 = [p.reshape(1, ATT_HEAD_DIM) for p in lam_params]
    vec = pl.BlockSpec((1, ATT_HEAD_DIM), lambda b, h: (0, 0))
    head_seq = pl.BlockSpec((seq, LANES), lambda b, h: (b, h))
    return pl.pallas_call(
        functools.partial(_attn_kernel, tile=tile),
        grid=(batch, N_ATT_HEADS),
        in_specs=[head_seq, head_seq, head_seq, vec, vec, vec, vec,
                  pl.BlockSpec((1, ATT_V_DIM), lambda b, h: (0, 0))],
        out_specs=head_seq,
        out_shape=jax.ShapeDtypeStruct((t, D_ATTN), _BF16),
        scratch_shapes=[pltpu.VMEM((2, LANES, tile), _BF16),
                        pltpu.VMEM((seq // tile, ATT_V_DIM + 2 * SUBLANES, tile), _BF16),
                        pltpu.VMEM((tile, tile), _F32),
                        pltpu.VMEM((2, tile, tile), _F32), pltpu.VMEM((2, tile, tile), _F32),
                        pltpu.VMEM((2, SUBLANES, tile), _F32),
                        pltpu.VMEM((2, ATT_V_DIM + 2 * SUBLANES, tile), _F32)],
        compiler_params=_cparams(("arbitrary", "arbitrary")),
        name="prompt_attn",
    )(q, kb, vb, lq1, lk1, lq2, lk2, w_subln.reshape(1, ATT_V_DIM))


def _ssd_kernel(xbc_ref, z_ref, dt_ref, dtt_ref, cprev_ref, h0_ref, convw_ref, convb_ref, dtb_row_ref, dtb_col_ref,
                alog_row_ref, alog_col_ref, dskip_ref, wnorm_ref, y_ref, hout_ref,
                xp_sc, st_sc, yg_sc, *, n_valid, precise):
    c = pl.program_id(1)
    L = SSM_CHUNK
    n_pairs = N_SSM_HEADS // 2
    hp = 2 * SSM_HEAD_DIM

    @pl.when(c == 0)
    def _():
        xp_sc[0:SUBLANES, :] = jnp.zeros((SUBLANES, CONV_DIM), _F32)
        xp_sc[SUBLANES - (CONV_WIDTH - 1):SUBLANES, :] = cprev_ref[...]
        st_sc[...] = h0_ref[...]

    xp_sc[SUBLANES:SUBLANES + L, :] = xbc_ref[...]
    base = SUBLANES - (CONV_WIDTH - 1)
    conv = convb_ref[...] + convw_ref[0:1, :] * xp_sc[base:base + L, :]
    for j in range(1, CONV_WIDTH):
        conv = conv + convw_ref[j:j + 1, :] * xp_sc[base + j:base + j + L, :]
    xp_sc[0:SUBLANES, :] = xp_sc[L:L + SUBLANES, :]
    xc = _silu(conv)
    xs = xc[:, :D_SSM]

    row_i = lax.broadcasted_iota(jnp.int32, (L, LANES), 0)
    col_i = lax.broadcasted_iota(jnp.int32, (L, LANES), 1)
    dt = _softplus(dt_ref[...] + dtb_row_ref[...])
    dtt = _softplus(dtt_ref[...] + dtb_col_ref[...])
    if n_valid < L:
        dt = jnp.where(row_i < n_valid, dt, 0.0)
        dtt = jnp.where(lax.broadcasted_iota(jnp.int32, dtt.shape, 1) < n_valid, dtt, 0.0)
    ad = dt * (-jnp.exp(alog_row_ref[...]))
    adt = dtt * (-jnp.exp(alog_col_ref[...]))
    tril = (col_i <= row_i).astype(_F32)
    triu = (row_i <= col_i).astype(_F32)
    acs = jnp.dot(tril, ad, preferred_element_type=_F32, precision=_HI)
    acst = jnp.dot(adt, triu, preferred_element_type=_F32, precision=_HI)
    causal = col_i <= row_i
    lane_lo = col_i < SSM_HEAD_DIM

    cb = []
    for g in range(SSM_GROUPS):
        bg = xc[:, D_SSM + g * SSM_STATE:D_SSM + (g + 1) * SSM_STATE]
        cg = xc[:, D_SSM + (SSM_GROUPS + g) * SSM_STATE:D_SSM + (SSM_GROUPS + g + 1) * SSM_STATE]
        cb.append((bg, cg, _dot_nt(cg, bg, precise)))

    for p in range(n_pairs):
        ha, hb = 2 * p, 2 * p + 1
        bg, cg, cbg = cb[ha // (N_SSM_HEADS // SSM_GROUPS)]
        col_a, col_b = acs[:, ha:ha + 1], acs[:, hb:hb + 1]
        lm_a = jnp.where(causal, jnp.exp(col_a - acst[ha:ha + 1, :]), 0.0)
        lm_b = jnp.where(causal, jnp.exp(col_b - acst[hb:hb + 1, :]), 0.0)
        dt_pair = jnp.where(lane_lo, dt[:, ha:ha + 1], dt[:, hb:hb + 1])
        x_pair = xs[:, p * hp:(p + 1) * hp] * dt_pair
        y_diag = jnp.where(lane_lo, _dot(cbg * lm_a, x_pair, precise), _dot(cbg * lm_b, x_pair, precise))
        st = st_sc[p]
        y_off = _dot_nt(cg, st, precise) * jnp.where(lane_lo, jnp.exp(col_a), jnp.exp(col_b))
        last_a, last_b = acs[L - 1:L, ha:ha + 1], acs[L - 1:L, hb:hb + 1]
        decay = jnp.where(lane_lo, jnp.exp(last_a - col_a), jnp.exp(last_b - col_b))
        upd = _dot((x_pair * decay).T, bg, precise)
        row_lo = row_i < SSM_HEAD_DIM
        st_sc[p] = jnp.where(row_lo, jnp.exp(last_a), jnp.exp(last_b)) * st + upd
        y_pair = y_diag + y_off + xs[:, p * hp:(p + 1) * hp] * dskip_ref[:, p * hp:(p + 1) * hp]
        zg = z_ref[:, p * hp:(p + 1) * hp].astype(_F32)
        yg_sc[:, p * hp:(p + 1) * hp] = y_pair * _silu(zg)

    gw = D_SSM // SSM_GROUPS
    for g in range(SSM_GROUPS):
        y_ref[:, g * gw:(g + 1) * gw] = _rms(yg_sc[:, g * gw:(g + 1) * gw],
                                             wnorm_ref[:, g * gw:(g + 1) * gw]).astype(y_ref.dtype)

    hout_ref[...] = st_sc[...]


def _ssd(xbc, z, dt, conv_prev, h0, conv_w, conv_b, dt_bias, a_log, d_skip, w_ssm_norm, batch, n_chunks,
         n_valid, precise, out_dtype):
    L = SSM_CHUNK
    rows = xbc.shape[0]
    n_pairs = N_SSM_HEADS // 2
    dtt = dt[:, :N_SSM_HEADS].T
    pad = LANES - N_SSM_HEADS
    dtb_row = jnp.pad(dt_bias, (0, pad)).reshape(1, LANES)
    alog_row = jnp.pad(a_log, (0, pad)).reshape(1, LANES)
    dskip = jnp.repeat(d_skip, SSM_HEAD_DIM).reshape(1, D_SSM)
    blk = lambda b, c: (b * n_chunks + c, 0)
    fixed = lambda b, c: (0, 0)
    y, h_out = pl.pallas_call(
        functools.partial(_ssd_kernel, n_valid=n_valid, precise=precise),
        grid=(batch, n_chunks),
        in_specs=[
            pl.BlockSpec((L, CONV_DIM), blk),
            pl.BlockSpec((L, D_SSM), blk),
            pl.BlockSpec((L, LANES), blk),
            pl.BlockSpec((N_SSM_HEADS, L), lambda b, c: (0, b * n_chunks + c)),
            pl.BlockSpec((None, CONV_WIDTH - 1, CONV_DIM), lambda b, c: (b, 0, 0)),
            pl.BlockSpec((None, n_pairs, 2 * SSM_HEAD_DIM, SSM_STATE), lambda b, c: (b, 0, 0, 0)),
            pl.BlockSpec((CONV_WIDTH, CONV_DIM), fixed),
            pl.BlockSpec((1, CONV_DIM), fixed),
            pl.BlockSpec((1, LANES), fixed),
            pl.BlockSpec((N_SSM_HEADS, 1), fixed),
            pl.BlockSpec((1, LANES), fixed),
            pl.BlockSpec((N_SSM_HEADS, 1), fixed),
            pl.BlockSpec((1, D_SSM), fixed),
            pl.BlockSpec((1, D_SSM), fixed),
        ],
        out_specs=[pl.BlockSpec((L, D_SSM), blk),
                   pl.BlockSpec((None, n_pairs, 2 * SSM_HEAD_DIM, SSM_STATE), lambda b, c: (b, 0, 0, 0))],
        out_shape=[jax.ShapeDtypeStruct((rows, D_SSM), out_dtype),
                   jax.ShapeDtypeStruct((batch, n_pairs, 2 * SSM_HEAD_DIM, SSM_STATE), _F32)],
        scratch_shapes=[pltpu.VMEM((SUBLANES + L, CONV_DIM), _F32),
                        pltpu.VMEM((n_pairs, 2 * SSM_HEAD_DIM, SSM_STATE), _F32),
                        pltpu.VMEM((L, D_SSM), _F32)],
        compiler_params=_cparams(("arbitrary", "arbitrary")),
        name="ssd_precise" if precise else "ssd",
    )(xbc, z, dt, dtt, conv_prev, h0.reshape(batch, n_pairs, 2 * SSM_HEAD_DIM, SSM_STATE),
      conv_w, conv_b.reshape(1, CONV_DIM), dtb_row, dt_bias.reshape(N_SSM_HEADS, 1),
      alog_row, a_log.reshape(N_SSM_HEADS, 1), dskip, w_ssm_norm.reshape(1, D_SSM))
    return y, h_out.reshape(batch, N_SSM_HEADS, SSM_HEAD_DIM, SSM_STATE)


def _outproj_router_kernel(x_ref, attn_ref, ssd_ref, wa_ref, wb_ref, wn_ref, wr_ref, br_ref, *rest, precise, n_blocks):
    if precise:
        h1_ref, t_ref, route_ref = rest[1:4]
        wa, wb = wa_ref, wb_ref
    else:
        h1_ref, t_ref, route_ref, wa, wb = rest
        _cast_weights_once(wa_ref, wa)
        _cast_weights_once(wb_ref, wb)

    @pl.when(pl.program_id(0) >= n_blocks)
    def _():
        t_ref[...] = jnp.zeros(t_ref.shape, _F32)

    @pl.when(pl.program_id(0) < n_blocks)
    def _():
        _outproj_router_body(x_ref, attn_ref, ssd_ref, wa, wb, wn_ref, wr_ref, br_ref, h1_ref, t_ref, route_ref, precise)


def _outproj_router_body(x_ref, attn_ref, ssd_ref, wa, wb, wn_ref, wr_ref, br_ref, h1_ref, t_ref, route_ref, precise):
    h1 = x_ref[...] + _dot(attn_ref[...], wa[...], precise) + _dot(ssd_ref[...], wb[...], precise)
    h1_ref[...] = h1
    t = _rms(h1, wn_ref[...])
    t_ref[...] = t
    logits = _dot(t, wr_ref[...], precise) + br_ref[...]
    lane = lax.broadcasted_iota(jnp.int32, logits.shape, 1).astype(_F32)
    neg = -jnp.inf
    big = float(LANES)
    is_g = lane < N_EXPERT_GROUPS
    gl = jnp.where(is_g, logits, neg)
    gmax = jnp.max(gl, axis=1, keepdims=True)
    g_idx = jnp.min(jnp.where(gl == gmax, lane, big), axis=1, keepdims=True)
    g_w = 1.0 / jnp.sum(jnp.where(is_g, jnp.exp(logits - gmax), 0.0), axis=1, keepdims=True)
    lo = N_EXPERT_GROUPS + EXPERTS_PER_GROUP * g_idx
    l1 = jnp.where(lane >= lo, jnp.where(lane < lo + EXPERTS_PER_GROUP, logits, neg), neg)
    m1 = jnp.max(l1, axis=1, keepdims=True)
    i1 = jnp.min(jnp.where(l1 == m1, lane, big), axis=1, keepdims=True)
    l2 = jnp.where(lane == i1, neg, l1)
    m2 = jnp.max(l2, axis=1, keepdims=True)
    i2 = jnp.min(jnp.where(l2 == m2, lane, big), axis=1, keepdims=True)
    r = jnp.exp(m2 - m1)
    p1 = 1.0 / (1.0 + r)
    w1 = g_w * p1
    w2 = g_w * (r * p1)
    e1 = i1 - N_EXPERT_GROUPS
    e2 = i2 - N_EXPERT_GROUPS
    route_ref[...] = jnp.where(lane == 0, e1, jnp.where(lane == 1, e2, jnp.where(lane == 2, w1,
                               jnp.where(lane == 3, w2, 0.0))))


def _outproj_router(x2d, attn, ssd, w_out, w_norm_ffn, w_router, b_router, tm, precise, t_rows, t_shared=None):
    t = x2d.shape[0]
    n_blocks = t // tm
    wdt = _F32 if precise else _BF16
    half_w = (D_ATTN, D_MODEL)
    fixed = lambda i: (0, 0)
    if precise:
        assert t_shared is not None and (t_rows - t) % tm == 0
        grid = (n_blocks,)
        row = lambda i: (i, 0)
        t_row = lambda i: (i + (t_rows - t) // tm, 0)
        extra_in, extra_args, aliases, scratch = [pl.BlockSpec(memory_space=pl.ANY)], [t_shared], {8: 1}, []
    else:
        grid = (n_blocks + pl.cdiv(t_rows - t, tm),)
        row = lambda i: (jnp.minimum(i, n_blocks - 1), 0)
        t_row = lambda i: (i, 0)
        extra_in, extra_args, aliases = [], [], {}
        scratch = [pltpu.VMEM(half_w, _BF16), pltpu.VMEM(half_w, _BF16)]
    return pl.pallas_call(
        functools.partial(_outproj_router_kernel, precise=precise, n_blocks=n_blocks),
        grid=grid,
        in_specs=[pl.BlockSpec((tm, D_MODEL), row), pl.BlockSpec((tm, D_ATTN), row), pl.BlockSpec((tm, D_SSM), row),
                  _resident(half_w, fixed), _resident(half_w, lambda i: (1, 0)),
                  pl.BlockSpec((1, D_MODEL), fixed), pl.BlockSpec((D_MODEL, LANES), fixed),
                  pl.BlockSpec((1, LANES), fixed)] + extra_in,
        out_specs=[pl.BlockSpec((tm, D_MODEL), row), pl.BlockSpec((tm, D_MODEL), t_row), pl.BlockSpec((tm, LANES), row)],
        out_shape=[jax.ShapeDtypeStruct((t, D_MODEL), _F32), jax.ShapeDtypeStruct((t_rows, D_MODEL), _F32),
                   jax.ShapeDtypeStruct((t, LANES), _F32)],
        scratch_shapes=scratch,
        input_output_aliases=aliases,
        compiler_params=_cparams(("arbitrary",)),
        name="outproj_router_precise" if precise else "outproj_router",
    )(x2d, attn, ssd, w_out, w_out, w_norm_ffn.reshape(1, D_MODEL), w_router.astype(wdt), b_router, *extra_args)


def _expert_kernel(te_ref, nv_ref, blk_ref, off_ref, idx_prev_ref, idx_ref, idx_next_ref, t_hbm, wg_ref, wu_ref, wd_ref,
                   y_hbm, xbuf, ybuf, sem_in, sem_out, *, tm, n_tiles):
    i = pl.program_id(0)
    i_prev = jnp.maximum(i - 1, 0)
    i_next = jnp.minimum(i + 1, n_tiles - 1)
    nv = nv_ref[i]
    nv_next = jnp.where(i + 1 < n_tiles, nv_ref[i_next], 0)
    f_half = D_FF_EXPERT // 2
    n_issue_chunks = 4
    per_chunk = tm // n_issue_chunks
    half = tm // 2
    n_pad = y_hbm.shape[0] // 2

    def gather_rows(idx, tile_i, s, rows):
        off = off_ref[tile_i]
        for r in rows:
            pltpu.make_async_copy(t_hbm.at[pl.ds(idx[0, 0, off + r], 1), :], xbuf.at[s, pl.ds(r, 1), :],
                                  sem_in.at[s]).start()

    def wait_gather(s):
        pltpu.make_async_copy(t_hbm.at[pl.ds(0, tm), :], xbuf.at[s], sem_in.at[s]).wait()

    def scatter_rows(idx, tile_i, s, rows):
        off = off_ref[tile_i] + 2 * tm
        valid = nv_ref[tile_i]
        for r in rows:
            spare = (r // half) * n_pad + (n_pad - half) + (r % half)
            dst = jnp.where(r < valid, idx[0, 0, off + r], spare)
            pltpu.make_async_copy(ybuf.at[s, pl.ds(r, 1), :], y_hbm.at[pl.ds(dst, 1), :], sem_out.at[s]).start()

    def wait_scatter(s):
        pltpu.make_async_copy(ybuf.at[s], y_hbm.at[pl.ds(0, tm), :], sem_out.at[s]).wait()

    @pl.when(i == 0)
    def _():
        gather_rows(idx_ref, i, 0, range(tm))
        ybuf[0, 0:half, :] = jnp.zeros((half, D_MODEL), _F32)
        for k in range(2):
            fill = pltpu.make_async_copy(ybuf.at[0, pl.ds(0, half), :],
                                         y_hbm.at[pl.ds((k + 1) * n_pad - half, half), :], sem_out.at[0])
            fill.start()
            fill.wait()

    def tile(s, has_prev):
        o = 1 - s

        def issue(c):
            rows = range(c * per_chunk, (c + 1) * per_chunk)
            gather_rows(idx_next_ref, i_next, o, rows)
            if has_prev:
                scatter_rows(idx_prev_ref, i_prev, o, rows)

        wait_gather(s)
        x = xbuf[s].astype(_BF16)
        parts = []
        for c in range(2):
            cols = slice(c * f_half, (c + 1) * f_half)
            hg = jnp.dot(x, wg_ref[:, cols].astype(_BF16), preferred_element_type=_F32)
            issue(2 * c)
            hu = jnp.dot(x, wu_ref[:, cols].astype(_BF16), preferred_element_type=_F32)
            issue(2 * c + 1)
            parts.append((_silu(hg) * hu).astype(_BF16))
        hh = jnp.concatenate(parts, axis=1)
        ybuf[s] = jnp.dot(hh, wd_ref[...].astype(_BF16), preferred_element_type=_F32)
        if has_prev:
            wait_scatter(o)

        @pl.when(nv_next == 0)
        def _():
            wait_gather(o)
            scatter_rows(idx_ref, i, s, range(tm))
            wait_scatter(s)

    used = nv > 0
    odd = lax.rem(i, 2) == 1

    @pl.when(i == 0)
    def _():
        tile(0, False)

    @pl.when(used & (i > 0) & jnp.logical_not(odd))
    def _():
        tile(0, True)

    @pl.when(used & odd)
    def _():
        tile(1, True)


def _moe_plan(e_ids, tm, n_tiles):
    n = e_ids.shape[0]
    n_assign = 2 * n
    n_pad = n + tm // 2
    e_flat = e_ids.reshape(-1)
    order = jnp.argsort(e_flat, stable=True).astype(jnp.int32)
    experts = jnp.arange(N_EXPERTS, dtype=jnp.int32)
    counts = jnp.sum((e_flat[:, None] == experts[None, :]).astype(jnp.int32), axis=0)
    seg_start = jnp.cumsum(counts) - counts
    tiles_per = (counts + tm - 1) // tm
    cum_tiles = jnp.cumsum(tiles_per)
    tile_start = cum_tiles - tiles_per
    n_used = cum_tiles[-1]
    tile_id = jnp.arange(n_tiles, dtype=jnp.int32)
    used = tile_id < n_used
    te = jnp.sum((cum_tiles[None, :] <= jnp.minimum(tile_id, n_used - 1)[:, None]).astype(jnp.int32), axis=1)
    of_tile = lambda v: jnp.sum(jnp.where(te[:, None] == experts[None, :], v[None, :], 0), axis=1)
    j = tile_id - of_tile(tile_start)
    nvalid = jnp.where(used, jnp.clip(of_tile(counts) - j * tm, 0, tm), 0).astype(jnp.int32)
    first = jnp.minimum(of_tile(seg_start) + j * tm, n_assign - 1)
    n_win = pl.cdiv(n_assign, tm)
    fill = (n_win + 1) * tm - n_assign
    tok = jnp.pad(order // 2, (0, fill)).reshape(n_win + 1, tm)
    dst = jnp.pad((order % 2) * n_pad + order // 2, (0, fill)).reshape(n_win + 1, tm)
    windows = jnp.concatenate([tok[:-1], tok[1:], dst[:-1], dst[1:]], axis=1).reshape(n_win, 1, 4 * tm)
    return te, nvalid, (first // tm).astype(jnp.int32), (first % tm).astype(jnp.int32), windows, n_pad


def _expert_mlp(t_all, e_ids, w_gate, w_up, w_down):
    n = t_all.shape[0]
    tm = EXPERT_TILE
    n_tiles = pl.cdiv(2 * n, tm) + N_EXPERTS
    te, nvalid, blk, off, windows, n_pad = _moe_plan(e_ids, tm, n_tiles)
    wmap = lambda i, te_ref, nv_ref, blk_ref, off_ref: (te_ref[i], 0, 0)

    def window_spec(shift):
        def index_map(i, te_ref, nv_ref, blk_ref, off_ref):
            return (blk_ref[jnp.clip(i + shift, 0, n_tiles - 1)], 0, 0)
        return pl.BlockSpec((1, 1, 4 * tm), index_map, memory_space=pltpu.SMEM)

    grid_spec = pltpu.PrefetchScalarGridSpec(
        num_scalar_prefetch=4,
        grid=(n_tiles,),
        in_specs=[
            window_spec(-1), window_spec(0), window_spec(1),
            pl.BlockSpec(memory_space=pl.ANY),
            pl.BlockSpec((None, D_MODEL, D_FF_EXPERT), wmap),
            pl.BlockSpec((None, D_MODEL, D_FF_EXPERT), wmap),
            pl.BlockSpec((None, D_FF_EXPERT, D_MODEL), wmap),
        ],
        out_specs=pl.BlockSpec(memory_space=pl.ANY),
        scratch_shapes=[pltpu.VMEM((2, tm, D_MODEL), _F32), pltpu.VMEM((2, tm, D_MODEL), _F32),
                        pltpu.SemaphoreType.DMA((2,)), pltpu.SemaphoreType.DMA((2,))],
    )
    y_rows = pl.pallas_call(
        functools.partial(_expert_kernel, tm=tm, n_tiles=n_tiles),
        grid_spec=grid_spec,
        out_shape=jax.ShapeDtypeStruct((2 * n_pad, D_MODEL), _F32),
        compiler_params=_cparams(("arbitrary",)),
        name="expert_mlp",
    )(te, nvalid, blk, off, windows, windows, windows, t_all, w_gate, w_up, w_down)
    return y_rows.reshape(2, n_pad, D_MODEL)


def _combine_ple_kernel(h1_ref, y_ref, route_ref, p_ref, wn_ref, wg_ref, wp_ref, wf_ref, o_ref, wg_sc):
    _cast_weights_once(wg_ref, wg_sc)
    route = route_ref[...]
    h2 = h1_ref[...] + route[:, 2:3] * y_ref[0] + route[:, 3:4] * y_ref[1]
    u = _rms(h2, wn_ref[...])
    gate = jnp.dot(u.astype(_BF16), wg_sc[...], preferred_element_type=_F32)
    gate = 1.0 / (1.0 + jnp.exp(-gate))
    pp = jnp.dot(p_ref[...].astype(_BF16), wp_ref[...], preferred_element_type=_F32)
    h3 = h2 + gate * pp
    o_ref[...] = _rms(h3, wf_ref[...])


def _combine_ple(h1, y_stk, route, p2d, w_norm_ple, w_ple_gate, w_ple_proj, w_norm_final, tm, row_off):
    t = h1.shape[0]
    off = row_off // tm
    row = lambda i: (i, 0)
    fixed = lambda i: (0, 0)
    return pl.pallas_call(
        _combine_ple_kernel,
        grid=(t // tm,),
        in_specs=[pl.BlockSpec((tm, D_MODEL), row),
                  pl.BlockSpec((2, tm, D_MODEL), lambda i: (0, i + off, 0)),
                  pl.BlockSpec((tm, LANES), row), pl.BlockSpec((tm, D_PLE), row),
                  pl.BlockSpec((1, D_MODEL), fixed), _resident((D_MODEL, D_MODEL), fixed),
                  pl.BlockSpec((D_PLE, D_MODEL), fixed), pl.BlockSpec((1, D_MODEL), fixed)],
        out_specs=pl.BlockSpec((tm, D_MODEL), row),
        out_shape=jax.ShapeDtypeStruct((t, D_MODEL), _F32),
        scratch_shapes=[pltpu.VMEM((D_MODEL, D_MODEL), _BF16)],
        compiler_params=_cparams(("arbitrary",)),
        name="combine_ple",
    )(h1, y_stk, route, p2d, w_norm_ple.reshape(1, D_MODEL), w_ple_gate, w_ple_proj.astype(_BF16),
      w_norm_final.reshape(1, D_MODEL))


def _sample_inproj_kernel(x_ref, wn_ref, w_ref, o_ref):
    u = _rms(x_ref[...], wn_ref[...])
    u_hi, u_lo = _split2(u)
    w_hi, w_lo = _split2(w_ref[...])
    nt = lambda a, b: lax.dot_general(a, b, _NT, preferred_element_type=_F32)
    o_ref[...] = nt(w_hi, u_hi) + nt(w_hi, u_lo) + nt(w_lo, u_hi)


def _sample_inproj(xs2d, w_norm, w_in_t, tn):
    rows = xs2d.shape[0]
    return pl.pallas_call(
        _sample_inproj_kernel,
        grid=(pl.cdiv(D_IN_PROJ, tn),),
        in_specs=[pl.BlockSpec((rows, D_MODEL), lambda j: (0, 0)), pl.BlockSpec((1, D_MODEL), lambda j: (0, 0)),
                  pl.BlockSpec((tn, D_MODEL), lambda j: (j, 0))],
        out_specs=pl.BlockSpec((tn, rows), lambda j: (j, 0)),
        out_shape=jax.ShapeDtypeStruct((D_IN_PROJ, rows), _F32),
        compiler_params=_cparams(("arbitrary",)),
        name="sample_inproj",
    )(xs2d, w_norm.reshape(1, D_MODEL), w_in_t).T


def _split2(x):
    hi = x.astype(_BF16)
    lo = (x - hi.astype(_F32)).astype(_BF16)
    return hi, lo


def _decode_attn_kernel(pt_ref, qkv_ref, c_ref, s1_ref, s2_ref, lq1_ref, lk1_ref, lq2_ref, lk2_ref, wsub_ref, *rest,
                        n_pages_step, n_steps, n_real):
    k_refs = rest[:n_pages_step]
    v_refs = rest[n_pages_step:2 * n_pages_step]
    o_ref, knew_ref, vnew_ref = rest[2 * n_pages_step:2 * n_pages_step + 3]
    q_sc, m_sc, l_sc, acc_sc, kt_sc, vt_sc = rest[2 * n_pages_step + 3:]
    j = pl.program_id(1)
    R = SAMPLE_ROWS
    hr = 2 * R
    page = kt_sc.shape[0] // N_ATT_HEADS

    @pl.when(j == 0)
    def _():
        c, s1, s2 = c_ref[...], s1_ref[...], s2_ref[...]
        lane = lax.broadcasted_iota(jnp.int32, (R, LANES), 1)
        kt_sc[...] = jnp.zeros(kt_sc.shape, _F32)
        vt_sc[...] = jnp.zeros(vt_sc.shape, _F32)
        for h in range(N_ATT_HEADS):
            sl = slice(h * LANES, (h + 1) * LANES)
            q = _rope_block(qkv_ref[:, sl], c, s1, s2) * (ATT_HEAD_DIM ** -0.5)
            k = _rope_block(qkv_ref[:, D_ATTN + h * LANES:D_ATTN + (h + 1) * LANES], c, s1, s2)
            v = qkv_ref[:, 2 * D_ATTN + h * LANES:2 * D_ATTN + (h + 1) * LANES]
            knew_ref[:, sl] = k
            vnew_ref[:, sl] = v
            kt_sc[pl.ds(h, R, stride=N_ATT_HEADS), :] = k
            vt_sc[pl.ds(h, R, stride=N_ATT_HEADS), :] = v
            q2 = jnp.concatenate([jnp.where(lane < ATT_HEAD_DIM, q, 0.0), jnp.where(lane >= ATT_HEAD_DIM, q, 0.0)], axis=0)
            hi, lo = _split2(q2)
            q_sc[h] = jnp.concatenate([hi, lo], axis=0)
        m_sc[...] = jnp.full(m_sc.shape, -jnp.inf, _F32)
        l_sc[...] = jnp.zeros(l_sc.shape, _F32)
        acc_sc[...] = jnp.zeros(acc_sc.shape, _F32)

    def head_rows(tiles, h):
        x = jnp.concatenate([t[pl.ds(h, page, stride=N_ATT_HEADS), :] for t in tiles], axis=0)
        return _split2(x)

    def process(k_tiles, v_tiles, mask):
        s_parts = []
        for h in range(N_ATT_HEADS):
            k_hi, k_lo = head_rows(k_tiles, h)
            q3 = q_sc[h]
            a = lax.dot_general(q3[0:2 * hr], k_hi, _NT, preferred_element_type=_F32)
            b = lax.dot_general(q3[0:hr], k_lo, _NT, preferred_element_type=_F32)
            s_parts.append(a[0:hr] + a[hr:2 * hr] + b)
        s = jnp.concatenate(s_parts, axis=0)
        if mask is not None:
            s = jnp.where(mask, s, -jnp.inf)
        m_prev = m_sc[...]
        m_new = jnp.maximum(m_prev, jnp.max(s, axis=1, keepdims=True))
        alpha = jnp.exp(m_prev - m_new)
        p = jnp.exp(s - m_new[:, :1])
        l_sc[...] = alpha * l_sc[...] + jnp.sum(p, axis=1, keepdims=True)
        m_sc[...] = m_new
        pv_parts = []
        for h in range(N_ATT_HEADS):
            v_hi, v_lo = head_rows(v_tiles, h)
            p_hi, p_lo = _split2(p[h * hr:(h + 1) * hr])
            a = jnp.dot(jnp.concatenate([p_hi, p_lo], axis=0), v_hi, preferred_element_type=_F32)
            b = jnp.dot(p_hi, v_lo, preferred_element_type=_F32)
            pv_parts.append(a[0:hr] + a[hr:2 * hr] + b)
        acc_sc[...] = alpha * acc_sc[...] + jnp.concatenate(pv_parts, axis=0)

    process(k_refs, v_refs, None)

    @pl.when(j == n_steps - 1)
    def _():
        rows = N_ATT_HEADS * hr
        row = lax.broadcasted_iota(jnp.int32, (rows, page), 0)
        col = lax.broadcasted_iota(jnp.int32, (rows, page), 1)
        qrow = row % R
        process([kt_sc], [vt_sc], (col <= qrow) & (col < n_real))
        o = acc_sc[...] / l_sc[...]
        lam = _lambda(lq1_ref, lk1_ref, lq2_ref, lk2_ref)
        wsub = wsub_ref[...]
        for h in range(N_ATT_HEADS):
            o1 = o[h * hr:h * hr + R]
            o2 = o[h * hr + R:(h + 1) * hr]
            o_ref[:, h * LANES:(h + 1) * LANES] = _diff_finalize(o1, o2, lam, wsub)


def _decode_attention(proj3, cache_k2, cache_v2, page_table, first_pos, lam_params, w_subln, n_real):
    bsz = proj3.shape[0]
    page = cache_k2.shape[2] // N_ATT_HEADS
    n_pages = page_table.shape[1]
    pg = PAGES_PER_STEP
    n_steps = n_pages // pg
    R = SAMPLE_ROWS
    c, s1, s2 = _rope_tables(first_pos, R)
    lq1, lk1, lq2, lk2 = [p.reshape(1, ATT_HEAD_DIM) for p in lam_params]
    fixed = lambda b, j, pt: (0, 0)
    tab = pl.BlockSpec((R, LANES), fixed)
    vec = pl.BlockSpec((1, ATT_HEAD_DIM), fixed)

    def page_spec(i):
        return pl.BlockSpec((None, None, page * N_ATT_HEADS, LANES), lambda b, j, pt: (0, pt[b, j * pg + i], 0, 0))

    out_row = pl.BlockSpec((None, R, D_ATTN), lambda b, j, pt: (b, 0, 0))
    grid_spec = pltpu.PrefetchScalarGridSpec(
        num_scalar_prefetch=1,
        grid=(bsz, n_steps),
        in_specs=[pl.BlockSpec((None, R, 3 * D_ATTN), lambda b, j, pt: (b, 0, 0)), tab, tab, tab, vec, vec, vec, vec,
                  pl.BlockSpec((1, ATT_V_DIM), fixed)]
                 + [page_spec(i) for i in range(pg)] + [page_spec(i) for i in range(pg)],
        out_specs=[out_row, out_row, out_row],
        scratch_shapes=[pltpu.VMEM((N_ATT_HEADS, 4 * R, LANES), _BF16),
                        pltpu.VMEM((N_ATT_HEADS * 2 * R, LANES), _F32), pltpu.VMEM((N_ATT_HEADS * 2 * R, LANES), _F32),
                        pltpu.VMEM((N_ATT_HEADS * 2 * R, LANES), _F32),
                        pltpu.VMEM((page * N_ATT_HEADS, LANES), _F32), pltpu.VMEM((page * N_ATT_HEADS, LANES), _F32)],
    )
    return pl.pallas_call(
        functools.partial(_decode_attn_kernel, n_pages_step=pg, n_steps=n_steps, n_real=n_real),
        grid_spec=grid_spec,
        out_shape=[jax.ShapeDtypeStruct((bsz, R, D_ATTN), _F32)] * 3,
        compiler_params=_cparams(("arbitrary", "arbitrary")),
        name="decode_attn",
    )(page_table, proj3, c, s1, s2, lq1, lk1, lq2, lk2, w_subln.reshape(1, ATT_V_DIM),
      *([cache_k2] * pg), *([cache_v2] * pg))


def kernel(x_prompt, x_sample, cache_k, cache_v, state_conv, state_ssm, page_table, p_prompt, p_sample, w_norm_mix, w_in, lambda_q1, lambda_k1, lambda_q2, lambda_k2, w_subln, conv_w, conv_b, dt_bias, A_log, D_skip, w_ssm_norm, w_out, w_norm_ffn, w_group_router, b_group_router, w_expert_router, b_expert_router, w_exp_gate, w_exp_up, w_exp_down, w_norm_ple, w_ple_gate, w_ple_proj, w_norm_final):
    bp, lp, _ = x_prompt.shape
    bs, ls, _ = x_sample.shape
    past = page_table.shape[1] * cache_k.shape[2]
    tp = bp * lp
    R = SAMPLE_ROWS
    ts = bs * R
    lam_params = (lambda_q1[0], lambda_k1[0], lambda_q2[0], lambda_k2[0])
    pad_r = LANES - N_EXPERT_GROUPS - N_EXPERTS
    w_router = jnp.pad(jnp.concatenate([w_group_router[0], w_expert_router[0]], axis=1), ((0, 0), (0, pad_r)))
    b_router = jnp.pad(jnp.concatenate([b_group_router[0], b_expert_router[0]]), (0, pad_r)).reshape(1, LANES)

    xp2d = x_prompt.reshape(tp, D_MODEL)
    w_in_t = w_in[0].T
    q, kf, kb, vf, vb, z, xbc, dt = _prompt_inproj(xp2d, w_norm_mix[0], w_in_t, lp, 512)
    attn_p = _prompt_attention(q, kb, vb, lam_params, w_subln[0], bp, lp, 512)
    ssd_p, ssm_p = _ssd(xbc, z, dt, jnp.zeros((bp, CONV_WIDTH - 1, CONV_DIM), _F32),
                        jnp.zeros((bp, N_SSM_HEADS, SSM_HEAD_DIM, SSM_STATE), _F32),
                        conv_w[0], conv_b[0], dt_bias[0], A_log[0], D_skip[0], w_ssm_norm[0],
                        bp, lp // SSM_CHUNK, SSM_CHUNK, False, _BF16)
    h1_p, t_all, route_p = _outproj_router(xp2d, attn_p, ssd_p, w_out[0], w_norm_ffn[0], w_router, b_router, 256, False,
                                           tp + ts)

    xs3 = jnp.pad(x_sample, ((0, 0), (0, R - ls), (0, 0)))
    xs2d = xs3.reshape(ts, D_MODEL)
    proj_s = _sample_inproj(xs2d, w_norm_mix[0], w_in_t, 512)
    proj3 = proj_s.reshape(bs, R, D_IN_PROJ)
    cache_rows = cache_k.shape[:2] + (cache_k.shape[2] * N_ATT_HEADS, LANES)
    attn_s, k_s, v_s = _decode_attention(proj3, cache_k.reshape(cache_rows), cache_v.reshape(cache_rows), page_table,
                                         past, lam_params, w_subln[0], ls)
    off = 3 * D_ATTN
    pad_rows = ((0, 0), (0, SSM_CHUNK - R), (0, 0))
    z_s = jnp.pad(proj3[:, :, off:off + D_SSM], pad_rows).reshape(bs * SSM_CHUNK, D_SSM)
    xbc_s3 = proj3[:, :, off + D_SSM:off + D_SSM + CONV_DIM]
    xbc_s = jnp.pad(xbc_s3, pad_rows).reshape(bs * SSM_CHUNK, CONV_DIM)
    dt_s = jnp.pad(proj3[:, :, off + D_SSM + CONV_DIM:], ((0, 0), (0, SSM_CHUNK - R), (0, LANES - N_SSM_HEADS)))
    dt_s = dt_s.reshape(bs * SSM_CHUNK, LANES)
    ssd_s, ssm_s = _ssd(xbc_s, z_s, dt_s, state_conv[0], state_ssm[0], conv_w[0], conv_b[0], dt_bias[0], A_log[0],
                        D_skip[0], w_ssm_norm[0], bs, 1, ls, True, _F32)
    ssd_s = ssd_s.reshape(bs, SSM_CHUNK, D_SSM)[:, :R].reshape(ts, D_SSM)
    h1_s, t_all, route_s = _outproj_router(xs2d, attn_s.reshape(ts, D_ATTN), ssd_s, w_out[0], w_norm_ffn[0],
                                           w_router, b_router, ts, True, tp + ts, t_shared=t_all)

    e_ids = jnp.concatenate([route_p[:, :2], route_s[:, :2]], axis=0).astype(jnp.int32)
    y_stk = _expert_mlp(t_all, e_ids, w_exp_gate[0], w_exp_up[0], w_exp_down[0])
    y_p = _combine_ple(h1_p, y_stk, route_p, p_prompt[0].reshape(tp, D_PLE), w_norm_ple[0], w_ple_gate[0],
                       w_ple_proj[0], w_norm_final, 256, 0)
    ps2d = jnp.pad(p_sample[0], ((0, 0), (0, R - ls), (0, 0))).reshape(ts, D_PLE)
    y_s = _combine_ple(h1_s, y_stk, route_s, ps2d, w_norm_ple[0], w_ple_gate[0], w_ple_proj[0], w_norm_final, ts, tp)

    y_prompt = y_p.reshape(bp, lp, D_MODEL)
    y_sample = y_s.reshape(bs, R, D_MODEL)[:, :ls]
    k_prompt = kf.reshape(1, bp, lp, N_ATT_HEADS, 2 * ATT_HEAD_DIM)
    v_prompt = vf.reshape(1, bp, lp, N_ATT_HEADS, ATT_V_DIM)
    conv_prompt = xbc.reshape(bp, lp, CONV_DIM)[:, lp - (CONV_WIDTH - 1):][None]
    ssm_prompt = ssm_p[None]
    k_sample = k_s[:, :ls].reshape(1, bs, ls, N_ATT_HEADS, 2 * ATT_HEAD_DIM)
    v_sample = v_s[:, :ls].reshape(1, bs, ls, N_ATT_HEADS, ATT_V_DIM)
    conv_sample = xbc_s3[:, ls - (CONV_WIDTH - 1):ls][None]
    ssm_sample = ssm_s[None]
    return (y_prompt, y_sample, k_prompt, v_prompt, conv_prompt, ssm_prompt, k_sample, v_sample, conv_sample, ssm_sample)
```

```python
import functools
import math

import jax
import jax.numpy as jnp
from jax import lax
from jax.experimental import pallas as pl
from jax.experimental.pallas import tpu as pltpu

D_MODEL = 2048
D_ATTN = 1024
D_SSM = 1024
ATT_HEAD_DIM = 64
N_ATT_HEADS = 8
ATT_V_DIM = 128
ROT_DIM = 16
ROPE_THETA = 500000.0
SSM_HEAD_DIM = 64
N_SSM_HEADS = 16
SSM_GROUPS = 2
SSM_STATE = 128
SSM_CHUNK = 128
CONV_WIDTH = 4
CONV_DIM = D_SSM + 2 * SSM_GROUPS * SSM_STATE
D_IN_PROJ = 3 * D_ATTN + D_SSM + CONV_DIM + N_SSM_HEADS
N_EXPERT_GROUPS = 4
EXPERTS_PER_GROUP = 8
N_EXPERTS = 32
D_FF_EXPERT = 512
D_PLE = 256
RMS_EPS = 1e-6
LAM_INIT = 0.8 - 0.6 * math.exp(-0.3 * 0)

LANES = 128
SUBLANES = 8
VMEM_LIMIT_BYTES = 56 * 2 ** 20

SAMPLE_ROWS = 8
EXPERT_TILE = 256
PAGES_PER_STEP = 8

_F32 = jnp.float32
_BF16 = jnp.bfloat16
_HI = lax.Precision.HIGHEST
_NT = (((1,), (1,)), ((), ()))


def _cparams(semantics):
    return pltpu.CompilerParams(dimension_semantics=semantics, vmem_limit_bytes=VMEM_LIMIT_BYTES)


def _rms(x, w):
    return x * lax.rsqrt(jnp.mean(x * x, axis=-1, keepdims=True) + RMS_EPS) * w


def _silu(x):
    return x * (1.0 / (1.0 + jnp.exp(-x)))


def _softplus(x):
    return jnp.maximum(x, 0.0) + jnp.log(1.0 + jnp.exp(-jnp.abs(x)))


def _dot(a, b, precise=False):
    if precise:
        return jnp.dot(a.astype(_F32), b.astype(_F32), preferred_element_type=_F32, precision=_HI)
    return jnp.dot(a.astype(_BF16), b.astype(_BF16), preferred_element_type=_F32)


def _dot_nt(a, b, precise=False):
    if precise:
        return lax.dot_general(a.astype(_F32), b.astype(_F32), _NT, preferred_element_type=_F32, precision=_HI)
    return lax.dot_general(a.astype(_BF16), b.astype(_BF16), _NT, preferred_element_type=_F32)


def _rope_block(y, c, s1, s2):
    return y * c + pltpu.roll(y, LANES - ROT_DIM // 2, 1) * s1 + pltpu.roll(y, ROT_DIM // 2, 1) * s2


def _rope_tables(first_pos, n):
    half = ROT_DIM // 2
    pos = first_pos + jnp.arange(n, dtype=jnp.int32)
    inv_freq = jnp.power(ROPE_THETA, -jnp.arange(half, dtype=_F32) * (2.0 / ROT_DIM))
    dim = jnp.arange(LANES, dtype=jnp.int32) % ATT_HEAD_DIM
    ang = pos.astype(_F32)[:, None] * inv_freq[dim % half][None, :]
    cos, sin = jnp.cos(ang), jnp.sin(ang)
    first, second = (dim < half)[None, :], ((dim >= half) & (dim < ROT_DIM))[None, :]
    c = jnp.where(first | second, cos, 1.0)
    s1 = jnp.where(first, -sin, 0.0)
    s2 = jnp.where(second, sin, 0.0)
    return c, s1, s2


def _cast_weights_once(w_ref, wb_sc, transposed=False, valid_out=None):
    @pl.when(pl.program_id(0) == 0)
    def _():
        if transposed:
            w = w_ref[0:wb_sc.shape[1], :]
            if valid_out is not None:
                w = jnp.where(lax.broadcasted_iota(jnp.int32, w.shape, 0) < valid_out, w, 0.0)
            w = w.T
        else:
            w = w_ref[...]
        wb_sc[...] = w.astype(wb_sc.dtype)


def _resident(shape, index_map):
    return pl.BlockSpec(shape, index_map, pipeline_mode=pl.Buffered(1))


def _inproj_qk_kernel(x_ref, wn_ref, w_ref, c_ref, s1_ref, s2_ref, q_ref, kf_ref, kb_ref, wb_sc):
    _cast_weights_once(w_ref, wb_sc, transposed=True)
    u = _rms(x_ref[...], wn_ref[...]).astype(_BF16)
    y = jnp.dot(u, wb_sc[...], preferred_element_type=_F32)
    c, s1, s2 = c_ref[...], s1_ref[...], s2_ref[...]
    q_scale = (ATT_HEAD_DIM ** -0.5) * math.log2(math.e)
    for j in range(2 * N_ATT_HEADS):
        r = _rope_block(y[:, j * LANES:(j + 1) * LANES], c, s1, s2)
        if j < N_ATT_HEADS:
            q_ref[:, j * LANES:(j + 1) * LANES] = (r * q_scale).astype(_BF16)
        else:
            jj = j - N_ATT_HEADS
            kf_ref[pl.ds(jj, r.shape[0], stride=N_ATT_HEADS), :] = r
            kb_ref[:, jj * LANES:(jj + 1) * LANES] = r.astype(_BF16)


def _inproj_vz_kernel(x_ref, wn_ref, w_ref, vf_ref, vb_ref, z_ref, wb_sc):
    _cast_weights_once(w_ref, wb_sc, transposed=True)
    u = _rms(x_ref[...], wn_ref[...]).astype(_BF16)
    y = jnp.dot(u, wb_sc[...], preferred_element_type=_F32)
    v = y[:, :D_ATTN]
    for h in range(N_ATT_HEADS):
        vf_ref[pl.ds(h, v.shape[0], stride=N_ATT_HEADS), :] = v[:, h * LANES:(h + 1) * LANES]
    vb_ref[...] = v.astype(_BF16)
    z_ref[...] = y[:, D_ATTN:].astype(_BF16)


def _inproj_xbc_kernel(x_ref, wn_ref, w_ref, xbc_ref, dt_ref, wb_sc):
    _cast_weights_once(w_ref, wb_sc, transposed=True, valid_out=CONV_DIM + N_SSM_HEADS)
    u = _rms(x_ref[...], wn_ref[...]).astype(_BF16)
    y = jnp.dot(u, wb_sc[...], preferred_element_type=_F32)
    xbc_ref[...] = y[:, :CONV_DIM]
    dt_ref[...] = y[:, CONV_DIM:]


def _prompt_inproj(x2d, w_norm, w_in_t, seq, tm):
    t = x2d.shape[0]
    nblk = t // tm
    per_seq = seq // tm
    wn = w_norm.reshape(1, D_MODEL)
    row = lambda i: (i, 0)
    fixed = lambda i: (0, 0)
    x_spec = pl.BlockSpec((tm, D_MODEL), row)
    wn_spec = pl.BlockSpec((1, D_MODEL), fixed)
    head_spec = pl.BlockSpec((tm * N_ATT_HEADS, LANES), row)
    wide = 2 * D_ATTN

    c, s1, s2 = _rope_tables(0, seq)
    tab_spec = pl.BlockSpec((tm, LANES), lambda i: (i % per_seq, 0))
    q, kf, kb = pl.pallas_call(
        _inproj_qk_kernel,
        grid=(nblk,),
        in_specs=[x_spec, wn_spec, _resident((wide, D_MODEL), fixed), tab_spec, tab_spec, tab_spec],
        out_specs=[pl.BlockSpec((tm, D_ATTN), row), head_spec, pl.BlockSpec((tm, D_ATTN), row)],
        out_shape=[jax.ShapeDtypeStruct((t, D_ATTN), _BF16), jax.ShapeDtypeStruct((t * N_ATT_HEADS, LANES), _F32),
                   jax.ShapeDtypeStruct((t, D_ATTN), _BF16)],
        scratch_shapes=[pltpu.VMEM((D_MODEL, wide), _BF16)],
        compiler_params=_cparams(("arbitrary",)),
        name="inproj_qk",
    )(x2d, wn, w_in_t, c, s1, s2)

    vf, vb, z = pl.pallas_call(
        _inproj_vz_kernel,
        grid=(nblk,),
        in_specs=[x_spec, wn_spec, _resident((wide, D_MODEL), lambda i: (1, 0))],
        out_specs=[head_spec, pl.BlockSpec((tm, D_ATTN), row), pl.BlockSpec((tm, D_SSM), row)],
        out_shape=[jax.ShapeDtypeStruct((t * N_ATT_HEADS, LANES), _F32), jax.ShapeDtypeStruct((t, D_ATTN), _BF16),
                   jax.ShapeDtypeStruct((t, D_SSM), _BF16)],
        scratch_shapes=[pltpu.VMEM((D_MODEL, wide), _BF16)],
        compiler_params=_cparams(("arbitrary",)),
        name="inproj_vz",
    )(x2d, wn, w_in_t)

    n_tail = CONV_DIM + LANES
    xbc, dt = pl.pallas_call(
        _inproj_xbc_kernel,
        grid=(nblk,),
        in_specs=[x_spec, wn_spec, _resident((wide, D_MODEL), lambda i: (2, 0))],
        out_specs=[pl.BlockSpec((tm, CONV_DIM), row), pl.BlockSpec((tm, LANES), row)],
        out_shape=[jax.ShapeDtypeStruct((t, CONV_DIM), _F32), jax.ShapeDtypeStruct((t, LANES), _F32)],
        scratch_shapes=[pltpu.VMEM((D_MODEL, n_tail), _BF16)],
        compiler_params=_cparams(("arbitrary",)),
        name="inproj_xbc",
    )(x2d, wn, w_in_t)
    return q, kf, kb, vf, vb, z, xbc, dt


def _lambda(lq1_ref, lk1_ref, lq2_ref, lk2_ref):
    a = jnp.sum(lq1_ref[...] * lk1_ref[...], axis=-1, keepdims=True)
    b = jnp.sum(lq2_ref[...] * lk2_ref[...], axis=-1, keepdims=True)
    return jnp.exp(a) - jnp.exp(b) + LAM_INIT


def _diff_finalize(o1, o2, lam, wsub):
    a = o1 - lam * o2
    return _rms(a, wsub) * (1.0 - LAM_INIT)


def _attn_kernel(q_ref, k_ref, v_ref, lq1_ref, lk1_ref, lq2_ref, lk2_ref, wsub_ref, o_ref,
                 qt_sc, vt_sc, bias_sc, sa_sc, sb_sc, m_sc, acc_sc, *, tile):
    n_kt = vt_sc.shape[0]
    dv = ATT_V_DIM
    masked_out = -1e30

    for c in range(n_kt):
        vt_sc[c, 0:dv, :] = v_ref[c * tile:(c + 1) * tile, :].astype(_F32).T.astype(_BF16)
        vt_sc[c, dv:, :] = jnp.ones((vt_sc.shape[1] - dv, tile), _BF16)
    key = lax.broadcasted_iota(jnp.int32, (tile, tile), 0)
    qry = lax.broadcasted_iota(jnp.int32, (tile, tile), 1)
    bias_sc[...] = jnp.where(key <= qry, 0.0, masked_out)

    def start_query_tile(qi):
        q = q_ref[pl.ds(pl.multiple_of(qi * tile, tile), tile), :].astype(_F32)
        lane = lax.broadcasted_iota(jnp.int32, (tile, LANES), 1)
        qt_sc[0] = jnp.where(lane < ATT_HEAD_DIM, q, 0.0).T.astype(_BF16)
        qt_sc[1] = jnp.where(lane >= ATT_HEAD_DIM, q, 0.0).T.astype(_BF16)
        m_sc[...] = jnp.full(m_sc.shape, -jnp.inf, _F32)
        acc_sc[...] = jnp.zeros(acc_sc.shape, _F32)
        scores(0, sa_sc)

    def scores(j, dst):
        k = k_ref[pl.ds(pl.multiple_of(j * tile, tile), tile), :]
        for mm in range(2):
            dst[mm] = jnp.dot(k, qt_sc[mm], preferred_element_type=_F32)

    def softmax_pv(j, src, diagonal=False):
        vt = vt_sc[j]
        for mm in range(2):
            st = src[mm]
            if diagonal:
                st = st + bias_sc[...]
            m_prev = m_sc[mm]
            m_new = jnp.maximum(m_prev, jnp.max(st, axis=0, keepdims=True))
            alpha = jnp.exp2(m_prev - m_new)
            pt = jnp.exp2(st - m_new[0:1, :]).astype(_BF16)
            acc_sc[mm] = alpha[0:1, :] * acc_sc[mm] + jnp.dot(vt, pt, preferred_element_type=_F32)
            m_sc[mm] = m_new

    def pair(jj, carry):
        j0 = 2 * jj
        scores(j0 + 1, sb_sc)
        softmax_pv(j0, sa_sc)
        scores(j0 + 2, sa_sc)
        softmax_pv(j0 + 1, sb_sc)
        return carry

    n_q = q_ref.shape[0] // tile
    lam = _lambda(lq1_ref, lk1_ref, lq2_ref, lk2_ref)

    def query_tile(qi, carry):
        lax.fori_loop(0, qi // 2, pair, 0)
        odd = lax.rem(qi, 2) == 1

        def finish_and_start_next():
            outs = []
            for mm in range(2):
                acc = acc_sc[mm]
                outs.append((acc[0:dv, :] / acc[dv:dv + 1, :]).T)
            rows = pl.ds(pl.multiple_of(qi * tile, tile), tile)
            o_ref[rows, :] = _diff_finalize(outs[0], outs[1], lam, wsub_ref[...]).astype(o_ref.dtype)
            start_query_tile(jnp.minimum(qi + 1, n_q - 1))

        @pl.when(odd)
        def _():
            scores(qi, sb_sc)
            softmax_pv(qi - 1, sa_sc)
            softmax_pv(qi, sb_sc, diagonal=True)
            finish_and_start_next()

        @pl.when(jnp.logical_not(odd))
        def _():
            softmax_pv(qi, sa_sc, diagonal=True)
            finish_and_start_next()

        return carry

    start_query_tile(0)
    lax.fori_loop(0, n_q, query_tile, 0)


def _prompt_attention(q, kb, vb, lam_params, w_subln, batch, seq, tile):
    t = q.shape[0]
    lq1, lk1, lq2, lk2 = [p.reshape(1, ATT_HEAD_DIM) for p in lam_params]
    vec = pl.BlockSpec((1, ATT_HEAD_DIM), lambda b, h: (0, 0))
    head_seq = pl.BlockSpec((seq, LANES), lambda b, h: (b, h))
    return pl.pallas_call(
        functools.partial(_attn_kernel, tile=tile),
        grid=(batch, N_ATT_HEADS),
        in_specs=[head_seq, head_seq, head_seq, vec, vec, vec, vec,
                  pl.BlockSpec((1, ATT_V_DIM), lambda b, h: (0, 0))],
        out_specs=head_seq,
        out_shape=jax.ShapeDtypeStruct((t, D_ATTN), _BF16),
        scratch_shapes=[pltpu.VMEM((2, LANES, tile), _BF16),
                        pltpu.VMEM((seq // tile, ATT_V_DIM + 2 * SUBLANES, tile), _BF16),
                        pltpu.VMEM((tile, tile), _F32),
                        pltpu.VMEM((2, tile, tile), _F32), pltpu.VMEM((2, tile, tile), _F32),
                        pltpu.VMEM((2, SUBLANES, tile), _F32),
                        pltpu.VMEM((2, ATT_V_DIM + 2 * SUBLANES, tile), _F32)],
        compiler_params=_cparams(("arbitrary", "arbitrary")),
        name="prompt_attn",
    )(q, kb, vb, lq1, lk1, lq2, lk2, w_subln.reshape(1, ATT_V_DIM))


def _ssd_kernel(xbc_ref, z_ref, dt_ref, dtt_ref, cprev_ref, h0_ref, convw_ref, convb_ref, dtb_row_ref, dtb_col_ref,
                alog_row_ref, alog_col_ref, dskip_ref, wnorm_ref, y_ref, hout_ref,
                xp_sc, st_sc, yg_sc, *, n_valid, precise):
    c = pl.program_id(1)
    L = SSM_CHUNK
    n_pairs = N_SSM_HEADS // 2
    hp = 2 * SSM_HEAD_DIM

    @pl.when(c == 0)
    def _():
        xp_sc[0:SUBLANES, :] = jnp.zeros((SUBLANES, CONV_DIM), _F32)
        xp_sc[SUBLANES - (CONV_WIDTH - 1):SUBLANES, :] = cprev_ref[...]
        st_sc[...] = h0_ref[...]

    xp_sc[SUBLANES:SUBLANES + L, :] = xbc_ref[...]
    base = SUBLANES - (CONV_WIDTH - 1)
    conv = convb_ref[...] + convw_ref[0:1, :] * xp_sc[base:base + L, :]
    for j in range(1, CONV_WIDTH):
        conv = conv + convw_ref[j:j + 1, :] * xp_sc[base + j:base + j + L, :]
    xp_sc[0:SUBLANES, :] = xp_sc[L:L + SUBLANES, :]
    xc = _silu(conv)
    xs = xc[:, :D_SSM]

    row_i = lax.broadcasted_iota(jnp.int32, (L, LANES), 0)
    col_i = lax.broadcasted_iota(jnp.int32, (L, LANES), 1)
    dt = _softplus(dt_ref[...] + dtb_row_ref[...])
    dtt = _softplus(dtt_ref[...] + dtb_col_ref[...])
    if n_valid < L:
        dt = jnp.where(row_i < n_valid, dt, 0.0)
        dtt = jnp.where(lax.broadcasted_iota(jnp.int32, dtt.shape, 1) < n_valid, dtt, 0.0)
    ad = dt * (-jnp.exp(alog_row_ref[...]))
    adt = dtt * (-jnp.exp(alog_col_ref[...]))
    tril = (col_i <= row_i).astype(_F32)
    triu = (row_i <= col_i).astype(_F32)
    acs = jnp.dot(tril, ad, preferred_element_type=_F32, precision=_HI)
    acst = jnp.dot(adt, triu, preferred_element_type=_F32, precision=_HI)
    causal = col_i <= row_i
    lane_lo = col_i < SSM_HEAD_DIM

    cb = []
    for g in range(SSM_GROUPS):
        bg = xc[:, D_SSM + g * SSM_STATE:D_SSM + (g + 1) * SSM_STATE]
        cg = xc[:, D_SSM + (SSM_GROUPS + g) * SSM_STATE:D_SSM + (SSM_GROUPS + g + 1) * SSM_STATE]
        cb.append((bg, cg, _dot_nt(cg, bg, precise)))

    for p in range(n_pairs):
        ha, hb = 2 * p, 2 * p + 1
        bg, cg, cbg = cb[ha // (N_SSM_HEADS // SSM_GROUPS)]
        col_a, col_b = acs[:, ha:ha + 1], acs[:, hb:hb + 1]
        lm_a = jnp.where(causal, jnp.exp(col_a - acst[ha:ha + 1, :]), 0.0)
        lm_b = jnp.where(causal, jnp.exp(col_b - acst[hb:hb + 1, :]), 0.0)
        dt_pair = jnp.where(lane_lo, dt[:, ha:ha + 1], dt[:, hb:hb + 1])
        x_pair = xs[:, p * hp:(p + 1) * hp] * dt_pair
        y_diag = jnp.where(lane_lo, _dot(cbg * lm_a, x_pair, precise), _dot(cbg * lm_b, x_pair, precise))
        st = st_sc[p]
        y_off = _dot_nt(cg, st, precise) * jnp.where(lane_lo, jnp.exp(col_a), jnp.exp(col_b))
        last_a, last_b = acs[L - 1:L, ha:ha + 1], acs[L - 1:L, hb:hb + 1]
        decay = jnp.where(lane_lo, jnp.exp(last_a - col_a), jnp.exp(last_b - col_b))
        upd = _dot((x_pair * decay).T, bg, precise)
        row_lo = row_i < SSM_HEAD_DIM
        st_sc[p] = jnp.where(row_lo, jnp.exp(last_a), jnp.exp(last_b)) * st + upd
        y_pair = y_diag + y_off + xs[:, p * hp:(p + 1) * hp] * dskip_ref[:, p * hp:(p + 1) * hp]
        zg = z_ref[:, p * hp:(p + 1) * hp].astype(_F32)
        yg_sc[:, p * hp:(p + 1) * hp] = y_pair * _silu(zg)

    gw = D_SSM // SSM_GROUPS
    for g in range(SSM_GROUPS):
        y_ref[:, g * gw:(g + 1) * gw] = _rms(yg_sc[:, g * gw:(g + 1) * gw],
                                             wnorm_ref[:, g * gw:(g + 1) * gw]).astype(y_ref.dtype)

    hout_ref[...] = st_sc[...]


def _ssd(xbc, z, dt, conv_prev, h0, conv_w, conv_b, dt_bias, a_log, d_skip, w_ssm_norm, batch, n_chunks,
         n_valid, precise, out_dtype):
    L = SSM_CHUNK
    rows = xbc.shape[0]
    n_pairs = N_SSM_HEADS // 2
    dtt = dt[:, :N_SSM_HEADS].T
    pad = LANES - N_SSM_HEADS
    dtb_row = jnp.pad(dt_bias, (0, pad)).reshape(1, LANES)
    alog_row = jnp.pad(a_log, (0, pad)).reshape(1, LANES)
    dskip = jnp.repeat(d_skip, SSM_HEAD_DIM).reshape(1, D_SSM)
    blk = lambda b, c: (b * n_chunks + c, 0)
    fixed = lambda b, c: (0, 0)
    y, h_out = pl.pallas_call(
        functools.partial(_ssd_kernel, n_valid=n_valid, precise=precise),
        grid=(batch, n_chunks),
        in_specs=[
            pl.BlockSpec((L, CONV_DIM), blk),
            pl.BlockSpec((L, D_SSM), blk),
            pl.BlockSpec((L, LANES), blk),
            pl.BlockSpec((N_SSM_HEADS, L), lambda b, c: (0, b * n_chunks + c)),
            pl.BlockSpec((None, CONV_WIDTH - 1, CONV_DIM), lambda b, c: (b, 0, 0)),
            pl.BlockSpec((None, n_pairs, 2 * SSM_HEAD_DIM, SSM_STATE), lambda b, c: (b, 0, 0, 0)),
            pl.BlockSpec((CONV_WIDTH, CONV_DIM), fixed),
            pl.BlockSpec((1, CONV_DIM), fixed),
            pl.BlockSpec((1, LANES), fixed),
            pl.BlockSpec((N_SSM_HEADS, 1), fixed),
            pl.BlockSpec((1, LANES), fixed),
            pl.BlockSpec((N_SSM_HEADS, 1), fixed),
            pl.BlockSpec((1, D_SSM), fixed),
            pl.BlockSpec((1, D_SSM), fixed),
        ],
        out_specs=[pl.BlockSpec((L, D_SSM), blk),
                   pl.BlockSpec((None, n_pairs, 2 * SSM_HEAD_DIM, SSM_STATE), lambda b, c: (b, 0, 0, 0))],
        out_shape=[jax.ShapeDtypeStruct((rows, D_SSM), out_dtype),
                   jax.ShapeDtypeStruct((batch, n_pairs, 2 * SSM_HEAD_DIM, SSM_STATE), _F32)],
        scratch_shapes=[pltpu.VMEM((SUBLANES + L, CONV_DIM), _F32),
                        pltpu.VMEM((n_pairs, 2 * SSM_HEAD_DIM, SSM_STATE), _F32),
                        pltpu.VMEM((L, D_SSM), _F32)],
        compiler_params=_cparams(("arbitrary", "arbitrary")),
        name="ssd_precise" if precise else "ssd",
    )(xbc, z, dt, dtt, conv_prev, h0.reshape(batch, n_pairs, 2 * SSM_HEAD_DIM, SSM_STATE),
      conv_w, conv_b.reshape(1, CONV_DIM), dtb_row, dt_bias.reshape(N_SSM_HEADS, 1),
      alog_row, a_log.reshape(N_SSM_HEADS, 1), dskip, w_ssm_norm.reshape(1, D_SSM))
    return y, h_out.reshape(batch, N_SSM_HEADS, SSM_HEAD_DIM, SSM_STATE)


def _outproj_router_kernel(x_ref, attn_ref, ssd_ref, wa_ref, wb_ref, wn_ref, wr_ref, br_ref, *rest, precise, n_blocks):
    if precise:
        h1_ref, t_ref, route_ref = rest[1:4]
        wa, wb = wa_ref, wb_ref
    else:
        h1_ref, t_ref, route_ref, wa, wb = rest
        _cast_weights_once(wa_ref, wa)
        _cast_weights_once(wb_ref, wb)

    @pl.when(pl.program_id(0) >= n_blocks)
    def _():
        t_ref[...] = jnp.zeros(t_ref.shape, _F32)

    @pl.when(pl.program_id(0) < n_blocks)
    def _():
        _outproj_router_body(x_ref, attn_ref, ssd_ref, wa, wb, wn_ref, wr_ref, br_ref, h1_ref, t_ref, route_ref, precise)


def _outproj_router_body(x_ref, attn_ref, ssd_ref, wa, wb, wn_ref, wr_ref, br_ref, h1_ref, t_ref, route_ref, precise):
    h1 = x_ref[...] + _dot(attn_ref[...], wa[...], precise) + _dot(ssd_ref[...], wb[...], precise)
    h1_ref[...] = h1
    t = _rms(h1, wn_ref[...])
    t_ref[...] = t
    logits = _dot(t, wr_ref[...], precise) + br_ref[...]
    lane = lax.broadcasted_iota(jnp.int32, logits.shape, 1).astype(_F32)
    neg = -jnp.inf
    big = float(LANES)
    is_g = lane < N_EXPERT_GROUPS
    gl = jnp.where(is_g, logits, neg)
    gmax = jnp.max(gl, axis=1, keepdims=True)
    g_idx = jnp.min(jnp.where(gl == gmax, lane, big), axis=1, keepdims=True)
    g_w = 1.0 / jnp.sum(jnp.where(is_g, jnp.exp(logits - gmax), 0.0), axis=1, keepdims=True)
    lo = N_EXPERT_GROUPS + EXPERTS_PER_GROUP * g_idx
    l1 = jnp.where(lane >= lo, jnp.where(lane < lo + EXPERTS_PER_GROUP, logits, neg), neg)
    m1 = jnp.max(l1, axis=1, keepdims=True)
    i1 = jnp.min(jnp.where(l1 == m1, lane, big), axis=1, keepdims=True)
    l2 = jnp.where(lane == i1, neg, l1)
    m2 = jnp.max(l2, axis=1, keepdims=True)
    i2 = jnp.min(jnp.where(l2 == m2, lane, big), axis=1, keepdims=True)
    r = jnp.exp(m2 - m1)
    p1 = 1.0 / (1.0 + r)
    w1 = g_w * p1
    w2 = g_w * (r * p1)
    e1 = i1 - N_EXPERT_GROUPS
    e2 = i2 - N_EXPERT_GROUPS
    route_ref[...] = jnp.where(lane == 0, e1, jnp.where(lane == 1, e2, jnp.where(lane == 2, w1,
                               jnp.where(lane == 3, w2, 0.0))))


def _outproj_router(x2d, attn, ssd, w_out, w_norm_ffn, w_router, b_router, tm, precise, t_rows, t_shared=None):
    t = x2d.shape[0]
    n_blocks = t // tm
    wdt = _F32 if precise else _BF16
    half_w = (D_ATTN, D_MODEL)
    fixed = lambda i: (0, 0)
    if precise:
        assert t_shared is not None and (t_rows - t) % tm == 0
        grid = (n_blocks,)
        row = lambda i: (i, 0)
        t_row = lambda i: (i + (t_rows - t) // tm, 0)
        extra_in, extra_args, aliases, scratch = [pl.BlockSpec(memory_space=pl.ANY)], [t_shared], {8: 1}, []
    else:
        grid = (n_blocks + pl.cdiv(t_rows - t, tm),)
        row = lambda i: (jnp.minimum(i, n_blocks - 1), 0)
        t_row = lambda i: (i, 0)
        extra_in, extra_args, aliases = [], [], {}
        scratch = [pltpu.VMEM(half_w, _BF16), pltpu.VMEM(half_w, _BF16)]
    return pl.pallas_call(
        functools.partial(_outproj_router_kernel, precise=precise, n_blocks=n_blocks),
        grid=grid,
        in_specs=[pl.BlockSpec((tm, D_MODEL), row), pl.BlockSpec((tm, D_ATTN), row), pl.BlockSpec((tm, D_SSM), row),
                  _resident(half_w, fixed), _resident(half_w, lambda i: (1, 0)),
                  pl.BlockSpec((1, D_MODEL), fixed), pl.BlockSpec((D_MODEL, LANES), fixed),
                  pl.BlockSpec((1, LANES), fixed)] + extra_in,
        out_specs=[pl.BlockSpec((tm, D_MODEL), row), pl.BlockSpec((tm, D_MODEL), t_row), pl.BlockSpec((tm, LANES), row)],
        out_shape=[jax.ShapeDtypeStruct((t, D_MODEL), _F32), jax.ShapeDtypeStruct((t_rows, D_MODEL), _F32),
                   jax.ShapeDtypeStruct((t, LANES), _F32)],
        scratch_shapes=scratch,
        input_output_aliases=aliases,
        compiler_params=_cparams(("arbitrary",)),
        name="outproj_router_precise" if precise else "outproj_router",
    )(x2d, attn, ssd, w_out, w_out, w_norm_ffn.reshape(1, D_MODEL), w_router.astype(wdt), b_router, *extra_args)


def _expert_kernel(te_ref, nv_ref, blk_ref, off_ref, idx_prev_ref, idx_ref, idx_next_ref, t_hbm, wg_ref, wu_ref, wd_ref,
                   y_hbm, xbuf, ybuf, sem_in, sem_out, wgb_sc, wub_sc, wdb_sc, *, tm, n_tiles):
    i = pl.program_id(0)
    i_prev = jnp.maximum(i - 1, 0)
    i_next = jnp.minimum(i + 1, n_tiles - 1)
    nv = nv_ref[i]
    nv_next = jnp.where(i + 1 < n_tiles, nv_ref[i_next], 0)
    f_half = D_FF_EXPERT // 2
    n_issue_chunks = 4
    per_chunk = tm // n_issue_chunks
    half = tm // 2
    n_pad = y_hbm.shape[0] // 2

    def gather_rows(idx, tile_i, s, rows):
        off = off_ref[tile_i]
        for r in rows:
            pltpu.make_async_copy(t_hbm.at[pl.ds(idx[0, 0, off + r], 1), :], xbuf.at[s, pl.ds(r, 1), :],
                                  sem_in.at[s]).start()

    def wait_gather(s):
        pltpu.make_async_copy(t_hbm.at[pl.ds(0, tm), :], xbuf.at[s], sem_in.at[s]).wait()

    def scatter_rows(idx, tile_i, s, rows):
        off = off_ref[tile_i] + 2 * tm
        valid = nv_ref[tile_i]
        for r in rows:
            spare = (r // half) * n_pad + (n_pad - half) + (r % half)
            dst = jnp.where(r < valid, idx[0, 0, off + r], spare)
            pltpu.make_async_copy(ybuf.at[s, pl.ds(r, 1), :], y_hbm.at[pl.ds(dst, 1), :], sem_out.at[s]).start()

    def wait_scatter(s):
        pltpu.make_async_copy(ybuf.at[s], y_hbm.at[pl.ds(0, tm), :], sem_out.at[s]).wait()

    @pl.when(i == 0)
    def _():
        gather_rows(idx_ref, i, 0, range(tm))
        ybuf[0, 0:half, :] = jnp.zeros((half, D_MODEL), _F32)
        for k in range(2):
            fill = pltpu.make_async_copy(ybuf.at[0, pl.ds(0, half), :],
                                         y_hbm.at[pl.ds((k + 1) * n_pad - half, half), :], sem_out.at[0])
            fill.start()
            fill.wait()

    def tile(s, has_prev):
        o = 1 - s

        def issue(c):
            rows = range(c * per_chunk, (c + 1) * per_chunk)
            gather_rows(idx_next_ref, i_next, o, rows)
            if has_prev:
                scatter_rows(idx_prev_ref, i_prev, o, rows)

        wait_gather(s)
        x = xbuf[s].astype(_BF16)
        parts = []
        for c in range(2):
            cols = slice(c * f_half, (c + 1) * f_half)
            hg = jnp.dot(x, wgb_sc[:, cols], preferred_element_type=_F32)
            issue(2 * c)
            hu = jnp.dot(x, wub_sc[:, cols], preferred_element_type=_F32)
            issue(2 * c + 1)
            parts.append((_silu(hg) * hu).astype(_BF16))
        hh = jnp.concatenate(parts, axis=1)
        ybuf[s] = jnp.dot(hh, wdb_sc[...], preferred_element_type=_F32)
        if has_prev:
            wait_scatter(o)

        @pl.when(nv_next == 0)
        def _():
            wait_gather(o)
            scatter_rows(idx_ref, i, s, range(tm))
            wait_scatter(s)

    used = nv > 0
    odd = lax.rem(i, 2) == 1

    @pl.when(used & ((i == 0) | (te_ref[i] != te_ref[i_prev])))
    def _():
        wgb_sc[...] = wg_ref[...].astype(_BF16)
        wub_sc[...] = wu_ref[...].astype(_BF16)
        wdb_sc[...] = wd_ref[...].astype(_BF16)

    @pl.when(i == 0)
    def _():
        tile(0, False)

    @pl.when(used & (i > 0) & jnp.logical_not(odd))
    def _():
        tile(0, True)

    @pl.when(used & odd)
    def _():
        tile(1, True)


def _moe_plan(e_ids, tm, n_tiles):
    n = e_ids.shape[0]
    n_assign = 2 * n
    n_pad = n + tm // 2
    e_flat = e_ids.reshape(-1)
    order = jnp.argsort(e_flat, stable=True).astype(jnp.int32)
    experts = jnp.arange(N_EXPERTS, dtype=jnp.int32)
    counts = jnp.sum((e_flat[:, None] == experts[None, :]).astype(jnp.int32), axis=0)
    seg_start = jnp.cumsum(counts) - counts
    tiles_per = (counts + tm - 1) // tm
    cum_tiles = jnp.cumsum(tiles_per)
    tile_start = cum_tiles - tiles_per
    n_used = cum_tiles[-1]
    tile_id = jnp.arange(n_tiles, dtype=jnp.int32)
    used = tile_id < n_used
    te = jnp.sum((cum_tiles[None, :] <= jnp.minimum(tile_id, n_used - 1)[:, None]).astype(jnp.int32), axis=1)
    of_tile = lambda v: jnp.sum(jnp.where(te[:, None] == experts[None, :], v[None, :], 0), axis=1)
    j = tile_id - of_tile(tile_start)
    nvalid = jnp.where(used, jnp.clip(of_tile(counts) - j * tm, 0, tm), 0).astype(jnp.int32)
    first = jnp.minimum(of_tile(seg_start) + j * tm, n_assign - 1)
    n_win = pl.cdiv(n_assign, tm)
    fill = (n_win + 1) * tm - n_assign
    tok = jnp.pad(order // 2, (0, fill)).reshape(n_win + 1, tm)
    dst = jnp.pad((order % 2) * n_pad + order // 2, (0, fill)).reshape(n_win + 1, tm)
    windows = jnp.concatenate([tok[:-1], tok[1:], dst[:-1], dst[1:]], axis=1).reshape(n_win, 1, 4 * tm)
    return te, nvalid, (first // tm).astype(jnp.int32), (first % tm).astype(jnp.int32), windows, n_pad


def _expert_mlp(t_all, e_ids, w_gate, w_up, w_down):
    n = t_all.shape[0]
    tm = EXPERT_TILE
    n_tiles = pl.cdiv(2 * n, tm) + N_EXPERTS
    te, nvalid, blk, off, windows, n_pad = _moe_plan(e_ids, tm, n_tiles)
    wmap = lambda i, te_ref, nv_ref, blk_ref, off_ref: (te_ref[i], 0, 0)

    def window_spec(shift):
        def index_map(i, te_ref, nv_ref, blk_ref, off_ref):
            return (blk_ref[jnp.clip(i + shift, 0, n_tiles - 1)], 0, 0)
        return pl.BlockSpec((1, 1, 4 * tm), index_map, memory_space=pltpu.SMEM)

    grid_spec = pltpu.PrefetchScalarGridSpec(
        num_scalar_prefetch=4,
        grid=(n_tiles,),
        in_specs=[
            window_spec(-1), window_spec(0), window_spec(1),
            pl.BlockSpec(memory_space=pl.ANY),
            pl.BlockSpec((None, D_MODEL, D_FF_EXPERT), wmap),
            pl.BlockSpec((None, D_MODEL, D_FF_EXPERT), wmap),
            pl.BlockSpec((None, D_FF_EXPERT, D_MODEL), wmap),
        ],
        out_specs=pl.BlockSpec(memory_space=pl.ANY),
        scratch_shapes=[pltpu.VMEM((2, tm, D_MODEL), _F32), pltpu.VMEM((2, tm, D_MODEL), _F32),
                        pltpu.SemaphoreType.DMA((2,)), pltpu.SemaphoreType.DMA((2,)),
                        pltpu.VMEM((D_MODEL, D_FF_EXPERT), _BF16), pltpu.VMEM((D_MODEL, D_FF_EXPERT), _BF16),
                        pltpu.VMEM((D_FF_EXPERT, D_MODEL), _BF16)],
    )
    y_rows = pl.pallas_call(
        functools.partial(_expert_kernel, tm=tm, n_tiles=n_tiles),
        grid_spec=grid_spec,
        out_shape=jax.ShapeDtypeStruct((2 * n_pad, D_MODEL), _F32),
        compiler_params=_cparams(("arbitrary",)),
        name="expert_mlp",
    )(te, nvalid, blk, off, windows, windows, windows, t_all, w_gate, w_up, w_down)
    return y_rows.reshape(2, n_pad, D_MODEL)


def _combine_ple_kernel(h1_ref, y_ref, route_ref, p_ref, wn_ref, wg_ref, wp_ref, wf_ref, o_ref, wg_sc):
    _cast_weights_once(wg_ref, wg_sc)
    route = route_ref[...]
    h2 = h1_ref[...] + route[:, 2:3] * y_ref[0] + route[:, 3:4] * y_ref[1]
    u = _rms(h2, wn_ref[...])
    gate = jnp.dot(u.astype(_BF16), wg_sc[...], preferred_element_type=_F32)
    gate = 1.0 / (1.0 + jnp.exp(-gate))
    pp = jnp.dot(p_ref[...].astype(_BF16), wp_ref[...], preferred_element_type=_F32)
    h3 = h2 + gate * pp
    o_ref[...] = _rms(h3, wf_ref[...])


def _combine_ple(h1, y_stk, route, p2d, w_norm_ple, w_ple_gate, w_ple_proj, w_norm_final, tm, row_off):
    t = h1.shape[0]
    off = row_off // tm
    row = lambda i: (i, 0)
    fixed = lambda i: (0, 0)
    return pl.pallas_call(
        _combine_ple_kernel,
        grid=(t // tm,),
        in_specs=[pl.BlockSpec((tm, D_MODEL), row),
                  pl.BlockSpec((2, tm, D_MODEL), lambda i: (0, i + off, 0)),
                  pl.BlockSpec((tm, LANES), row), pl.BlockSpec((tm, D_PLE), row),
                  pl.BlockSpec((1, D_MODEL), fixed), _resident((D_MODEL, D_MODEL), fixed),
                  pl.BlockSpec((D_PLE, D_MODEL), fixed), pl.BlockSpec((1, D_MODEL), fixed)],
        out_specs=pl.BlockSpec((tm, D_MODEL), row),
        out_shape=jax.ShapeDtypeStruct((t, D_MODEL), _F32),
        scratch_shapes=[pltpu.VMEM((D_MODEL, D_MODEL), _BF16)],
        compiler_params=_cparams(("arbitrary",)),
        name="combine_ple",
    )(h1, y_stk, route, p2d, w_norm_ple.reshape(1, D_MODEL), w_ple_gate, w_ple_proj.astype(_BF16),
      w_norm_final.reshape(1, D_MODEL))


def _sample_inproj_kernel(x_ref, wn_ref, w_ref, o_ref):
    u = _rms(x_ref[...], wn_ref[...])
    u_hi, u_lo = _split2(u)
    w_hi, w_lo = _split2(w_ref[...])
    nt = lambda a, b: lax.dot_general(a, b, _NT, preferred_element_type=_F32)
    o_ref[...] = nt(w_hi, u_hi) + nt(w_hi, u_lo) + nt(w_lo, u_hi)


def _sample_inproj(xs2d, w_norm, w_in_t, tn):
    rows = xs2d.shape[0]
    return pl.pallas_call(
        _sample_inproj_kernel,
        grid=(pl.cdiv(D_IN_PROJ, tn),),
        in_specs=[pl.BlockSpec((rows, D_MODEL), lambda j: (0, 0)), pl.BlockSpec((1, D_MODEL), lambda j: (0, 0)),
                  pl.BlockSpec((tn, D_MODEL), lambda j: (j, 0))],
        out_specs=pl.BlockSpec((tn, rows), lambda j: (j, 0)),
        out_shape=jax.ShapeDtypeStruct((D_IN_PROJ, rows), _F32),
        compiler_params=_cparams(("arbitrary",)),
        name="sample_inproj",
    )(xs2d, w_norm.reshape(1, D_MODEL), w_in_t).T


def _split2(x):
    hi = x.astype(_BF16)
    lo = (x - hi.astype(_F32)).astype(_BF16)
    return hi, lo


def _decode_attn_kernel(pt_ref, qkv_ref, c_ref, s1_ref, s2_ref, lq1_ref, lk1_ref, lq2_ref, lk2_ref, wsub_ref, *rest,
                        n_pages_step, n_steps, n_real):
    k_refs = rest[:n_pages_step]
    v_refs = rest[n_pages_step:2 * n_pages_step]
    o_ref, knew_ref, vnew_ref = rest[2 * n_pages_step:2 * n_pages_step + 3]
    q_sc, m_sc, l_sc, acc_sc, kt_sc, vt_sc = rest[2 * n_pages_step + 3:]
    j = pl.program_id(1)
    R = SAMPLE_ROWS
    hr = 2 * R
    page = kt_sc.shape[0] // N_ATT_HEADS

    @pl.when(j == 0)
    def _():
        c, s1, s2 = c_ref[...], s1_ref[...], s2_ref[...]
        lane = lax.broadcasted_iota(jnp.int32, (R, LANES), 1)
        kt_sc[...] = jnp.zeros(kt_sc.shape, _F32)
        vt_sc[...] = jnp.zeros(vt_sc.shape, _F32)
        for h in range(N_ATT_HEADS):
            sl = slice(h * LANES, (h + 1) * LANES)
            q = _rope_block(qkv_ref[:, sl], c, s1, s2) * (ATT_HEAD_DIM ** -0.5)
            k = _rope_block(qkv_ref[:, D_ATTN + h * LANES:D_ATTN + (h + 1) * LANES], c, s1, s2)
            v = qkv_ref[:, 2 * D_ATTN + h * LANES:2 * D_ATTN + (h + 1) * LANES]
            knew_ref[:, sl] = k
            vnew_ref[:, sl] = v
            kt_sc[pl.ds(h, R, stride=N_ATT_HEADS), :] = k
            vt_sc[pl.ds(h, R, stride=N_ATT_HEADS), :] = v
            q2 = jnp.concatenate([jnp.where(lane < ATT_HEAD_DIM, q, 0.0), jnp.where(lane >= ATT_HEAD_DIM, q, 0.0)], axis=0)
            hi, lo = _split2(q2)
            q_sc[h] = jnp.concatenate([hi, lo], axis=0)
        m_sc[...] = jnp.full(m_sc.shape, -jnp.inf, _F32)
        l_sc[...] = jnp.zeros(l_sc.shape, _F32)
        acc_sc[...] = jnp.zeros(acc_sc.shape, _F32)

    def head_rows(tiles, h):
        x = jnp.concatenate([t[pl.ds(h, page, stride=N_ATT_HEADS), :] for t in tiles], axis=0)
        return _split2(x)

    def process(k_tiles, v_tiles, mask):
        s_parts = []
        for h in range(N_ATT_HEADS):
            k_hi, k_lo = head_rows(k_tiles, h)
            q3 = q_sc[h]
            a = lax.dot_general(q3[0:2 * hr], k_hi, _NT, preferred_element_type=_F32)
            b = lax.dot_general(q3[0:hr], k_lo, _NT, preferred_element_type=_F32)
            s_parts.append(a[0:hr] + a[hr:2 * hr] + b)
        s = jnp.concatenate(s_parts, axis=0)
        if mask is not None:
            s = jnp.where(mask, s, -jnp.inf)
        m_prev = m_sc[...]
        m_new = jnp.maximum(m_prev, jnp.max(s, axis=1, keepdims=True))
        alpha = jnp.exp(m_prev - m_new)
        p = jnp.exp(s - m_new[:, :1])
        l_sc[...] = alpha * l_sc[...] + jnp.sum(p, axis=1, keepdims=True)
        m_sc[...] = m_new
        pv_parts = []
        for h in range(N_ATT_HEADS):
            v_hi, v_lo = head_rows(v_tiles, h)
            p_hi, p_lo = _split2(p[h * hr:(h + 1) * hr])
            a = jnp.dot(jnp.concatenate([p_hi, p_lo], axis=0), v_hi, preferred_element_type=_F32)
            b = jnp.dot(p_hi, v_lo, preferred_element_type=_F32)
            pv_parts.append(a[0:hr] + a[hr:2 * hr] + b)
        acc_sc[...] = alpha * acc_sc[...] + jnp.concatenate(pv_parts, axis=0)

    process(k_refs, v_refs, None)

    @pl.when(j == n_steps - 1)
    def _():
        rows = N_ATT_HEADS * hr
        row = lax.broadcasted_iota(jnp.int32, (rows, page), 0)
        col = lax.broadcasted_iota(jnp.int32, (rows, page), 1)
        qrow = row % R
        process([kt_sc], [vt_sc], (col <= qrow) & (col < n_real))
        o = acc_sc[...] / l_sc[...]
        lam = _lambda(lq1_ref, lk1_ref, lq2_ref, lk2_ref)
        wsub = wsub_ref[...]
        for h in range(N_ATT_HEADS):
            o1 = o[h * hr:h * hr + R]
            o2 = o[h * hr + R:(h + 1) * hr]
            o_ref[:, h * LANES:(h + 1) * LANES] = _diff_finalize(o1, o2, lam, wsub)


def _decode_attention(proj3, cache_k2, cache_v2, page_table, first_pos, lam_params, w_subln, n_real):
    bsz = proj3.shape[0]
    page = cache_k2.shape[2] // N_ATT_HEADS
    n_pages = page_table.shape[1]
    pg = PAGES_PER_STEP
    n_steps = n_pages // pg
    R = SAMPLE_ROWS
    c, s1, s2 = _rope_tables(first_pos, R)
    lq1, lk1, lq2, lk2 = [p.reshape(1, ATT_HEAD_DIM) for p in lam_params]
    fixed = lambda b, j, pt: (0, 0)
    tab = pl.BlockSpec((R, LANES), fixed)
    vec = pl.BlockSpec((1, ATT_HEAD_DIM), fixed)

    def page_spec(i):
        return pl.BlockSpec((None, None, page * N_ATT_HEADS, LANES), lambda b, j, pt: (0, pt[b, j * pg + i], 0, 0))

    out_row = pl.BlockSpec((None, R, D_ATTN), lambda b, j, pt: (b, 0, 0))
    grid_spec = pltpu.PrefetchScalarGridSpec(
        num_scalar_prefetch=1,
        grid=(bsz, n_steps),
        in_specs=[pl.BlockSpec((None, R, 3 * D_ATTN), lambda b, j, pt: (b, 0, 0)), tab, tab, tab, vec, vec, vec, vec,
                  pl.BlockSpec((1, ATT_V_DIM), fixed)]
                 + [page_spec(i) for i in range(pg)] + [page_spec(i) for i in range(pg)],
        out_specs=[out_row, out_row, out_row],
        scratch_shapes=[pltpu.VMEM((N_ATT_HEADS, 4 * R, LANES), _BF16),
                        pltpu.VMEM((N_ATT_HEADS * 2 * R, LANES), _F32), pltpu.VMEM((N_ATT_HEADS * 2 * R, LANES), _F32),
                        pltpu.VMEM((N_ATT_HEADS * 2 * R, LANES), _F32),
                        pltpu.VMEM((page * N_ATT_HEADS, LANES), _F32), pltpu.VMEM((page * N_ATT_HEADS, LANES), _F32)],
    )
    return pl.pallas_call(
        functools.partial(_decode_attn_kernel, n_pages_step=pg, n_steps=n_steps, n_real=n_real),
        grid_spec=grid_spec,
        out_shape=[jax.ShapeDtypeStruct((bsz, R, D_ATTN), _F32)] * 3,
        compiler_params=_cparams(("arbitrary", "arbitrary")),
        name="decode_attn",
    )(page_table, proj3, c, s1, s2, lq1, lk1, lq2, lk2, w_subln.reshape(1, ATT_V_DIM),
      *([cache_k2] * pg), *([cache_v2] * pg))


def kernel(x_prompt, x_sample, cache_k, cache_v, state_conv, state_ssm, page_table, p_prompt, p_sample, w_norm_mix, w_in, lambda_q1, lambda_k1, lambda_q2, lambda_k2, w_subln, conv_w, conv_b, dt_bias, A_log, D_skip, w_ssm_norm, w_out, w_norm_ffn, w_group_router, b_group_router, w_expert_router, b_expert_router, w_exp_gate, w_exp_up, w_exp_down, w_norm_ple, w_ple_gate, w_ple_proj, w_norm_final):
    bp, lp, _ = x_prompt.shape
    bs, ls, _ = x_sample.shape
    past = page_table.shape[1] * cache_k.shape[2]
    tp = bp * lp
    R = SAMPLE_ROWS
    ts = bs * R
    lam_params = (lambda_q1[0], lambda_k1[0], lambda_q2[0], lambda_k2[0])
    pad_r = LANES - N_EXPERT_GROUPS - N_EXPERTS
    w_router = jnp.pad(jnp.concatenate([w_group_router[0], w_expert_router[0]], axis=1), ((0, 0), (0, pad_r)))
    b_router = jnp.pad(jnp.concatenate([b_group_router[0], b_expert_router[0]]), (0, pad_r)).reshape(1, LANES)

    xp2d = x_prompt.reshape(tp, D_MODEL)
    w_in_t = w_in[0].T
    q, kf, kb, vf, vb, z, xbc, dt = _prompt_inproj(xp2d, w_norm_mix[0], w_in_t, lp, 512)
    attn_p = _prompt_attention(q, kb, vb, lam_params, w_subln[0], bp, lp, 512)
    ssd_p, ssm_p = _ssd(xbc, z, dt, jnp.zeros((bp, CONV_WIDTH - 1, CONV_DIM), _F32),
                        jnp.zeros((bp, N_SSM_HEADS, SSM_HEAD_DIM, SSM_STATE), _F32),
                        conv_w[0], conv_b[0], dt_bias[0], A_log[0], D_skip[0], w_ssm_norm[0],
                        bp, lp // SSM_CHUNK, SSM_CHUNK, False, _BF16)
    h1_p, t_all, route_p = _outproj_router(xp2d, attn_p, ssd_p, w_out[0], w_norm_ffn[0], w_router, b_router, 256, False,
                                           tp + ts)

    xs3 = jnp.pad(x_sample, ((0, 0), (0, R - ls), (0, 0)))
    xs2d = xs3.reshape(ts, D_MODEL)
    proj_s = _sample_inproj(xs2d, w_norm_mix[0], w_in_t, 512)
    proj3 = proj_s.reshape(bs, R, D_IN_PROJ)
    cache_rows = cache_k.shape[:2] + (cache_k.shape[2] * N_ATT_HEADS, LANES)
    attn_s, k_s, v_s = _decode_attention(proj3, cache_k.reshape(cache_rows), cache_v.reshape(cache_rows), page_table,
                                         past, lam_params, w_subln[0], ls)
    off = 3 * D_ATTN
    pad_rows = ((0, 0), (0, SSM_CHUNK - R), (0, 0))
    z_s = jnp.pad(proj3[:, :, off:off + D_SSM], pad_rows).reshape(bs * SSM_CHUNK, D_SSM)
    xbc_s3 = proj3[:, :, off + D_SSM:off + D_SSM + CONV_DIM]
    xbc_s = jnp.pad(xbc_s3, pad_rows).reshape(bs * SSM_CHUNK, CONV_DIM)
    dt_s = jnp.pad(proj3[:, :, off + D_SSM + CONV_DIM:], ((0, 0), (0, SSM_CHUNK - R), (0, LANES - N_SSM_HEADS)))
    dt_s = dt_s.reshape(bs * SSM_CHUNK, LANES)
    ssd_s, ssm_s = _ssd(xbc_s, z_s, dt_s, state_conv[0], state_ssm[0], conv_w[0], conv_b[0], dt_bias[0], A_log[0],
                        D_skip[0], w_ssm_norm[0], bs, 1, ls, True, _F32)
    ssd_s = ssd_s.reshape(bs, SSM_CHUNK, D_SSM)[:, :R].reshape(ts, D_SSM)
    h1_s, t_all, route_s = _outproj_router(xs2d, attn_s.reshape(ts, D_ATTN), ssd_s, w_out[0], w_norm_ffn[0],
                                           w_router, b_router, ts, True, tp + ts, t_shared=t_all)

    e_ids = jnp.concatenate([route_p[:, :2], route_s[:, :2]], axis=0).astype(jnp.int32)
    y_stk = _expert_mlp(t_all, e_ids, w_exp_gate[0], w_exp_up[0], w_exp_down[0])
    y_p = _combine_ple(h1_p, y_stk, route_p, p_prompt[0].reshape(tp, D_PLE), w_norm_ple[0], w_ple_gate[0],
                       w_ple_proj[0], w_norm_final, 256, 0)
    ps2d = jnp.pad(p_sample[0], ((0, 0), (0, R - ls), (0, 0))).reshape(ts, D_PLE)
    y_s = _combine_ple(h1_s, y_stk, route_s, ps2d, w_norm_ple[0], w_ple_gate[0], w_ple_proj[0], w_norm_final, ts, tp)

    y_prompt = y_p.reshape(bp, lp, D_MODEL)
    y_sample = y_s.reshape(bs, R, D_MODEL)[:, :ls]
    k_prompt = kf.reshape(1, bp, lp, N_ATT_HEADS, 2 * ATT_HEAD_DIM)
    v_prompt = vf.reshape(1, bp, lp, N_ATT_HEADS, ATT_V_DIM)
    conv_prompt = xbc.reshape(bp, lp, CONV_DIM)[:, lp - (CONV_WIDTH - 1):][None]
    ssm_prompt = ssm_p[None]
    k_sample = k_s[:, :ls].reshape(1, bs, ls, N_ATT_HEADS, 2 * ATT_HEAD_DIM)
    v_sample = v_s[:, :ls].reshape(1, bs, ls, N_ATT_HEADS, ATT_V_DIM)
    conv_sample = xbc_s3[:, ls - (CONV_WIDTH - 1):ls][None]
    ssm_sample = ssm_s[None]
    return (y_prompt, y_sample, k_prompt, v_prompt, conv_prompt, ssm_prompt, k_sample, v_sample, conv_sample, ssm_sample)
```
